```python
import jax, jax.numpy as jnp
from jax import lax
import numpy as np

D_MODEL = 2048
BATCH = 8
SEQ = 4096
DEPTH = 4

N_MIXERS = 2
N_SSD_LAYERS = (DEPTH + N_MIXERS - 1) // N_MIXERS
N_SC_LAYERS = DEPTH // N_MIXERS

SSM_EXPAND = 2
D_INNER = SSM_EXPAND * D_MODEL
SSM_HEAD_DIM = 64
SSM_HEADS = D_INNER // SSM_HEAD_DIM
SSM_GROUPS = 8
SSM_HEADS_PER_GROUP = SSM_HEADS // SSM_GROUPS
SSM_STATE = 128
SSM_CONV = 4
SSM_CHUNK = 128
SSM_BC_DIM = SSM_GROUPS * SSM_STATE
SSM_CONV_DIM = D_INNER + 2 * SSM_BC_DIM
SSM_IN_DIM = D_INNER + SSM_CONV_DIM + SSM_HEADS

SC_DIM = D_MODEL
SC_WIDTH = 3

D_FF = 5632
FFN_CONV = 3

EPS = 1e-5
DT_MIN = 1e-3
DT_MAX = 1e-1
A_MIN = 1.0
A_MAX = 16.0

kernel_name = "hybrid_ssd_shortconv_convffn_trunk"


def rms_norm(x, w):
    x32 = x.astype(jnp.float32)
    y = x32 * lax.rsqrt(jnp.mean(x32 * x32, axis=-1, keepdims=True) + EPS)
    return (y * w.astype(jnp.float32)).astype(x.dtype)


def causal_dwconv(x, w, b=None):
    width, ch = w.shape
    y = lax.conv_general_dilated(
        x, w[:, None, :].astype(x.dtype), window_strides=(1,),
        padding=[(width - 1, 0)], dimension_numbers=("NWC", "WIO", "NWC"),
        feature_group_count=ch)
    if b is not None:
        y = y + b.astype(x.dtype)
    return y


def segsum_exp(a):
    t = a.shape[-1]
    cs = jnp.cumsum(a, axis=-1)
    diff = cs[..., :, None] - cs[..., None, :]
    mask = jnp.tril(jnp.ones((t, t), dtype=bool))
    return jnp.exp(jnp.where(mask, diff, -jnp.inf))


def ssd_chunked(xh, dt, a, bm, cm):
    f32 = jnp.float32
    b_, s_, h_, p_ = xh.shape
    l_ = SSM_CHUNK
    c_ = s_ // l_
    g_, k_, n_ = SSM_GROUPS, SSM_HEADS_PER_GROUP, SSM_STATE
    xdt = (xh.astype(f32) * dt[..., None]).reshape(b_, c_, l_, g_, k_, p_)
    da = jnp.moveaxis((dt * a).reshape(b_, c_, l_, g_, k_), 2, -1)
    a_cs = jnp.cumsum(da, axis=-1)
    bc = bm.astype(f32).reshape(b_, c_, l_, g_, n_)
    cc = cm.astype(f32).reshape(b_, c_, l_, g_, n_)
    cb = jnp.einsum("bclgn,bcsgn->bcgls", cc, bc)
    decay = segsum_exp(da)
    y_diag = jnp.einsum("bcgls,bcgkls,bcsgkp->bclgkp", cb, decay, xdt)
    decay_to_end = jnp.exp(a_cs[..., -1:] - a_cs)
    states = jnp.einsum("bclgn,bcgkl,bclgkp->bcgkpn", bc, decay_to_end, xdt)
    chunk_decay = jnp.exp(a_cs[..., -1])

    def step(hstate, inp):
        st, dec = inp
        return hstate * dec[..., None, None] + st, hstate

    h0 = jnp.zeros((b_, g_, k_, p_, n_), f32)
    _, prev = lax.scan(step, h0, (jnp.moveaxis(states, 1, 0), jnp.moveaxis(chunk_decay, 1, 0)))
    prev = jnp.moveaxis(prev, 0, 1)
    y_off = jnp.einsum("bclgn,bcgkpn,bcgkl->bclgkp", cc, prev, jnp.exp(a_cs))
    return (y_diag + y_off).reshape(b_, s_, h_, p_)


def ssd_mixer(u, w_in, conv_w, conv_b, dt_bias, a_log, d_skip, norm_w, w_out):
    f32 = jnp.float32
    b_, s_, _ = u.shape
    zxbcdt = u @ w_in
    z, xbc, dt_raw = jnp.split(zxbcdt, [D_INNER, D_INNER + SSM_CONV_DIM], axis=-1)
    xbc = jax.nn.silu(causal_dwconv(xbc, conv_w, conv_b))
    xs, bm, cm = jnp.split(xbc, [D_INNER, D_INNER + SSM_BC_DIM], axis=-1)
    dt = jax.nn.softplus(dt_raw.astype(f32) + dt_bias.astype(f32))
    a = -jnp.exp(a_log.astype(f32))
    xh = xs.reshape(b_, s_, SSM_HEADS, SSM_HEAD_DIM)
    y = ssd_chunked(xh, dt, a,
                    bm.reshape(b_, s_, SSM_GROUPS, SSM_STATE),
                    cm.reshape(b_, s_, SSM_GROUPS, SSM_STATE))
    y = y + xh.astype(f32) * d_skip.astype(f32)[:, None]
    yg = (y.reshape(b_, s_, D_INNER) * jax.nn.silu(z.astype(f32)))
    yg = yg.reshape(b_, s_, SSM_GROUPS, D_INNER // SSM_GROUPS)
    yg = yg * lax.rsqrt(jnp.mean(yg * yg, axis=-1, keepdims=True) + EPS)
    y = (yg.reshape(b_, s_, D_INNER) * norm_w.astype(f32)).astype(u.dtype)
    return y @ w_out


def short_conv_mixer(u, w_in, conv_w, w_out):
    bg, cg, h = jnp.split(u @ w_in, 3, axis=-1)
    return (bg * causal_dwconv(cg * h, conv_w)) @ w_out


def conv_ffn(u, w_up, conv_w, conv_b, w_down):
    hu = causal_dwconv(u @ w_up, conv_w, conv_b)
    g, v = jnp.split(hu, 2, axis=-1)
    return (jax.nn.silu(g) * v) @ w_down


def _fwd_setup_inputs(seed: int = 0) -> dict:
    key = jax.random.key(seed)
    ks = jax.random.split(key, 24)
    f32 = jnp.float32
    nrm = lambda k, shape, scale: jax.random.normal(k, shape, f32) * scale
    La, Lb = N_SSD_LAYERS, N_SC_LAYERS
    dt0 = jnp.exp(jax.random.uniform(ks[5], (La, SSM_HEADS), f32,
                                     float(np.log(DT_MIN)), float(np.log(DT_MAX))))
    return {
        "x": jax.random.normal(ks[0], (BATCH, SEQ, D_MODEL), f32),
        "mix_norm_w": 1.0 + nrm(ks[1], (DEPTH, D_MODEL), 0.02),
        "ffn_norm_w": 1.0 + nrm(ks[2], (DEPTH, D_MODEL), 0.02),
        "final_norm_w": 1.0 + nrm(ks[3], (D_MODEL,), 0.02),
        "ssd_w_in": nrm(ks[4], (La, D_MODEL, SSM_IN_DIM), D_MODEL ** -0.5),
        "ssd_conv_w": nrm(ks[6], (La, SSM_CONV, SSM_CONV_DIM), SSM_CONV ** -0.5),
        "ssd_conv_b": nrm(ks[7], (La, SSM_CONV_DIM), 0.02),
        "ssd_dt_bias": dt0 + jnp.log(-jnp.expm1(-dt0)),
        "ssd_a_log": jnp.log(jax.random.uniform(ks[8], (La, SSM_HEADS), f32, A_MIN, A_MAX)),
        "ssd_d": 1.0 + nrm(ks[9], (La, SSM_HEADS), 0.02),
        "ssd_norm_w": 1.0 + nrm(ks[10], (La, D_INNER), 0.02),
        "ssd_w_out": nrm(ks[11], (La, D_INNER, D_MODEL), D_INNER ** -0.5),
        "sc_w_in": nrm(ks[12], (Lb, D_MODEL, 3 * SC_DIM), D_MODEL ** -0.5),
        "sc_conv_w": nrm(ks[13], (Lb, SC_WIDTH, SC_DIM), SC_WIDTH ** -0.5),
        "sc_w_out": nrm(ks[14], (Lb, SC_DIM, D_MODEL), SC_DIM ** -0.5),
        "ffn_w_up": nrm(ks[15], (DEPTH, D_MODEL, 2 * D_FF), D_MODEL ** -0.5),
        "ffn_conv_w": nrm(ks[16], (DEPTH, FFN_CONV, 2 * D_FF), FFN_CONV ** -0.5),
        "ffn_conv_b": nrm(ks[17], (DEPTH, 2 * D_FF), 0.02),
        "ffn_w_down": nrm(ks[18], (DEPTH, D_FF, D_MODEL), D_FF ** -0.5),
    }


def _fwd_reference(x, mix_norm_w, ffn_norm_w, final_norm_w,
              ssd_w_in, ssd_conv_w, ssd_conv_b, ssd_dt_bias, ssd_a_log, ssd_d,
              ssd_norm_w, ssd_w_out,
              sc_w_in, sc_conv_w, sc_w_out,
              ffn_w_up, ffn_conv_w, ffn_conv_b, ffn_w_down):
    for i in range(DEPTH):
        j = i // N_MIXERS
        h = rms_norm(x, mix_norm_w[i])
        if i % N_MIXERS == 0:
            x = x + ssd_mixer(h, ssd_w_in[j], ssd_conv_w[j], ssd_conv_b[j], ssd_dt_bias[j],
                              ssd_a_log[j], ssd_d[j], ssd_norm_w[j], ssd_w_out[j])
        else:
            x = x + short_conv_mixer(h, sc_w_in[j], sc_conv_w[j], sc_w_out[j])
        h = rms_norm(x, ffn_norm_w[i])
        x = x + conv_ffn(h, ffn_w_up[i], ffn_conv_w[i], ffn_conv_b[i], ffn_w_down[i])
    return rms_norm(x, final_norm_w)


import jax as _jax
import jax.numpy as _jnp

TWIN_FORMAT = 'train_step'
FWD_PARAMS = ['x', 'mix_norm_w', 'ffn_norm_w', 'final_norm_w', 'ssd_w_in', 'ssd_conv_w', 'ssd_conv_b', 'ssd_dt_bias', 'ssd_a_log', 'ssd_d', 'ssd_norm_w', 'ssd_w_out', 'sc_w_in', 'sc_conv_w', 'sc_w_out', 'ffn_w_up', 'ffn_conv_w', 'ffn_conv_b', 'ffn_w_down']
TWIN_WEIGHTS = ['mix_norm_w', 'ffn_norm_w', 'final_norm_w', 'ssd_w_in', 'ssd_conv_w', 'ssd_conv_b', 'ssd_dt_bias', 'ssd_a_log', 'ssd_d', 'ssd_norm_w', 'ssd_w_out', 'sc_w_in', 'sc_conv_w', 'sc_w_out', 'ffn_w_up', 'ffn_conv_w', 'ffn_conv_b', 'ffn_w_down']
TWIN_DIFF_INPUT = 'x'
TWIN_INPUTS = ['x', 'mix_norm_w', 'ffn_norm_w', 'final_norm_w', 'ssd_w_in', 'ssd_conv_w', 'ssd_conv_b', 'ssd_dt_bias', 'ssd_a_log', 'ssd_d', 'ssd_norm_w', 'ssd_w_out', 'sc_w_in', 'sc_conv_w', 'sc_w_out', 'ffn_w_up', 'ffn_conv_w', 'ffn_conv_b', 'ffn_w_down', 'loss_target', 'm_mix_norm_w', 'm_ffn_norm_w', 'm_final_norm_w', 'm_ssd_w_in', 'm_ssd_conv_w', 'm_ssd_conv_b', 'm_ssd_dt_bias', 'm_ssd_a_log', 'm_ssd_d', 'm_ssd_norm_w', 'm_ssd_w_out', 'm_sc_w_in', 'm_sc_conv_w', 'm_sc_w_out', 'm_ffn_w_up', 'm_ffn_conv_w', 'm_ffn_conv_b', 'm_ffn_w_down', 'v_mix_norm_w', 'v_ffn_norm_w', 'v_final_norm_w', 'v_ssd_w_in', 'v_ssd_conv_w', 'v_ssd_conv_b', 'v_ssd_dt_bias', 'v_ssd_a_log', 'v_ssd_d', 'v_ssd_norm_w', 'v_ssd_w_out', 'v_sc_w_in', 'v_sc_conv_w', 'v_sc_w_out', 'v_ffn_w_up', 'v_ffn_conv_w', 'v_ffn_conv_b', 'v_ffn_w_down']
TWIN_OUTPUTS = ['loss', 'grad_x', 'grad_mix_norm_w', 'grad_ffn_norm_w', 'grad_final_norm_w', 'grad_ssd_w_in', 'grad_ssd_conv_w', 'grad_ssd_conv_b', 'grad_ssd_dt_bias', 'grad_ssd_a_log', 'grad_ssd_d', 'grad_ssd_norm_w', 'grad_ssd_w_out', 'grad_sc_w_in', 'grad_sc_conv_w', 'grad_sc_w_out', 'grad_ffn_w_up', 'grad_ffn_conv_w', 'grad_ffn_conv_b', 'grad_ffn_w_down', 'delta_mix_norm_w', 'delta_ffn_norm_w', 'delta_final_norm_w', 'delta_ssd_w_in', 'delta_ssd_conv_w', 'delta_ssd_conv_b', 'delta_ssd_dt_bias', 'delta_ssd_a_log', 'delta_ssd_d', 'delta_ssd_norm_w', 'delta_ssd_w_out', 'delta_sc_w_in', 'delta_sc_conv_w', 'delta_sc_w_out', 'delta_ffn_w_up', 'delta_ffn_conv_w', 'delta_ffn_conv_b', 'delta_ffn_w_down', 'new_m_mix_norm_w', 'new_m_ffn_norm_w', 'new_m_final_norm_w', 'new_m_ssd_w_in', 'new_m_ssd_conv_w', 'new_m_ssd_conv_b', 'new_m_ssd_dt_bias', 'new_m_ssd_a_log', 'new_m_ssd_d', 'new_m_ssd_norm_w', 'new_m_ssd_w_out', 'new_m_sc_w_in', 'new_m_sc_conv_w', 'new_m_sc_w_out', 'new_m_ffn_w_up', 'new_m_ffn_conv_w', 'new_m_ffn_conv_b', 'new_m_ffn_w_down', 'new_v_mix_norm_w', 'new_v_ffn_norm_w', 'new_v_final_norm_w', 'new_v_ssd_w_in', 'new_v_ssd_conv_w', 'new_v_ssd_conv_b', 'new_v_ssd_dt_bias', 'new_v_ssd_a_log', 'new_v_ssd_d', 'new_v_ssd_norm_w', 'new_v_ssd_w_out', 'new_v_sc_w_in', 'new_v_sc_conv_w', 'new_v_sc_w_out', 'new_v_ffn_w_up', 'new_v_ffn_conv_w', 'new_v_ffn_conv_b', 'new_v_ffn_w_down']
TWIN_LEAF_KINDS = {'loss': 'loss', 'grad_x': 'grad_x', 'grad_mix_norm_w': 'grad_w', 'grad_ffn_norm_w': 'grad_w', 'grad_final_norm_w': 'grad_w', 'grad_ssd_w_in': 'grad_w', 'grad_ssd_conv_w': 'grad_w', 'grad_ssd_conv_b': 'grad_w', 'grad_ssd_dt_bias': 'grad_w', 'grad_ssd_a_log': 'grad_w', 'grad_ssd_d': 'grad_w', 'grad_ssd_norm_w': 'grad_w', 'grad_ssd_w_out': 'grad_w', 'grad_sc_w_in': 'grad_w', 'grad_sc_conv_w': 'grad_w', 'grad_sc_w_out': 'grad_w', 'grad_ffn_w_up': 'grad_w', 'grad_ffn_conv_w': 'grad_w', 'grad_ffn_conv_b': 'grad_w', 'grad_ffn_w_down': 'grad_w', 'delta_mix_norm_w': 'delta_w', 'delta_ffn_norm_w': 'delta_w', 'delta_final_norm_w': 'delta_w', 'delta_ssd_w_in': 'delta_w', 'delta_ssd_conv_w': 'delta_w', 'delta_ssd_conv_b': 'delta_w', 'delta_ssd_dt_bias': 'delta_w', 'delta_ssd_a_log': 'delta_w', 'delta_ssd_d': 'delta_w', 'delta_ssd_norm_w': 'delta_w', 'delta_ssd_w_out': 'delta_w', 'delta_sc_w_in': 'delta_w', 'delta_sc_conv_w': 'delta_w', 'delta_sc_w_out': 'delta_w', 'delta_ffn_w_up': 'delta_w', 'delta_ffn_conv_w': 'delta_w', 'delta_ffn_conv_b': 'delta_w', 'delta_ffn_w_down': 'delta_w', 'new_m_mix_norm_w': 'new_m', 'new_m_ffn_norm_w': 'new_m', 'new_m_final_norm_w': 'new_m', 'new_m_ssd_w_in': 'new_m', 'new_m_ssd_conv_w': 'new_m', 'new_m_ssd_conv_b': 'new_m', 'new_m_ssd_dt_bias': 'new_m', 'new_m_ssd_a_log': 'new_m', 'new_m_ssd_d': 'new_m', 'new_m_ssd_norm_w': 'new_m', 'new_m_ssd_w_out': 'new_m', 'new_m_sc_w_in': 'new_m', 'new_m_sc_conv_w': 'new_m', 'new_m_sc_w_out': 'new_m', 'new_m_ffn_w_up': 'new_m', 'new_m_ffn_conv_w': 'new_m', 'new_m_ffn_conv_b': 'new_m', 'new_m_ffn_w_down': 'new_m', 'new_v_mix_norm_w': 'new_v', 'new_v_ffn_norm_w': 'new_v', 'new_v_final_norm_w': 'new_v', 'new_v_ssd_w_in': 'new_v', 'new_v_ssd_conv_w': 'new_v', 'new_v_ssd_conv_b': 'new_v', 'new_v_ssd_dt_bias': 'new_v', 'new_v_ssd_a_log': 'new_v', 'new_v_ssd_d': 'new_v', 'new_v_ssd_norm_w': 'new_v', 'new_v_ssd_w_out': 'new_v', 'new_v_sc_w_in': 'new_v', 'new_v_sc_conv_w': 'new_v', 'new_v_sc_w_out': 'new_v', 'new_v_ffn_w_up': 'new_v', 'new_v_ffn_conv_w': 'new_v', 'new_v_ffn_conv_b': 'new_v', 'new_v_ffn_w_down': 'new_v'}


def _forward(args):
    return _fwd_reference(*[args[k] for k in FWD_PARAMS])


def _output_shape():
    out = _jax.eval_shape(lambda: _forward(_fwd_setup_inputs(0)))
    return out.shape, out.dtype

N_MICROBATCH = 1
ADAM_LR = 0.001
ADAM_B1 = 0.9
ADAM_B2 = 0.999
ADAM_EPS = 1e-08
ADAM_WD = 0.01
ADAM_STEP = 10
PER_EXAMPLE_BATCH_AXIS = {'x': 0, 'loss_target': 0}
SHARED_INPUTS = []
_WEIGHT_DTYPES = {'mix_norm_w': _jnp.float32, 'ffn_norm_w': _jnp.float32, 'final_norm_w': _jnp.float32, 'ssd_w_in': _jnp.float32, 'ssd_conv_w': _jnp.float32, 'ssd_conv_b': _jnp.float32, 'ssd_dt_bias': _jnp.float32, 'ssd_a_log': _jnp.float32, 'ssd_d': _jnp.float32, 'ssd_norm_w': _jnp.float32, 'ssd_w_out': _jnp.float32, 'sc_w_in': _jnp.float32, 'sc_conv_w': _jnp.float32, 'sc_w_out': _jnp.float32, 'ffn_w_up': _jnp.float32, 'ffn_conv_w': _jnp.float32, 'ffn_conv_b': _jnp.float32, 'ffn_w_down': _jnp.float32}
MOMENT_SCALE = {'mix_norm_w': 1.238648e-01, 'ffn_norm_w': 6.102153e-02, 'final_norm_w': 1.599655e+01, 'ssd_w_in': 6.405310e-02, 'ssd_conv_w': 5.894840e-02, 'ssd_conv_b': 7.805347e-02, 'ssd_dt_bias': 1.478109e-01, 'ssd_a_log': 1.458691e-01, 'ssd_d': 3.362747e-01, 'ssd_norm_w': 6.731845e-02, 'ssd_w_out': 9.454595e-02, 'sc_w_in': 5.599725e-02, 'sc_conv_w': 5.694248e-02, 'sc_w_out': 5.590432e-02, 'ffn_w_up': 2.549231e-02, 'ffn_conv_w': 2.538908e-02, 'ffn_conv_b': 2.469777e-02, 'ffn_w_down': 4.158559e-02}


def _to_microbatches(a, axis):
    t = _jnp.moveaxis(a, axis, 0)
    t = t.reshape((N_MICROBATCH, t.shape[0] // N_MICROBATCH) + t.shape[1:])
    return _jnp.moveaxis(t, 1, axis + 1)


def setup_inputs(seed: int = 0) -> dict:
    inp = _fwd_setup_inputs(seed)
    key = _jax.random.fold_in(_jax.random.key(seed), 7919)
    shape, _ = _output_shape()
    out = dict(inp)
    out["loss_target"] = _jax.random.normal(_jax.random.fold_in(key, 0), shape, _jnp.float32)
    for i, name in enumerate(TWIN_WEIGHTS):
        w = inp[name].astype(_jnp.float32)
        if MOMENT_SCALE is None:
            s = _jnp.sqrt(_jnp.mean(_jnp.square(w)) + 1e-30)
        else:
            s = MOMENT_SCALE[name]
        km, kv = _jax.random.split(_jax.random.fold_in(key, i + 1))
        out[name] = w
        out["m_" + name] = s * _jax.random.normal(km, w.shape, _jnp.float32)
        out["v_" + name] = (s * s) * _jax.random.uniform(kv, w.shape, _jnp.float32, 0.5, 1.5)
    if N_MICROBATCH > 1:
        for name, axis in PER_EXAMPLE_BATCH_AXIS.items():
            out[name] = _to_microbatches(out[name], axis)
    return {'x': out['x'], 'mix_norm_w': out['mix_norm_w'], 'ffn_norm_w': out['ffn_norm_w'], 'final_norm_w': out['final_norm_w'], 'ssd_w_in': out['ssd_w_in'], 'ssd_conv_w': out['ssd_conv_w'], 'ssd_conv_b': out['ssd_conv_b'], 'ssd_dt_bias': out['ssd_dt_bias'], 'ssd_a_log': out['ssd_a_log'], 'ssd_d': out['ssd_d'], 'ssd_norm_w': out['ssd_norm_w'], 'ssd_w_out': out['ssd_w_out'], 'sc_w_in': out['sc_w_in'], 'sc_conv_w': out['sc_conv_w'], 'sc_w_out': out['sc_w_out'], 'ffn_w_up': out['ffn_w_up'], 'ffn_conv_w': out['ffn_conv_w'], 'ffn_conv_b': out['ffn_conv_b'], 'ffn_w_down': out['ffn_w_down'], 'loss_target': out['loss_target'], 'm_mix_norm_w': out['m_mix_norm_w'], 'm_ffn_norm_w': out['m_ffn_norm_w'], 'm_final_norm_w': out['m_final_norm_w'], 'm_ssd_w_in': out['m_ssd_w_in'], 'm_ssd_conv_w': out['m_ssd_conv_w'], 'm_ssd_conv_b': out['m_ssd_conv_b'], 'm_ssd_dt_bias': out['m_ssd_dt_bias'], 'm_ssd_a_log': out['m_ssd_a_log'], 'm_ssd_d': out['m_ssd_d'], 'm_ssd_norm_w': out['m_ssd_norm_w'], 'm_ssd_w_out': out['m_ssd_w_out'], 'm_sc_w_in': out['m_sc_w_in'], 'm_sc_conv_w': out['m_sc_conv_w'], 'm_sc_w_out': out['m_sc_w_out'], 'm_ffn_w_up': out['m_ffn_w_up'], 'm_ffn_conv_w': out['m_ffn_conv_w'], 'm_ffn_conv_b': out['m_ffn_conv_b'], 'm_ffn_w_down': out['m_ffn_w_down'], 'v_mix_norm_w': out['v_mix_norm_w'], 'v_ffn_norm_w': out['v_ffn_norm_w'], 'v_final_norm_w': out['v_final_norm_w'], 'v_ssd_w_in': out['v_ssd_w_in'], 'v_ssd_conv_w': out['v_ssd_conv_w'], 'v_ssd_conv_b': out['v_ssd_conv_b'], 'v_ssd_dt_bias': out['v_ssd_dt_bias'], 'v_ssd_a_log': out['v_ssd_a_log'], 'v_ssd_d': out['v_ssd_d'], 'v_ssd_norm_w': out['v_ssd_norm_w'], 'v_ssd_w_out': out['v_ssd_w_out'], 'v_sc_w_in': out['v_sc_w_in'], 'v_sc_conv_w': out['v_sc_conv_w'], 'v_sc_w_out': out['v_sc_w_out'], 'v_ffn_w_up': out['v_ffn_w_up'], 'v_ffn_conv_w': out['v_ffn_conv_w'], 'v_ffn_conv_b': out['v_ffn_conv_b'], 'v_ffn_w_down': out['v_ffn_w_down']}


def _loss(weights, diff, rest, loss_target):
    with _jax.named_scope("forward"):
        args = {**rest, TWIN_DIFF_INPUT: diff, **{k: w.astype(_WEIGHT_DTYPES[k]) for k, w in weights.items()}}
        y = _forward(args)
    with _jax.named_scope("loss_head"):
        err = _jnp.square(y.astype(_jnp.float32) - loss_target)
        return 0.5 * _jnp.sum(_jnp.mean(err, axis=-1)) if err.ndim else 0.5 * err


def _adamw(w, g, m, v):
    m = ADAM_B1 * m + (1.0 - ADAM_B1) * g
    v = ADAM_B2 * v + (1.0 - ADAM_B2) * _jnp.square(g)
    m_hat = m / (1.0 - ADAM_B1 ** ADAM_STEP)
    v_hat = v / (1.0 - ADAM_B2 ** ADAM_STEP)
    delta = -ADAM_LR * (m_hat / (_jnp.sqrt(v_hat) + ADAM_EPS) + ADAM_WD * w)
    return delta, m, v


def reference(x, mix_norm_w, ffn_norm_w, final_norm_w, ssd_w_in, ssd_conv_w, ssd_conv_b, ssd_dt_bias, ssd_a_log, ssd_d, ssd_norm_w, ssd_w_out, sc_w_in, sc_conv_w, sc_w_out, ffn_w_up, ffn_conv_w, ffn_conv_b, ffn_w_down, loss_target, m_mix_norm_w, m_ffn_norm_w, m_final_norm_w, m_ssd_w_in, m_ssd_conv_w, m_ssd_conv_b, m_ssd_dt_bias, m_ssd_a_log, m_ssd_d, m_ssd_norm_w, m_ssd_w_out, m_sc_w_in, m_sc_conv_w, m_sc_w_out, m_ffn_w_up, m_ffn_conv_w, m_ffn_conv_b, m_ffn_w_down, v_mix_norm_w, v_ffn_norm_w, v_final_norm_w, v_ssd_w_in, v_ssd_conv_w, v_ssd_conv_b, v_ssd_dt_bias, v_ssd_a_log, v_ssd_d, v_ssd_norm_w, v_ssd_w_out, v_sc_w_in, v_sc_conv_w, v_sc_w_out, v_ffn_w_up, v_ffn_conv_w, v_ffn_conv_b, v_ffn_w_down):
    given = dict(x=x, mix_norm_w=mix_norm_w, ffn_norm_w=ffn_norm_w, final_norm_w=final_norm_w, ssd_w_in=ssd_w_in, ssd_conv_w=ssd_conv_w, ssd_conv_b=ssd_conv_b, ssd_dt_bias=ssd_dt_bias, ssd_a_log=ssd_a_log, ssd_d=ssd_d, ssd_norm_w=ssd_norm_w, ssd_w_out=ssd_w_out, sc_w_in=sc_w_in, sc_conv_w=sc_conv_w, sc_w_out=sc_w_out, ffn_w_up=ffn_w_up, ffn_conv_w=ffn_conv_w, ffn_conv_b=ffn_conv_b, ffn_w_down=ffn_w_down, loss_target=loss_target, m_mix_norm_w=m_mix_norm_w, m_ffn_norm_w=m_ffn_norm_w, m_final_norm_w=m_final_norm_w, m_ssd_w_in=m_ssd_w_in, m_ssd_conv_w=m_ssd_conv_w, m_ssd_conv_b=m_ssd_conv_b, m_ssd_dt_bias=m_ssd_dt_bias, m_ssd_a_log=m_ssd_a_log, m_ssd_d=m_ssd_d, m_ssd_norm_w=m_ssd_norm_w, m_ssd_w_out=m_ssd_w_out, m_sc_w_in=m_sc_w_in, m_sc_conv_w=m_sc_conv_w, m_sc_w_out=m_sc_w_out, m_ffn_w_up=m_ffn_w_up, m_ffn_conv_w=m_ffn_conv_w, m_ffn_conv_b=m_ffn_conv_b, m_ffn_w_down=m_ffn_w_down, v_mix_norm_w=v_mix_norm_w, v_ffn_norm_w=v_ffn_norm_w, v_final_norm_w=v_final_norm_w, v_ssd_w_in=v_ssd_w_in, v_ssd_conv_w=v_ssd_conv_w, v_ssd_conv_b=v_ssd_conv_b, v_ssd_dt_bias=v_ssd_dt_bias, v_ssd_a_log=v_ssd_a_log, v_ssd_d=v_ssd_d, v_ssd_norm_w=v_ssd_norm_w, v_ssd_w_out=v_ssd_w_out, v_sc_w_in=v_sc_w_in, v_sc_conv_w=v_sc_conv_w, v_sc_w_out=v_sc_w_out, v_ffn_w_up=v_ffn_w_up, v_ffn_conv_w=v_ffn_conv_w, v_ffn_conv_b=v_ffn_conv_b, v_ffn_w_down=v_ffn_w_down)
    weights = {n: given[n] for n in TWIN_WEIGHTS}
    shared = {n: given[n] for n in SHARED_INPUTS}
    per_example = {n: given[n] for n in ['x']}
    grad_fn = _jax.value_and_grad(_loss, argnums=(0, 1))

    def one_microbatch(ex, loss_target):
        ex = dict(ex)
        diff = ex.pop(TWIN_DIFF_INPUT)
        return grad_fn(weights, diff, {**shared, **ex}, loss_target)

    if N_MICROBATCH == 1:
        loss, (grad_w, grad_x) = one_microbatch(per_example, given["loss_target"])
    else:
        def body(carry, xs):
            loss_sum, grad_sum = carry
            l_k, (gw_k, gx_k) = one_microbatch(xs[0], xs[1])
            with _jax.named_scope("update"):
                return (loss_sum + l_k, _jax.tree.map(_jnp.add, grad_sum, gw_k)), gx_k

        init = (_jnp.zeros((), _jnp.float32), _jax.tree.map(_jnp.zeros_like, weights))
        (loss, grad_w), grad_x = _jax.lax.scan(body, init, (per_example, given["loss_target"]))
    with _jax.named_scope("update"):
        delta_w, new_m, new_v = {}, {}, {}
        for n in TWIN_WEIGHTS:
            delta_w[n], new_m[n], new_v[n] = _adamw(weights[n], grad_w[n], given["m_" + n], given["v_" + n])
    return (loss, grad_x, *[grad_w[n] for n in TWIN_WEIGHTS], *[delta_w[n] for n in TWIN_WEIGHTS],
            *[new_m[n] for n in TWIN_WEIGHTS], *[new_v[n] for n in TWIN_WEIGHTS])
```

```python
import functools
import math

import jax
import jax.numpy as jnp
from jax import lax
from jax.experimental import pallas as pl
from jax.experimental.pallas import tpu as pltpu

F32 = jnp.float32
BF16 = jnp.bfloat16
MESH = pl.DeviceIdType.MESH

N_DEV = 8
N_CHIP = 4
EPS = 1e-5
HEAD_DIM = 64
STATE = 128
CHUNK = 128
PAIR = 2 * HEAD_DIM
GROUP_W = 8 * HEAD_DIM
HALO = 16
LANE = 128
VMEM_LIMIT = 56 * 1024 * 1024

ADAM_LR = 0.001
ADAM_B1 = 0.9
ADAM_B2 = 0.999
ADAM_EPS = 1e-08
ADAM_WD = 0.01
ADAM_STEP = 10


def _pick(n, candidates):
    for c in candidates:
        if c <= n and n % c == 0:
            return c
    return n


def _params(*sem):
    return pltpu.CompilerParams(dimension_semantics=sem, vmem_limit_bytes=VMEM_LIMIT)


def _sigmoid(x):
    return 1.0 / (1.0 + jnp.exp(-x))


_DIMS = {
    "nn": (((1,), (0,)), ((), ())),
    "nt": (((1,), (1,)), ((), ())),
    "tn": (((0,), (0,)), ((), ())),
}


def _matmul(mode, a, b, *, grid, a_spec, b_spec, o_spec, out_shape, acc_shape, name, res=None, res_spec=None):
    nk = grid[2]
    dims = _DIMS[mode]

    def body(*refs):
        if res is None:
            a_ref, b_ref, o_ref = refs[:3]
            r_ref, scratch = None, refs[3:]
        else:
            a_ref, b_ref, r_ref, o_ref = refs[:4]
            scratch = refs[4:]
        part = lax.dot_general(a_ref[...], b_ref[...], dims, preferred_element_type=F32)

        def finish(acc):
            if r_ref is not None:
                acc = acc + r_ref[...]
            o_ref[...] = acc.astype(o_ref.dtype)

        if nk == 1:
            finish(part)
        else:
            acc_ref = scratch[0]
            k = pl.program_id(2)

            @pl.when(k == 0)
            def _():
                acc_ref[...] = part

            @pl.when(k > 0)
            def _():
                acc_ref[...] += part

            @pl.when(k == nk - 1)
            def _():
                finish(acc_ref[...])

    in_specs = [a_spec, b_spec] + ([res_spec] if res is not None else [])
    args = (a, b) + ((res,) if res is not None else ())
    return pl.pallas_call(
        body, name=name, grid=grid, in_specs=in_specs, out_specs=o_spec, out_shape=out_shape,
        scratch_shapes=[pltpu.VMEM(acc_shape, F32)] if nk > 1 else [],
        compiler_params=_params("parallel", "parallel", "arbitrary"),
    )(*args)


def _mm_nn(a, b, *, out_dtype, res=None, name):
    m, kd = a.shape
    n = b.shape[1]
    tm = _pick(m, (512, 256, 128))
    tn = _pick(n, (1152, 1024, 512, 384, 256, 128))
    tk = kd if kd <= 2048 else _pick(kd, (1408, 1024, 512, 256, 128))
    grid = (n // tn, m // tm, kd // tk)
    return _matmul(
        "nn", a, b, res=res, grid=grid, name=name,
        a_spec=pl.BlockSpec((tm, tk), lambda j, i, k: (i, k)),
        b_spec=pl.BlockSpec((tk, tn), lambda j, i, k: (k, j)),
        res_spec=pl.BlockSpec((tm, tn), lambda j, i, k: (i, j)),
        o_spec=pl.BlockSpec((tm, tn), lambda j, i, k: (i, j)),
        out_shape=jax.ShapeDtypeStruct((m, n), out_dtype), acc_shape=(tm, tn))


def _mm_nt(a, b, *, out_dtype, name):
    m, kd = a.shape
    n = b.shape[0]
    tm = _pick(m, (512, 256, 128))
    tn = _pick(n, (1408, 1024, 512, 256, 128))
    tk = kd if kd <= 2048 else _pick(kd, (1152, 1024, 512, 384, 256, 128))
    grid = (n // tn, m // tm, kd // tk)
    return _matmul(
        "nt", a, b, grid=grid, name=name,
        a_spec=pl.BlockSpec((tm, tk), lambda j, i, k: (i, k)),
        b_spec=pl.BlockSpec((tn, tk), lambda j, i, k: (j, k)),
        o_spec=pl.BlockSpec((tm, tn), lambda j, i, k: (i, j)),
        out_shape=jax.ShapeDtypeStruct((m, n), out_dtype), acc_shape=(tm, tn))


def _mm_tn(a, b, *, out_dtype, name):
    kd, m = a.shape
    n = b.shape[1]
    tm = _pick(m, (512, 256, 128))
    tn = _pick(n, (1152, 1024, 512, 384, 256, 128))
    tk = _pick(kd, (1024, 512, 256, 128))
    grid = (n // tn, m // tm, kd // tk)
    return _matmul(
        "tn", a, b, grid=grid, name=name,
        a_spec=pl.BlockSpec((tk, tm), lambda j, i, k: (k, i)),
        b_spec=pl.BlockSpec((tk, tn), lambda j, i, k: (k, j)),
        o_spec=pl.BlockSpec((tm, tn), lambda j, i, k: (i, j)),
        out_shape=jax.ShapeDtypeStruct((m, n), out_dtype), acc_shape=(tm, tn))


def _in_tile(nb, c):
    return math.gcd(nb, c)


def _lin_in_fwd(h, wg, parts, *, name):
    t, d = h.shape
    nb = wg.shape[2]
    c = N_DEV * nb // parts
    w = _in_tile(nb, c)
    nbw, cw = nb // w, c // w
    tm = _pick(t, (1024,) if w < 512 else (512, 256, 128))
    grid = (N_DEV * nbw, t // tm, 1)
    return _matmul(
        "nn", h, wg, grid=grid, name=name,
        a_spec=pl.BlockSpec((tm, d), lambda j, i, k: (i, 0)),
        b_spec=pl.BlockSpec((None, d, w), lambda j, i, k: (j // nbw, 0, j % nbw)),
        o_spec=pl.BlockSpec((None, tm, w), lambda j, i, k: (j // cw, i, j % cw)),
        out_shape=jax.ShapeDtypeStruct((parts, t, c), BF16), acc_shape=(tm, w))


def _lin_in_dx(dact, wg, *, name):
    parts, t, c = dact.shape
    d, nb = wg.shape[1], wg.shape[2]
    w = _in_tile(nb, c)
    nbw, cw = nb // w, c // w
    tm = _pick(t, (1024,) if w < 512 else (512, 256, 128))
    tn = _pick(d, (2048,) if w < 512 else (1024, 512, 256, 128))
    grid = (d // tn, t // tm, N_DEV * nbw)
    return _matmul(
        "nt", dact, wg, grid=grid, name=name,
        a_spec=pl.BlockSpec((None, tm, w), lambda j, i, k: (k // cw, i, k % cw)),
        b_spec=pl.BlockSpec((None, tn, w), lambda j, i, k: (k // nbw, j, k % nbw)),
        o_spec=pl.BlockSpec((tm, tn), lambda j, i, k: (i, j)),
        out_shape=jax.ShapeDtypeStruct((t, d), F32), acc_shape=(tm, tn))


def _lin_in_dw(h, dact, nb, *, name):
    t, d = h.shape
    parts, _, c = dact.shape
    w = _in_tile(nb, c)
    nbw, cw = nb // w, c // w
    tm = _pick(d, (1024,) if w < 512 else (512, 256, 128))
    tk = _pick(t, (1024, 512, 256, 128))
    grid = (N_DEV * nbw, d // tm, t // tk)
    return _matmul(
        "tn", h, dact, grid=grid, name=name,
        a_spec=pl.BlockSpec((tk, tm), lambda j, i, k: (k, i)),
        b_spec=pl.BlockSpec((None, tk, w), lambda j, i, k: (j // cw, k, j % cw)),
        o_spec=pl.BlockSpec((None, tm, w), lambda j, i, k: (j // nbw, i, j % nbw)),
        out_shape=jax.ShapeDtypeStruct((N_DEV, d, nb), BF16), acc_shape=(tm, w))


def _fold8(v):
    rows, c = v.shape
    return v.reshape(rows // 8, 8, c).sum(axis=0)


def _accumulate(ref, val, first):
    @pl.when(first)
    def _():
        ref[...] = val

    @pl.when(jnp.logical_not(first))
    def _():
        ref[...] += val


def _cast_layer(w_stack, layer, *, name):
    _, r, c = w_stack.shape
    tr = _pick(r, (256, 128, 64, 32, 16))

    def body(w_ref, o_ref):
        o_ref[...] = w_ref[...].astype(BF16)

    return pl.pallas_call(
        body, name=name, grid=(r // tr,),
        in_specs=[pl.BlockSpec((None, tr, c), lambda i: (layer, i, 0))],
        out_specs=pl.BlockSpec((tr, c), lambda i: (i, 0)),
        out_shape=jax.ShapeDtypeStruct((r, c), BF16),
        compiler_params=_params("parallel"),
    )(w_stack)


def _rmsnorm_fwd(x, w, *, name):
    t, d = x.shape
    tt = _pick(t, (256, 128))

    def body(x_ref, w_ref, o_ref):
        xv = x_ref[...]
        r = lax.rsqrt(jnp.mean(xv * xv, axis=1, keepdims=True) + EPS)
        o_ref[...] = (xv * r * w_ref[...]).astype(BF16)

    return pl.pallas_call(
        body, name=name, grid=(t // tt,),
        in_specs=[pl.BlockSpec((tt, d), lambda i: (i, 0)), pl.BlockSpec((1, d), lambda i: (0, 0))],
        out_specs=pl.BlockSpec((tt, d), lambda i: (i, 0)),
        out_shape=jax.ShapeDtypeStruct((t, d), BF16),
        compiler_params=_params("parallel"),
    )(x, w.reshape(1, d))


def _rmsnorm_bwd(dh, x, w, dres, *, name):
    t, d = x.shape
    tt = _pick(t, (256, 128))

    def body(dh_ref, x_ref, w_ref, dres_ref, dx_ref, dxb_ref, dw_ref):
        xv = x_ref[...]
        r = lax.rsqrt(jnp.mean(xv * xv, axis=1, keepdims=True) + EPS)
        xhat = xv * r
        dhv = dh_ref[...].astype(F32)
        dxhat = dhv * w_ref[...]
        dx = dres_ref[...] + r * (dxhat - xhat * jnp.mean(dxhat * xhat, axis=1, keepdims=True))
        dx_ref[...] = dx
        dxb_ref[...] = dx.astype(BF16)
        _accumulate(dw_ref, _fold8(dhv * xhat), pl.program_id(0) == 0)

    row = pl.BlockSpec((tt, d), lambda i: (i, 0))
    return pl.pallas_call(
        body, name=name, grid=(t // tt,),
        in_specs=[row, row, pl.BlockSpec((1, d), lambda i: (0, 0)), row],
        out_specs=[row, row, pl.BlockSpec((8, d), lambda i: (0, 0))],
        out_shape=[jax.ShapeDtypeStruct((t, d), F32), jax.ShapeDtypeStruct((t, d), BF16),
                   jax.ShapeDtypeStruct((8, d), F32)],
        compiler_params=_params("arbitrary"),
    )(dh, x, w.reshape(1, d), dres)


def _loss_head(x, w, target, *, name):
    t, d = x.shape
    tt = _pick(t, (256, 128))

    def body(x_ref, w_ref, tg_ref, dx_ref, dxb_ref, dw_ref, ls_ref):
        xv = x_ref[...]
        wv = w_ref[...]
        r = lax.rsqrt(jnp.mean(xv * xv, axis=1, keepdims=True) + EPS)
        xhat = xv * r
        err = xhat * wv - tg_ref[...]
        dy = err * (1.0 / d)
        dxhat = dy * wv
        dx = r * (dxhat - xhat * jnp.mean(dxhat * xhat, axis=1, keepdims=True))
        dx_ref[...] = dx
        dxb_ref[...] = dx.astype(BF16)
        first = pl.program_id(0) == 0
        _accumulate(dw_ref, _fold8(dy * xhat), first)
        _accumulate(ls_ref, _fold8(err * err) * (0.5 / d), first)

    row = pl.BlockSpec((tt, d), lambda i: (i, 0))
    acc = pl.BlockSpec((8, d), lambda i: (0, 0))
    return pl.pallas_call(
        body, name=name, grid=(t // tt,),
        in_specs=[row, pl.BlockSpec((1, d), lambda i: (0, 0)), row],
        out_specs=[row, row, acc, acc],
        out_shape=[jax.ShapeDtypeStruct((t, d), F32), jax.ShapeDtypeStruct((t, d), BF16),
                   jax.ShapeDtypeStruct((8, d), F32), jax.ShapeDtypeStruct((8, d), F32)],
        compiler_params=_params("arbitrary"),
    )(x, w.reshape(1, d), target)


def _conv_causal(e, tap, width):
    acc = None
    for k in range(width):
        s = width - 1 - k
        term = (e if s == 0 else pltpu.roll(e, s, 0)) * tap(k)
        acc = term if acc is None else acc + term
    return acc


def _conv_anticausal(e, tap, width):
    rows = e.shape[0]
    acc = None
    for k in range(width):
        s = width - 1 - k
        term = (e if s == 0 else pltpu.roll(e, rows - s, 0)) * tap(k)
        acc = term if acc is None else acc + term
    return acc


def _extend(prev, cur, nxt, first, last):
    parts = []
    if prev is not None:
        parts.append(jnp.where(first, 0.0, prev.astype(F32)))
    parts.append(cur.astype(F32))
    if nxt is not None:
        parts.append(jnp.where(last, 0.0, nxt.astype(F32)))
    return jnp.concatenate(parts, axis=0)


def _prev_idx(i, tt):
    return jnp.maximum(i * (tt // HALO) - 1, 0)


def _next_idx(i, tt, t):
    return jnp.minimum((i + 1) * (tt // HALO), t // HALO - 1)


def _ffn_act_fwd(u3, cw, cb, *, name):
    _, t, f = u3.shape
    tt = _pick(t, (512, 256, 128))
    tc = _pick(f, (512, 256, 128))
    width = cw.shape[0]

    def body(u_ref, up_ref, w_ref, b_ref, o_ref):
        first = pl.program_id(1) == 0
        pre = []
        for p in range(2):
            e = _extend(up_ref[p], u_ref[p], None, first, None)
            pre.append(_conv_causal(e, lambda k: w_ref[k, p:p + 1, :], width)[HALO:] + b_ref[p:p + 1, :])
        g, v = pre
        o_ref[...] = (g * _sigmoid(g) * v).astype(BF16)

    return pl.pallas_call(
        body, name=name, grid=(f // tc, t // tt),
        in_specs=[pl.BlockSpec((2, tt, tc), lambda j, i: (0, i, j)),
                  pl.BlockSpec((2, HALO, tc), lambda j, i: (0, _prev_idx(i, tt), j)),
                  pl.BlockSpec((width, 2, tc), lambda j, i: (0, 0, j)),
                  pl.BlockSpec((2, tc), lambda j, i: (0, j))],
        out_specs=pl.BlockSpec((tt, tc), lambda j, i: (i, j)),
        out_shape=jax.ShapeDtypeStruct((t, f), BF16),
        compiler_params=_params("parallel", "parallel"),
    )(u3, u3, cw, cb)


def _ffn_act_bwd(u3, da, cw, cb, *, name):
    _, t, f = u3.shape
    tt = _pick(t, (512, 256, 128))
    tc = _pick(f, (512, 256, 128))
    width = cw.shape[0]
    nt = t // tt
    ctr = slice(HALO, HALO + tt)

    def body(u_ref, up_ref, un_ref, da_ref, dan_ref, w_ref, b_ref, du_ref, dcw_ref, dcb_ref):
        i = pl.program_id(1)
        first, last = i == 0, i == nt - 1
        ext, pre = [], []
        for p in range(2):
            e = _extend(up_ref[p], u_ref[p], un_ref[p], first, last)
            ext.append(e)
            pre.append(_conv_causal(e, lambda k: w_ref[k, p:p + 1, :], width) + b_ref[p:p + 1, :])
        g, v = pre
        sg = _sigmoid(g)
        dae = _extend(jnp.zeros((HALO, tc), F32), da_ref[...], dan_ref[...], False, last)
        dpre = (dae * v * (sg * (1.0 + g * (1.0 - sg))), dae * (g * sg))

        @pl.when(first)
        def _():
            dcw_ref[...] = jnp.zeros_like(dcw_ref)
            dcb_ref[...] = jnp.zeros_like(dcb_ref)

        for p in range(2):
            du_ref[p] = _conv_anticausal(dpre[p], lambda k: w_ref[k, p:p + 1, :], width)[ctr].astype(BF16)
            dc = dpre[p][ctr]
            dcb_ref[p:p + 1, :] += jnp.sum(dc, axis=0, keepdims=True)
            for k in range(width):
                s = width - 1 - k
                xs = (ext[p] if s == 0 else pltpu.roll(ext[p], s, 0))[ctr]
                dcw_ref[k, p:p + 1, :] += jnp.sum(dc * xs, axis=0, keepdims=True)

    return pl.pallas_call(
        body, name=name, grid=(f // tc, nt),
        in_specs=[pl.BlockSpec((2, tt, tc), lambda j, i: (0, i, j)),
                  pl.BlockSpec((2, HALO, tc), lambda j, i: (0, _prev_idx(i, tt), j)),
                  pl.BlockSpec((2, HALO, tc), lambda j, i: (0, _next_idx(i, tt, t), j)),
                  pl.BlockSpec((tt, tc), lambda j, i: (i, j)),
                  pl.BlockSpec((HALO, tc), lambda j, i: (_next_idx(i, tt, t), j)),
                  pl.BlockSpec((width, 2, tc), lambda j, i: (0, 0, j)),
                  pl.BlockSpec((2, tc), lambda j, i: (0, j))],
        out_specs=[pl.BlockSpec((2, tt, tc), lambda j, i: (0, i, j)),
                   pl.BlockSpec((width, 2, tc), lambda j, i: (0, 0, j)),
                   pl.BlockSpec((2, tc), lambda j, i: (0, j))],
        out_shape=[jax.ShapeDtypeStruct((2, t, f), BF16), jax.ShapeDtypeStruct((width, 2, f), F32),
                   jax.ShapeDtypeStruct((2, f), F32)],
        compiler_params=_params("parallel", "arbitrary"),
    )(u3, u3, u3, da, da, cw, cb)


def _sc_act_fwd(p3, cw, *, name):
    _, t, c = p3.shape
    tt = _pick(t, (512, 256, 128))
    tc = _pick(c, (512, 256, 128))
    width = cw.shape[0]

    def body(p_ref, pp_ref, w_ref, o_ref):
        first = pl.program_id(1) == 0
        q = _extend(pp_ref[1], p_ref[1], None, first, None) * _extend(pp_ref[2], p_ref[2], None, first, None)
        cq = _conv_causal(q, lambda k: w_ref[k:k + 1, :], width)[HALO:]
        o_ref[...] = (p_ref[0].astype(F32) * cq).astype(BF16)

    return pl.pallas_call(
        body, name=name, grid=(c // tc, t // tt),
        in_specs=[pl.BlockSpec((3, tt, tc), lambda j, i: (0, i, j)),
                  pl.BlockSpec((3, HALO, tc), lambda j, i: (0, _prev_idx(i, tt), j)),
                  pl.BlockSpec((width, tc), lambda j, i: (0, j))],
        out_specs=pl.BlockSpec((tt, tc), lambda j, i: (i, j)),
        out_shape=jax.ShapeDtypeStruct((t, c), BF16),
        compiler_params=_params("parallel", "parallel"),
    )(p3, p3, cw)


def _sc_act_bwd(p3, da, cw, *, name):
    _, t, c = p3.shape
    tt = _pick(t, (512, 256, 128))
    tc = _pick(c, (512, 256, 128))
    width = cw.shape[0]
    nt = t // tt
    ctr = slice(HALO, HALO + tt)

    def body(p_ref, pp_ref, pn_ref, da_ref, dan_ref, w_ref, dp_ref, dcw_ref):
        i = pl.program_id(1)
        first, last = i == 0, i == nt - 1
        tap = lambda k: w_ref[k:k + 1, :]
        bg, cg, hh = (_extend(pp_ref[p], p_ref[p], pn_ref[p], first, last) for p in range(3))
        q = cg * hh
        cq = _conv_causal(q, tap, width)
        dae = _extend(jnp.zeros((HALO, tc), F32), da_ref[...], dan_ref[...], False, last)
        dcq = dae * bg
        dq = _conv_anticausal(dcq, tap, width)[ctr]
        dp_ref[0] = (dae * cq)[ctr].astype(BF16)
        dp_ref[1] = (dq * hh[ctr]).astype(BF16)
        dp_ref[2] = (dq * cg[ctr]).astype(BF16)

        @pl.when(first)
        def _():
            dcw_ref[...] = jnp.zeros_like(dcw_ref)

        dc = dcq[ctr]
        for k in range(width):
            s = width - 1 - k
            qs = (q if s == 0 else pltpu.roll(q, s, 0))[ctr]
            dcw_ref[k:k + 1, :] += jnp.sum(dc * qs, axis=0, keepdims=True)

    return pl.pallas_call(
        body, name=name, grid=(c // tc, nt),
        in_specs=[pl.BlockSpec((3, tt, tc), lambda j, i: (0, i, j)),
                  pl.BlockSpec((3, HALO, tc), lambda j, i: (0, _prev_idx(i, tt), j)),
                  pl.BlockSpec((3, HALO, tc), lambda j, i: (0, _next_idx(i, tt, t), j)),
                  pl.BlockSpec((tt, tc), lambda j, i: (i, j)),
                  pl.BlockSpec((HALO, tc), lambda j, i: (_next_idx(i, tt, t), j)),
                  pl.BlockSpec((width, tc), lambda j, i: (0, j))],
        out_specs=[pl.BlockSpec((3, tt, tc), lambda j, i: (0, i, j)),
                   pl.BlockSpec((width, tc), lambda j, i: (0, j))],
        out_shape=[jax.ShapeDtypeStruct((3, t, c), BF16), jax.ShapeDtypeStruct((width, c), F32)],
        compiler_params=_params("parallel", "arbitrary"),
    )(p3, p3, p3, da, da, cw)


def _ssd_conv_fwd(zx, cw, cb, col0, *, name):
    t = zx.shape[0]
    width, c = cw.shape
    tt = _pick(t, (512, 256, 128))
    tc = _pick(math.gcd(c, col0), (512, 256, 128))
    off = col0 // tc

    def body(x_ref, xp_ref, w_ref, b_ref, o_ref):
        first = pl.program_id(1) == 0
        e = _extend(xp_ref[...], x_ref[...], None, first, None)
        pre = _conv_causal(e, lambda k: w_ref[k:k + 1, :], width)[HALO:] + b_ref[...]
        o_ref[...] = (pre * _sigmoid(pre)).astype(BF16)

    return pl.pallas_call(
        body, name=name, grid=(c // tc, t // tt),
        in_specs=[pl.BlockSpec((tt, tc), lambda j, i: (i, off + j)),
                  pl.BlockSpec((HALO, tc), lambda j, i: (_prev_idx(i, tt), off + j)),
                  pl.BlockSpec((width, tc), lambda j, i: (0, j)),
                  pl.BlockSpec((1, tc), lambda j, i: (0, j))],
        out_specs=pl.BlockSpec((tt, tc), lambda j, i: (i, j)),
        out_shape=jax.ShapeDtypeStruct((t, c), BF16),
        compiler_params=_params("parallel", "parallel"),
    )(zx, zx, cw, cb)


def _ssd_conv_bwd(zx, dxc, cw, cb, dzx, col0, woff, *, name):
    t = zx.shape[0]
    width = cw.shape[0]
    c = dxc.shape[1]
    tt = _pick(t, (512, 256, 128))
    tc = _pick(math.gcd(math.gcd(c, col0), woff) if woff else math.gcd(c, col0), (512, 256, 128))
    nt = t // tt
    xoff, wo = (col0 + woff) // tc, woff // tc
    ctr = slice(HALO, HALO + tt)

    def body(x_ref, xp_ref, xn_ref, d_ref, dn_ref, w_ref, b_ref, dzx_in, dzx_ref, dcw_ref, dcb_ref):
        del dzx_in
        i = pl.program_id(1)
        first, last = i == 0, i == nt - 1
        tap = lambda k: w_ref[k:k + 1, :]
        e = _extend(xp_ref[...], x_ref[...], xn_ref[...], first, last)
        pre = _conv_causal(e, tap, width) + b_ref[...]
        sg = _sigmoid(pre)
        de = _extend(jnp.zeros((HALO, tc), F32), d_ref[...], dn_ref[...], False, last)
        dpre = de * (sg * (1.0 + pre * (1.0 - sg)))
        dzx_ref[...] = _conv_anticausal(dpre, tap, width)[ctr].astype(BF16)

        @pl.when(first)
        def _():
            dcw_ref[...] = jnp.zeros_like(dcw_ref)
            dcb_ref[...] = jnp.zeros_like(dcb_ref)

        dc = dpre[ctr]
        dcb_ref[...] += jnp.sum(dc, axis=0, keepdims=True)
        for k in range(width):
            s = width - 1 - k
            xs = (e if s == 0 else pltpu.roll(e, s, 0))[ctr]
            dcw_ref[k:k + 1, :] += jnp.sum(dc * xs, axis=0, keepdims=True)

    return pl.pallas_call(
        body, name=name, grid=(c // tc, nt),
        in_specs=[pl.BlockSpec((tt, tc), lambda j, i: (i, xoff + j)),
                  pl.BlockSpec((HALO, tc), lambda j, i: (_prev_idx(i, tt), xoff + j)),
                  pl.BlockSpec((HALO, tc), lambda j, i: (_next_idx(i, tt, t), xoff + j)),
                  pl.BlockSpec((tt, tc), lambda j, i: (i, j)),
                  pl.BlockSpec((HALO, tc), lambda j, i: (_next_idx(i, tt, t), j)),
                  pl.BlockSpec((width, tc), lambda j, i: (0, wo + j)),
                  pl.BlockSpec((1, tc), lambda j, i: (0, wo + j)),
                  pl.BlockSpec(memory_space=pl.ANY)],
        out_specs=[pl.BlockSpec((tt, tc), lambda j, i: (i, xoff + j)),
                   pl.BlockSpec((width, tc), lambda j, i: (0, j)),
                   pl.BlockSpec((1, tc), lambda j, i: (0, j))],
        out_shape=[jax.ShapeDtypeStruct(dzx.shape, dzx.dtype), jax.ShapeDtypeStruct((width, c), F32),
                   jax.ShapeDtypeStruct((1, c), F32)],
        input_output_aliases={7: 0},
        compiler_params=_params("parallel", "arbitrary"),
    )(zx, zx, zx, dxc, dxc, cw, cb, dzx)


def _ssd_put_ddt(ddt_g, dzx, col, *, name):
    g, t, _ = ddt_g.shape
    tt = _pick(t, (512, 256, 128))

    def body(d_ref, dzx_in, dzx_ref):
        del dzx_in
        dzx_ref[...] = jnp.sum(d_ref[...], axis=0).astype(BF16)

    return pl.pallas_call(
        body, name=name, grid=(t // tt,),
        in_specs=[pl.BlockSpec((g, tt, LANE), lambda i: (0, i, 0)), pl.BlockSpec(memory_space=pl.ANY)],
        out_specs=pl.BlockSpec((tt, LANE), lambda i: (i, col)),
        out_shape=jax.ShapeDtypeStruct(dzx.shape, dzx.dtype),
        input_output_aliases={1: 0},
        compiler_params=_params("parallel"),
    )(ddt_g, dzx)


def _dot(a, b, mode):
    return lax.dot_general(a, b, _DIMS[mode], preferred_element_type=F32)


def _dot_exact(m01, v, mode="nn"):
    hi = v.astype(BF16)
    r1 = v - hi.astype(F32)
    mid = r1.astype(BF16)
    lo = (r1 - mid.astype(F32)).astype(BF16)
    return _dot(m01, hi, mode) + _dot(m01, mid, mode) + _dot(m01, lo, mode)


def _softplus(x):
    return jnp.maximum(x, 0.0) + jnp.log(1.0 + jnp.exp(-jnp.abs(x)))


def _head_vectors(g, dt_raw, bias, alog):
    n = CHUNK
    dt = _softplus(dt_raw + bias)
    a = -jnp.exp(alog)
    tri = (lax.broadcasted_iota(jnp.int32, (n, n), 0) >= lax.broadcasted_iota(jnp.int32, (n, n), 1)).astype(BF16)
    cs = _dot_exact(tri, dt * a)
    return dt, a, cs, cs.T


def _col(v, lane_ids, h):
    return jnp.sum(jnp.where(lane_ids == h, v, 0.0), axis=1, keepdims=True)


def _row(vt, sub_ids, h):
    return jnp.sum(jnp.where(sub_ids == h, vt, 0.0), axis=0, keepdims=True)


def _ssd_specs(di, bc, nc, rev):
    cidx = (lambda c: nc - 1 - c) if rev else (lambda c: c)
    wide = lambda off: pl.BlockSpec((CHUNK, GROUP_W), lambda g, c: (cidx(c), off + g))
    lane = lambda off: pl.BlockSpec((CHUNK, LANE), lambda g, c: (cidx(c), off + g))
    fixed = lambda off: pl.BlockSpec((CHUNK, LANE), lambda g, c: (cidx(c), off))
    vec = pl.BlockSpec((1, LANE), lambda g, c: (0, 0))
    gvec = pl.BlockSpec((1, GROUP_W), lambda g, c: (0, g))
    state = pl.BlockSpec((None, None, 4, PAIR, STATE), lambda g, c: (g, cidx(c), 0, 0, 0))
    return wide, lane, fixed, vec, gvec, state


def _ssd_fwd(xbc, zx, bias, alog, dexp, nw, *, name):
    t = xbc.shape[0]
    di = nw.shape[1]
    bc = (xbc.shape[1] - di) // 2
    ng, nc = di // GROUP_W, t // CHUNK
    wide, lane, fixed, vec, gvec, state = _ssd_specs(di, bc, nc, rev=False)

    def body(xs_ref, b_ref, c_ref, dt_ref, z_ref, bias_ref, alog_ref, dexp_ref, nw_ref,
             yn_ref, y_ref, st_ref, s_scr):
        g, c = pl.program_id(0), pl.program_id(1)

        @pl.when(c == 0)
        def _():
            s_scr[...] = jnp.zeros_like(s_scr)

        n = CHUNK
        dt, a, cs, cst = _head_vectors(g, dt_ref[...].astype(F32), bias_ref[...], alog_ref[...])
        lane_ids = lax.broadcasted_iota(jnp.int32, (n, LANE), 1)
        sub_ids = lax.broadcasted_iota(jnp.int32, (LANE, n), 0)
        causal = lax.broadcasted_iota(jnp.int32, (n, n), 0) >= lax.broadcasted_iota(jnp.int32, (n, n), 1)
        half = lax.broadcasted_iota(jnp.int32, (1, PAIR), 1) < HEAD_DIM
        half_rows = lax.broadcasted_iota(jnp.int32, (PAIR, 1), 0) < HEAD_DIM
        bm, cm = b_ref[...], c_ref[...]
        gm = _dot(cm, bm, "nt")
        x = xs_ref[...].astype(F32)
        ys = []
        for q in range(4):
            h0 = g * 8 + 2 * q
            col = [_col(cs, lane_ids, h0 + e) for e in range(2)]
            row = [_row(cst, sub_ids, h0 + e) for e in range(2)]
            dtc = [_col(dt, lane_ids, h0 + e) for e in range(2)]
            last = [col[e][n - 1:n, :] for e in range(2)]
            xd = x[:, q * PAIR:(q + 1) * PAIR] * jnp.where(half, dtc[0], dtc[1])
            xd_bf = xd.astype(BF16)
            yd = []
            for e in range(2):
                lm = jnp.exp(jnp.where(causal, col[e] - row[e], -1e30))
                yd.append(_dot((gm * lm).astype(BF16), xd_bf, "nn"))
            s = s_scr[q]
            st_ref[q] = s
            ecs = jnp.where(half, jnp.exp(col[0]), jnp.exp(col[1]))
            dte = jnp.where(half, jnp.exp(last[0] - col[0]), jnp.exp(last[1] - col[1]))
            yoff = ecs * _dot(cm, s.astype(BF16), "nt")
            snew = _dot((xd * dte).astype(BF16), bm, "tn")
            s_scr[q] = s * jnp.where(half_rows, jnp.exp(last[0]), jnp.exp(last[1])) + snew
            ys.append(jnp.where(half, yd[0], yd[1]) + yoff)
        y = jnp.concatenate(ys, axis=1) + dexp_ref[...] * x
        y_ref[...] = y.astype(BF16)
        z = z_ref[...].astype(F32)
        yg = y * (z * _sigmoid(z))
        r = lax.rsqrt(jnp.mean(yg * yg, axis=1, keepdims=True) + EPS)
        yn_ref[...] = (yg * r * nw_ref[...]).astype(BF16)

    dtcol = (2 * di + 2 * bc) // LANE
    return pl.pallas_call(
        body, name=name, grid=(ng, nc),
        in_specs=[wide(0), lane(di // LANE), lane((di + bc) // LANE), fixed(dtcol), wide(0),
                  vec, vec, gvec, gvec],
        out_specs=[wide(0), wide(0), state],
        out_shape=[jax.ShapeDtypeStruct((t, di), BF16), jax.ShapeDtypeStruct((t, di), BF16),
                   jax.ShapeDtypeStruct((ng, nc, 4, PAIR, STATE), F32)],
        scratch_shapes=[pltpu.VMEM((4, PAIR, STATE), F32)],
        compiler_params=_params("parallel", "arbitrary"),
    )(xbc, xbc, xbc, zx, zx, bias, alog, dexp, nw)


def _ssd_bwd(dyn, y, xbc, zx, states, bias, alog, dexp, nw, *, name):
    t = xbc.shape[0]
    di = nw.shape[1]
    bc = (xbc.shape[1] - di) // 2
    ng, nc = di // GROUP_W, t // CHUNK
    wide, lane, fixed, vec, gvec, state = _ssd_specs(di, bc, nc, rev=True)
    acc = lambda w: pl.BlockSpec((None, 8, w), lambda g, c: (g, 0, 0))

    def body(dyn_ref, y_ref, z_ref, nw_ref, xs_ref, b_ref, c_ref, dt_ref, bias_ref, alog_ref, dexp_ref, st_ref,
             dz_ref, dxs_ref, db_ref, dc_ref, ddt_ref, small_ref, dnw_ref, ddexp_ref, ds_scr):
        g, c = pl.program_id(0), pl.program_id(1)

        @pl.when(c == 0)
        def _():
            ds_scr[...] = jnp.zeros_like(ds_scr)
            small_ref[...] = jnp.zeros_like(small_ref)
            dnw_ref[...] = jnp.zeros_like(dnw_ref)
            ddexp_ref[...] = jnp.zeros_like(ddexp_ref)

        n = CHUNK
        yv = y_ref[...].astype(F32)
        z = z_ref[...].astype(F32)
        sz = _sigmoid(z)
        silu = z * sz
        yg = yv * silu
        r = lax.rsqrt(jnp.mean(yg * yg, axis=1, keepdims=True) + EPS)
        yhat = yg * r
        dynv = dyn_ref[...].astype(F32)
        dnw_ref[0:1, :] += jnp.sum(dynv * yhat, axis=0, keepdims=True)
        dyhat = dynv * nw_ref[...]
        dyg = r * (dyhat - yhat * jnp.mean(dyhat * yhat, axis=1, keepdims=True))
        dz_ref[...] = (dyg * yv * (sz * (1.0 + z * (1.0 - sz)))).astype(BF16)
        dy = dyg * silu

        dt_in = dt_ref[...].astype(F32) + bias_ref[...]
        dt, a, cs, cst = _head_vectors(g, dt_ref[...].astype(F32), bias_ref[...], alog_ref[...])
        lane_ids = lax.broadcasted_iota(jnp.int32, (n, LANE), 1)
        sub_ids = lax.broadcasted_iota(jnp.int32, (LANE, n), 0)
        ri = lax.broadcasted_iota(jnp.int32, (n, n), 0)
        ci = lax.broadcasted_iota(jnp.int32, (n, n), 1)
        causal, causal_t = ri >= ci, ci >= ri
        is_last = lax.broadcasted_iota(jnp.int32, (n, 1), 0) == n - 1
        half = lax.broadcasted_iota(jnp.int32, (1, PAIR), 1) < HEAD_DIM
        half_rows = lax.broadcasted_iota(jnp.int32, (PAIR, 1), 0) < HEAD_DIM
        bm, cm = b_ref[...], c_ref[...]
        bf = bm.astype(F32)
        gm, gmt = _dot(cm, bm, "nt"), _dot(bm, cm, "nt")
        x = xs_ref[...].astype(F32)
        dexp = dexp_ref[...]

        dg_sum = jnp.zeros((n, n), F32)
        dgt_sum = jnp.zeros((n, n), F32)
        db_off = jnp.zeros((n, STATE), F32)
        dc_off = jnp.zeros((n, STATE), F32)
        dcs_blk = jnp.zeros((n, LANE), F32)
        ddt_blk = jnp.zeros((n, LANE), F32)
        dxs = []
        for q in range(4):
            h0 = g * 8 + 2 * q
            sl = slice(q * PAIR, (q + 1) * PAIR)
            col = [_col(cs, lane_ids, h0 + e) for e in range(2)]
            row = [_row(cst, sub_ids, h0 + e) for e in range(2)]
            dtc = [_col(dt, lane_ids, h0 + e) for e in range(2)]
            last = [col[e][n - 1:n, :] for e in range(2)]
            xp, dyp = x[:, sl], dy[:, sl]
            dtp = jnp.where(half, dtc[0], dtc[1])
            xd = xp * dtp
            xd_bf, dyp_bf = xd.astype(BF16), dyp.astype(BF16)
            ecs = jnp.where(half, jnp.exp(col[0]), jnp.exp(col[1]))
            dte = jnp.where(half, jnp.exp(last[0] - col[0]), jnp.exp(last[1] - col[1]))
            s, ds = st_ref[q], ds_scr[q]
            s_bf, ds_bf = s.astype(BF16), ds.astype(BF16)
            yoff = ecs * _dot(cm, s_bf, "nt")
            edy_bf = (ecs * dyp).astype(BF16)
            dc_off += _dot(edy_bf, s_bf, "nn")
            bds = _dot(bm, ds_bf, "nt")
            sds = s * ds
            zs = []
            for e in range(2):
                msk = half if e == 0 else jnp.logical_not(half)
                msk_rows = half_rows if e == 0 else jnp.logical_not(half_rows)
                lm = jnp.exp(jnp.where(causal, col[e] - row[e], -1e30))
                lmt = jnp.exp(jnp.where(causal_t, row[e] - col[e], -1e30))
                dym_bf = jnp.where(msk, dyp, 0.0).astype(BF16)
                xdm_bf = jnp.where(msk, xd, 0.0).astype(BF16)
                dm = _dot(dym_bf, xd_bf, "nt")
                dmt = _dot(xdm_bf, dyp_bf, "nt")
                m, mt = gm * lm, gmt * lmt
                dcs = jnp.sum(dm * m, axis=1, keepdims=True) - jnp.sum(dmt * mt, axis=1, keepdims=True)
                dg_sum += dm * lm
                dgt_sum += dmt * lmt
                zs.append(_dot(mt.astype(BF16), dyp_bf, "nn"))
                we = _dot(xdm_bf, ds_bf, "nn")
                dte_col = jnp.exp(last[e] - col[e])
                te = dte_col * jnp.sum(we * bf, axis=1, keepdims=True)
                db_off += dte_col * we
                dcs += jnp.sum(jnp.where(msk, dyp * yoff, 0.0), axis=1, keepdims=True) - te
                tail = jnp.exp(last[e]) * jnp.sum(jnp.where(msk_rows, sds, 0.0), keepdims=True) \
                    + jnp.sum(te, keepdims=True)
                dcs += jnp.where(is_last, tail, 0.0)
                dcs_blk += jnp.where(lane_ids == h0 + e, dcs, 0.0)
            dxd = jnp.where(half, zs[0], zs[1]) + dte * bds
            dxs.append(dxd * dtp + dexp[:, sl] * dyp)
            ddexp_ref[0:1, sl] += jnp.sum(dyp * xp, axis=0, keepdims=True)
            rs = dxd * xp
            for e in range(2):
                msk = half if e == 0 else jnp.logical_not(half)
                ddt_blk += jnp.where(lane_ids == h0 + e, jnp.sum(jnp.where(msk, rs, 0.0), axis=1, keepdims=True), 0.0)
            ds_scr[q] = ds * jnp.where(half_rows, jnp.exp(last[0]), jnp.exp(last[1])) + _dot(edy_bf, cm, "tn")

        dxs_ref[...] = jnp.concatenate(dxs, axis=1).astype(BF16)
        dc_ref[...] = (_dot(dg_sum.astype(BF16), bm, "nn") + dc_off).astype(BF16)
        db_ref[...] = (_dot(dgt_sum.astype(BF16), cm, "nn") + db_off).astype(BF16)
        upper = (ri <= ci).astype(BF16)
        dda = _dot_exact(upper, dcs_blk)
        ddt = dda * a + ddt_blk
        small_ref[0:1, :] += jnp.sum(dda * dt, axis=0, keepdims=True) * a
        ddt_raw = ddt * _sigmoid(dt_in)
        small_ref[1:2, :] += jnp.sum(ddt_raw, axis=0, keepdims=True)
        ddt_ref[...] = ddt_raw

    dtcol = (2 * di + 2 * bc) // LANE
    tot = 2 * di + 2 * bc + LANE
    return pl.pallas_call(
        body, name=name, grid=(ng, nc),
        in_specs=[wide(0), wide(0), wide(0), gvec, wide(0), lane(di // LANE), lane((di + bc) // LANE),
                  fixed(dtcol), vec, vec, gvec, state],
        out_specs=[wide(0), wide(0), lane(0), lane(0),
                   pl.BlockSpec((None, CHUNK, LANE), lambda g, c: (g, nc - 1 - c, 0)),
                   acc(LANE), acc(GROUP_W), acc(GROUP_W)],
        out_shape=[jax.ShapeDtypeStruct((t, tot), BF16), jax.ShapeDtypeStruct((t, di), BF16),
                   jax.ShapeDtypeStruct((t, bc), BF16), jax.ShapeDtypeStruct((t, bc), BF16),
                   jax.ShapeDtypeStruct((ng, t, LANE), F32), jax.ShapeDtypeStruct((ng, 8, LANE), F32),
                   jax.ShapeDtypeStruct((ng, 8, GROUP_W), F32), jax.ShapeDtypeStruct((ng, 8, GROUP_W), F32)],
        scratch_shapes=[pltpu.VMEM((4, PAIR, STATE), F32)],
        compiler_params=_params("parallel", "arbitrary"),
    )(dyn, y, zx, nw, xbc, xbc, xbc, zx, bias, alog, dexp, states)


HBM_ANY = pl.BlockSpec(memory_space=pl.ANY)


def _place():
    x, y, c = lax.axis_index("x"), lax.axis_index("y"), lax.axis_index("c")
    chips = [(1 - x, y), (x, 1 - y), (1 - x, 1 - y)]
    return x, y, c, chips


def _all_gather(arrs, *, name):
    n = len(arrs)

    def body(*refs):
        ins, outs = refs[:n], refs[n:2 * n]
        send, recv, loc = refs[2 * n:]
        x, y, c, chips = _place()
        me, sib = (x, y, c), (x, y, 1 - c)

        def blk(a, p):
            return outs[a].at[4 * p[0] + 2 * p[1] + p[2]]

        def cp(a, k, block, to, src=None):
            return pltpu.make_async_remote_copy(
                src_ref=blk(a, block) if src is None else src, dst_ref=blk(a, block),
                send_sem=send.at[a * 7 + k], recv_sem=recv.at[a * 7 + k], device_id=to, device_id_type=MESH)

        mine = [pltpu.make_async_copy(ins[a], blk(a, me), loc.at[a]) for a in range(n)]
        for m in mine:
            m.start()
        started = []
        for a in range(n):
            started.append(cp(a, 0, me, sib, src=ins[a]))
            started += [cp(a, 1 + j, me, (*chip, c), src=ins[a]) for j, chip in enumerate(chips)]
        for s in started:
            s.start()
        for j, chip in enumerate(chips):
            for a in range(n):
                cp(a, 1 + j, (*chip, c), me).wait_recv()
                fwd = cp(a, 4 + j, (*chip, c), sib)
                fwd.start()
                started.append(fwd)
        for a in range(n):
            cp(a, 0, sib, me).wait_recv()
            for j, chip in enumerate(chips):
                cp(a, 4 + j, (*chip, 1 - c), me).wait_recv()
        for s in started:
            s.wait_send()
        for m in mine:
            m.wait()

    return pl.pallas_call(
        body, name=name,
        in_specs=[HBM_ANY] * n, out_specs=[HBM_ANY] * n,
        out_shape=[jax.ShapeDtypeStruct((N_DEV,) + a.shape, a.dtype) for a in arrs],
        scratch_shapes=[pltpu.SemaphoreType.DMA((7 * n,)), pltpu.SemaphoreType.DMA((7 * n,)),
                        pltpu.SemaphoreType.DMA((n,))],
    )(*arrs)


def _pair_exchange(grads, *, name):
    n = len(grads)

    def body(*refs):
        ins, owns, gots = refs[:n], refs[n:2 * n], refs[2 * n:3 * n]
        send, recv, loc = refs[3 * n:]
        x, y, c, _ = _place()
        sib = (x, y, 1 - c)
        remote, local = [], []
        for a in range(n):
            for k in range(N_CHIP):
                remote.append(pltpu.make_async_remote_copy(
                    src_ref=ins[a].at[2 * k + 1 - c], dst_ref=gots[a].at[k],
                    send_sem=send.at[a * N_CHIP + k], recv_sem=recv.at[a * N_CHIP + k],
                    device_id=sib, device_id_type=MESH))
                local.append(pltpu.make_async_copy(ins[a].at[2 * k + c], owns[a].at[k], loc.at[a * N_CHIP + k]))
        for cpy in remote + local:
            cpy.start()
        for cpy in remote + local:
            cpy.wait()

    shp = lambda g: jax.ShapeDtypeStruct((N_CHIP,) + g.shape[1:], g.dtype)
    outs = pl.pallas_call(
        body, name=name,
        in_specs=[HBM_ANY] * n, out_specs=[HBM_ANY] * (2 * n),
        out_shape=[shp(g) for g in grads] * 2,
        scratch_shapes=[pltpu.SemaphoreType.DMA((N_CHIP * n,)), pltpu.SemaphoreType.DMA((N_CHIP * n,)),
                        pltpu.SemaphoreType.DMA((N_CHIP * n,))],
    )(*grads)
    return outs[:n], outs[n:]


def _chip_exchange(sums, *, name):
    n = len(sums)

    def body(*refs):
        ins, outs = refs[:n], refs[n:2 * n]
        send, recv, loc = refs[2 * n:]
        x, y, c, chips = _place()
        remote, local = [], []
        for a in range(n):
            for j, chip in enumerate(chips):
                remote.append(pltpu.make_async_remote_copy(
                    src_ref=ins[a].at[2 * chip[0] + chip[1]], dst_ref=outs[a].at[j],
                    send_sem=send.at[a * 3 + j], recv_sem=recv.at[a * 3 + j],
                    device_id=(*chip, c), device_id_type=MESH))
            local.append(pltpu.make_async_copy(ins[a].at[2 * x + y], outs[a].at[3], loc.at[a]))
        for cpy in remote + local:
            cpy.start()
        for cpy in remote + local:
            cpy.wait()

    return pl.pallas_call(
        body, name=name,
        in_specs=[HBM_ANY] * n, out_specs=[HBM_ANY] * n,
        out_shape=[jax.ShapeDtypeStruct(s.shape, s.dtype) for s in sums],
        scratch_shapes=[pltpu.SemaphoreType.DMA((3 * n,)), pltpu.SemaphoreType.DMA((3 * n,)),
                        pltpu.SemaphoreType.DMA((n,))],
    )(*sums)


def _add_bf16(a, b, *, name):
    k, r, c = a.shape
    tr = _pick(r, (256, 128, 64, 32, 16))

    def body(a_ref, b_ref, o_ref):
        o_ref[...] = (a_ref[...].astype(F32) + b_ref[...].astype(F32)).astype(BF16)

    spec = pl.BlockSpec((None, tr, c), lambda q, i: (q, i, 0))
    return pl.pallas_call(
        body, name=name, grid=(k, r // tr), in_specs=[spec, spec], out_specs=spec,
        out_shape=jax.ShapeDtypeStruct(a.shape, BF16), compiler_params=_params("parallel", "parallel"),
    )(a, b)


def _all_reduce_small(v, *, name):
    r = v.shape[0]

    def body(v_ref, o_ref, buf, send, recv):
        x, y, c, _ = _place()
        me = 4 * x + 2 * y + c
        buf[me] = v_ref[...]
        copies = []
        for rel in range(1, N_DEV):
            fx, fy, fc = rel >> 2 & 1, rel >> 1 & 1, rel & 1
            peer = ((1 - x) if fx else x, (1 - y) if fy else y, (1 - c) if fc else c)
            copies.append(pltpu.make_async_remote_copy(
                src_ref=v_ref, dst_ref=buf.at[me], send_sem=send.at[rel - 1], recv_sem=recv.at[rel - 1],
                device_id=peer, device_id_type=MESH))
        for cpy in copies:
            cpy.start()
        for cpy in copies:
            cpy.wait()
        acc = buf[0]
        for d in range(1, N_DEV):
            acc = acc + buf[d]
        o_ref[...] = acc

    return pl.pallas_call(
        body, name=name,
        in_specs=[pl.BlockSpec(memory_space=pltpu.VMEM)], out_specs=pl.BlockSpec(memory_space=pltpu.VMEM),
        out_shape=jax.ShapeDtypeStruct(v.shape, F32),
        scratch_shapes=[pltpu.VMEM((N_DEV, r, LANE), F32), pltpu.SemaphoreType.DMA((N_DEV - 1,)),
                        pltpu.SemaphoreType.DMA((N_DEV - 1,))],
        compiler_params=pltpu.CompilerParams(vmem_limit_bytes=VMEM_LIMIT),
    )(v)


def _adamw_math(w, g, m, v):
    m = ADAM_B1 * m + (1.0 - ADAM_B1) * g
    v = ADAM_B2 * v + (1.0 - ADAM_B2) * (g * g)
    m_hat = m / (1.0 - ADAM_B1 ** ADAM_STEP)
    v_hat = v / (1.0 - ADAM_B2 ** ADAM_STEP)
    delta = -ADAM_LR * (m_hat / (jnp.sqrt(v_hat) + ADAM_EPS) + ADAM_WD * w)
    return delta, m, v


def _adamw_layer(w, m, v, parts, layer, prev, *, name):
    nl, r, c = w.shape
    tr = _pick(r, (256, 128, 64, 32, 16))

    def body(w_ref, m_ref, v_ref, p_ref, *rest):
        g_ref, d_ref, mo_ref, vo_ref = rest[-4:]
        g = p_ref[0].astype(F32)
        for k in range(1, N_CHIP):
            g = g + p_ref[k].astype(F32)
        delta, mn, vn = _adamw_math(w_ref[...], g, m_ref[...], v_ref[...])
        g_ref[...] = g
        d_ref[...] = delta
        mo_ref[...] = mn
        vo_ref[...] = vn

    lay = pl.BlockSpec((None, tr, c), lambda i: (layer, i, 0))
    ins = [w, m, v, parts] + (list(prev) if prev is not None else [])
    in_specs = [lay, lay, lay, pl.BlockSpec((N_CHIP, tr, c), lambda i: (0, i, 0))]
    in_specs += [HBM_ANY] * (4 if prev is not None else 0)
    return pl.pallas_call(
        body, name=name, grid=(r // tr,), in_specs=in_specs, out_specs=[lay] * 4,
        out_shape=[jax.ShapeDtypeStruct(w.shape, F32)] * 4,
        input_output_aliases={4 + q: q for q in range(4)} if prev is not None else {},
        compiler_params=_params("parallel"),
    )(*ins)


def _adamw_small(w, g, m, v, *, name):
    def body(w_ref, g_ref, m_ref, v_ref, d_ref, mo_ref, vo_ref):
        d_ref[...], mo_ref[...], vo_ref[...] = _adamw_math(w_ref[...], g_ref[...], m_ref[...], v_ref[...])

    vm = pl.BlockSpec(memory_space=pltpu.VMEM)
    return pl.pallas_call(
        body, name=name, in_specs=[vm] * 4, out_specs=[vm] * 3,
        out_shape=[jax.ShapeDtypeStruct(w.shape, F32)] * 3,
        compiler_params=pltpu.CompilerParams(vmem_limit_bytes=VMEM_LIMIT),
    )(w, g, m, v)


def _pack(arrs):
    flat = jnp.concatenate([a.reshape(-1).astype(F32) for a in arrs])
    pad = (-flat.shape[0]) % (8 * LANE)
    return jnp.pad(flat, (0, pad)).reshape(-1, LANE)


def _unpack(packed, shapes):
    flat = packed.reshape(-1)
    out, off = [], 0
    for s in shapes:
        size = math.prod(s)
        out.append(flat[off:off + size].reshape(s))
        off += size
    return out


WEIGHTS = ['mix_norm_w', 'ffn_norm_w', 'final_norm_w', 'ssd_w_in', 'ssd_conv_w', 'ssd_conv_b', 'ssd_dt_bias',
           'ssd_a_log', 'ssd_d', 'ssd_norm_w', 'ssd_w_out', 'sc_w_in', 'sc_conv_w', 'sc_w_out', 'ffn_w_up',
           'ffn_conv_w', 'ffn_conv_b', 'ffn_w_down']
BIG = ('ssd_w_in', 'ssd_w_out', 'sc_w_in', 'sc_w_out', 'ffn_w_up', 'ffn_w_down')
SHARDED_SMALL = ('ssd_conv_w', 'sc_conv_w', 'ffn_conv_w')


def _lane_pad(v):
    return jnp.pad(v.astype(F32), (0, LANE - v.shape[0])).reshape(1, LANE)


def _gather_cols(g):
    return jnp.moveaxis(g, 0, -2).reshape(g.shape[1:-1] + (N_DEV * g.shape[-1],))


def _reduce_scatter(grads, tag):
    own, got = _pair_exchange(grads, name=f"rs_pair_{tag}")
    sums = [_add_bf16(o, g, name=f"rs_add_{tag}{a}") for a, (o, g) in enumerate(zip(own, got))]
    return _chip_exchange(sums, name=f"rs_chip_{tag}")


def kernel(x, mix_norm_w, ffn_norm_w, final_norm_w, ssd_w_in, ssd_conv_w, ssd_conv_b, ssd_dt_bias, ssd_a_log, ssd_d, ssd_norm_w, ssd_w_out, sc_w_in, sc_conv_w, sc_w_out, ffn_w_up, ffn_conv_w, ffn_conv_b, ffn_w_down, loss_target, m_mix_norm_w, m_ffn_norm_w, m_final_norm_w, m_ssd_w_in, m_ssd_conv_w, m_ssd_conv_b, m_ssd_dt_bias, m_ssd_a_log, m_ssd_d, m_ssd_norm_w, m_ssd_w_out, m_sc_w_in, m_sc_conv_w, m_sc_w_out, m_ffn_w_up, m_ffn_conv_w, m_ffn_conv_b, m_ffn_w_down, v_mix_norm_w, v_ffn_norm_w, v_final_norm_w, v_ssd_w_in, v_ssd_conv_w, v_ssd_conv_b, v_ssd_dt_bias, v_ssd_a_log, v_ssd_d, v_ssd_norm_w, v_ssd_w_out, v_sc_w_in, v_sc_conv_w, v_sc_w_out, v_ffn_w_up, v_ffn_conv_w, v_ffn_conv_b, v_ffn_w_down):
    args = locals()
    wt = {n: args[n] for n in WEIGHTS}
    mom = {n: args["m_" + n] for n in WEIGHTS}
    var = {n: args["v_" + n] for n in WEIGHTS}

    t, d = x.shape[-2], x.shape[-1]
    cur = x.reshape(t, d)
    target = loss_target.reshape(t, d)
    depth = mix_norm_w.shape[0]
    n_ssd, n_sc = ssd_w_in.shape[0], sc_w_in.shape[0]
    heads = ssd_dt_bias.shape[1]
    di = ssd_norm_w.shape[1]
    conv_dim = ssd_conv_b.shape[1]
    bc = (conv_dim - di) // 2
    in_dim = N_DEV * ssd_w_in.shape[2]
    in_pad = di + conv_dim + LANE
    ff = ffn_w_down.shape[1] * N_DEV
    me = 4 * lax.axis_index("x") + 2 * lax.axis_index("y") + lax.axis_index("c")

    full = {n: [] for n in BIG}
    for j in range(n_ssd):
        shards = [_cast_layer(ssd_w_in, j, name=f"cast_ssd_in{j}"), _cast_layer(ssd_w_out, j, name=f"cast_ssd_out{j}")]
        g_in, g_out = _all_gather(shards, name=f"ag_ssd{j}")
        w_in = jnp.swapaxes(g_in, 0, 1).reshape(d, in_dim)
        full['ssd_w_in'].append(jnp.pad(w_in, ((0, 0), (0, in_pad - in_dim))))
        full['ssd_w_out'].append(g_out.reshape(di, d))
    for j in range(n_sc):
        shards = [_cast_layer(sc_w_in, j, name=f"cast_sc_in{j}"), _cast_layer(sc_w_out, j, name=f"cast_sc_out{j}")]
        g_in, g_out = _all_gather(shards, name=f"ag_sc{j}")
        full['sc_w_in'].append(g_in)
        full['sc_w_out'].append(g_out.reshape(-1, d))
    for i in range(depth):
        shards = [_cast_layer(ffn_w_up, i, name=f"cast_ffn_up{i}"), _cast_layer(ffn_w_down, i, name=f"cast_ffn_down{i}")]
        g_in, g_out = _all_gather(shards, name=f"ag_ffn{i}")
        full['ffn_w_up'].append(g_in)
        full['ffn_w_down'].append(g_out.reshape(ff, d))
    conv_full = [_gather_cols(g) for g in _all_gather([wt[n] for n in SHARDED_SMALL], name="ag_conv")]
    ssd_cw, sc_cw, ffn_cw = conv_full
    ffn_cw = ffn_cw.reshape(depth, ffn_cw.shape[1], 2, ff)
    ffn_cb = ffn_conv_b.reshape(depth, 2, ff)
    dexp = jnp.repeat(ssd_d.astype(F32), HEAD_DIM, axis=1)

    saved = []
    for i in range(depth):
        j = i // 2
        rec = {"x_mix": cur}
        h = _rmsnorm_fwd(cur, mix_norm_w[i], name=f"norm_mix{i}")
        rec["h_mix"] = h
        if i % 2 == 0:
            zx = _mm_nn(h, full['ssd_w_in'][j], out_dtype=BF16, name=f"ssd_in{j}")
            xbc = _ssd_conv_fwd(zx, ssd_cw[j], ssd_conv_b[j].reshape(1, conv_dim), di, name=f"ssd_conv{j}")
            ssd_vecs = (_lane_pad(ssd_dt_bias[j]), _lane_pad(ssd_a_log[j]), dexp[j].reshape(1, di),
                        ssd_norm_w[j].reshape(1, di))
            yn, y, states = _ssd_fwd(xbc, zx, *ssd_vecs, name=f"ssd_core{j}")
            cur = _mm_nn(yn, full['ssd_w_out'][j], res=cur, out_dtype=F32, name=f"ssd_out{j}")
            rec.update(zx=zx, xbc=xbc, yn=yn, y=y, states=states, vecs=ssd_vecs)
        else:
            p3 = _lin_in_fwd(h, full['sc_w_in'][j], 3, name=f"sc_in{j}")
            act = _sc_act_fwd(p3, sc_cw[j], name=f"sc_act{j}")
            cur = _mm_nn(act, full['sc_w_out'][j], res=cur, out_dtype=F32, name=f"sc_out{j}")
            rec.update(p3=p3, act=act)
        rec["x_ffn"] = cur
        h = _rmsnorm_fwd(cur, ffn_norm_w[i], name=f"norm_ffn{i}")
        u3 = _lin_in_fwd(h, full['ffn_w_up'][i], 2, name=f"ffn_up{i}")
        act = _ffn_act_fwd(u3, ffn_cw[i], ffn_cb[i], name=f"ffn_act{i}")
        cur = _mm_nn(act, full['ffn_w_down'][i], res=cur, out_dtype=F32, name=f"ffn_down{i}")
        rec.update(h_ffn=h, u3=u3, ffn_act=act)
        saved.append(rec)

    dx, dxb, dw_final, loss8 = _loss_head(cur, final_norm_w, target, name="loss_head")

    small = {n: [None] * wt[n].shape[0] for n in WEIGHTS if n not in BIG and n != 'final_norm_w'}
    parts = {n: [None] * wt[n].shape[0] for n in BIG}
    for i in reversed(range(depth)):
        j = i // 2
        rec = saved[i]
        nb_up = ffn_w_up.shape[2]
        da = _mm_nt(dxb, full['ffn_w_down'][i], out_dtype=BF16, name=f"ffn_down_dx{i}")
        g_down = _mm_tn(rec["ffn_act"], dxb, out_dtype=BF16, name=f"ffn_down_dw{i}")
        du3, dcw, dcb = _ffn_act_bwd(rec["u3"], da, ffn_cw[i], ffn_cb[i], name=f"ffn_act_bwd{i}")
        dh = _lin_in_dx(du3, full['ffn_w_up'][i], name=f"ffn_up_dx{i}")
        g_up = _lin_in_dw(rec["h_ffn"], du3, nb_up, name=f"ffn_up_dw{i}")
        dx, dxb, dwn = _rmsnorm_bwd(dh, rec["x_ffn"], ffn_norm_w[i], dx, name=f"norm_ffn_bwd{i}")
        small['ffn_conv_w'][i] = dcw.reshape(dcw.shape[0], 2 * ff)
        small['ffn_conv_b'][i] = dcb.reshape(2 * ff)
        small['ffn_norm_w'][i] = dwn.sum(axis=0)
        parts['ffn_w_up'][i], parts['ffn_w_down'][i] = _reduce_scatter(
            [g_up, g_down.reshape(N_DEV, ff // N_DEV, d)], f"ffn{i}")

        if i % 2 == 0:
            zx, xbc = rec["zx"], rec["xbc"]
            cw, cb = ssd_cw[j], ssd_conv_b[j].reshape(1, conv_dim)
            dyn = _mm_nt(dxb, full['ssd_w_out'][j], out_dtype=BF16, name=f"ssd_out_dx{j}")
            g_out = _mm_tn(rec["yn"], dxb, out_dtype=BF16, name=f"ssd_out_dw{j}")
            dzx, dxs, db, dc, ddt_g, vec_acc, dnw, ddexp = _ssd_bwd(
                dyn, rec["y"], xbc, zx, rec["states"], *rec["vecs"], name=f"ssd_core_bwd{j}")
            dzx, dcw_x, dcb_x = _ssd_conv_bwd(zx, dxs, cw, cb, dzx, di, 0, name=f"ssd_conv_bwd_x{j}")
            dzx, dcw_b, dcb_b = _ssd_conv_bwd(zx, db, cw, cb, dzx, di, di, name=f"ssd_conv_bwd_b{j}")
            dzx, dcw_c, dcb_c = _ssd_conv_bwd(zx, dc, cw, cb, dzx, di, di + bc, name=f"ssd_conv_bwd_c{j}")
            dzx = _ssd_put_ddt(ddt_g, dzx, (di + conv_dim) // LANE, name=f"ssd_put_ddt{j}")
            dh = _mm_nt(dzx, full['ssd_w_in'][j], out_dtype=F32, name=f"ssd_in_dx{j}")
            g_in = _mm_tn(rec["h_mix"], dzx, out_dtype=BF16, name=f"ssd_in_dw{j}")
            g_in = jnp.swapaxes(g_in[:, :in_dim].reshape(d, N_DEV, in_dim // N_DEV), 0, 1)
            small['ssd_conv_w'][j] = jnp.concatenate([dcw_x, dcw_b, dcw_c], axis=1)
            small['ssd_conv_b'][j] = jnp.concatenate([dcb_x, dcb_b, dcb_c], axis=1).reshape(conv_dim)
            small['ssd_a_log'][j] = vec_acc[:, 0, :heads].sum(axis=0)
            small['ssd_dt_bias'][j] = vec_acc[:, 1, :heads].sum(axis=0)
            small['ssd_d'][j] = ddexp[:, 0, :].reshape(heads, HEAD_DIM).sum(axis=1)
            small['ssd_norm_w'][j] = dnw[:, 0, :].reshape(di)
            parts['ssd_w_in'][j], parts['ssd_w_out'][j] = _reduce_scatter(
                [g_in, g_out.reshape(N_DEV, di // N_DEV, d)], f"ssd{j}")
        else:
            nb_in = sc_w_in.shape[2]
            da = _mm_nt(dxb, full['sc_w_out'][j], out_dtype=BF16, name=f"sc_out_dx{j}")
            g_out = _mm_tn(rec["act"], dxb, out_dtype=BF16, name=f"sc_out_dw{j}")
            dp3, dcw = _sc_act_bwd(rec["p3"], da, sc_cw[j], name=f"sc_act_bwd{j}")
            dh = _lin_in_dx(dp3, full['sc_w_in'][j], name=f"sc_in_dx{j}")
            g_in = _lin_in_dw(rec["h_mix"], dp3, nb_in, name=f"sc_in_dw{j}")
            small['sc_conv_w'][j] = dcw
            parts['sc_w_in'][j], parts['sc_w_out'][j] = _reduce_scatter(
                [g_in, g_out.reshape(N_DEV, g_out.shape[0] // N_DEV, d)], f"sc{j}")
        dx, dxb, dwn = _rmsnorm_bwd(dh, rec["x_mix"], mix_norm_w[i], dx, name=f"norm_mix_bwd{i}")
        small['mix_norm_w'][i] = dwn.sum(axis=0)

    small_names = [n for n in WEIGHTS if n not in BIG]
    partial = {n: jnp.stack(small[n]) for n in small}
    partial['final_norm_w'] = dw_final.sum(axis=0)
    full_shapes = [partial[n].shape for n in small_names]
    packed = _pack([loss8.sum().reshape(1)] + [partial[n] for n in small_names])
    total = _unpack(_all_reduce_small(packed, name="ar_small"), [(1,)] + full_shapes)
    loss = total[0].reshape(())
    grads = dict(zip(small_names, total[1:]))
    for n in SHARDED_SMALL:
        nb = wt[n].shape[-1]
        grads[n] = lax.dynamic_slice_in_dim(grads[n], me * nb, nb, axis=grads[n].ndim - 1)

    delta, new_m, new_v = {}, {}, {}
    shapes = [wt[n].shape for n in small_names]
    outs = _adamw_small(*[_pack([src[n] for n in small_names]) for src in (wt, grads, mom, var)], name="adamw_small")
    for dst, packed_out in zip((delta, new_m, new_v), outs):
        dst.update(zip(small_names, _unpack(packed_out, shapes)))
    for n in BIG:
        prev = None
        for layer in range(wt[n].shape[0]):
            prev = _adamw_layer(wt[n], mom[n], var[n], parts[n][layer], layer, prev, name=f"adamw_{n}{layer}")
        grads[n], delta[n], new_m[n], new_v[n] = prev

    return (loss, dx.reshape(x.shape), *[grads[n] for n in WEIGHTS], *[delta[n] for n in WEIGHTS],
            *[new_m[n] for n in WEIGHTS], *[new_v[n] for n in WEIGHTS])
```

```python
import functools
import math

import jax
import jax.numpy as jnp
from jax import lax
from jax.experimental import pallas as pl
from jax.experimental.pallas import tpu as pltpu

F32 = jnp.float32
BF16 = jnp.bfloat16
MESH = pl.DeviceIdType.MESH

N_DEV = 8
N_CHIP = 4
EPS = 1e-5
HEAD_DIM = 64
STATE = 128
CHUNK = 128
PAIR = 2 * HEAD_DIM
GROUP_W = 8 * HEAD_DIM
HALO = 16
LANE = 128
VMEM_LIMIT = 56 * 1024 * 1024

ADAM_LR = 0.001
ADAM_B1 = 0.9
ADAM_B2 = 0.999
ADAM_EPS = 1e-08
ADAM_WD = 0.01
ADAM_STEP = 10


def _pick(n, candidates):
    for c in candidates:
        if c <= n and n % c == 0:
            return c
    return n


def _params(*sem):
    return pltpu.CompilerParams(dimension_semantics=sem, vmem_limit_bytes=VMEM_LIMIT)


def _sigmoid(x):
    return 0.5 * jnp.tanh(0.5 * x) + 0.5


_DIMS = {
    "nn": (((1,), (0,)), ((), ())),
    "nt": (((1,), (1,)), ((), ())),
    "tn": (((0,), (0,)), ((), ())),
}


def _matmul(mode, a, b, *, grid, a_spec, b_spec, o_spec, out_shape, acc_shape, name, res=None, res_spec=None):
    nk = grid[2]
    dims = _DIMS[mode]

    def body(*refs):
        if res is None:
            a_ref, b_ref, o_ref = refs[:3]
            r_ref, scratch = None, refs[3:]
        else:
            a_ref, b_ref, r_ref, o_ref = refs[:4]
            scratch = refs[4:]
        part = lax.dot_general(a_ref[...], b_ref[...], dims, preferred_element_type=F32)

        def finish(acc):
            if r_ref is not None:
                acc = acc + r_ref[...]
            o_ref[...] = acc.astype(o_ref.dtype)

        if nk == 1:
            finish(part)
        else:
            acc_ref = scratch[0]
            k = pl.program_id(2)

            @pl.when(k == 0)
            def _():
                acc_ref[...] = part

            @pl.when(k > 0)
            def _():
                acc_ref[...] += part

            @pl.when(k == nk - 1)
            def _():
                finish(acc_ref[...])

    in_specs = [a_spec, b_spec] + ([res_spec] if res is not None else [])
    args = (a, b) + ((res,) if res is not None else ())
    return pl.pallas_call(
        body, name=name, grid=grid, in_specs=in_specs, out_specs=o_spec, out_shape=out_shape,
        scratch_shapes=[pltpu.VMEM(acc_shape, F32)] if nk > 1 else [],
        compiler_params=_params("parallel", "parallel", "arbitrary"),
    )(*args)


def _mm_nn(a, b, *, out_dtype, res=None, name):
    m, kd = a.shape
    n = b.shape[1]
    tm = _pick(m, (512, 256, 128))
    tn = _pick(n, (1152, 1024, 512, 384, 256, 128))
    tk = kd if kd <= 2048 else _pick(kd, (1408, 1024, 512, 256, 128))
    grid = (n // tn, m // tm, kd // tk)
    return _matmul(
        "nn", a, b, res=res, grid=grid, name=name,
        a_spec=pl.BlockSpec((tm, tk), lambda j, i, k: (i, k)),
        b_spec=pl.BlockSpec((tk, tn), lambda j, i, k: (k, j)),
        res_spec=pl.BlockSpec((tm, tn), lambda j, i, k: (i, j)),
        o_spec=pl.BlockSpec((tm, tn), lambda j, i, k: (i, j)),
        out_shape=jax.ShapeDtypeStruct((m, n), out_dtype), acc_shape=(tm, tn))


def _mm_nt(a, b, *, out_dtype, name):
    m, kd = a.shape
    n = b.shape[0]
    tm = _pick(m, (512, 256, 128))
    tn = _pick(n, (1408, 1024, 512, 256, 128))
    tk = kd if kd <= 2048 else _pick(kd, (1152, 1024, 512, 384, 256, 128))
    grid = (n // tn, m // tm, kd // tk)
    return _matmul(
        "nt", a, b, grid=grid, name=name,
        a_spec=pl.BlockSpec((tm, tk), lambda j, i, k: (i, k)),
        b_spec=pl.BlockSpec((tn, tk), lambda j, i, k: (j, k)),
        o_spec=pl.BlockSpec((tm, tn), lambda j, i, k: (i, j)),
        out_shape=jax.ShapeDtypeStruct((m, n), out_dtype), acc_shape=(tm, tn))


def _mm_tn(a, b, *, out_dtype, name):
    kd, m = a.shape
    n = b.shape[1]
    tm = _pick(m, (512, 256, 128))
    tn = _pick(n, (1152, 1024, 512, 384, 256, 128))
    tk = _pick(kd, (1024, 512, 256, 128))
    grid = (n // tn, m // tm, kd // tk)
    return _matmul(
        "tn", a, b, grid=grid, name=name,
        a_spec=pl.BlockSpec((tk, tm), lambda j, i, k: (k, i)),
        b_spec=pl.BlockSpec((tk, tn), lambda j, i, k: (k, j)),
        o_spec=pl.BlockSpec((tm, tn), lambda j, i, k: (i, j)),
        out_shape=jax.ShapeDtypeStruct((m, n), out_dtype), acc_shape=(tm, tn))


def _in_tile(nb, c):
    return math.gcd(nb, c)


def _lin_in_fwd(h, wg, parts, *, name):
    t, d = h.shape
    nb = wg.shape[2]
    c = N_DEV * nb // parts
    w = _in_tile(nb, c)
    nbw, cw = nb // w, c // w
    tm = _pick(t, (1024,) if w < 512 else (512, 256, 128))
    grid = (N_DEV * nbw, t // tm, 1)
    return _matmul(
        "nn", h, wg, grid=grid, name=name,
        a_spec=pl.BlockSpec((tm, d), lambda j, i, k: (i, 0)),
        b_spec=pl.BlockSpec((None, d, w), lambda j, i, k: (j // nbw, 0, j % nbw)),
        o_spec=pl.BlockSpec((None, tm, w), lambda j, i, k: (j // cw, i, j % cw)),
        out_shape=jax.ShapeDtypeStruct((parts, t, c), BF16), acc_shape=(tm, w))


def _lin_in_dx(dact, wg, *, name):
    parts, t, c = dact.shape
    d, nb = wg.shape[1], wg.shape[2]
    w = _in_tile(nb, c)
    nbw, cw = nb // w, c // w
    tm = _pick(t, (1024,) if w < 512 else (512, 256, 128))
    tn = _pick(d, (2048,) if w < 512 else (1024, 512, 256, 128))
    grid = (d // tn, t // tm, N_DEV * nbw)
    return _matmul(
        "nt", dact, wg, grid=grid, name=name,
        a_spec=pl.BlockSpec((None, tm, w), lambda j, i, k: (k // cw, i, k % cw)),
        b_spec=pl.BlockSpec((None, tn, w), lambda j, i, k: (k // nbw, j, k % nbw)),
        o_spec=pl.BlockSpec((tm, tn), lambda j, i, k: (i, j)),
        out_shape=jax.ShapeDtypeStruct((t, d), F32), acc_shape=(tm, tn))


def _lin_in_dw(h, dact, nb, *, name):
    t, d = h.shape
    parts, _, c = dact.shape
    w = _in_tile(nb, c)
    nbw, cw = nb // w, c // w
    tm = _pick(d, (1024,) if w < 512 else (512, 256, 128))
    tk = _pick(t, (1024, 512, 256, 128))
    grid = (N_DEV * nbw, d // tm, t // tk)
    return _matmul(
        "tn", h, dact, grid=grid, name=name,
        a_spec=pl.BlockSpec((tk, tm), lambda j, i, k: (k, i)),
        b_spec=pl.BlockSpec((None, tk, w), lambda j, i, k: (j // cw, k, j % cw)),
        o_spec=pl.BlockSpec((None, tm, w), lambda j, i, k: (j // nbw, i, j % nbw)),
        out_shape=jax.ShapeDtypeStruct((N_DEV, d, nb), BF16), acc_shape=(tm, w))


def _fold8(v):
    rows, c = v.shape
    return v.reshape(rows // 8, 8, c).sum(axis=0)


def _accumulate(ref, val, first):
    @pl.when(first)
    def _():
        ref[...] = val

    @pl.when(jnp.logical_not(first))
    def _():
        ref[...] += val


def _cast_layer(w_stack, layer, me, *, name):
    _, r, c = w_stack.shape
    tr = _pick(r, (256, 128, 64, 32, 16))

    def body(me_ref, w_ref, o_ref):
        del me_ref
        o_ref[...] = w_ref[...].astype(BF16)

    return pl.pallas_call(
        body, name=name,
        grid_spec=pltpu.PrefetchScalarGridSpec(
            num_scalar_prefetch=1, grid=(r // tr,),
            in_specs=[pl.BlockSpec((None, tr, c), lambda i, me_ref: (layer, i, 0))],
            out_specs=pl.BlockSpec((None, tr, c), lambda i, me_ref: (me_ref[0], i, 0))),
        out_shape=jax.ShapeDtypeStruct((N_DEV, r, c), BF16),
        compiler_params=_params("parallel"),
    )(me, w_stack)


def _rmsnorm_fwd(x, w, *, name):
    t, d = x.shape
    tt = _pick(t, (256, 128))

    def body(x_ref, w_ref, o_ref):
        xv = x_ref[...]
        r = lax.rsqrt(jnp.mean(xv * xv, axis=1, keepdims=True) + EPS)
        o_ref[...] = (xv * r * w_ref[...]).astype(BF16)

    return pl.pallas_call(
        body, name=name, grid=(t // tt,),
        in_specs=[pl.BlockSpec((tt, d), lambda i: (i, 0)), pl.BlockSpec((1, d), lambda i: (0, 0))],
        out_specs=pl.BlockSpec((tt, d), lambda i: (i, 0)),
        out_shape=jax.ShapeDtypeStruct((t, d), BF16),
        compiler_params=_params("parallel"),
    )(x, w.reshape(1, d))


def _rmsnorm_bwd(dh, x, w, dres, *, name):
    t, d = x.shape
    tt = _pick(t, (256, 128))

    def body(dh_ref, x_ref, w_ref, dres_ref, dx_ref, dxb_ref, dw_ref):
        xv = x_ref[...]
        r = lax.rsqrt(jnp.mean(xv * xv, axis=1, keepdims=True) + EPS)
        xhat = xv * r
        dhv = dh_ref[...].astype(F32)
        dxhat = dhv * w_ref[...]
        dx = dres_ref[...] + r * (dxhat - xhat * jnp.mean(dxhat * xhat, axis=1, keepdims=True))
        dx_ref[...] = dx
        dxb_ref[...] = dx.astype(BF16)
        _accumulate(dw_ref, _fold8(dhv * xhat), pl.program_id(0) == 0)

    row = pl.BlockSpec((tt, d), lambda i: (i, 0))
    return pl.pallas_call(
        body, name=name, grid=(t // tt,),
        in_specs=[row, row, pl.BlockSpec((1, d), lambda i: (0, 0)), row],
        out_specs=[row, row, pl.BlockSpec((8, d), lambda i: (0, 0))],
        out_shape=[jax.ShapeDtypeStruct((t, d), F32), jax.ShapeDtypeStruct((t, d), BF16),
                   jax.ShapeDtypeStruct((8, d), F32)],
        compiler_params=_params("arbitrary"),
    )(dh, x, w.reshape(1, d), dres)


def _loss_head(x, w, target, *, name):
    t, d = x.shape
    tt = _pick(t, (256, 128))

    def body(x_ref, w_ref, tg_ref, dx_ref, dxb_ref, dw_ref, ls_ref):
        xv = x_ref[...]
        wv = w_ref[...]
        r = lax.rsqrt(jnp.mean(xv * xv, axis=1, keepdims=True) + EPS)
        xhat = xv * r
        err = xhat * wv - tg_ref[...]
        dy = err * (1.0 / d)
        dxhat = dy * wv
        dx = r * (dxhat - xhat * jnp.mean(dxhat * xhat, axis=1, keepdims=True))
        dx_ref[...] = dx
        dxb_ref[...] = dx.astype(BF16)
        first = pl.program_id(0) == 0
        _accumulate(dw_ref, _fold8(dy * xhat), first)
        _accumulate(ls_ref, _fold8(err * err) * (0.5 / d), first)

    row = pl.BlockSpec((tt, d), lambda i: (i, 0))
    acc = pl.BlockSpec((8, d), lambda i: (0, 0))
    return pl.pallas_call(
        body, name=name, grid=(t // tt,),
        in_specs=[row, pl.BlockSpec((1, d), lambda i: (0, 0)), row],
        out_specs=[row, row, acc, acc],
        out_shape=[jax.ShapeDtypeStruct((t, d), F32), jax.ShapeDtypeStruct((t, d), BF16),
                   jax.ShapeDtypeStruct((8, d), F32), jax.ShapeDtypeStruct((8, d), F32)],
        compiler_params=_params("arbitrary"),
    )(x, w.reshape(1, d), target)


def _conv_causal(e, tap, width):
    acc = None
    for k in range(width):
        s = width - 1 - k
        term = (e if s == 0 else pltpu.roll(e, s, 0)) * tap(k)
        acc = term if acc is None else acc + term
    return acc


def _conv_anticausal(e, tap, width):
    rows = e.shape[0]
    acc = None
    for k in range(width):
        s = width - 1 - k
        term = (e if s == 0 else pltpu.roll(e, rows - s, 0)) * tap(k)
        acc = term if acc is None else acc + term
    return acc


def _extend(prev, cur, nxt, first, last):
    parts = []
    if prev is not None:
        parts.append(jnp.where(first, 0.0, prev.astype(F32)))
    parts.append(cur.astype(F32))
    if nxt is not None:
        parts.append(jnp.where(last, 0.0, nxt.astype(F32)))
    return jnp.concatenate(parts, axis=0)


def _prev_idx(i, tt):
    return jnp.maximum(i * (tt // HALO) - 1, 0)


def _next_idx(i, tt, t):
    return jnp.minimum((i + 1) * (tt // HALO), t // HALO - 1)


def _ffn_act_fwd(u3, cw, cb, *, name):
    _, t, f = u3.shape
    tt = _pick(t, (512, 256, 128))
    tc = _pick(f, (512, 256, 128))
    width = cw.shape[0]

    def body(u_ref, up_ref, w_ref, b_ref, o_ref):
        first = pl.program_id(1) == 0
        pre = []
        for p in range(2):
            e = _extend(up_ref[p], u_ref[p], None, first, None)
            pre.append(_conv_causal(e, lambda k: w_ref[k, p:p + 1, :], width)[HALO:] + b_ref[p:p + 1, :])
        g, v = pre
        o_ref[...] = (g * _sigmoid(g) * v).astype(BF16)

    return pl.pallas_call(
        body, name=name, grid=(f // tc, t // tt),
        in_specs=[pl.BlockSpec((2, tt, tc), lambda j, i: (0, i, j)),
                  pl.BlockSpec((2, HALO, tc), lambda j, i: (0, _prev_idx(i, tt), j)),
                  pl.BlockSpec((width, 2, tc), lambda j, i: (0, 0, j)),
                  pl.BlockSpec((2, tc), lambda j, i: (0, j))],
        out_specs=pl.BlockSpec((tt, tc), lambda j, i: (i, j)),
        out_shape=jax.ShapeDtypeStruct((t, f), BF16),
        compiler_params=_params("parallel", "parallel"),
    )(u3, u3, cw, cb)


def _ffn_act_bwd(u3, da, cw, cb, *, name):
    _, t, f = u3.shape
    tt = _pick(t, (512, 256, 128))
    tc = _pick(f, (512, 256, 128))
    width = cw.shape[0]
    nt = t // tt
    ctr = slice(HALO, HALO + tt)

    def body(u_ref, up_ref, un_ref, da_ref, dan_ref, w_ref, b_ref, du_ref, dcw_ref, dcb_ref):
        i = pl.program_id(1)
        first, last = i == 0, i == nt - 1
        ext, pre = [], []
        for p in range(2):
            e = _extend(up_ref[p], u_ref[p], un_ref[p], first, last)
            ext.append(e)
            pre.append(_conv_causal(e, lambda k: w_ref[k, p:p + 1, :], width) + b_ref[p:p + 1, :])
        g, v = pre
        sg = _sigmoid(g)
        dae = _extend(jnp.zeros((HALO, tc), F32), da_ref[...], dan_ref[...], False, last)
        dpre = (dae * v * (sg * (1.0 + g * (1.0 - sg))), dae * (g * sg))

        @pl.when(first)
        def _():
            dcw_ref[...] = jnp.zeros_like(dcw_ref)
            dcb_ref[...] = jnp.zeros_like(dcb_ref)

        for p in range(2):
            du_ref[p] = _conv_anticausal(dpre[p], lambda k: w_ref[k, p:p + 1, :], width)[ctr].astype(BF16)
            dc = dpre[p][ctr]
            dcb_ref[p:p + 1, :] += jnp.sum(dc, axis=0, keepdims=True)
            for k in range(width):
                s = width - 1 - k
                xs = (ext[p] if s == 0 else pltpu.roll(ext[p], s, 0))[ctr]
                dcw_ref[k, p:p + 1, :] += jnp.sum(dc * xs, axis=0, keepdims=True)

    return pl.pallas_call(
        body, name=name, grid=(f // tc, nt),
        in_specs=[pl.BlockSpec((2, tt, tc), lambda j, i: (0, i, j)),
                  pl.BlockSpec((2, HALO, tc), lambda j, i: (0, _prev_idx(i, tt), j)),
                  pl.BlockSpec((2, HALO, tc), lambda j, i: (0, _next_idx(i, tt, t), j)),
                  pl.BlockSpec((tt, tc), lambda j, i: (i, j)),
                  pl.BlockSpec((HALO, tc), lambda j, i: (_next_idx(i, tt, t), j)),
                  pl.BlockSpec((width, 2, tc), lambda j, i: (0, 0, j)),
                  pl.BlockSpec((2, tc), lambda j, i: (0, j))],
        out_specs=[pl.BlockSpec((2, tt, tc), lambda j, i: (0, i, j)),
                   pl.BlockSpec((width, 2, tc), lambda j, i: (0, 0, j)),
                   pl.BlockSpec((2, tc), lambda j, i: (0, j))],
        out_shape=[jax.ShapeDtypeStruct((2, t, f), BF16), jax.ShapeDtypeStruct((width, 2, f), F32),
                   jax.ShapeDtypeStruct((2, f), F32)],
        compiler_params=_params("parallel", "arbitrary"),
    )(u3, u3, u3, da, da, cw, cb)


def _sc_act_fwd(p3, cw, *, name):
    _, t, c = p3.shape
    tt = _pick(t, (512, 256, 128))
    tc = _pick(c, (512, 256, 128))
    width = cw.shape[0]

    def body(p_ref, pp_ref, w_ref, o_ref):
        first = pl.program_id(1) == 0
        q = _extend(pp_ref[1], p_ref[1], None, first, None) * _extend(pp_ref[2], p_ref[2], None, first, None)
        cq = _conv_causal(q, lambda k: w_ref[k:k + 1, :], width)[HALO:]
        o_ref[...] = (p_ref[0].astype(F32) * cq).astype(BF16)

    return pl.pallas_call(
        body, name=name, grid=(c // tc, t // tt),
        in_specs=[pl.BlockSpec((3, tt, tc), lambda j, i: (0, i, j)),
                  pl.BlockSpec((3, HALO, tc), lambda j, i: (0, _prev_idx(i, tt), j)),
                  pl.BlockSpec((width, tc), lambda j, i: (0, j))],
        out_specs=pl.BlockSpec((tt, tc), lambda j, i: (i, j)),
        out_shape=jax.ShapeDtypeStruct((t, c), BF16),
        compiler_params=_params("parallel", "parallel"),
    )(p3, p3, cw)


def _sc_act_bwd(p3, da, cw, *, name):
    _, t, c = p3.shape
    tt = _pick(t, (512, 256, 128))
    tc = _pick(c, (512, 256, 128))
    width = cw.shape[0]
    nt = t // tt
    ctr = slice(HALO, HALO + tt)

    def body(p_ref, pp_ref, pn_ref, da_ref, dan_ref, w_ref, dp_ref, dcw_ref):
        i = pl.program_id(1)
        first, last = i == 0, i == nt - 1
        tap = lambda k: w_ref[k:k + 1, :]
        bg, cg, hh = (_extend(pp_ref[p], p_ref[p], pn_ref[p], first, last) for p in range(3))
        q = cg * hh
        cq = _conv_causal(q, tap, width)
        dae = _extend(jnp.zeros((HALO, tc), F32), da_ref[...], dan_ref[...], False, last)
        dcq = dae * bg
        dq = _conv_anticausal(dcq, tap, width)[ctr]
        dp_ref[0] = (dae * cq)[ctr].astype(BF16)
        dp_ref[1] = (dq * hh[ctr]).astype(BF16)
        dp_ref[2] = (dq * cg[ctr]).astype(BF16)

        @pl.when(first)
        def _():
            dcw_ref[...] = jnp.zeros_like(dcw_ref)

        dc = dcq[ctr]
        for k in range(width):
            s = width - 1 - k
            qs = (q if s == 0 else pltpu.roll(q, s, 0))[ctr]
            dcw_ref[k:k + 1, :] += jnp.sum(dc * qs, axis=0, keepdims=True)

    return pl.pallas_call(
        body, name=name, grid=(c // tc, nt),
        in_specs=[pl.BlockSpec((3, tt, tc), lambda j, i: (0, i, j)),
                  pl.BlockSpec((3, HALO, tc), lambda j, i: (0, _prev_idx(i, tt), j)),
                  pl.BlockSpec((3, HALO, tc), lambda j, i: (0, _next_idx(i, tt, t), j)),
                  pl.BlockSpec((tt, tc), lambda j, i: (i, j)),
                  pl.BlockSpec((HALO, tc), lambda j, i: (_next_idx(i, tt, t), j)),
                  pl.BlockSpec((width, tc), lambda j, i: (0, j))],
        out_specs=[pl.BlockSpec((3, tt, tc), lambda j, i: (0, i, j)),
                   pl.BlockSpec((width, tc), lambda j, i: (0, j))],
        out_shape=[jax.ShapeDtypeStruct((3, t, c), BF16), jax.ShapeDtypeStruct((width, c), F32)],
        compiler_params=_params("parallel", "arbitrary"),
    )(p3, p3, p3, da, da, cw)


def _ssd_conv_fwd(zx, cw, cb, col0, *, name):
    t = zx.shape[0]
    width, c = cw.shape
    tt = _pick(t, (512, 256, 128))
    tc = _pick(math.gcd(c, col0), (512, 256, 128))
    off = col0 // tc

    def body(x_ref, xp_ref, w_ref, b_ref, o_ref):
        first = pl.program_id(1) == 0
        e = _extend(xp_ref[...], x_ref[...], None, first, None)
        pre = _conv_causal(e, lambda k: w_ref[k:k + 1, :], width)[HALO:] + b_ref[...]
        o_ref[...] = (pre * _sigmoid(pre)).astype(BF16)

    return pl.pallas_call(
        body, name=name, grid=(c // tc, t // tt),
        in_specs=[pl.BlockSpec((tt, tc), lambda j, i: (i, off + j)),
                  pl.BlockSpec((HALO, tc), lambda j, i: (_prev_idx(i, tt), off + j)),
                  pl.BlockSpec((width, tc), lambda j, i: (0, j)),
                  pl.BlockSpec((1, tc), lambda j, i: (0, j))],
        out_specs=pl.BlockSpec((tt, tc), lambda j, i: (i, j)),
        out_shape=jax.ShapeDtypeStruct((t, c), BF16),
        compiler_params=_params("parallel", "parallel"),
    )(zx, zx, cw, cb)


def _ssd_conv_bwd(zx, dxc, cw, cb, dzx, col0, woff, *, name):
    t = zx.shape[0]
    width = cw.shape[0]
    c = dxc.shape[1]
    tt = _pick(t, (512, 256, 128))
    tc = _pick(math.gcd(math.gcd(c, col0), woff) if woff else math.gcd(c, col0), (512, 256, 128))
    nt = t // tt
    xoff, wo = (col0 + woff) // tc, woff // tc
    ctr = slice(HALO, HALO + tt)

    def body(x_ref, xp_ref, xn_ref, d_ref, dn_ref, w_ref, b_ref, dzx_in, dzx_ref, dcw_ref, dcb_ref):
        del dzx_in
        i = pl.program_id(1)
        first, last = i == 0, i == nt - 1
        tap = lambda k: w_ref[k:k + 1, :]
        e = _extend(xp_ref[...], x_ref[...], xn_ref[...], first, last)
        pre = _conv_causal(e, tap, width) + b_ref[...]
        sg = _sigmoid(pre)
        de = _extend(jnp.zeros((HALO, tc), F32), d_ref[...], dn_ref[...], False, last)
        dpre = de * (sg * (1.0 + pre * (1.0 - sg)))
        dzx_ref[...] = _conv_anticausal(dpre, tap, width)[ctr].astype(BF16)

        @pl.when(first)
        def _():
            dcw_ref[...] = jnp.zeros_like(dcw_ref)
            dcb_ref[...] = jnp.zeros_like(dcb_ref)

        dc = dpre[ctr]
        dcb_ref[...] += jnp.sum(dc, axis=0, keepdims=True)
        for k in range(width):
            s = width - 1 - k
            xs = (e if s == 0 else pltpu.roll(e, s, 0))[ctr]
            dcw_ref[k:k + 1, :] += jnp.sum(dc * xs, axis=0, keepdims=True)

    return pl.pallas_call(
        body, name=name, grid=(c // tc, nt),
        in_specs=[pl.BlockSpec((tt, tc), lambda j, i: (i, xoff + j)),
                  pl.BlockSpec((HALO, tc), lambda j, i: (_prev_idx(i, tt), xoff + j)),
                  pl.BlockSpec((HALO, tc), lambda j, i: (_next_idx(i, tt, t), xoff + j)),
                  pl.BlockSpec((tt, tc), lambda j, i: (i, j)),
                  pl.BlockSpec((HALO, tc), lambda j, i: (_next_idx(i, tt, t), j)),
                  pl.BlockSpec((width, tc), lambda j, i: (0, wo + j)),
                  pl.BlockSpec((1, tc), lambda j, i: (0, wo + j)),
                  pl.BlockSpec(memory_space=pl.ANY)],
        out_specs=[pl.BlockSpec((tt, tc), lambda j, i: (i, xoff + j)),
                   pl.BlockSpec((width, tc), lambda j, i: (0, j)),
                   pl.BlockSpec((1, tc), lambda j, i: (0, j))],
        out_shape=[jax.ShapeDtypeStruct(dzx.shape, dzx.dtype), jax.ShapeDtypeStruct((width, c), F32),
                   jax.ShapeDtypeStruct((1, c), F32)],
        input_output_aliases={7: 0},
        compiler_params=_params("parallel", "arbitrary"),
    )(zx, zx, zx, dxc, dxc, cw, cb, dzx)


def _ssd_put_ddt(ddt_g, dzx, col, *, name):
    g, t, _ = ddt_g.shape
    tt = _pick(t, (512, 256, 128))

    def body(d_ref, dzx_in, dzx_ref):
        del dzx_in
        dzx_ref[...] = jnp.sum(d_ref[...], axis=0).astype(BF16)

    return pl.pallas_call(
        body, name=name, grid=(t // tt,),
        in_specs=[pl.BlockSpec((g, tt, LANE), lambda i: (0, i, 0)), pl.BlockSpec(memory_space=pl.ANY)],
        out_specs=pl.BlockSpec((tt, LANE), lambda i: (i, col)),
        out_shape=jax.ShapeDtypeStruct(dzx.shape, dzx.dtype),
        input_output_aliases={1: 0},
        compiler_params=_params("parallel"),
    )(ddt_g, dzx)


def _dot(a, b, mode):
    return lax.dot_general(a, b, _DIMS[mode], preferred_element_type=F32)


def _dot_exact(m01, v, mode="nn"):
    hi = v.astype(BF16)
    r1 = v - hi.astype(F32)
    mid = r1.astype(BF16)
    lo = (r1 - mid.astype(F32)).astype(BF16)
    return _dot(m01, hi, mode) + _dot(m01, mid, mode) + _dot(m01, lo, mode)


def _softplus(x):
    return jnp.maximum(x, 0.0) + jnp.log(1.0 + jnp.exp(-jnp.abs(x)))


def _head_vectors(g, dt_raw, bias, alog):
    n = CHUNK
    dt = _softplus(dt_raw + bias)
    a = -jnp.exp(alog)
    tri = (lax.broadcasted_iota(jnp.int32, (n, n), 0) >= lax.broadcasted_iota(jnp.int32, (n, n), 1)).astype(BF16)
    cs = _dot_exact(tri, dt * a)
    return dt, a, cs, cs.T


def _col(v, lane_ids, h):
    return jnp.sum(jnp.where(lane_ids == h, v, 0.0), axis=1, keepdims=True)


def _row(vt, sub_ids, h):
    return jnp.sum(jnp.where(sub_ids == h, vt, 0.0), axis=0, keepdims=True)


def _ssd_specs(di, bc, nc, rev):
    cidx = (lambda c: nc - 1 - c) if rev else (lambda c: c)
    wide = lambda off: pl.BlockSpec((CHUNK, GROUP_W), lambda g, c: (cidx(c), off + g))
    lane = lambda off: pl.BlockSpec((CHUNK, LANE), lambda g, c: (cidx(c), off + g))
    fixed = lambda off: pl.BlockSpec((CHUNK, LANE), lambda g, c: (cidx(c), off))
    vec = pl.BlockSpec((1, LANE), lambda g, c: (0, 0))
    gvec = pl.BlockSpec((1, GROUP_W), lambda g, c: (0, g))
    state = pl.BlockSpec((None, None, 4, PAIR, STATE), lambda g, c: (g, cidx(c), 0, 0, 0))
    return wide, lane, fixed, vec, gvec, state


def _ssd_fwd(xbc, zx, bias, alog, dexp, nw, *, name):
    t = xbc.shape[0]
    di = nw.shape[1]
    bc = (xbc.shape[1] - di) // 2
    ng, nc = di // GROUP_W, t // CHUNK
    wide, lane, fixed, vec, gvec, state = _ssd_specs(di, bc, nc, rev=False)

    def body(xs_ref, b_ref, c_ref, dt_ref, z_ref, bias_ref, alog_ref, dexp_ref, nw_ref,
             yn_ref, y_ref, st_ref, s_scr):
        g, c = pl.program_id(0), pl.program_id(1)

        @pl.when(c == 0)
        def _():
            s_scr[...] = jnp.zeros_like(s_scr)

        n = CHUNK
        dt, a, cs, cst = _head_vectors(g, dt_ref[...].astype(F32), bias_ref[...], alog_ref[...])
        lane_ids = lax.broadcasted_iota(jnp.int32, (n, LANE), 1)
        sub_ids = lax.broadcasted_iota(jnp.int32, (LANE, n), 0)
        causal = lax.broadcasted_iota(jnp.int32, (n, n), 0) >= lax.broadcasted_iota(jnp.int32, (n, n), 1)
        half = lax.broadcasted_iota(jnp.int32, (1, PAIR), 1) < HEAD_DIM
        half_rows = lax.broadcasted_iota(jnp.int32, (PAIR, 1), 0) < HEAD_DIM
        bm, cm = b_ref[...], c_ref[...]
        gm = _dot(cm, bm, "nt")
        x = xs_ref[...].astype(F32)
        ys = []
        for q in range(4):
            h0 = g * 8 + 2 * q
            col = [_col(cs, lane_ids, h0 + e) for e in range(2)]
            row = [_row(cst, sub_ids, h0 + e) for e in range(2)]
            dtc = [_col(dt, lane_ids, h0 + e) for e in range(2)]
            last = [col[e][n - 1:n, :] for e in range(2)]
            xd = x[:, q * PAIR:(q + 1) * PAIR] * jnp.where(half, dtc[0], dtc[1])
            xd_bf = xd.astype(BF16)
            yd = []
            for e in range(2):
                lm = jnp.exp(jnp.where(causal, col[e] - row[e], -1e30))
                yd.append(_dot((gm * lm).astype(BF16), xd_bf, "nn"))
            s = s_scr[q]
            st_ref[q] = s
            ecs = jnp.where(half, jnp.exp(col[0]), jnp.exp(col[1]))
            dte = jnp.where(half, jnp.exp(last[0] - col[0]), jnp.exp(last[1] - col[1]))
            yoff = ecs * _dot(cm, s.astype(BF16), "nt")
            snew = _dot((xd * dte).astype(BF16), bm, "tn")
            s_scr[q] = s * jnp.where(half_rows, jnp.exp(last[0]), jnp.exp(last[1])) + snew
            ys.append(jnp.where(half, yd[0], yd[1]) + yoff)
        y = jnp.concatenate(ys, axis=1) + dexp_ref[...] * x
        y_ref[...] = y.astype(BF16)
        z = z_ref[...].astype(F32)
        yg = y * (z * _sigmoid(z))
        r = lax.rsqrt(jnp.mean(yg * yg, axis=1, keepdims=True) + EPS)
        yn_ref[...] = (yg * r * nw_ref[...]).astype(BF16)

    dtcol = (2 * di + 2 * bc) // LANE
    return pl.pallas_call(
        body, name=name, grid=(ng, nc),
        in_specs=[wide(0), lane(di // LANE), lane((di + bc) // LANE), fixed(dtcol), wide(0),
                  vec, vec, gvec, gvec],
        out_specs=[wide(0), wide(0), state],
        out_shape=[jax.ShapeDtypeStruct((t, di), BF16), jax.ShapeDtypeStruct((t, di), BF16),
                   jax.ShapeDtypeStruct((ng, nc, 4, PAIR, STATE), F32)],
        scratch_shapes=[pltpu.VMEM((4, PAIR, STATE), F32)],
        compiler_params=_params("parallel", "arbitrary"),
    )(xbc, xbc, xbc, zx, zx, bias, alog, dexp, nw)


def _ssd_bwd(dyn, y, xbc, zx, states, bias, alog, dexp, nw, *, name):
    t = xbc.shape[0]
    di = nw.shape[1]
    bc = (xbc.shape[1] - di) // 2
    ng, nc = di // GROUP_W, t // CHUNK
    wide, lane, fixed, vec, gvec, state = _ssd_specs(di, bc, nc, rev=True)
    acc = lambda w: pl.BlockSpec((None, 8, w), lambda g, c: (g, 0, 0))

    def body(dyn_ref, y_ref, z_ref, nw_ref, xs_ref, b_ref, c_ref, dt_ref, bias_ref, alog_ref, dexp_ref, st_ref,
             dz_ref, dxs_ref, db_ref, dc_ref, ddt_ref, small_ref, dnw_ref, ddexp_ref, ds_scr):
        g, c = pl.program_id(0), pl.program_id(1)

        @pl.when(c == 0)
        def _():
            ds_scr[...] = jnp.zeros_like(ds_scr)
            small_ref[...] = jnp.zeros_like(small_ref)
            dnw_ref[...] = jnp.zeros_like(dnw_ref)
            ddexp_ref[...] = jnp.zeros_like(ddexp_ref)

        n = CHUNK
        yv = y_ref[...].astype(F32)
        z = z_ref[...].astype(F32)
        sz = _sigmoid(z)
        silu = z * sz
        yg = yv * silu
        r = lax.rsqrt(jnp.mean(yg * yg, axis=1, keepdims=True) + EPS)
        yhat = yg * r
        dynv = dyn_ref[...].astype(F32)
        dnw_ref[0:1, :] += jnp.sum(dynv * yhat, axis=0, keepdims=True)
        dyhat = dynv * nw_ref[...]
        dyg = r * (dyhat - yhat * jnp.mean(dyhat * yhat, axis=1, keepdims=True))
        dz_ref[...] = (dyg * yv * (sz * (1.0 + z * (1.0 - sz)))).astype(BF16)
        dy = dyg * silu

        dt_in = dt_ref[...].astype(F32) + bias_ref[...]
        dt, a, cs, cst = _head_vectors(g, dt_ref[...].astype(F32), bias_ref[...], alog_ref[...])
        lane_ids = lax.broadcasted_iota(jnp.int32, (n, LANE), 1)
        sub_ids = lax.broadcasted_iota(jnp.int32, (LANE, n), 0)
        ri = lax.broadcasted_iota(jnp.int32, (n, n), 0)
        ci = lax.broadcasted_iota(jnp.int32, (n, n), 1)
        causal, causal_t = ri >= ci, ci >= ri
        is_last = lax.broadcasted_iota(jnp.int32, (n, 1), 0) == n - 1
        half = lax.broadcasted_iota(jnp.int32, (1, PAIR), 1) < HEAD_DIM
        half_rows = lax.broadcasted_iota(jnp.int32, (PAIR, 1), 0) < HEAD_DIM
        bm, cm = b_ref[...], c_ref[...]
        bf = bm.astype(F32)
        gm, gmt = _dot(cm, bm, "nt"), _dot(bm, cm, "nt")
        x = xs_ref[...].astype(F32)
        dexp = dexp_ref[...]

        dg_sum = jnp.zeros((n, n), F32)
        dgt_sum = jnp.zeros((n, n), F32)
        db_off = jnp.zeros((n, STATE), F32)
        dc_off = jnp.zeros((n, STATE), F32)
        dcs_blk = jnp.zeros((n, LANE), F32)
        ddt_blk = jnp.zeros((n, LANE), F32)
        dxs = []
        for q in range(4):
            h0 = g * 8 + 2 * q
            sl = slice(q * PAIR, (q + 1) * PAIR)
            col = [_col(cs, lane_ids, h0 + e) for e in range(2)]
            row = [_row(cst, sub_ids, h0 + e) for e in range(2)]
            dtc = [_col(dt, lane_ids, h0 + e) for e in range(2)]
            last = [col[e][n - 1:n, :] for e in range(2)]
            xp, dyp = x[:, sl], dy[:, sl]
            dtp = jnp.where(half, dtc[0], dtc[1])
            xd = xp * dtp
            xd_bf, dyp_bf = xd.astype(BF16), dyp.astype(BF16)
            ecs = jnp.where(half, jnp.exp(col[0]), jnp.exp(col[1]))
            dte = jnp.where(half, jnp.exp(last[0] - col[0]), jnp.exp(last[1] - col[1]))
            s, ds = st_ref[q], ds_scr[q]
            s_bf, ds_bf = s.astype(BF16), ds.astype(BF16)
            yoff = ecs * _dot(cm, s_bf, "nt")
            edy_bf = (ecs * dyp).astype(BF16)
            dc_off += _dot(edy_bf, s_bf, "nn")
            bds = _dot(bm, ds_bf, "nt")
            sds = s * ds
            zs = []
            for e in range(2):
                msk = half if e == 0 else jnp.logical_not(half)
                msk_rows = half_rows if e == 0 else jnp.logical_not(half_rows)
                lm = jnp.exp(jnp.where(causal, col[e] - row[e], -1e30))
                lmt = jnp.exp(jnp.where(causal_t, row[e] - col[e], -1e30))
                dym_bf = jnp.where(msk, dyp, 0.0).astype(BF16)
                xdm_bf = jnp.where(msk, xd, 0.0).astype(BF16)
                dm = _dot(dym_bf, xd_bf, "nt")
                dmt = _dot(xdm_bf, dyp_bf, "nt")
                m, mt = gm * lm, gmt * lmt
                dcs = jnp.sum(dm * m, axis=1, keepdims=True) - jnp.sum(dmt * mt, axis=1, keepdims=True)
                dg_sum += dm * lm
                dgt_sum += dmt * lmt
                zs.append(_dot(mt.astype(BF16), dyp_bf, "nn"))
                we = _dot(xdm_bf, ds_bf, "nn")
                dte_col = jnp.exp(last[e] - col[e])
                te = dte_col * jnp.sum(we * bf, axis=1, keepdims=True)
                db_off += dte_col * we
                dcs += jnp.sum(jnp.where(msk, dyp * yoff, 0.0), axis=1, keepdims=True) - te
                tail = jnp.exp(last[e]) * jnp.sum(jnp.where(msk_rows, sds, 0.0), keepdims=True) \
                    + jnp.sum(te, keepdims=True)
                dcs += jnp.where(is_last, tail, 0.0)
                dcs_blk += jnp.where(lane_ids == h0 + e, dcs, 0.0)
            dxd = jnp.where(half, zs[0], zs[1]) + dte * bds
            dxs.append(dxd * dtp + dexp[:, sl] * dyp)
            ddexp_ref[0:1, sl] += jnp.sum(dyp * xp, axis=0, keepdims=True)
            rs = dxd * xp
            for e in range(2):
                msk = half if e == 0 else jnp.logical_not(half)
                ddt_blk += jnp.where(lane_ids == h0 + e, jnp.sum(jnp.where(msk, rs, 0.0), axis=1, keepdims=True), 0.0)
            ds_scr[q] = ds * jnp.where(half_rows, jnp.exp(last[0]), jnp.exp(last[1])) + _dot(edy_bf, cm, "tn")

        dxs_ref[...] = jnp.concatenate(dxs, axis=1).astype(BF16)
        dc_ref[...] = (_dot(dg_sum.astype(BF16), bm, "nn") + dc_off).astype(BF16)
        db_ref[...] = (_dot(dgt_sum.astype(BF16), cm, "nn") + db_off).astype(BF16)
        upper = (ri <= ci).astype(BF16)
        dda = _dot_exact(upper, dcs_blk)
        ddt = dda * a + ddt_blk
        small_ref[0:1, :] += jnp.sum(dda * dt, axis=0, keepdims=True) * a
        ddt_raw = ddt * _sigmoid(dt_in)
        small_ref[1:2, :] += jnp.sum(ddt_raw, axis=0, keepdims=True)
        ddt_ref[...] = ddt_raw

    dtcol = (2 * di + 2 * bc) // LANE
    tot = 2 * di + 2 * bc + LANE
    return pl.pallas_call(
        body, name=name, grid=(ng, nc),
        in_specs=[wide(0), wide(0), wide(0), gvec, wide(0), lane(di // LANE), lane((di + bc) // LANE),
                  fixed(dtcol), vec, vec, gvec, state],
        out_specs=[wide(0), wide(0), lane(0), lane(0),
                   pl.BlockSpec((None, CHUNK, LANE), lambda g, c: (g, nc - 1 - c, 0)),
                   acc(LANE), acc(GROUP_W), acc(GROUP_W)],
        out_shape=[jax.ShapeDtypeStruct((t, tot), BF16), jax.ShapeDtypeStruct((t, di), BF16),
                   jax.ShapeDtypeStruct((t, bc), BF16), jax.ShapeDtypeStruct((t, bc), BF16),
                   jax.ShapeDtypeStruct((ng, t, LANE), F32), jax.ShapeDtypeStruct((ng, 8, LANE), F32),
                   jax.ShapeDtypeStruct((ng, 8, GROUP_W), F32), jax.ShapeDtypeStruct((ng, 8, GROUP_W), F32)],
        scratch_shapes=[pltpu.VMEM((4, PAIR, STATE), F32)],
        compiler_params=_params("parallel", "arbitrary"),
    )(dyn, y, zx, nw, xbc, xbc, xbc, zx, bias, alog, dexp, states)


HBM_ANY = pl.BlockSpec(memory_space=pl.ANY)


def _place():
    x, y, c = lax.axis_index("x"), lax.axis_index("y"), lax.axis_index("c")
    chips = [(1 - x, y), (x, 1 - y), (1 - x, 1 - y)]
    return x, y, c, chips


def _all_gather(arrs, *, name, inplace=False):
    n = len(arrs)

    def body(*refs):
        ins, outs = refs[:n], refs[n:2 * n]
        send, recv, loc = refs[2 * n:]
        x, y, c, chips = _place()
        me, sib = (x, y, c), (x, y, 1 - c)

        def blk(a, p):
            return outs[a].at[4 * p[0] + 2 * p[1] + p[2]]

        def cp(a, k, block, to, src=None):
            return pltpu.make_async_remote_copy(
                src_ref=blk(a, block) if src is None else src, dst_ref=blk(a, block),
                send_sem=send.at[a * 7 + k], recv_sem=recv.at[a * 7 + k], device_id=to, device_id_type=MESH)

        src = [None if inplace else ins[a] for a in range(n)]
        mine = [] if inplace else [pltpu.make_async_copy(ins[a], blk(a, me), loc.at[a]) for a in range(n)]
        for m in mine:
            m.start()
        started = []
        for a in range(n):
            started.append(cp(a, 0, me, sib, src=src[a]))
            started += [cp(a, 1 + j, me, (*chip, c), src=src[a]) for j, chip in enumerate(chips)]
        for s in started:
            s.start()
        for j, chip in enumerate(chips):
            for a in range(n):
                cp(a, 1 + j, (*chip, c), me).wait_recv()
                fwd = cp(a, 4 + j, (*chip, c), sib)
                fwd.start()
                started.append(fwd)
        for a in range(n):
            cp(a, 0, sib, me).wait_recv()
            for j, chip in enumerate(chips):
                cp(a, 4 + j, (*chip, 1 - c), me).wait_recv()
        for s in started:
            s.wait_send()
        for m in mine:
            m.wait()

    return pl.pallas_call(
        body, name=name,
        in_specs=[HBM_ANY] * n, out_specs=[HBM_ANY] * n,
        out_shape=[jax.ShapeDtypeStruct(a.shape if inplace else (N_DEV,) + a.shape, a.dtype) for a in arrs],
        input_output_aliases={a: a for a in range(n)} if inplace else {},
        scratch_shapes=[pltpu.SemaphoreType.DMA((7 * n,)), pltpu.SemaphoreType.DMA((7 * n,)),
                        pltpu.SemaphoreType.DMA((n,))],
    )(*arrs)


def _pair_exchange(grads, *, name):
    n = len(grads)

    def body(*refs):
        ins, gots = refs[:n], refs[n:2 * n]
        send, recv = refs[2 * n:]
        x, y, c, _ = _place()
        copies = []
        for a in range(n):
            for k in range(N_CHIP):
                copies.append(pltpu.make_async_remote_copy(
                    src_ref=ins[a].at[2 * k + 1 - c], dst_ref=gots[a].at[k],
                    send_sem=send.at[a * N_CHIP + k], recv_sem=recv.at[a * N_CHIP + k],
                    device_id=(x, y, 1 - c), device_id_type=MESH))
        for cpy in copies:
            cpy.start()
        for cpy in copies:
            cpy.wait()

    return pl.pallas_call(
        body, name=name,
        in_specs=[HBM_ANY] * n, out_specs=[HBM_ANY] * n,
        out_shape=[jax.ShapeDtypeStruct((N_CHIP,) + g.shape[1:], g.dtype) for g in grads],
        scratch_shapes=[pltpu.SemaphoreType.DMA((N_CHIP * n,)), pltpu.SemaphoreType.DMA((N_CHIP * n,))],
    )(*grads)


def _chip_exchange(sums, *, name):
    n = len(sums)

    def body(*refs):
        ins, outs = refs[:n], refs[n:2 * n]
        send, recv = refs[2 * n:]
        x, y, c, chips = _place()
        copies = []
        for a in range(n):
            for j, chip in enumerate(chips):
                copies.append(pltpu.make_async_remote_copy(
                    src_ref=ins[a].at[2 * chip[0] + chip[1]], dst_ref=outs[a].at[j],
                    send_sem=send.at[a * 3 + j], recv_sem=recv.at[a * 3 + j],
                    device_id=(*chip, c), device_id_type=MESH))
        for cpy in copies:
            cpy.start()
        for cpy in copies:
            cpy.wait()

    return pl.pallas_call(
        body, name=name,
        in_specs=[HBM_ANY] * n, out_specs=[HBM_ANY] * n,
        out_shape=[jax.ShapeDtypeStruct((N_CHIP - 1,) + s.shape[1:], s.dtype) for s in sums],
        scratch_shapes=[pltpu.SemaphoreType.DMA((3 * n,)), pltpu.SemaphoreType.DMA((3 * n,))],
    )(*sums)


def _add_pair(grad, got, core, *, name):
    k, r, c = got.shape
    tr = _pick(r, (256, 128, 64, 32, 16))

    def body(core_ref, a_ref, b_ref, o_ref):
        del core_ref
        o_ref[...] = (a_ref[...].astype(F32) + b_ref[...].astype(F32)).astype(BF16)

    spec = pl.BlockSpec((None, tr, c), lambda q, i, core_ref: (q, i, 0))
    return pl.pallas_call(
        body, name=name,
        grid_spec=pltpu.PrefetchScalarGridSpec(
            num_scalar_prefetch=1, grid=(k, r // tr),
            in_specs=[pl.BlockSpec((None, tr, c), lambda q, i, core_ref: (2 * q + core_ref[0], i, 0)), spec],
            out_specs=spec),
        out_shape=jax.ShapeDtypeStruct(got.shape, BF16), compiler_params=_params("parallel", "parallel"),
    )(core, grad, got)


def _all_reduce_small(v, *, name):
    r = v.shape[0]

    def body(v_ref, o_ref, buf, send, recv):
        x, y, c, _ = _place()
        me = 4 * x + 2 * y + c
        buf[me] = v_ref[...]
        copies = []
        for rel in range(1, N_DEV):
            fx, fy, fc = rel >> 2 & 1, rel >> 1 & 1, rel & 1
            peer = ((1 - x) if fx else x, (1 - y) if fy else y, (1 - c) if fc else c)
            copies.append(pltpu.make_async_remote_copy(
                src_ref=v_ref, dst_ref=buf.at[me], send_sem=send.at[rel - 1], recv_sem=recv.at[rel - 1],
                device_id=peer, device_id_type=MESH))
        for cpy in copies:
            cpy.start()
        for cpy in copies:
            cpy.wait()
        acc = buf[0]
        for d in range(1, N_DEV):
            acc = acc + buf[d]
        o_ref[...] = acc

    return pl.pallas_call(
        body, name=name,
        in_specs=[pl.BlockSpec(memory_space=pltpu.VMEM)], out_specs=pl.BlockSpec(memory_space=pltpu.VMEM),
        out_shape=jax.ShapeDtypeStruct(v.shape, F32),
        scratch_shapes=[pltpu.VMEM((N_DEV, r, LANE), F32), pltpu.SemaphoreType.DMA((N_DEV - 1,)),
                        pltpu.SemaphoreType.DMA((N_DEV - 1,))],
        compiler_params=pltpu.CompilerParams(vmem_limit_bytes=VMEM_LIMIT),
    )(v)


def _adamw_math(w, g, m, v):
    m = ADAM_B1 * m + (1.0 - ADAM_B1) * g
    v = ADAM_B2 * v + (1.0 - ADAM_B2) * (g * g)
    m_hat = m / (1.0 - ADAM_B1 ** ADAM_STEP)
    v_hat = v / (1.0 - ADAM_B2 ** ADAM_STEP)
    delta = -ADAM_LR * (m_hat / (jnp.sqrt(v_hat) + ADAM_EPS) + ADAM_WD * w)
    return delta, m, v


def _adamw_layer(w, m, v, sums, recv, chip, layer, prev, *, name):
    nl, r, c = w.shape
    tr = _pick(r, (256, 128, 64, 32, 16))

    def body(chip_ref, w_ref, m_ref, v_ref, s_ref, p_ref, *rest):
        del chip_ref
        g_ref, d_ref, mo_ref, vo_ref = rest[-4:]
        g = s_ref[...].astype(F32)
        for k in range(N_CHIP - 1):
            g = g + p_ref[k].astype(F32)
        delta, mn, vn = _adamw_math(w_ref[...], g, m_ref[...], v_ref[...])
        g_ref[...] = g
        d_ref[...] = delta
        mo_ref[...] = mn
        vo_ref[...] = vn

    lay = pl.BlockSpec((None, tr, c), lambda i, chip_ref: (layer, i, 0))
    ins = [w, m, v, sums, recv] + (list(prev) if prev is not None else [])
    in_specs = [lay, lay, lay, pl.BlockSpec((None, tr, c), lambda i, chip_ref: (chip_ref[0], i, 0)),
                pl.BlockSpec((N_CHIP - 1, tr, c), lambda i, chip_ref: (0, i, 0))]
    in_specs += [HBM_ANY] * (4 if prev is not None else 0)
    return pl.pallas_call(
        body, name=name,
        grid_spec=pltpu.PrefetchScalarGridSpec(
            num_scalar_prefetch=1, grid=(r // tr,), in_specs=in_specs, out_specs=[lay] * 4),
        out_shape=[jax.ShapeDtypeStruct(w.shape, F32)] * 4,
        input_output_aliases={6 + q: q for q in range(4)} if prev is not None else {},
        compiler_params=_params("parallel"),
    )(chip, *ins)


def _adamw_small(w, g, m, v, *, name):
    def body(w_ref, g_ref, m_ref, v_ref, d_ref, mo_ref, vo_ref):
        d_ref[...], mo_ref[...], vo_ref[...] = _adamw_math(w_ref[...], g_ref[...], m_ref[...], v_ref[...])

    vm = pl.BlockSpec(memory_space=pltpu.VMEM)
    return pl.pallas_call(
        body, name=name, in_specs=[vm] * 4, out_specs=[vm] * 3,
        out_shape=[jax.ShapeDtypeStruct(w.shape, F32)] * 3,
        compiler_params=pltpu.CompilerParams(vmem_limit_bytes=VMEM_LIMIT),
    )(w, g, m, v)


def _pack(arrs):
    flat = jnp.concatenate([a.reshape(-1).astype(F32) for a in arrs])
    pad = (-flat.shape[0]) % (8 * LANE)
    return jnp.pad(flat, (0, pad)).reshape(-1, LANE)


def _unpack(packed, shapes):
    flat = packed.reshape(-1)
    out, off = [], 0
    for s in shapes:
        size = math.prod(s)
        out.append(flat[off:off + size].reshape(s))
        off += size
    return out


WEIGHTS = ['mix_norm_w', 'ffn_norm_w', 'final_norm_w', 'ssd_w_in', 'ssd_conv_w', 'ssd_conv_b', 'ssd_dt_bias',
           'ssd_a_log', 'ssd_d', 'ssd_norm_w', 'ssd_w_out', 'sc_w_in', 'sc_conv_w', 'sc_w_out', 'ffn_w_up',
           'ffn_conv_w', 'ffn_conv_b', 'ffn_w_down']
BIG = ('ssd_w_in', 'ssd_w_out', 'sc_w_in', 'sc_w_out', 'ffn_w_up', 'ffn_w_down')
SHARDED_SMALL = ('ssd_conv_w', 'sc_conv_w', 'ffn_conv_w')


def _lane_pad(v):
    return jnp.pad(v.astype(F32), (0, LANE - v.shape[0])).reshape(1, LANE)


def _gather_cols(g):
    return jnp.moveaxis(g, 0, -2).reshape(g.shape[1:-1] + (N_DEV * g.shape[-1],))


def _reduce_scatter(grads, core, tag):
    got = _pair_exchange(grads, name=f"rs_pair_{tag}")
    sums = [_add_pair(g, o, core, name=f"rs_add_{tag}{a}") for a, (g, o) in enumerate(zip(grads, got))]
    return list(zip(sums, _chip_exchange(sums, name=f"rs_chip_{tag}")))


def kernel(x, mix_norm_w, ffn_norm_w, final_norm_w, ssd_w_in, ssd_conv_w, ssd_conv_b, ssd_dt_bias, ssd_a_log, ssd_d, ssd_norm_w, ssd_w_out, sc_w_in, sc_conv_w, sc_w_out, ffn_w_up, ffn_conv_w, ffn_conv_b, ffn_w_down, loss_target, m_mix_norm_w, m_ffn_norm_w, m_final_norm_w, m_ssd_w_in, m_ssd_conv_w, m_ssd_conv_b, m_ssd_dt_bias, m_ssd_a_log, m_ssd_d, m_ssd_norm_w, m_ssd_w_out, m_sc_w_in, m_sc_conv_w, m_sc_w_out, m_ffn_w_up, m_ffn_conv_w, m_ffn_conv_b, m_ffn_w_down, v_mix_norm_w, v_ffn_norm_w, v_final_norm_w, v_ssd_w_in, v_ssd_conv_w, v_ssd_conv_b, v_ssd_dt_bias, v_ssd_a_log, v_ssd_d, v_ssd_norm_w, v_ssd_w_out, v_sc_w_in, v_sc_conv_w, v_sc_w_out, v_ffn_w_up, v_ffn_conv_w, v_ffn_conv_b, v_ffn_w_down):
    args = locals()
    wt = {n: args[n] for n in WEIGHTS}
    mom = {n: args["m_" + n] for n in WEIGHTS}
    var = {n: args["v_" + n] for n in WEIGHTS}

    t, d = x.shape[-2], x.shape[-1]
    cur = x.reshape(t, d)
    target = loss_target.reshape(t, d)
    depth = mix_norm_w.shape[0]
    n_ssd, n_sc = ssd_w_in.shape[0], sc_w_in.shape[0]
    heads = ssd_dt_bias.shape[1]
    di = ssd_norm_w.shape[1]
    conv_dim = ssd_conv_b.shape[1]
    bc = (conv_dim - di) // 2
    in_dim = N_DEV * ssd_w_in.shape[2]
    in_pad = di + conv_dim + LANE
    ff = ffn_w_down.shape[1] * N_DEV
    me = 4 * lax.axis_index("x") + 2 * lax.axis_index("y") + lax.axis_index("c")
    me_s = me.astype(jnp.int32).reshape(1)
    core_s = lax.axis_index("c").astype(jnp.int32).reshape(1)
    chip_s = (2 * lax.axis_index("x") + lax.axis_index("y")).astype(jnp.int32).reshape(1)

    full = {n: [] for n in BIG}
    for j in range(n_ssd):
        shards = [_cast_layer(ssd_w_in, j, me_s, name=f"cast_ssd_in{j}"),
                  _cast_layer(ssd_w_out, j, me_s, name=f"cast_ssd_out{j}")]
        g_in, g_out = _all_gather(shards, name=f"ag_ssd{j}", inplace=True)
        w_in = jnp.swapaxes(g_in, 0, 1).reshape(d, in_dim)
        full['ssd_w_in'].append(jnp.pad(w_in, ((0, 0), (0, in_pad - in_dim))))
        full['ssd_w_out'].append(g_out.reshape(di, d))
    for j in range(n_sc):
        shards = [_cast_layer(sc_w_in, j, me_s, name=f"cast_sc_in{j}"),
                  _cast_layer(sc_w_out, j, me_s, name=f"cast_sc_out{j}")]
        g_in, g_out = _all_gather(shards, name=f"ag_sc{j}", inplace=True)
        full['sc_w_in'].append(g_in)
        full['sc_w_out'].append(g_out.reshape(-1, d))
    for i in range(depth):
        shards = [_cast_layer(ffn_w_up, i, me_s, name=f"cast_ffn_up{i}"),
                  _cast_layer(ffn_w_down, i, me_s, name=f"cast_ffn_down{i}")]
        g_in, g_out = _all_gather(shards, name=f"ag_ffn{i}", inplace=True)
        full['ffn_w_up'].append(g_in)
        full['ffn_w_down'].append(g_out.reshape(ff, d))
    conv_full = [_gather_cols(g) for g in _all_gather([wt[n] for n in SHARDED_SMALL], name="ag_conv")]
    ssd_cw, sc_cw, ffn_cw = conv_full
    ffn_cw = ffn_cw.reshape(depth, ffn_cw.shape[1], 2, ff)
    ffn_cb = ffn_conv_b.reshape(depth, 2, ff)
    dexp = jnp.repeat(ssd_d.astype(F32), HEAD_DIM, axis=1)

    saved = []
    for i in range(depth):
        j = i // 2
        rec = {"x_mix": cur}
        h = _rmsnorm_fwd(cur, mix_norm_w[i], name=f"norm_mix{i}")
        rec["h_mix"] = h
        if i % 2 == 0:
            zx = _mm_nn(h, full['ssd_w_in'][j], out_dtype=BF16, name=f"ssd_in{j}")
            xbc = _ssd_conv_fwd(zx, ssd_cw[j], ssd_conv_b[j].reshape(1, conv_dim), di, name=f"ssd_conv{j}")
            ssd_vecs = (_lane_pad(ssd_dt_bias[j]), _lane_pad(ssd_a_log[j]), dexp[j].reshape(1, di),
                        ssd_norm_w[j].reshape(1, di))
            yn, y, states = _ssd_fwd(xbc, zx, *ssd_vecs, name=f"ssd_core{j}")
            cur = _mm_nn(yn, full['ssd_w_out'][j], res=cur, out_dtype=F32, name=f"ssd_out{j}")
            rec.update(zx=zx, xbc=xbc, yn=yn, y=y, states=states, vecs=ssd_vecs)
        else:
            p3 = _lin_in_fwd(h, full['sc_w_in'][j], 3, name=f"sc_in{j}")
            act = _sc_act_fwd(p3, sc_cw[j], name=f"sc_act{j}")
            cur = _mm_nn(act, full['sc_w_out'][j], res=cur, out_dtype=F32, name=f"sc_out{j}")
            rec.update(p3=p3, act=act)
        rec["x_ffn"] = cur
        h = _rmsnorm_fwd(cur, ffn_norm_w[i], name=f"norm_ffn{i}")
        u3 = _lin_in_fwd(h, full['ffn_w_up'][i], 2, name=f"ffn_up{i}")
        act = _ffn_act_fwd(u3, ffn_cw[i], ffn_cb[i], name=f"ffn_act{i}")
        cur = _mm_nn(act, full['ffn_w_down'][i], res=cur, out_dtype=F32, name=f"ffn_down{i}")
        rec.update(h_ffn=h, u3=u3, ffn_act=act)
        saved.append(rec)

    dx, dxb, dw_final, loss8 = _loss_head(cur, final_norm_w, target, name="loss_head")

    small = {n: [None] * wt[n].shape[0] for n in WEIGHTS if n not in BIG and n != 'final_norm_w'}
    parts = {n: [None] * wt[n].shape[0] for n in BIG}
    for i in reversed(range(depth)):
        j = i // 2
        rec = saved[i]
        nb_up = ffn_w_up.shape[2]
        da = _mm_nt(dxb, full['ffn_w_down'][i], out_dtype=BF16, name=f"ffn_down_dx{i}")
        g_down = _mm_tn(rec["ffn_act"], dxb, out_dtype=BF16, name=f"ffn_down_dw{i}")
        du3, dcw, dcb = _ffn_act_bwd(rec["u3"], da, ffn_cw[i], ffn_cb[i], name=f"ffn_act_bwd{i}")
        dh = _lin_in_dx(du3, full['ffn_w_up'][i], name=f"ffn_up_dx{i}")
        g_up = _lin_in_dw(rec["h_ffn"], du3, nb_up, name=f"ffn_up_dw{i}")
        dx, dxb, dwn = _rmsnorm_bwd(dh, rec["x_ffn"], ffn_norm_w[i], dx, name=f"norm_ffn_bwd{i}")
        small['ffn_conv_w'][i] = dcw.reshape(dcw.shape[0], 2 * ff)
        small['ffn_conv_b'][i] = dcb.reshape(2 * ff)
        small['ffn_norm_w'][i] = dwn.sum(axis=0)
        parts['ffn_w_up'][i], parts['ffn_w_down'][i] = _reduce_scatter(
            [g_up, g_down.reshape(N_DEV, ff // N_DEV, d)], core_s, f"ffn{i}")

        if i % 2 == 0:
            zx, xbc = rec["zx"], rec["xbc"]
            cw, cb = ssd_cw[j], ssd_conv_b[j].reshape(1, conv_dim)
            dyn = _mm_nt(dxb, full['ssd_w_out'][j], out_dtype=BF16, name=f"ssd_out_dx{j}")
            g_out = _mm_tn(rec["yn"], dxb, out_dtype=BF16, name=f"ssd_out_dw{j}")
            dzx, dxs, db, dc, ddt_g, vec_acc, dnw, ddexp = _ssd_bwd(
                dyn, rec["y"], xbc, zx, rec["states"], *rec["vecs"], name=f"ssd_core_bwd{j}")
            dzx, dcw_x, dcb_x = _ssd_conv_bwd(zx, dxs, cw, cb, dzx, di, 0, name=f"ssd_conv_bwd_x{j}")
            dzx, dcw_b, dcb_b = _ssd_conv_bwd(zx, db, cw, cb, dzx, di, di, name=f"ssd_conv_bwd_b{j}")
            dzx, dcw_c, dcb_c = _ssd_conv_bwd(zx, dc, cw, cb, dzx, di, di + bc, name=f"ssd_conv_bwd_c{j}")
            dzx = _ssd_put_ddt(ddt_g, dzx, (di + conv_dim) // LANE, name=f"ssd_put_ddt{j}")
            dh = _mm_nt(dzx, full['ssd_w_in'][j], out_dtype=F32, name=f"ssd_in_dx{j}")
            g_in = _mm_tn(rec["h_mix"], dzx, out_dtype=BF16, name=f"ssd_in_dw{j}")
            g_in = jnp.swapaxes(g_in[:, :in_dim].reshape(d, N_DEV, in_dim // N_DEV), 0, 1)
            small['ssd_conv_w'][j] = jnp.concatenate([dcw_x, dcw_b, dcw_c], axis=1)
            small['ssd_conv_b'][j] = jnp.concatenate([dcb_x, dcb_b, dcb_c], axis=1).reshape(conv_dim)
            small['ssd_a_log'][j] = vec_acc[:, 0, :heads].sum(axis=0)
            small['ssd_dt_bias'][j] = vec_acc[:, 1, :heads].sum(axis=0)
            small['ssd_d'][j] = ddexp[:, 0, :].reshape(heads, HEAD_DIM).sum(axis=1)
            small['ssd_norm_w'][j] = dnw[:, 0, :].reshape(di)
            parts['ssd_w_in'][j], parts['ssd_w_out'][j] = _reduce_scatter(
                [g_in, g_out.reshape(N_DEV, di // N_DEV, d)], core_s, f"ssd{j}")
        else:
            nb_in = sc_w_in.shape[2]
            da = _mm_nt(dxb, full['sc_w_out'][j], out_dtype=BF16, name=f"sc_out_dx{j}")
            g_out = _mm_tn(rec["act"], dxb, out_dtype=BF16, name=f"sc_out_dw{j}")
            dp3, dcw = _sc_act_bwd(rec["p3"], da, sc_cw[j], name=f"sc_act_bwd{j}")
            dh = _lin_in_dx(dp3, full['sc_w_in'][j], name=f"sc_in_dx{j}")
            g_in = _lin_in_dw(rec["h_mix"], dp3, nb_in, name=f"sc_in_dw{j}")
            small['sc_conv_w'][j] = dcw
            parts['sc_w_in'][j], parts['sc_w_out'][j] = _reduce_scatter(
                [g_in, g_out.reshape(N_DEV, g_out.shape[0] // N_DEV, d)], core_s, f"sc{j}")
        dx, dxb, dwn = _rmsnorm_bwd(dh, rec["x_mix"], mix_norm_w[i], dx, name=f"norm_mix_bwd{i}")
        small['mix_norm_w'][i] = dwn.sum(axis=0)

    small_names = [n for n in WEIGHTS if n not in BIG]
    partial = {n: jnp.stack(small[n]) for n in small}
    partial['final_norm_w'] = dw_final.sum(axis=0)
    full_shapes = [partial[n].shape for n in small_names]
    packed = _pack([loss8.sum().reshape(1)] + [partial[n] for n in small_names])
    total = _unpack(_all_reduce_small(packed, name="ar_small"), [(1,)] + full_shapes)
    loss = total[0].reshape(())
    grads = dict(zip(small_names, total[1:]))
    for n in SHARDED_SMALL:
        nb = wt[n].shape[-1]
        grads[n] = lax.dynamic_slice_in_dim(grads[n], me * nb, nb, axis=grads[n].ndim - 1)

    delta, new_m, new_v = {}, {}, {}
    shapes = [wt[n].shape for n in small_names]
    outs = _adamw_small(*[_pack([src[n] for n in small_names]) for src in (wt, grads, mom, var)], name="adamw_small")
    for dst, packed_out in zip((delta, new_m, new_v), outs):
        dst.update(zip(small_names, _unpack(packed_out, shapes)))
    for n in BIG:
        prev = None
        for layer in range(wt[n].shape[0]):
            prev = _adamw_layer(wt[n], mom[n], var[n], *parts[n][layer], chip_s, layer, prev,
                                name=f"adamw_{n}{layer}")
        grads[n], delta[n], new_m[n], new_v[n] = prev

    return (loss, dx.reshape(x.shape), *[grads[n] for n in WEIGHTS], *[delta[n] for n in WEIGHTS],
            *[new_m[n] for n in WEIGHTS], *[new_v[n] for n in WEIGHTS])
```

```python
import functools
import math

import jax
import jax.numpy as jnp
from jax import lax
from jax.experimental import pallas as pl
from jax.experimental.pallas import tpu as pltpu

F32 = jnp.float32
BF16 = jnp.bfloat16
MESH = pl.DeviceIdType.MESH

N_DEV = 8
N_CHIP = 4
EPS = 1e-5
HEAD_DIM = 64
STATE = 128
CHUNK = 128
PAIR = 2 * HEAD_DIM
GROUP_W = 8 * HEAD_DIM
HALO = 16
LANE = 128
VMEM_LIMIT = 56 * 1024 * 1024

ADAM_LR = 0.001
ADAM_B1 = 0.9
ADAM_B2 = 0.999
ADAM_EPS = 1e-08
ADAM_WD = 0.01
ADAM_STEP = 10


def _pick(n, candidates):
    for c in candidates:
        if c <= n and n % c == 0:
            return c
    return n


def _params(*sem):
    return pltpu.CompilerParams(dimension_semantics=sem, vmem_limit_bytes=VMEM_LIMIT)


def _sigmoid(x):
    return 0.5 * jnp.tanh(0.5 * x) + 0.5


_DIMS = {
    "nn": (((1,), (0,)), ((), ())),
    "nt": (((1,), (1,)), ((), ())),
    "tn": (((0,), (0,)), ((), ())),
}


def _matmul(mode, a, b, *, grid, a_spec, b_spec, o_spec, out_shape, acc_shape, name, res=None, res_spec=None):
    nk = grid[2]
    dims = _DIMS[mode]

    def body(*refs):
        if res is None:
            a_ref, b_ref, o_ref = refs[:3]
            r_ref, scratch = None, refs[3:]
        else:
            a_ref, b_ref, r_ref, o_ref = refs[:4]
            scratch = refs[4:]
        part = lax.dot_general(a_ref[...], b_ref[...], dims, preferred_element_type=F32)

        def finish(acc):
            if r_ref is not None:
                acc = acc + r_ref[...]
            o_ref[...] = acc.astype(o_ref.dtype)

        if nk == 1:
            finish(part)
        else:
            acc_ref = scratch[0]
            k = pl.program_id(2)

            @pl.when(k == 0)
            def _():
                acc_ref[...] = part

            @pl.when(k > 0)
            def _():
                acc_ref[...] += part

            @pl.when(k == nk - 1)
            def _():
                finish(acc_ref[...])

    in_specs = [a_spec, b_spec] + ([res_spec] if res is not None else [])
    args = (a, b) + ((res,) if res is not None else ())
    return pl.pallas_call(
        body, name=name, grid=grid, in_specs=in_specs, out_specs=o_spec, out_shape=out_shape,
        scratch_shapes=[pltpu.VMEM(acc_shape, F32)] if nk > 1 else [],
        compiler_params=_params("parallel", "parallel", "arbitrary"),
    )(*args)


def _mm_nn(a, b, *, out_dtype, res=None, name):
    m, kd = a.shape
    n = b.shape[1]
    tm = _pick(m, (512, 256, 128))
    tn = _pick(n, (1152, 1024, 512, 384, 256, 128))
    tk = kd if kd <= 2048 else _pick(kd, (1408, 1024, 512, 256, 128))
    grid = (n // tn, m // tm, kd // tk)
    return _matmul(
        "nn", a, b, res=res, grid=grid, name=name,
        a_spec=pl.BlockSpec((tm, tk), lambda j, i, k: (i, k)),
        b_spec=pl.BlockSpec((tk, tn), lambda j, i, k: (k, j)),
        res_spec=pl.BlockSpec((tm, tn), lambda j, i, k: (i, j)),
        o_spec=pl.BlockSpec((tm, tn), lambda j, i, k: (i, j)),
        out_shape=jax.ShapeDtypeStruct((m, n), out_dtype), acc_shape=(tm, tn))


def _mm_nt(a, b, *, out_dtype, name):
    m, kd = a.shape
    n = b.shape[0]
    tm = _pick(m, (512, 256, 128))
    tn = _pick(n, (1408, 1024, 512, 256, 128))
    tk = kd if kd <= 2048 else _pick(kd, (1152, 1024, 512, 384, 256, 128))
    grid = (n // tn, m // tm, kd // tk)
    return _matmul(
        "nt", a, b, grid=grid, name=name,
        a_spec=pl.BlockSpec((tm, tk), lambda j, i, k: (i, k)),
        b_spec=pl.BlockSpec((tn, tk), lambda j, i, k: (j, k)),
        o_spec=pl.BlockSpec((tm, tn), lambda j, i, k: (i, j)),
        out_shape=jax.ShapeDtypeStruct((m, n), out_dtype), acc_shape=(tm, tn))


def _mm_tn(a, b, *, out_dtype, name):
    kd, m = a.shape
    n = b.shape[1]
    tm = _pick(m, (512, 256, 128))
    tn = _pick(n, (1152, 1024, 512, 384, 256, 128))
    tk = _pick(kd, (1024, 512, 256, 128))
    grid = (n // tn, m // tm, kd // tk)
    return _matmul(
        "tn", a, b, grid=grid, name=name,
        a_spec=pl.BlockSpec((tk, tm), lambda j, i, k: (k, i)),
        b_spec=pl.BlockSpec((tk, tn), lambda j, i, k: (k, j)),
        o_spec=pl.BlockSpec((tm, tn), lambda j, i, k: (i, j)),
        out_shape=jax.ShapeDtypeStruct((m, n), out_dtype), acc_shape=(tm, tn))


def _in_tile(nb, c):
    return math.gcd(nb, c)


def _lin_in_fwd(h, wg, parts, *, name):
    t, d = h.shape
    nb = wg.shape[2]
    c = N_DEV * nb // parts
    w = _in_tile(nb, c)
    nbw, cw = nb // w, c // w
    tm = _pick(t, (1024,) if w < 512 else (512, 256, 128))
    grid = (N_DEV * nbw, t // tm, 1)
    return _matmul(
        "nn", h, wg, grid=grid, name=name,
        a_spec=pl.BlockSpec((tm, d), lambda j, i, k: (i, 0)),
        b_spec=pl.BlockSpec((None, d, w), lambda j, i, k: (j // nbw, 0, j % nbw)),
        o_spec=pl.BlockSpec((None, tm, w), lambda j, i, k: (j // cw, i, j % cw)),
        out_shape=jax.ShapeDtypeStruct((parts, t, c), BF16), acc_shape=(tm, w))


def _lin_in_dx(dact, wg, *, name):
    parts, t, c = dact.shape
    d, nb = wg.shape[1], wg.shape[2]
    w = _in_tile(nb, c)
    nbw, cw = nb // w, c // w
    tm = _pick(t, (1024,) if w < 512 else (512, 256, 128))
    tn = _pick(d, (2048,) if w < 512 else (1024, 512, 256, 128))
    grid = (d // tn, t // tm, N_DEV * nbw)
    return _matmul(
        "nt", dact, wg, grid=grid, name=name,
        a_spec=pl.BlockSpec((None, tm, w), lambda j, i, k: (k // cw, i, k % cw)),
        b_spec=pl.BlockSpec((None, tn, w), lambda j, i, k: (k // nbw, j, k % nbw)),
        o_spec=pl.BlockSpec((tm, tn), lambda j, i, k: (i, j)),
        out_shape=jax.ShapeDtypeStruct((t, d), F32), acc_shape=(tm, tn))


def _lin_in_dw(h, dact, nb, *, name):
    t, d = h.shape
    parts, _, c = dact.shape
    w = _in_tile(nb, c)
    nbw, cw = nb // w, c // w
    tm = _pick(d, (1024,) if w < 512 else (512, 256, 128))
    tk = _pick(t, (1024, 512, 256, 128))
    grid = (N_DEV * nbw, d // tm, t // tk)
    return _matmul(
        "tn", h, dact, grid=grid, name=name,
        a_spec=pl.BlockSpec((tk, tm), lambda j, i, k: (k, i)),
        b_spec=pl.BlockSpec((None, tk, w), lambda j, i, k: (j // cw, k, j % cw)),
        o_spec=pl.BlockSpec((None, tm, w), lambda j, i, k: (j // nbw, i, j % nbw)),
        out_shape=jax.ShapeDtypeStruct((N_DEV, d, nb), BF16), acc_shape=(tm, w))


def _fold8(v):
    rows, c = v.shape
    return v.reshape(rows // 8, 8, c).sum(axis=0)


def _accumulate(ref, val, first):
    @pl.when(first)
    def _():
        ref[...] = val

    @pl.when(jnp.logical_not(first))
    def _():
        ref[...] += val


def _cast_layer(w_stack, layer, me, *, name):
    _, r, c = w_stack.shape
    tr = _pick(r, (256, 128, 64, 32, 16))

    def body(me_ref, w_ref, o_ref):
        del me_ref
        o_ref[...] = w_ref[...].astype(BF16)

    return pl.pallas_call(
        body, name=name,
        grid_spec=pltpu.PrefetchScalarGridSpec(
            num_scalar_prefetch=1, grid=(r // tr,),
            in_specs=[pl.BlockSpec((None, tr, c), lambda i, me_ref: (layer, i, 0))],
            out_specs=pl.BlockSpec((None, tr, c), lambda i, me_ref: (me_ref[0], i, 0))),
        out_shape=jax.ShapeDtypeStruct((N_DEV, r, c), BF16),
        compiler_params=_params("parallel"),
    )(me, w_stack)


def _rmsnorm_fwd(x, w, *, name):
    t, d = x.shape
    tt = _pick(t, (256, 128))

    def body(x_ref, w_ref, o_ref):
        xv = x_ref[...]
        r = lax.rsqrt(jnp.mean(xv * xv, axis=1, keepdims=True) + EPS)
        o_ref[...] = (xv * r * w_ref[...]).astype(BF16)

    return pl.pallas_call(
        body, name=name, grid=(t // tt,),
        in_specs=[pl.BlockSpec((tt, d), lambda i: (i, 0)), pl.BlockSpec((1, d), lambda i: (0, 0))],
        out_specs=pl.BlockSpec((tt, d), lambda i: (i, 0)),
        out_shape=jax.ShapeDtypeStruct((t, d), BF16),
        compiler_params=_params("parallel"),
    )(x, w.reshape(1, d))


def _rmsnorm_bwd(dh, x, w, dres, *, name):
    t, d = x.shape
    tt = _pick(t, (256, 128))

    def body(dh_ref, x_ref, w_ref, dres_ref, dx_ref, dxb_ref, dw_ref):
        xv = x_ref[...]
        r = lax.rsqrt(jnp.mean(xv * xv, axis=1, keepdims=True) + EPS)
        xhat = xv * r
        dhv = dh_ref[...].astype(F32)
        dxhat = dhv * w_ref[...]
        dx = dres_ref[...] + r * (dxhat - xhat * jnp.mean(dxhat * xhat, axis=1, keepdims=True))
        dx_ref[...] = dx
        dxb_ref[...] = dx.astype(BF16)
        _accumulate(dw_ref, _fold8(dhv * xhat), pl.program_id(0) == 0)

    row = pl.BlockSpec((tt, d), lambda i: (i, 0))
    return pl.pallas_call(
        body, name=name, grid=(t // tt,),
        in_specs=[row, row, pl.BlockSpec((1, d), lambda i: (0, 0)), row],
        out_specs=[row, row, pl.BlockSpec((8, d), lambda i: (0, 0))],
        out_shape=[jax.ShapeDtypeStruct((t, d), F32), jax.ShapeDtypeStruct((t, d), BF16),
                   jax.ShapeDtypeStruct((8, d), F32)],
        compiler_params=_params("arbitrary"),
    )(dh, x, w.reshape(1, d), dres)


def _loss_head(x, w, target, *, name):
    t, d = x.shape
    tt = _pick(t, (256, 128))

    def body(x_ref, w_ref, tg_ref, dx_ref, dxb_ref, dw_ref, ls_ref):
        xv = x_ref[...]
        wv = w_ref[...]
        r = lax.rsqrt(jnp.mean(xv * xv, axis=1, keepdims=True) + EPS)
        xhat = xv * r
        err = xhat * wv - tg_ref[...]
        dy = err * (1.0 / d)
        dxhat = dy * wv
        dx = r * (dxhat - xhat * jnp.mean(dxhat * xhat, axis=1, keepdims=True))
        dx_ref[...] = dx
        dxb_ref[...] = dx.astype(BF16)
        first = pl.program_id(0) == 0
        _accumulate(dw_ref, _fold8(dy * xhat), first)
        _accumulate(ls_ref, _fold8(err * err) * (0.5 / d), first)

    row = pl.BlockSpec((tt, d), lambda i: (i, 0))
    acc = pl.BlockSpec((8, d), lambda i: (0, 0))
    return pl.pallas_call(
        body, name=name, grid=(t // tt,),
        in_specs=[row, pl.BlockSpec((1, d), lambda i: (0, 0)), row],
        out_specs=[row, row, acc, acc],
        out_shape=[jax.ShapeDtypeStruct((t, d), F32), jax.ShapeDtypeStruct((t, d), BF16),
                   jax.ShapeDtypeStruct((8, d), F32), jax.ShapeDtypeStruct((8, d), F32)],
        compiler_params=_params("arbitrary"),
    )(x, w.reshape(1, d), target)


def _conv_causal(e, tap, width):
    acc = None
    for k in range(width):
        s = width - 1 - k
        term = (e if s == 0 else pltpu.roll(e, s, 0)) * tap(k)
        acc = term if acc is None else acc + term
    return acc


def _conv_anticausal(e, tap, width):
    rows = e.shape[0]
    acc = None
    for k in range(width):
        s = width - 1 - k
        term = (e if s == 0 else pltpu.roll(e, rows - s, 0)) * tap(k)
        acc = term if acc is None else acc + term
    return acc


def _extend(prev, cur, nxt, first, last):
    parts = []
    if prev is not None:
        parts.append(jnp.where(first, 0.0, prev.astype(F32)))
    parts.append(cur.astype(F32))
    if nxt is not None:
        parts.append(jnp.where(last, 0.0, nxt.astype(F32)))
    return jnp.concatenate(parts, axis=0)


def _prev_idx(i, tt):
    return jnp.maximum(i * (tt // HALO) - 1, 0)


def _next_idx(i, tt, t):
    return jnp.minimum((i + 1) * (tt // HALO), t // HALO - 1)


def _ffn_act_fwd(u3, cw, cb, *, name):
    _, t, f = u3.shape
    tt = _pick(t, (512, 256, 128))
    tc = _pick(f, (512, 256, 128))
    width = cw.shape[0]

    def body(u_ref, up_ref, w_ref, b_ref, o_ref):
        first = pl.program_id(1) == 0
        pre = []
        for p in range(2):
            e = _extend(up_ref[p], u_ref[p], None, first, None)
            pre.append(_conv_causal(e, lambda k: w_ref[k, p:p + 1, :], width)[HALO:] + b_ref[p:p + 1, :])
        g, v = pre
        o_ref[...] = (g * _sigmoid(g) * v).astype(BF16)

    return pl.pallas_call(
        body, name=name, grid=(f // tc, t // tt),
        in_specs=[pl.BlockSpec((2, tt, tc), lambda j, i: (0, i, j)),
                  pl.BlockSpec((2, HALO, tc), lambda j, i: (0, _prev_idx(i, tt), j)),
                  pl.BlockSpec((width, 2, tc), lambda j, i: (0, 0, j)),
                  pl.BlockSpec((2, tc), lambda j, i: (0, j))],
        out_specs=pl.BlockSpec((tt, tc), lambda j, i: (i, j)),
        out_shape=jax.ShapeDtypeStruct((t, f), BF16),
        compiler_params=_params("parallel", "parallel"),
    )(u3, u3, cw, cb)


def _ffn_act_bwd(u3, da, cw, cb, *, name):
    _, t, f = u3.shape
    tt = _pick(t, (512, 256, 128))
    tc = _pick(f, (512, 256, 128))
    width = cw.shape[0]
    nt = t // tt
    ctr = slice(HALO, HALO + tt)

    def body(u_ref, up_ref, un_ref, da_ref, dan_ref, w_ref, b_ref, du_ref, dcw_ref, dcb_ref):
        i = pl.program_id(1)
        first, last = i == 0, i == nt - 1
        ext, pre = [], []
        for p in range(2):
            e = _extend(up_ref[p], u_ref[p], un_ref[p], first, last)
            ext.append(e)
            pre.append(_conv_causal(e, lambda k: w_ref[k, p:p + 1, :], width) + b_ref[p:p + 1, :])
        g, v = pre
        sg = _sigmoid(g)
        dae = _extend(jnp.zeros((HALO, tc), F32), da_ref[...], dan_ref[...], False, last)
        dpre = (dae * v * (sg * (1.0 + g * (1.0 - sg))), dae * (g * sg))

        @pl.when(first)
        def _():
            dcw_ref[...] = jnp.zeros_like(dcw_ref)
            dcb_ref[...] = jnp.zeros_like(dcb_ref)

        for p in range(2):
            du_ref[p] = _conv_anticausal(dpre[p], lambda k: w_ref[k, p:p + 1, :], width)[ctr].astype(BF16)
            dc = dpre[p][ctr]
            dcb_ref[p:p + 1, :] += jnp.sum(dc, axis=0, keepdims=True)
            for k in range(width):
                s = width - 1 - k
                xs = (ext[p] if s == 0 else pltpu.roll(ext[p], s, 0))[ctr]
                dcw_ref[k, p:p + 1, :] += jnp.sum(dc * xs, axis=0, keepdims=True)

    return pl.pallas_call(
        body, name=name, grid=(f // tc, nt),
        in_specs=[pl.BlockSpec((2, tt, tc), lambda j, i: (0, i, j)),
                  pl.BlockSpec((2, HALO, tc), lambda j, i: (0, _prev_idx(i, tt), j)),
                  pl.BlockSpec((2, HALO, tc), lambda j, i: (0, _next_idx(i, tt, t), j)),
                  pl.BlockSpec((tt, tc), lambda j, i: (i, j)),
                  pl.BlockSpec((HALO, tc), lambda j, i: (_next_idx(i, tt, t), j)),
                  pl.BlockSpec((width, 2, tc), lambda j, i: (0, 0, j)),
                  pl.BlockSpec((2, tc), lambda j, i: (0, j))],
        out_specs=[pl.BlockSpec((2, tt, tc), lambda j, i: (0, i, j)),
                   pl.BlockSpec((width, 2, tc), lambda j, i: (0, 0, j)),
                   pl.BlockSpec((2, tc), lambda j, i: (0, j))],
        out_shape=[jax.ShapeDtypeStruct((2, t, f), BF16), jax.ShapeDtypeStruct((width, 2, f), F32),
                   jax.ShapeDtypeStruct((2, f), F32)],
        compiler_params=_params("parallel", "arbitrary"),
    )(u3, u3, u3, da, da, cw, cb)


def _sc_act_fwd(p3, cw, *, name):
    _, t, c = p3.shape
    tt = _pick(t, (512, 256, 128))
    tc = _pick(c, (512, 256, 128))
    width = cw.shape[0]

    def body(p_ref, pp_ref, w_ref, o_ref):
        first = pl.program_id(1) == 0
        q = _extend(pp_ref[1], p_ref[1], None, first, None) * _extend(pp_ref[2], p_ref[2], None, first, None)
        cq = _conv_causal(q, lambda k: w_ref[k:k + 1, :], width)[HALO:]
        o_ref[...] = (p_ref[0].astype(F32) * cq).astype(BF16)

    return pl.pallas_call(
        body, name=name, grid=(c // tc, t // tt),
        in_specs=[pl.BlockSpec((3, tt, tc), lambda j, i: (0, i, j)),
                  pl.BlockSpec((3, HALO, tc), lambda j, i: (0, _prev_idx(i, tt), j)),
                  pl.BlockSpec((width, tc), lambda j, i: (0, j))],
        out_specs=pl.BlockSpec((tt, tc), lambda j, i: (i, j)),
        out_shape=jax.ShapeDtypeStruct((t, c), BF16),
        compiler_params=_params("parallel", "parallel"),
    )(p3, p3, cw)


def _sc_act_bwd(p3, da, cw, *, name):
    _, t, c = p3.shape
    tt = _pick(t, (512, 256, 128))
    tc = _pick(c, (512, 256, 128))
    width = cw.shape[0]
    nt = t // tt
    ctr = slice(HALO, HALO + tt)

    def body(p_ref, pp_ref, pn_ref, da_ref, dan_ref, w_ref, dp_ref, dcw_ref):
        i = pl.program_id(1)
        first, last = i == 0, i == nt - 1
        tap = lambda k: w_ref[k:k + 1, :]
        bg, cg, hh = (_extend(pp_ref[p], p_ref[p], pn_ref[p], first, last) for p in range(3))
        q = cg * hh
        cq = _conv_causal(q, tap, width)
        dae = _extend(jnp.zeros((HALO, tc), F32), da_ref[...], dan_ref[...], False, last)
        dcq = dae * bg
        dq = _conv_anticausal(dcq, tap, width)[ctr]
        dp_ref[0] = (dae * cq)[ctr].astype(BF16)
        dp_ref[1] = (dq * hh[ctr]).astype(BF16)
        dp_ref[2] = (dq * cg[ctr]).astype(BF16)

        @pl.when(first)
        def _():
            dcw_ref[...] = jnp.zeros_like(dcw_ref)

        dc = dcq[ctr]
        for k in range(width):
            s = width - 1 - k
            qs = (q if s == 0 else pltpu.roll(q, s, 0))[ctr]
            dcw_ref[k:k + 1, :] += jnp.sum(dc * qs, axis=0, keepdims=True)

    return pl.pallas_call(
        body, name=name, grid=(c // tc, nt),
        in_specs=[pl.BlockSpec((3, tt, tc), lambda j, i: (0, i, j)),
                  pl.BlockSpec((3, HALO, tc), lambda j, i: (0, _prev_idx(i, tt), j)),
                  pl.BlockSpec((3, HALO, tc), lambda j, i: (0, _next_idx(i, tt, t), j)),
                  pl.BlockSpec((tt, tc), lambda j, i: (i, j)),
                  pl.BlockSpec((HALO, tc), lambda j, i: (_next_idx(i, tt, t), j)),
                  pl.BlockSpec((width, tc), lambda j, i: (0, j))],
        out_specs=[pl.BlockSpec((3, tt, tc), lambda j, i: (0, i, j)),
                   pl.BlockSpec((width, tc), lambda j, i: (0, j))],
        out_shape=[jax.ShapeDtypeStruct((3, t, c), BF16), jax.ShapeDtypeStruct((width, c), F32)],
        compiler_params=_params("parallel", "arbitrary"),
    )(p3, p3, p3, da, da, cw)


def _ssd_conv_fwd(zx, cw, cb, col0, *, name):
    t = zx.shape[0]
    width, c = cw.shape
    tt = _pick(t, (512, 256, 128))
    tc = _pick(math.gcd(c, col0), (512, 256, 128))
    off = col0 // tc

    def body(x_ref, xp_ref, w_ref, b_ref, o_ref):
        first = pl.program_id(1) == 0
        e = _extend(xp_ref[...], x_ref[...], None, first, None)
        pre = _conv_causal(e, lambda k: w_ref[k:k + 1, :], width)[HALO:] + b_ref[...]
        o_ref[...] = (pre * _sigmoid(pre)).astype(BF16)

    return pl.pallas_call(
        body, name=name, grid=(c // tc, t // tt),
        in_specs=[pl.BlockSpec((tt, tc), lambda j, i: (i, off + j)),
                  pl.BlockSpec((HALO, tc), lambda j, i: (_prev_idx(i, tt), off + j)),
                  pl.BlockSpec((width, tc), lambda j, i: (0, j)),
                  pl.BlockSpec((1, tc), lambda j, i: (0, j))],
        out_specs=pl.BlockSpec((tt, tc), lambda j, i: (i, j)),
        out_shape=jax.ShapeDtypeStruct((t, c), BF16),
        compiler_params=_params("parallel", "parallel"),
    )(zx, zx, cw, cb)


def _ssd_conv_bwd(zx, dxc, cw, cb, dzx, col0, woff, *, name):
    t = zx.shape[0]
    width = cw.shape[0]
    c = dxc.shape[1]
    tt = _pick(t, (512, 256, 128))
    tc = _pick(math.gcd(math.gcd(c, col0), woff) if woff else math.gcd(c, col0), (512, 256, 128))
    nt = t // tt
    xoff, wo = (col0 + woff) // tc, woff // tc
    ctr = slice(HALO, HALO + tt)

    def body(x_ref, xp_ref, xn_ref, d_ref, dn_ref, w_ref, b_ref, dzx_in, dzx_ref, dcw_ref, dcb_ref):
        del dzx_in
        i = pl.program_id(1)
        first, last = i == 0, i == nt - 1
        tap = lambda k: w_ref[k:k + 1, :]
        e = _extend(xp_ref[...], x_ref[...], xn_ref[...], first, last)
        pre = _conv_causal(e, tap, width) + b_ref[...]
        sg = _sigmoid(pre)
        de = _extend(jnp.zeros((HALO, tc), F32), d_ref[...], dn_ref[...], False, last)
        dpre = de * (sg * (1.0 + pre * (1.0 - sg)))
        dzx_ref[...] = _conv_anticausal(dpre, tap, width)[ctr].astype(BF16)

        @pl.when(first)
        def _():
            dcw_ref[...] = jnp.zeros_like(dcw_ref)
            dcb_ref[...] = jnp.zeros_like(dcb_ref)

        dc = dpre[ctr]
        dcb_ref[...] += jnp.sum(dc, axis=0, keepdims=True)
        for k in range(width):
            s = width - 1 - k
            xs = (e if s == 0 else pltpu.roll(e, s, 0))[ctr]
            dcw_ref[k:k + 1, :] += jnp.sum(dc * xs, axis=0, keepdims=True)

    return pl.pallas_call(
        body, name=name, grid=(c // tc, nt),
        in_specs=[pl.BlockSpec((tt, tc), lambda j, i: (i, xoff + j)),
                  pl.BlockSpec((HALO, tc), lambda j, i: (_prev_idx(i, tt), xoff + j)),
                  pl.BlockSpec((HALO, tc), lambda j, i: (_next_idx(i, tt, t), xoff + j)),
                  pl.BlockSpec((tt, tc), lambda j, i: (i, j)),
                  pl.BlockSpec((HALO, tc), lambda j, i: (_next_idx(i, tt, t), j)),
                  pl.BlockSpec((width, tc), lambda j, i: (0, wo + j)),
                  pl.BlockSpec((1, tc), lambda j, i: (0, wo + j)),
                  pl.BlockSpec(memory_space=pl.ANY)],
        out_specs=[pl.BlockSpec((tt, tc), lambda j, i: (i, xoff + j)),
                   pl.BlockSpec((width, tc), lambda j, i: (0, j)),
                   pl.BlockSpec((1, tc), lambda j, i: (0, j))],
        out_shape=[jax.ShapeDtypeStruct(dzx.shape, dzx.dtype), jax.ShapeDtypeStruct((width, c), F32),
                   jax.ShapeDtypeStruct((1, c), F32)],
        input_output_aliases={7: 0},
        compiler_params=_params("parallel", "arbitrary"),
    )(zx, zx, zx, dxc, dxc, cw, cb, dzx)


def _ssd_put_ddt(ddt_g, dzx, col, *, name):
    g, t, _ = ddt_g.shape
    tt = _pick(t, (512, 256, 128))

    def body(d_ref, dzx_in, dzx_ref):
        del dzx_in
        dzx_ref[...] = jnp.sum(d_ref[...], axis=0).astype(BF16)

    return pl.pallas_call(
        body, name=name, grid=(t // tt,),
        in_specs=[pl.BlockSpec((g, tt, LANE), lambda i: (0, i, 0)), pl.BlockSpec(memory_space=pl.ANY)],
        out_specs=pl.BlockSpec((tt, LANE), lambda i: (i, col)),
        out_shape=jax.ShapeDtypeStruct(dzx.shape, dzx.dtype),
        input_output_aliases={1: 0},
        compiler_params=_params("parallel"),
    )(ddt_g, dzx)


def _dot(a, b, mode):
    return lax.dot_general(a, b, _DIMS[mode], preferred_element_type=F32)


def _dot_exact(m01, v, mode="nn"):
    hi = v.astype(BF16)
    r1 = v - hi.astype(F32)
    mid = r1.astype(BF16)
    lo = (r1 - mid.astype(F32)).astype(BF16)
    return _dot(m01, hi, mode) + _dot(m01, mid, mode) + _dot(m01, lo, mode)


def _softplus(x):
    return jnp.maximum(x, 0.0) + jnp.log(1.0 + jnp.exp(-jnp.abs(x)))


def _head_vectors(g, dt_raw, bias, alog):
    n = CHUNK
    dt = _softplus(dt_raw + bias)
    a = -jnp.exp(alog)
    tri = (lax.broadcasted_iota(jnp.int32, (n, n), 0) >= lax.broadcasted_iota(jnp.int32, (n, n), 1)).astype(BF16)
    cs = _dot_exact(tri, dt * a)
    return dt, a, cs, cs.T


def _col(v, lane_ids, h):
    return jnp.sum(jnp.where(lane_ids == h, v, 0.0), axis=1, keepdims=True)


def _row(vt, sub_ids, h):
    return jnp.sum(jnp.where(sub_ids == h, vt, 0.0), axis=0, keepdims=True)


def _ssd_specs(di, bc, nc, rev):
    cidx = (lambda c: nc - 1 - c) if rev else (lambda c: c)
    wide = lambda off: pl.BlockSpec((CHUNK, GROUP_W), lambda g, c: (cidx(c), off + g))
    lane = lambda off: pl.BlockSpec((CHUNK, LANE), lambda g, c: (cidx(c), off + g))
    fixed = lambda off: pl.BlockSpec((CHUNK, LANE), lambda g, c: (cidx(c), off))
    vec = pl.BlockSpec((1, LANE), lambda g, c: (0, 0))
    gvec = pl.BlockSpec((1, GROUP_W), lambda g, c: (0, g))
    state = pl.BlockSpec((None, None, 4, PAIR, STATE), lambda g, c: (g, cidx(c), 0, 0, 0))
    return wide, lane, fixed, vec, gvec, state


def _ssd_fwd(xbc, zx, bias, alog, dexp, nw, *, name):
    t = xbc.shape[0]
    di = nw.shape[1]
    bc = (xbc.shape[1] - di) // 2
    ng, nc = di // GROUP_W, t // CHUNK
    wide, lane, fixed, vec, gvec, state = _ssd_specs(di, bc, nc, rev=False)

    def body(xs_ref, b_ref, c_ref, dt_ref, z_ref, bias_ref, alog_ref, dexp_ref, nw_ref,
             yn_ref, y_ref, st_ref, s_scr):
        g, c = pl.program_id(0), pl.program_id(1)

        @pl.when(c == 0)
        def _():
            s_scr[...] = jnp.zeros_like(s_scr)

        n = CHUNK
        dt, a, cs, cst = _head_vectors(g, dt_ref[...].astype(F32), bias_ref[...], alog_ref[...])
        lane_ids = lax.broadcasted_iota(jnp.int32, (n, LANE), 1)
        sub_ids = lax.broadcasted_iota(jnp.int32, (LANE, n), 0)
        causal = lax.broadcasted_iota(jnp.int32, (n, n), 0) >= lax.broadcasted_iota(jnp.int32, (n, n), 1)
        half = lax.broadcasted_iota(jnp.int32, (1, PAIR), 1) < HEAD_DIM
        half_rows = lax.broadcasted_iota(jnp.int32, (PAIR, 1), 0) < HEAD_DIM
        bm, cm = b_ref[...], c_ref[...]
        gm = _dot(cm, bm, "nt")
        x = xs_ref[...].astype(F32)
        ys = []
        for q in range(4):
            h0 = g * 8 + 2 * q
            col = [_col(cs, lane_ids, h0 + e) for e in range(2)]
            row = [_row(cst, sub_ids, h0 + e) for e in range(2)]
            dtc = [_col(dt, lane_ids, h0 + e) for e in range(2)]
            last = [col[e][n - 1:n, :] for e in range(2)]
            xd = x[:, q * PAIR:(q + 1) * PAIR] * jnp.where(half, dtc[0], dtc[1])
            xd_bf = xd.astype(BF16)
            yd = []
            for e in range(2):
                lm = jnp.exp(jnp.where(causal, col[e] - row[e], -1e30))
                yd.append(_dot((gm * lm).astype(BF16), xd_bf, "nn"))
            s = s_scr[q]
            st_ref[q] = s
            ecs = jnp.where(half, jnp.exp(col[0]), jnp.exp(col[1]))
            dte = jnp.where(half, jnp.exp(last[0] - col[0]), jnp.exp(last[1] - col[1]))
            yoff = ecs * _dot(cm, s.astype(BF16), "nt")
            snew = _dot((xd * dte).astype(BF16), bm, "tn")
            s_scr[q] = s * jnp.where(half_rows, jnp.exp(last[0]), jnp.exp(last[1])) + snew
            ys.append(jnp.where(half, yd[0], yd[1]) + yoff)
        y = jnp.concatenate(ys, axis=1) + dexp_ref[...] * x
        y_ref[...] = y.astype(BF16)
        z = z_ref[...].astype(F32)
        yg = y * (z * _sigmoid(z))
        r = lax.rsqrt(jnp.mean(yg * yg, axis=1, keepdims=True) + EPS)
        yn_ref[...] = (yg * r * nw_ref[...]).astype(BF16)

    dtcol = (2 * di + 2 * bc) // LANE
    return pl.pallas_call(
        body, name=name, grid=(ng, nc),
        in_specs=[wide(0), lane(di // LANE), lane((di + bc) // LANE), fixed(dtcol), wide(0),
                  vec, vec, gvec, gvec],
        out_specs=[wide(0), wide(0), state],
        out_shape=[jax.ShapeDtypeStruct((t, di), BF16), jax.ShapeDtypeStruct((t, di), BF16),
                   jax.ShapeDtypeStruct((ng, nc, 4, PAIR, STATE), F32)],
        scratch_shapes=[pltpu.VMEM((4, PAIR, STATE), F32)],
        compiler_params=_params("parallel", "arbitrary"),
    )(xbc, xbc, xbc, zx, zx, bias, alog, dexp, nw)


def _ssd_bwd(dyn, y, xbc, zx, states, bias, alog, dexp, nw, *, name):
    t = xbc.shape[0]
    di = nw.shape[1]
    bc = (xbc.shape[1] - di) // 2
    ng, nc = di // GROUP_W, t // CHUNK
    wide, lane, fixed, vec, gvec, state = _ssd_specs(di, bc, nc, rev=True)
    acc = lambda w: pl.BlockSpec((None, 8, w), lambda g, c: (g, 0, 0))

    def body(dyn_ref, y_ref, z_ref, nw_ref, xs_ref, b_ref, c_ref, dt_ref, bias_ref, alog_ref, dexp_ref, st_ref,
             dz_ref, dxs_ref, db_ref, dc_ref, ddt_ref, small_ref, dnw_ref, ddexp_ref, ds_scr):
        g, c = pl.program_id(0), pl.program_id(1)

        @pl.when(c == 0)
        def _():
            ds_scr[...] = jnp.zeros_like(ds_scr)
            small_ref[...] = jnp.zeros_like(small_ref)
            dnw_ref[...] = jnp.zeros_like(dnw_ref)
            ddexp_ref[...] = jnp.zeros_like(ddexp_ref)

        n = CHUNK
        yv = y_ref[...].astype(F32)
        z = z_ref[...].astype(F32)
        sz = _sigmoid(z)
        silu = z * sz
        yg = yv * silu
        r = lax.rsqrt(jnp.mean(yg * yg, axis=1, keepdims=True) + EPS)
        yhat = yg * r
        dynv = dyn_ref[...].astype(F32)
        dnw_ref[0:1, :] += jnp.sum(dynv * yhat, axis=0, keepdims=True)
        dyhat = dynv * nw_ref[...]
        dyg = r * (dyhat - yhat * jnp.mean(dyhat * yhat, axis=1, keepdims=True))
        dz_ref[...] = (dyg * yv * (sz * (1.0 + z * (1.0 - sz)))).astype(BF16)
        dy = dyg * silu

        dt_in = dt_ref[...].astype(F32) + bias_ref[...]
        dt, a, cs, cst = _head_vectors(g, dt_ref[...].astype(F32), bias_ref[...], alog_ref[...])
        lane_ids = lax.broadcasted_iota(jnp.int32, (n, LANE), 1)
        sub_ids = lax.broadcasted_iota(jnp.int32, (LANE, n), 0)
        ri = lax.broadcasted_iota(jnp.int32, (n, n), 0)
        ci = lax.broadcasted_iota(jnp.int32, (n, n), 1)
        causal, causal_t = ri >= ci, ci >= ri
        is_last = lax.broadcasted_iota(jnp.int32, (n, 1), 0) == n - 1
        half = lax.broadcasted_iota(jnp.int32, (1, PAIR), 1) < HEAD_DIM
        half_rows = lax.broadcasted_iota(jnp.int32, (PAIR, 1), 0) < HEAD_DIM
        bm, cm = b_ref[...], c_ref[...]
        bf = bm.astype(F32)
        gm, gmt = _dot(cm, bm, "nt"), _dot(bm, cm, "nt")
        x = xs_ref[...].astype(F32)
        dexp = dexp_ref[...]

        dg_sum = jnp.zeros((n, n), F32)
        dgt_sum = jnp.zeros((n, n), F32)
        db_off = jnp.zeros((n, STATE), F32)
        dc_off = jnp.zeros((n, STATE), F32)
        dcs_blk = jnp.zeros((n, LANE), F32)
        ddt_blk = jnp.zeros((n, LANE), F32)
        dxs = []
        for q in range(4):
            h0 = g * 8 + 2 * q
            sl = slice(q * PAIR, (q + 1) * PAIR)
            col = [_col(cs, lane_ids, h0 + e) for e in range(2)]
            row = [_row(cst, sub_ids, h0 + e) for e in range(2)]
            dtc = [_col(dt, lane_ids, h0 + e) for e in range(2)]
            last = [col[e][n - 1:n, :] for e in range(2)]
            xp, dyp = x[:, sl], dy[:, sl]
            dtp = jnp.where(half, dtc[0], dtc[1])
            xd = xp * dtp
            xd_bf, dyp_bf = xd.astype(BF16), dyp.astype(BF16)
            ecs = jnp.where(half, jnp.exp(col[0]), jnp.exp(col[1]))
            dte = jnp.where(half, jnp.exp(last[0] - col[0]), jnp.exp(last[1] - col[1]))
            s, ds = st_ref[q], ds_scr[q]
            s_bf, ds_bf = s.astype(BF16), ds.astype(BF16)
            yoff = ecs * _dot(cm, s_bf, "nt")
            edy_bf = (ecs * dyp).astype(BF16)
            dc_off += _dot(edy_bf, s_bf, "nn")
            bds = _dot(bm, ds_bf, "nt")
            sds = s * ds
            zs = []
            for e in range(2):
                msk = half if e == 0 else jnp.logical_not(half)
                msk_rows = half_rows if e == 0 else jnp.logical_not(half_rows)
                lm = jnp.exp(jnp.where(causal, col[e] - row[e], -1e30))
                lmt = jnp.exp(jnp.where(causal_t, row[e] - col[e], -1e30))
                dym_bf = jnp.where(msk, dyp, 0.0).astype(BF16)
                xdm_bf = jnp.where(msk, xd, 0.0).astype(BF16)
                dm = _dot(dym_bf, xd_bf, "nt")
                dmt = _dot(xdm_bf, dyp_bf, "nt")
                m, mt = gm * lm, gmt * lmt
                dcs = jnp.sum(dm * m, axis=1, keepdims=True) - jnp.sum(dmt * mt, axis=1, keepdims=True)
                dg_sum += dm * lm
                dgt_sum += dmt * lmt
                zs.append(_dot(mt.astype(BF16), dyp_bf, "nn"))
                we = _dot(xdm_bf, ds_bf, "nn")
                dte_col = jnp.exp(last[e] - col[e])
                te = dte_col * jnp.sum(we * bf, axis=1, keepdims=True)
                db_off += dte_col * we
                dcs += jnp.sum(jnp.where(msk, dyp * yoff, 0.0), axis=1, keepdims=True) - te
                tail = jnp.exp(last[e]) * jnp.sum(jnp.where(msk_rows, sds, 0.0), keepdims=True) \
                    + jnp.sum(te, keepdims=True)
                dcs += jnp.where(is_last, tail, 0.0)
                dcs_blk += jnp.where(lane_ids == h0 + e, dcs, 0.0)
            dxd = jnp.where(half, zs[0], zs[1]) + dte * bds
            dxs.append(dxd * dtp + dexp[:, sl] * dyp)
            ddexp_ref[0:1, sl] += jnp.sum(dyp * xp, axis=0, keepdims=True)
            rs = dxd * xp
            for e in range(2):
                msk = half if e == 0 else jnp.logical_not(half)
                ddt_blk += jnp.where(lane_ids == h0 + e, jnp.sum(jnp.where(msk, rs, 0.0), axis=1, keepdims=True), 0.0)
            ds_scr[q] = ds * jnp.where(half_rows, jnp.exp(last[0]), jnp.exp(last[1])) + _dot(edy_bf, cm, "tn")

        dxs_ref[...] = jnp.concatenate(dxs, axis=1).astype(BF16)
        dc_ref[...] = (_dot(dg_sum.astype(BF16), bm, "nn") + dc_off).astype(BF16)
        db_ref[...] = (_dot(dgt_sum.astype(BF16), cm, "nn") + db_off).astype(BF16)
        upper = (ri <= ci).astype(BF16)
        dda = _dot_exact(upper, dcs_blk)
        ddt = dda * a + ddt_blk
        small_ref[0:1, :] += jnp.sum(dda * dt, axis=0, keepdims=True) * a
        ddt_raw = ddt * _sigmoid(dt_in)
        small_ref[1:2, :] += jnp.sum(ddt_raw, axis=0, keepdims=True)
        ddt_ref[...] = ddt_raw

    dtcol = (2 * di + 2 * bc) // LANE
    tot = 2 * di + 2 * bc + LANE
    return pl.pallas_call(
        body, name=name, grid=(ng, nc),
        in_specs=[wide(0), wide(0), wide(0), gvec, wide(0), lane(di // LANE), lane((di + bc) // LANE),
                  fixed(dtcol), vec, vec, gvec, state],
        out_specs=[wide(0), wide(0), lane(0), lane(0),
                   pl.BlockSpec((None, CHUNK, LANE), lambda g, c: (g, nc - 1 - c, 0)),
                   acc(LANE), acc(GROUP_W), acc(GROUP_W)],
        out_shape=[jax.ShapeDtypeStruct((t, tot), BF16), jax.ShapeDtypeStruct((t, di), BF16),
                   jax.ShapeDtypeStruct((t, bc), BF16), jax.ShapeDtypeStruct((t, bc), BF16),
                   jax.ShapeDtypeStruct((ng, t, LANE), F32), jax.ShapeDtypeStruct((ng, 8, LANE), F32),
                   jax.ShapeDtypeStruct((ng, 8, GROUP_W), F32), jax.ShapeDtypeStruct((ng, 8, GROUP_W), F32)],
        scratch_shapes=[pltpu.VMEM((4, PAIR, STATE), F32)],
        compiler_params=_params("parallel", "arbitrary"),
    )(dyn, y, zx, nw, xbc, xbc, xbc, zx, bias, alog, dexp, states)


HBM_ANY = pl.BlockSpec(memory_space=pl.ANY)


def _place():
    x, y, c = lax.axis_index("x"), lax.axis_index("y"), lax.axis_index("c")
    chips = [(1 - x, y), (x, 1 - y), (1 - x, 1 - y)]
    return x, y, c, chips


def _all_gather(arrs, *, name, inplace=False):
    n = len(arrs)

    def body(*refs):
        ins, outs = refs[:n], refs[n:2 * n]
        send, recv, loc = refs[2 * n:]
        x, y, c, chips = _place()
        me, sib = (x, y, c), (x, y, 1 - c)

        def blk(a, p):
            return outs[a].at[4 * p[0] + 2 * p[1] + p[2]]

        def cp(a, k, block, to, src=None):
            return pltpu.make_async_remote_copy(
                src_ref=blk(a, block) if src is None else src, dst_ref=blk(a, block),
                send_sem=send.at[a * 7 + k], recv_sem=recv.at[a * 7 + k], device_id=to, device_id_type=MESH)

        src = [None if inplace else ins[a] for a in range(n)]
        mine = [] if inplace else [pltpu.make_async_copy(ins[a], blk(a, me), loc.at[a]) for a in range(n)]
        for m in mine:
            m.start()
        started = []
        for a in range(n):
            started.append(cp(a, 0, me, sib, src=src[a]))
            started += [cp(a, 1 + j, me, (*chip, c), src=src[a]) for j, chip in enumerate(chips)]
        for s in started:
            s.start()
        for j, chip in enumerate(chips):
            for a in range(n):
                cp(a, 1 + j, (*chip, c), me).wait_recv()
                fwd = cp(a, 4 + j, (*chip, c), sib)
                fwd.start()
                started.append(fwd)
        for a in range(n):
            cp(a, 0, sib, me).wait_recv()
            for j, chip in enumerate(chips):
                cp(a, 4 + j, (*chip, 1 - c), me).wait_recv()
        for s in started:
            s.wait_send()
        for m in mine:
            m.wait()

    return pl.pallas_call(
        body, name=name,
        in_specs=[HBM_ANY] * n, out_specs=[HBM_ANY] * n,
        out_shape=[jax.ShapeDtypeStruct(a.shape if inplace else (N_DEV,) + a.shape, a.dtype) for a in arrs],
        input_output_aliases={a: a for a in range(n)} if inplace else {},
        scratch_shapes=[pltpu.SemaphoreType.DMA((7 * n,)), pltpu.SemaphoreType.DMA((7 * n,)),
                        pltpu.SemaphoreType.DMA((n,))],
    )(*arrs)


HBM_SPEC = pl.BlockSpec(memory_space=pltpu.HBM)
SEM_SPEC = pl.BlockSpec(memory_space=pltpu.SEMAPHORE)
SPLIT_EFFECT = pltpu.SideEffectType.DATAFLOW_SIDE_EFFECTING


def _split_start(arrs, plan, n_copies, after, *, name):
    m = len(arrs)

    def body(*refs):
        send, recv, token = refs[m + 1], refs[m + 2], refs[-1]
        for i, (src, dst, to) in enumerate(plan(refs[:m])):
            pltpu.make_async_remote_copy(src_ref=src, dst_ref=dst, send_sem=send.at[i], recv_sem=recv.at[i],
                                         device_id=to, device_id_type=MESH).start()
        token[...] = jnp.zeros_like(token)

    outs = pl.pallas_call(
        body, name=name,
        out_shape=(pltpu.SemaphoreType.DMA((n_copies,)), pltpu.SemaphoreType.DMA((n_copies,)),
                   *[pltpu.HBM(a.shape, a.dtype) for a in arrs], jax.ShapeDtypeStruct((8, LANE), F32)),
        in_specs=[HBM_SPEC] * m + [HBM_ANY],
        out_specs=(SEM_SPEC, SEM_SPEC, *[HBM_SPEC] * m, pl.BlockSpec(memory_space=pltpu.VMEM)),
        input_output_aliases={i: 2 + i for i in range(m)},
        compiler_params=pltpu.CompilerParams(has_side_effects=SPLIT_EFFECT),
    )(*[pltpu.with_memory_space_constraint(a, pltpu.HBM) for a in arrs], after)
    return outs[0], outs[1], list(outs[2:2 + m]), outs[-1]


def _split_wait(arrs, send, recv, after, plan, *, name):
    m = len(arrs)

    def body(*refs):
        send_ref, recv_ref = refs[m], refs[m + 1]
        for i, (src, dst, to) in enumerate(plan(refs[:m])):
            cp = pltpu.make_async_remote_copy(src_ref=src, dst_ref=dst, send_sem=send_ref.at[i],
                                              recv_sem=recv_ref.at[i], device_id=to, device_id_type=MESH)
            cp.wait_send()
            cp.wait_recv()

    outs = pl.pallas_call(
        body, name=name,
        out_shape=[pltpu.HBM(a.shape, a.dtype) for a in arrs],
        in_specs=[HBM_SPEC] * m + [SEM_SPEC, SEM_SPEC, HBM_ANY], out_specs=[HBM_SPEC] * m,
        input_output_aliases={i: i for i in range(m)},
        compiler_params=pltpu.CompilerParams(has_side_effects=SPLIT_EFFECT),
    )(*arrs, send, recv, after)
    return list(outs)


def _dev(p):
    return 4 * p[0] + 2 * p[1] + p[2]


def _plan_gather_ici(bufs):
    x, y, c, chips = _place()
    me = _dev((x, y, c))
    peers = [(x, y, 1 - c)] + [(*chip, c) for chip in chips]
    return [(b.at[me], b.at[me], p) for b in bufs for p in peers]


def _plan_gather_d2d(bufs):
    x, y, c, chips = _place()
    return [(b.at[_dev((*chip, c))], b.at[_dev((*chip, c))], (x, y, 1 - c)) for b in bufs for chip in chips]


def _plan_pair(refs):
    n = len(refs) // 2
    x, y, c, _ = _place()
    return [(refs[a].at[2 * k + 1 - c], refs[n + a].at[k], (x, y, 1 - c)) for a in range(n) for k in range(N_CHIP)]


def _plan_chip(refs):
    n = len(refs) // 2
    x, y, c, chips = _place()
    return [(refs[a].at[2 * chip[0] + chip[1]], refs[n + a].at[j], (*chip, c))
            for a in range(n) for j, chip in enumerate(chips)]


def _land(shape, dtype):
    return lax.empty(shape, dtype)


def _with_tokens(v, *tokens):
    for t in tokens:
        if t is not None:
            v = v + t[0, 0].astype(v.dtype)
    return v


def _pair_exchange(grads, *, name):
    n = len(grads)

    def body(*refs):
        ins, gots = refs[:n], refs[n:2 * n]
        send, recv = refs[2 * n:]
        x, y, c, _ = _place()
        copies = []
        for a in range(n):
            for k in range(N_CHIP):
                copies.append(pltpu.make_async_remote_copy(
                    src_ref=ins[a].at[2 * k + 1 - c], dst_ref=gots[a].at[k],
                    send_sem=send.at[a * N_CHIP + k], recv_sem=recv.at[a * N_CHIP + k],
                    device_id=(x, y, 1 - c), device_id_type=MESH))
        for cpy in copies:
            cpy.start()
        for cpy in copies:
            cpy.wait()

    return pl.pallas_call(
        body, name=name,
        in_specs=[HBM_ANY] * n, out_specs=[HBM_ANY] * n,
        out_shape=[jax.ShapeDtypeStruct((N_CHIP,) + g.shape[1:], g.dtype) for g in grads],
        scratch_shapes=[pltpu.SemaphoreType.DMA((N_CHIP * n,)), pltpu.SemaphoreType.DMA((N_CHIP * n,))],
    )(*grads)


def _chip_exchange(sums, *, name):
    n = len(sums)

    def body(*refs):
        ins, outs = refs[:n], refs[n:2 * n]
        send, recv = refs[2 * n:]
        x, y, c, chips = _place()
        copies = []
        for a in range(n):
            for j, chip in enumerate(chips):
                copies.append(pltpu.make_async_remote_copy(
                    src_ref=ins[a].at[2 * chip[0] + chip[1]], dst_ref=outs[a].at[j],
                    send_sem=send.at[a * 3 + j], recv_sem=recv.at[a * 3 + j],
                    device_id=(*chip, c), device_id_type=MESH))
        for cpy in copies:
            cpy.start()
        for cpy in copies:
            cpy.wait()

    return pl.pallas_call(
        body, name=name,
        in_specs=[HBM_ANY] * n, out_specs=[HBM_ANY] * n,
        out_shape=[jax.ShapeDtypeStruct((N_CHIP - 1,) + s.shape[1:], s.dtype) for s in sums],
        scratch_shapes=[pltpu.SemaphoreType.DMA((3 * n,)), pltpu.SemaphoreType.DMA((3 * n,))],
    )(*sums)


def _add_pair(grad, got, core, *, name):
    k, r, c = got.shape
    tr = _pick(r, (256, 128, 64, 32, 16))

    def body(core_ref, a_ref, b_ref, o_ref):
        del core_ref
        o_ref[...] = (a_ref[...].astype(F32) + b_ref[...].astype(F32)).astype(BF16)

    spec = pl.BlockSpec((None, tr, c), lambda q, i, core_ref: (q, i, 0))
    return pl.pallas_call(
        body, name=name,
        grid_spec=pltpu.PrefetchScalarGridSpec(
            num_scalar_prefetch=1, grid=(k, r // tr),
            in_specs=[pl.BlockSpec((None, tr, c), lambda q, i, core_ref: (2 * q + core_ref[0], i, 0)), spec],
            out_specs=spec),
        out_shape=jax.ShapeDtypeStruct(got.shape, BF16), compiler_params=_params("parallel", "parallel"),
    )(core, grad, got)


def _all_reduce_small(v, *, name):
    r = v.shape[0]

    def body(v_ref, o_ref, buf, send, recv):
        x, y, c, _ = _place()
        me = 4 * x + 2 * y + c
        buf[me] = v_ref[...]
        copies = []
        for rel in range(1, N_DEV):
            fx, fy, fc = rel >> 2 & 1, rel >> 1 & 1, rel & 1
            peer = ((1 - x) if fx else x, (1 - y) if fy else y, (1 - c) if fc else c)
            copies.append(pltpu.make_async_remote_copy(
                src_ref=v_ref, dst_ref=buf.at[me], send_sem=send.at[rel - 1], recv_sem=recv.at[rel - 1],
                device_id=peer, device_id_type=MESH))
        for cpy in copies:
            cpy.start()
        for cpy in copies:
            cpy.wait()
        acc = buf[0]
        for d in range(1, N_DEV):
            acc = acc + buf[d]
        o_ref[...] = acc

    return pl.pallas_call(
        body, name=name,
        in_specs=[pl.BlockSpec(memory_space=pltpu.VMEM)], out_specs=pl.BlockSpec(memory_space=pltpu.VMEM),
        out_shape=jax.ShapeDtypeStruct(v.shape, F32),
        scratch_shapes=[pltpu.VMEM((N_DEV, r, LANE), F32), pltpu.SemaphoreType.DMA((N_DEV - 1,)),
                        pltpu.SemaphoreType.DMA((N_DEV - 1,))],
        compiler_params=pltpu.CompilerParams(vmem_limit_bytes=VMEM_LIMIT),
    )(v)


def _adamw_math(w, g, m, v):
    m = ADAM_B1 * m + (1.0 - ADAM_B1) * g
    v = ADAM_B2 * v + (1.0 - ADAM_B2) * (g * g)
    m_hat = m / (1.0 - ADAM_B1 ** ADAM_STEP)
    v_hat = v / (1.0 - ADAM_B2 ** ADAM_STEP)
    delta = -ADAM_LR * (m_hat / (jnp.sqrt(v_hat) + ADAM_EPS) + ADAM_WD * w)
    return delta, m, v


def _adamw_layer(w, m, v, sums, recv, chip, layer, prev, *, name):
    nl, r, c = w.shape
    tr = _pick(r, (256, 128, 64, 32, 16))

    def body(chip_ref, w_ref, m_ref, v_ref, s_ref, p_ref, *rest):
        del chip_ref
        g_ref, d_ref, mo_ref, vo_ref = rest[-4:]
        g = s_ref[...].astype(F32)
        for k in range(N_CHIP - 1):
            g = g + p_ref[k].astype(F32)
        delta, mn, vn = _adamw_math(w_ref[...], g, m_ref[...], v_ref[...])
        g_ref[...] = g
        d_ref[...] = delta
        mo_ref[...] = mn
        vo_ref[...] = vn

    lay = pl.BlockSpec((None, tr, c), lambda i, chip_ref: (layer, i, 0))
    ins = [w, m, v, sums, recv] + (list(prev) if prev is not None else [])
    in_specs = [lay, lay, lay, pl.BlockSpec((None, tr, c), lambda i, chip_ref: (chip_ref[0], i, 0)),
                pl.BlockSpec((N_CHIP - 1, tr, c), lambda i, chip_ref: (0, i, 0))]
    in_specs += [HBM_ANY] * (4 if prev is not None else 0)
    return pl.pallas_call(
        body, name=name,
        grid_spec=pltpu.PrefetchScalarGridSpec(
            num_scalar_prefetch=1, grid=(r // tr,), in_specs=in_specs, out_specs=[lay] * 4),
        out_shape=[jax.ShapeDtypeStruct(w.shape, F32)] * 4,
        input_output_aliases={6 + q: q for q in range(4)} if prev is not None else {},
        compiler_params=_params("parallel"),
    )(chip, *ins)


def _adamw_small(w, g, m, v, *, name):
    def body(w_ref, g_ref, m_ref, v_ref, d_ref, mo_ref, vo_ref):
        d_ref[...], mo_ref[...], vo_ref[...] = _adamw_math(w_ref[...], g_ref[...], m_ref[...], v_ref[...])

    vm = pl.BlockSpec(memory_space=pltpu.VMEM)
    return pl.pallas_call(
        body, name=name, in_specs=[vm] * 4, out_specs=[vm] * 3,
        out_shape=[jax.ShapeDtypeStruct(w.shape, F32)] * 3,
        compiler_params=pltpu.CompilerParams(vmem_limit_bytes=VMEM_LIMIT),
    )(w, g, m, v)


def _pack(arrs):
    flat = jnp.concatenate([a.reshape(-1).astype(F32) for a in arrs])
    pad = (-flat.shape[0]) % (8 * LANE)
    return jnp.pad(flat, (0, pad)).reshape(-1, LANE)


def _unpack(packed, shapes):
    flat = packed.reshape(-1)
    out, off = [], 0
    for s in shapes:
        size = math.prod(s)
        out.append(flat[off:off + size].reshape(s))
        off += size
    return out


WEIGHTS = ['mix_norm_w', 'ffn_norm_w', 'final_norm_w', 'ssd_w_in', 'ssd_conv_w', 'ssd_conv_b', 'ssd_dt_bias',
           'ssd_a_log', 'ssd_d', 'ssd_norm_w', 'ssd_w_out', 'sc_w_in', 'sc_conv_w', 'sc_w_out', 'ffn_w_up',
           'ffn_conv_w', 'ffn_conv_b', 'ffn_w_down']
BIG = ('ssd_w_in', 'ssd_w_out', 'sc_w_in', 'sc_w_out', 'ffn_w_up', 'ffn_w_down')
SHARDED_SMALL = ('ssd_conv_w', 'sc_conv_w', 'ffn_conv_w')


def _lane_pad(v):
    return jnp.pad(v.astype(F32), (0, LANE - v.shape[0])).reshape(1, LANE)


def _gather_cols(g):
    return jnp.moveaxis(g, 0, -2).reshape(g.shape[1:-1] + (N_DEV * g.shape[-1],))


class _Gather:
    def __init__(self, bufs, tag):
        self.bufs, self.tag = bufs, tag

    def start_ici(self, after):
        self.sems = _split_start(self.bufs, _plan_gather_ici, 4 * len(self.bufs), after, name=f"ag_ici_start_{self.tag}")
        return self.sems[3]

    def hand_on(self, after):
        send, recv, bufs, _ = self.sems
        bufs = _split_wait(bufs, send, recv, after, _plan_gather_ici, name=f"ag_ici_wait_{self.tag}")
        self.sems = _split_start(bufs, _plan_gather_d2d, 3 * len(bufs), after, name=f"ag_d2d_start_{self.tag}")
        return self.sems[3]

    def finish(self, after):
        send, recv, bufs, _ = self.sems
        return _split_wait(bufs, send, recv, after, _plan_gather_d2d, name=f"ag_d2d_wait_{self.tag}")


class _Scatter:
    def __init__(self, grads, core, tag):
        self.grads, self.core, self.tag = grads, core, tag

    def start_pair(self, after):
        lands = [_land((N_CHIP,) + g.shape[1:], g.dtype) for g in self.grads]
        self.sems = _split_start(self.grads + lands, _plan_pair, N_CHIP * len(lands), after,
                                 name=f"rs_pair_start_{self.tag}")
        return self.sems[3]

    def start_chip(self, after):
        n = len(self.grads)
        send, recv, arrs, _ = self.sems
        arrs = _split_wait(arrs, send, recv, after, _plan_pair, name=f"rs_pair_wait_{self.tag}")
        self.sums = [_add_pair(g, o, self.core, name=f"rs_add_{self.tag}{a}")
                     for a, (g, o) in enumerate(zip(arrs[:n], arrs[n:]))]
        lands = [_land((N_CHIP - 1,) + s.shape[1:], s.dtype) for s in self.sums]
        self.sems = _split_start(self.sums + lands, _plan_chip, (N_CHIP - 1) * n, after,
                                 name=f"rs_chip_start_{self.tag}")
        return self.sems[3]

    def finish(self, after):
        n = len(self.grads)
        send, recv, arrs, _ = self.sems
        arrs = _split_wait(arrs, send, recv, after, _plan_chip, name=f"rs_chip_wait_{self.tag}")
        return list(zip(arrs[:n], arrs[n:]))


def kernel(x, mix_norm_w, ffn_norm_w, final_norm_w, ssd_w_in, ssd_conv_w, ssd_conv_b, ssd_dt_bias, ssd_a_log, ssd_d, ssd_norm_w, ssd_w_out, sc_w_in, sc_conv_w, sc_w_out, ffn_w_up, ffn_conv_w, ffn_conv_b, ffn_w_down, loss_target, m_mix_norm_w, m_ffn_norm_w, m_final_norm_w, m_ssd_w_in, m_ssd_conv_w, m_ssd_conv_b, m_ssd_dt_bias, m_ssd_a_log, m_ssd_d, m_ssd_norm_w, m_ssd_w_out, m_sc_w_in, m_sc_conv_w, m_sc_w_out, m_ffn_w_up, m_ffn_conv_w, m_ffn_conv_b, m_ffn_w_down, v_mix_norm_w, v_ffn_norm_w, v_final_norm_w, v_ssd_w_in, v_ssd_conv_w, v_ssd_conv_b, v_ssd_dt_bias, v_ssd_a_log, v_ssd_d, v_ssd_norm_w, v_ssd_w_out, v_sc_w_in, v_sc_conv_w, v_sc_w_out, v_ffn_w_up, v_ffn_conv_w, v_ffn_conv_b, v_ffn_w_down):
    args = locals()
    wt = {n: args[n] for n in WEIGHTS}
    mom = {n: args["m_" + n] for n in WEIGHTS}
    var = {n: args["v_" + n] for n in WEIGHTS}

    t, d = x.shape[-2], x.shape[-1]
    cur = x.reshape(t, d)
    target = loss_target.reshape(t, d)
    depth = mix_norm_w.shape[0]
    n_ssd, n_sc = ssd_w_in.shape[0], sc_w_in.shape[0]
    heads = ssd_dt_bias.shape[1]
    di = ssd_norm_w.shape[1]
    conv_dim = ssd_conv_b.shape[1]
    bc = (conv_dim - di) // 2
    in_dim = N_DEV * ssd_w_in.shape[2]
    in_pad = di + conv_dim + LANE
    ff = ffn_w_down.shape[1] * N_DEV
    me = 4 * lax.axis_index("x") + 2 * lax.axis_index("y") + lax.axis_index("c")
    me_s = me.astype(jnp.int32).reshape(1)
    core_s = lax.axis_index("c").astype(jnp.int32).reshape(1)
    chip_s = (2 * lax.axis_index("x") + lax.axis_index("y")).astype(jnp.int32).reshape(1)

    names_of = {"ssd": ('ssd_w_in', 'ssd_w_out'), "sc": ('sc_w_in', 'sc_w_out'), "ffn": ('ffn_w_up', 'ffn_w_down')}
    order = []
    for i in range(depth):
        order += [("ssd" if i % 2 == 0 else "sc", i // 2), ("ffn", i)]
    gathers = [_Gather([_cast_layer(wt[n], idx, me_s, name=f"cast_{n}{idx}") for n in names_of[kind]], f"{kind}{idx}")
               for kind, idx in order]
    conv_full = [_gather_cols(g) for g in _all_gather([wt[n] for n in SHARDED_SMALL], name="ag_conv")]
    tok_a = gathers[0].start_ici(conv_full[0])
    tok_b = gathers[0].hand_on(tok_a)
    tok_c = gathers[1].start_ici(tok_b)
    weights = [None] * len(order)
    weights[0] = gathers[0].finish(tok_c)
    ssd_cw, sc_cw, ffn_cw = conv_full
    ffn_cw = ffn_cw.reshape(depth, ffn_cw.shape[1], 2, ff)
    ffn_cb = ffn_conv_b.reshape(depth, 2, ff)
    dexp = jnp.repeat(ssd_d.astype(F32), HEAD_DIM, axis=1)

    n_sub = len(order)
    full = {n: [None] * wt[n].shape[0] for n in BIG}

    def prefetch(s, after):
        return gathers[s + 2].start_ici(after) if s + 2 < n_sub else None

    def hand_on(s, after):
        return gathers[s + 1].hand_on(after) if s + 1 < n_sub else None

    def arrive(s, after):
        if s + 1 < n_sub:
            weights[s + 1] = gathers[s + 1].finish(after)

    def use(s):
        kind, idx = order[s]
        g_in, g_out = weights[s]
        if kind == "ssd":
            w_in = jnp.swapaxes(g_in, 0, 1).reshape(d, in_dim)
            g_in = jnp.pad(w_in, ((0, 0), (0, in_pad - in_dim)))
        n_in, n_out = names_of[kind]
        full[n_in][idx], full[n_out][idx] = g_in, g_out.reshape(-1, d)

    saved = []
    for i in range(depth):
        j = i // 2
        s = 2 * i
        use(s)
        rec = {"x_mix": cur}
        tok = prefetch(s, cur)
        h = _rmsnorm_fwd(cur, _with_tokens(mix_norm_w[i], tok, tok_c if i == 0 else None), name=f"norm_mix{i}")
        rec["h_mix"] = h
        if i % 2 == 0:
            zx = _mm_nn(h, full['ssd_w_in'][j], out_dtype=BF16, name=f"ssd_in{j}")
            cb = _with_tokens(ssd_conv_b[j].reshape(1, conv_dim), hand_on(s, zx))
            xbc = _ssd_conv_fwd(zx, ssd_cw[j], cb, di, name=f"ssd_conv{j}")
            ssd_vecs = (_lane_pad(ssd_dt_bias[j]), _lane_pad(ssd_a_log[j]), dexp[j].reshape(1, di),
                        ssd_norm_w[j].reshape(1, di))
            yn, y, states = _ssd_fwd(xbc, zx, *ssd_vecs, name=f"ssd_core{j}")
            cur = _mm_nn(yn, full['ssd_w_out'][j], res=cur, out_dtype=F32, name=f"ssd_out{j}")
            rec.update(zx=zx, xbc=xbc, yn=yn, y=y, states=states, vecs=ssd_vecs)
        else:
            p3 = _lin_in_fwd(h, full['sc_w_in'][j], 3, name=f"sc_in{j}")
            act = _sc_act_fwd(p3, _with_tokens(sc_cw[j], hand_on(s, p3)), name=f"sc_act{j}")
            cur = _mm_nn(act, full['sc_w_out'][j], res=cur, out_dtype=F32, name=f"sc_out{j}")
            rec.update(p3=p3, act=act)
        arrive(s, cur)
        s += 1
        use(s)
        rec["x_ffn"] = cur
        h = _rmsnorm_fwd(cur, _with_tokens(ffn_norm_w[i], prefetch(s, cur)), name=f"norm_ffn{i}")
        u3 = _lin_in_fwd(h, full['ffn_w_up'][i], 2, name=f"ffn_up{i}")
        act = _ffn_act_fwd(u3, ffn_cw[i], _with_tokens(ffn_cb[i], hand_on(s, u3)), name=f"ffn_act{i}")
        cur = _mm_nn(act, full['ffn_w_down'][i], res=cur, out_dtype=F32, name=f"ffn_down{i}")
        arrive(s, cur)
        rec.update(h_ffn=h, u3=u3, ffn_act=act)
        saved.append(rec)

    dx, dxb, dw_final, loss8 = _loss_head(cur, final_norm_w, target, name="loss_head")

    small = {n: [None] * wt[n].shape[0] for n in WEIGHTS if n not in BIG and n != 'final_norm_w'}
    scatters = [None] * n_sub
    pending = None

    def chip_step(after):
        return pending.start_chip(after) if pending is not None else None

    for i in reversed(range(depth)):
        j = i // 2
        rec = saved[i]
        nb_up = ffn_w_up.shape[2]
        da = _mm_nt(dxb, full['ffn_w_down'][i], out_dtype=BF16, name=f"ffn_down_dx{i}")
        g_down = _mm_tn(rec["ffn_act"], dxb, out_dtype=BF16, name=f"ffn_down_dw{i}")
        du3, dcw, dcb = _ffn_act_bwd(rec["u3"], da, ffn_cw[i], _with_tokens(ffn_cb[i], chip_step(da)),
                                     name=f"ffn_act_bwd{i}")
        dh = _lin_in_dx(du3, full['ffn_w_up'][i], name=f"ffn_up_dx{i}")
        g_up = _lin_in_dw(rec["h_ffn"], du3, nb_up, name=f"ffn_up_dw{i}")
        pending = scatters[2 * i + 1] = _Scatter([g_up, g_down.reshape(N_DEV, ff // N_DEV, d)], core_s, f"ffn{i}")
        tok = pending.start_pair(dh)
        dx, dxb, dwn = _rmsnorm_bwd(dh, rec["x_ffn"], _with_tokens(ffn_norm_w[i], tok), dx, name=f"norm_ffn_bwd{i}")
        small['ffn_conv_w'][i] = dcw.reshape(dcw.shape[0], 2 * ff)
        small['ffn_conv_b'][i] = dcb.reshape(2 * ff)
        small['ffn_norm_w'][i] = dwn.sum(axis=0)

        if i % 2 == 0:
            zx, xbc = rec["zx"], rec["xbc"]
            cw, cb = ssd_cw[j], ssd_conv_b[j].reshape(1, conv_dim)
            dyn = _mm_nt(dxb, full['ssd_w_out'][j], out_dtype=BF16, name=f"ssd_out_dx{j}")
            g_out = _mm_tn(rec["yn"], dxb, out_dtype=BF16, name=f"ssd_out_dw{j}")
            bias_t = _with_tokens(rec["vecs"][0], chip_step(dyn))
            dzx, dxs, db, dc, ddt_g, vec_acc, dnw, ddexp = _ssd_bwd(
                dyn, rec["y"], xbc, zx, rec["states"], bias_t, *rec["vecs"][1:], name=f"ssd_core_bwd{j}")
            dzx, dcw_x, dcb_x = _ssd_conv_bwd(zx, dxs, cw, cb, dzx, di, 0, name=f"ssd_conv_bwd_x{j}")
            dzx, dcw_b, dcb_b = _ssd_conv_bwd(zx, db, cw, cb, dzx, di, di, name=f"ssd_conv_bwd_b{j}")
            dzx, dcw_c, dcb_c = _ssd_conv_bwd(zx, dc, cw, cb, dzx, di, di + bc, name=f"ssd_conv_bwd_c{j}")
            dzx = _ssd_put_ddt(ddt_g, dzx, (di + conv_dim) // LANE, name=f"ssd_put_ddt{j}")
            dh = _mm_nt(dzx, full['ssd_w_in'][j], out_dtype=F32, name=f"ssd_in_dx{j}")
            g_in = _mm_tn(rec["h_mix"], dzx, out_dtype=BF16, name=f"ssd_in_dw{j}")
            g_in = jnp.swapaxes(g_in[:, :in_dim].reshape(d, N_DEV, in_dim // N_DEV), 0, 1)
            small['ssd_conv_w'][j] = jnp.concatenate([dcw_x, dcw_b, dcw_c], axis=1)
            small['ssd_conv_b'][j] = jnp.concatenate([dcb_x, dcb_b, dcb_c], axis=1).reshape(conv_dim)
            small['ssd_a_log'][j] = vec_acc[:, 0, :heads].sum(axis=0)
            small['ssd_dt_bias'][j] = vec_acc[:, 1, :heads].sum(axis=0)
            small['ssd_d'][j] = ddexp[:, 0, :].reshape(heads, HEAD_DIM).sum(axis=1)
            small['ssd_norm_w'][j] = dnw[:, 0, :].reshape(di)
            g_out = g_out.reshape(N_DEV, di // N_DEV, d)
        else:
            nb_in = sc_w_in.shape[2]
            da = _mm_nt(dxb, full['sc_w_out'][j], out_dtype=BF16, name=f"sc_out_dx{j}")
            g_out = _mm_tn(rec["act"], dxb, out_dtype=BF16, name=f"sc_out_dw{j}")
            dp3, dcw = _sc_act_bwd(rec["p3"], da, _with_tokens(sc_cw[j], chip_step(da)), name=f"sc_act_bwd{j}")
            dh = _lin_in_dx(dp3, full['sc_w_in'][j], name=f"sc_in_dx{j}")
            g_in = _lin_in_dw(rec["h_mix"], dp3, nb_in, name=f"sc_in_dw{j}")
            small['sc_conv_w'][j] = dcw
            g_out = g_out.reshape(N_DEV, g_out.shape[0] // N_DEV, d)
        pending = scatters[2 * i] = _Scatter([g_in, g_out], core_s, f"{order[2 * i][0]}{j}")
        tok = pending.start_pair(dh)
        dx, dxb, dwn = _rmsnorm_bwd(dh, rec["x_mix"], _with_tokens(mix_norm_w[i], tok), dx, name=f"norm_mix_bwd{i}")
        small['mix_norm_w'][i] = dwn.sum(axis=0)
    tok_last = chip_step(dx)

    small_names = [n for n in WEIGHTS if n not in BIG]
    partial = {n: jnp.stack(small[n]) for n in small}
    partial['final_norm_w'] = dw_final.sum(axis=0)
    full_shapes = [partial[n].shape for n in small_names]
    packed = _pack([loss8.sum().reshape(1)] + [partial[n] for n in small_names])
    total = _unpack(_all_reduce_small(packed, name="ar_small"), [(1,)] + full_shapes)
    loss = total[0].reshape(())
    grads = dict(zip(small_names, total[1:]))
    for n in SHARDED_SMALL:
        nb = wt[n].shape[-1]
        grads[n] = lax.dynamic_slice_in_dim(grads[n], me * nb, nb, axis=grads[n].ndim - 1)

    delta, new_m, new_v = {}, {}, {}
    shapes = [wt[n].shape for n in small_names]
    outs = _adamw_small(*[_pack([src[n] for n in small_names]) for src in (wt, grads, mom, var)], name="adamw_small")
    for dst, packed_out in zip((delta, new_m, new_v), outs):
        dst.update(zip(small_names, _unpack(packed_out, shapes)))
    parts = {n: [None] * wt[n].shape[0] for n in BIG}
    for s in range(1, n_sub):
        kind, idx = order[s]
        parts[names_of[kind][0]][idx], parts[names_of[kind][1]][idx] = scatters[s].finish(tok_last)
    first_in, first_out = names_of[order[0][0]]
    last_out = tok_last
    for n in reversed(BIG):
        prev = None
        for layer in reversed(range(wt[n].shape[0])):
            if parts[n][layer] is None:
                parts[first_in][0], parts[first_out][0] = scatters[0].finish(last_out)
            prev = _adamw_layer(wt[n], mom[n], var[n], *parts[n][layer], chip_s, layer, prev,
                                name=f"adamw_{n}{layer}")
            last_out = prev[1]
        grads[n], delta[n], new_m[n], new_v[n] = prev

    return (loss, dx.reshape(x.shape), *[grads[n] for n in WEIGHTS], *[delta[n] for n in WEIGHTS],
            *[new_m[n] for n in WEIGHTS], *[new_v[n] for n in WEIGHTS])
```

```python
import functools
import math

import jax
import jax.numpy as jnp
from jax import lax
from jax.experimental import pallas as pl
from jax.experimental.pallas import tpu as pltpu

F32 = jnp.float32
BF16 = jnp.bfloat16
MESH = pl.DeviceIdType.MESH

N_DEV = 8
N_CHIP = 4
EPS = 1e-5
HEAD_DIM = 64
STATE = 128
CHUNK = 128
PAIR = 2 * HEAD_DIM
GROUP_W = 8 * HEAD_DIM
HALO = 16
LANE = 128
VMEM_LIMIT = 56 * 1024 * 1024

ADAM_LR = 0.001
ADAM_B1 = 0.9
ADAM_B2 = 0.999
ADAM_EPS = 1e-08
ADAM_WD = 0.01
ADAM_STEP = 10


def _pick(n, candidates):
    for c in candidates:
        if c <= n and n % c == 0:
            return c
    return n


def _params(*sem):
    return pltpu.CompilerParams(dimension_semantics=sem, vmem_limit_bytes=VMEM_LIMIT)


def _sigmoid(x):
    return 0.5 * jnp.tanh(0.5 * x) + 0.5


_DIMS = {
    "nn": (((1,), (0,)), ((), ())),
    "nt": (((1,), (1,)), ((), ())),
    "tn": (((0,), (0,)), ((), ())),
}


def _matmul(mode, a, b, *, grid, a_spec, b_spec, o_spec, out_shape, acc_shape, name, res=None, res_spec=None,
            part_fn=None):
    nk = grid[2]
    dims = _DIMS[mode]
    if part_fn is None:
        part_fn = lambda a_ref, b_ref: lax.dot_general(a_ref[...], b_ref[...], dims, preferred_element_type=F32)

    def body(*refs):
        if res is None:
            a_ref, b_ref, o_ref = refs[:3]
            r_ref, scratch = None, refs[3:]
        else:
            a_ref, b_ref, r_ref, o_ref = refs[:4]
            scratch = refs[4:]
        part = part_fn(a_ref, b_ref)

        def finish(acc):
            if r_ref is not None:
                acc = acc + r_ref[...]
            o_ref[...] = acc.astype(o_ref.dtype)

        if nk == 1:
            finish(part)
        else:
            acc_ref = scratch[0]
            k = pl.program_id(2)

            @pl.when(k == 0)
            def _():
                acc_ref[...] = part

            @pl.when(k > 0)
            def _():
                acc_ref[...] += part

            @pl.when(k == nk - 1)
            def _():
                finish(acc_ref[...])

    in_specs = [a_spec, b_spec] + ([res_spec] if res is not None else [])
    args = (a, b) + ((res,) if res is not None else ())
    return pl.pallas_call(
        body, name=name, grid=grid, in_specs=in_specs, out_specs=o_spec, out_shape=out_shape,
        scratch_shapes=[pltpu.VMEM(acc_shape, F32)] if nk > 1 else [],
        compiler_params=_params("parallel", "parallel", "arbitrary"),
    )(*args)


def _mm_nn(a, b, *, out_dtype, res=None, out_parts=1, name):
    m, kd = a.shape
    n = b.shape[1]
    c = n // out_parts
    tm = _pick(m, (512, 256, 128))
    tn = _pick(c, (1152, 1024, 512, 384, 256, 128))
    tk = kd if kd <= 2048 else _pick(kd, (2816, 2048, 1024, 512, 256, 128))
    grid = (n // tn, m // tm, kd // tk)
    if out_parts == 1:
        o_spec = pl.BlockSpec((tm, tn), lambda j, i, k: (i, j))
        out_shape = jax.ShapeDtypeStruct((m, n), out_dtype)
    else:
        o_spec = _stacked_spec(tm, tn, c, lambda j, i, k: (i, j))
        out_shape = jax.ShapeDtypeStruct((out_parts, m, c), out_dtype)
    return _matmul(
        "nn", a, b, res=res, grid=grid, name=name,
        a_spec=pl.BlockSpec((tm, tk), lambda j, i, k: (i, k)),
        b_spec=pl.BlockSpec((tk, tn), lambda j, i, k: (k, j)),
        res_spec=pl.BlockSpec((tm, tn), lambda j, i, k: (i, j)),
        o_spec=o_spec, out_shape=out_shape, acc_shape=(tm, tn))


def _stacked_spec(rows, width, c, row_col):
    per = c // width

    def index(j, i, k):
        r, q = row_col(j, i, k)
        return q // per, r, q % per

    return pl.BlockSpec((None, rows, width), index)


def _mm_nt(a, b, *, out_dtype, name):
    stacked = a.ndim == 3
    m = a.shape[-2]
    n, kd = b.shape
    c = a.shape[-1]
    tm = _pick(m, (512, 256, 128))
    tn = _pick(n, (1408, 1024, 512, 256, 128))
    tk = c if c <= 2048 else _pick(c, (3456, 2816, 2048, 1024, 512, 384, 256, 128))
    grid = (n // tn, m // tm, kd // tk)
    a_spec = (_stacked_spec(tm, tk, c, lambda j, i, k: (i, k)) if stacked
              else pl.BlockSpec((tm, tk), lambda j, i, k: (i, k)))
    return _matmul(
        "nt", a, b, grid=grid, name=name, a_spec=a_spec,
        b_spec=pl.BlockSpec((tn, tk), lambda j, i, k: (j, k)),
        o_spec=pl.BlockSpec((tm, tn), lambda j, i, k: (i, j)),
        out_shape=jax.ShapeDtypeStruct((m, n), out_dtype), acc_shape=(tm, tn))


def _mm_tn(a, b, *, out_dtype, name):
    stacked = b.ndim == 3
    kd, m = a.shape
    c = b.shape[-1]
    n = c * (b.shape[0] if stacked else 1)
    tm = _pick(m, (512, 256, 128))
    tn = _pick(c, (1152, 1024, 512, 384, 256, 128))
    tk = _pick(kd, (2048, 1024, 512, 256, 128))
    grid = (n // tn, m // tm, kd // tk)
    b_spec = (_stacked_spec(tk, tn, c, lambda j, i, k: (k, j)) if stacked
              else pl.BlockSpec((tk, tn), lambda j, i, k: (k, j)))
    return _matmul(
        "tn", a, b, grid=grid, name=name,
        a_spec=pl.BlockSpec((tk, tm), lambda j, i, k: (k, i)), b_spec=b_spec,
        o_spec=pl.BlockSpec((tm, tn), lambda j, i, k: (i, j)),
        out_shape=jax.ShapeDtypeStruct((m, n), out_dtype), acc_shape=(tm, tn))


def _in_tile(nb, c):
    return math.gcd(nb, c)


def _lin_in_fwd(h, wg, parts, *, name):
    t, d = h.shape
    nb = wg.shape[2]
    c = N_DEV * nb // parts
    w = _in_tile(nb, c)
    nbw, cw = nb // w, c // w
    tm = _pick(t, (1024,) if w < 512 else (512, 256, 128))
    grid = (N_DEV * nbw, t // tm, 1)
    return _matmul(
        "nn", h, wg, grid=grid, name=name,
        a_spec=pl.BlockSpec((tm, d), lambda j, i, k: (i, 0)),
        b_spec=pl.BlockSpec((None, d, w), lambda j, i, k: (j // nbw, 0, j % nbw)),
        o_spec=pl.BlockSpec((None, tm, w), lambda j, i, k: (j // cw, i, j % cw)),
        out_shape=jax.ShapeDtypeStruct((parts, t, c), BF16), acc_shape=(tm, w))


def _lin_in_dx(dact, wg, *, name):
    parts, t, c = dact.shape
    d, nb = wg.shape[1], wg.shape[2]
    per = c // (2 * nb)
    tm = _pick(t, (512, 256, 128))
    tn = _pick(d, (1024, 512, 256, 128))
    grid = (d // tn, t // tm, N_DEV // 2)

    def pair(a_ref, b_ref):
        return (lax.dot_general(a_ref[:, :nb], b_ref[0], _DIMS["nt"], preferred_element_type=F32)
                + lax.dot_general(a_ref[:, nb:], b_ref[1], _DIMS["nt"], preferred_element_type=F32))

    return _matmul(
        "nt", dact, wg, grid=grid, name=name, part_fn=pair,
        a_spec=pl.BlockSpec((None, tm, 2 * nb), lambda j, i, k: (k // per, i, k % per)),
        b_spec=pl.BlockSpec((2, tn, nb), lambda j, i, k: (k, j, 0)),
        o_spec=pl.BlockSpec((tm, tn), lambda j, i, k: (i, j)),
        out_shape=jax.ShapeDtypeStruct((t, d), F32), acc_shape=(tm, tn))


def _lin_in_dw(h, dact, nb, *, name):
    t, d = h.shape
    parts, _, c = dact.shape
    w = _in_tile(nb, c)
    nbw, cw = nb // w, c // w
    tm = _pick(d, (1024,) if w < 512 else (512, 256, 128))
    tk = _pick(t, (2048, 1024, 512, 256, 128))
    grid = (N_DEV * nbw, d // tm, t // tk)
    return _matmul(
        "tn", h, dact, grid=grid, name=name,
        a_spec=pl.BlockSpec((tk, tm), lambda j, i, k: (k, i)),
        b_spec=pl.BlockSpec((None, tk, w), lambda j, i, k: (j // cw, k, j % cw)),
        o_spec=pl.BlockSpec((None, tm, w), lambda j, i, k: (j // nbw, i, j % nbw)),
        out_shape=jax.ShapeDtypeStruct((N_DEV, d, nb), BF16), acc_shape=(tm, w))


def _fold8(v):
    rows, c = v.shape
    return v.reshape(rows // 8, 8, c).sum(axis=0)


def _accumulate(ref, val, first):
    @pl.when(first)
    def _():
        ref[...] = val

    @pl.when(jnp.logical_not(first))
    def _():
        ref[...] += val


def _cast_layer(w_stack, layer, me, *, name):
    _, r, c = w_stack.shape
    tr = _pick(r, (256, 128, 64, 32, 16))

    def body(me_ref, w_ref, o_ref):
        del me_ref
        o_ref[...] = w_ref[...].astype(BF16)

    return pl.pallas_call(
        body, name=name,
        grid_spec=pltpu.PrefetchScalarGridSpec(
            num_scalar_prefetch=1, grid=(r // tr,),
            in_specs=[pl.BlockSpec((None, tr, c), lambda i, me_ref: (layer, i, 0))],
            out_specs=pl.BlockSpec((None, tr, c), lambda i, me_ref: (me_ref[0], i, 0))),
        out_shape=jax.ShapeDtypeStruct((N_DEV, r, c), BF16),
        compiler_params=_params("parallel"),
    )(me, w_stack)


def _rmsnorm_fwd(x, w, *, name):
    t, d = x.shape
    tt = _pick(t, (256, 128))

    def body(x_ref, w_ref, o_ref):
        xv = x_ref[...]
        r = lax.rsqrt(jnp.mean(xv * xv, axis=1, keepdims=True) + EPS)
        o_ref[...] = (xv * r * w_ref[...]).astype(BF16)

    return pl.pallas_call(
        body, name=name, grid=(t // tt,),
        in_specs=[pl.BlockSpec((tt, d), lambda i: (i, 0)), pl.BlockSpec((1, d), lambda i: (0, 0))],
        out_specs=pl.BlockSpec((tt, d), lambda i: (i, 0)),
        out_shape=jax.ShapeDtypeStruct((t, d), BF16),
        compiler_params=_params("parallel"),
    )(x, w.reshape(1, d))


def _rmsnorm_bwd(dh, x, w, dres, *, name):
    t, d = x.shape
    tt = _pick(t, (256, 128))

    def body(dh_ref, x_ref, w_ref, dres_ref, dx_ref, dxb_ref, dw_ref):
        xv = x_ref[...]
        r = lax.rsqrt(jnp.mean(xv * xv, axis=1, keepdims=True) + EPS)
        xhat = xv * r
        dhv = dh_ref[...].astype(F32)
        dxhat = dhv * w_ref[...]
        dx = dres_ref[...] + r * (dxhat - xhat * jnp.mean(dxhat * xhat, axis=1, keepdims=True))
        dx_ref[...] = dx
        dxb_ref[...] = dx.astype(BF16)
        _accumulate(dw_ref, _fold8(dhv * xhat), pl.program_id(0) == 0)

    row = pl.BlockSpec((tt, d), lambda i: (i, 0))
    return pl.pallas_call(
        body, name=name, grid=(t // tt,),
        in_specs=[row, row, pl.BlockSpec((1, d), lambda i: (0, 0)), row],
        out_specs=[row, row, pl.BlockSpec((8, d), lambda i: (0, 0))],
        out_shape=[jax.ShapeDtypeStruct((t, d), F32), jax.ShapeDtypeStruct((t, d), BF16),
                   jax.ShapeDtypeStruct((8, d), F32)],
        compiler_params=_params("arbitrary"),
    )(dh, x, w.reshape(1, d), dres)


def _loss_head(x, w, target, *, name):
    t, d = x.shape
    tt = _pick(t, (256, 128))

    def body(x_ref, w_ref, tg_ref, dx_ref, dxb_ref, dw_ref, ls_ref):
        xv = x_ref[...]
        wv = w_ref[...]
        r = lax.rsqrt(jnp.mean(xv * xv, axis=1, keepdims=True) + EPS)
        xhat = xv * r
        err = xhat * wv - tg_ref[...]
        dy = err * (1.0 / d)
        dxhat = dy * wv
        dx = r * (dxhat - xhat * jnp.mean(dxhat * xhat, axis=1, keepdims=True))
        dx_ref[...] = dx
        dxb_ref[...] = dx.astype(BF16)
        first = pl.program_id(0) == 0
        _accumulate(dw_ref, _fold8(dy * xhat), first)
        _accumulate(ls_ref, _fold8(err * err) * (0.5 / d), first)

    row = pl.BlockSpec((tt, d), lambda i: (i, 0))
    acc = pl.BlockSpec((8, d), lambda i: (0, 0))
    return pl.pallas_call(
        body, name=name, grid=(t // tt,),
        in_specs=[row, pl.BlockSpec((1, d), lambda i: (0, 0)), row],
        out_specs=[row, row, acc, acc],
        out_shape=[jax.ShapeDtypeStruct((t, d), F32), jax.ShapeDtypeStruct((t, d), BF16),
                   jax.ShapeDtypeStruct((8, d), F32), jax.ShapeDtypeStruct((8, d), F32)],
        compiler_params=_params("arbitrary"),
    )(x, w.reshape(1, d), target)


def _conv_causal(e, tap, width):
    acc = None
    for k in range(width):
        s = width - 1 - k
        term = (e if s == 0 else pltpu.roll(e, s, 0)) * tap(k)
        acc = term if acc is None else acc + term
    return acc


def _conv_anticausal(e, tap, width):
    rows = e.shape[0]
    acc = None
    for k in range(width):
        s = width - 1 - k
        term = (e if s == 0 else pltpu.roll(e, rows - s, 0)) * tap(k)
        acc = term if acc is None else acc + term
    return acc


def _extend(prev, cur, nxt, first, last):
    parts = []
    if prev is not None:
        parts.append(jnp.where(first, 0.0, prev.astype(F32)))
    parts.append(cur.astype(F32))
    if nxt is not None:
        parts.append(jnp.where(last, 0.0, nxt.astype(F32)))
    return jnp.concatenate(parts, axis=0)


def _prev_idx(i, tt):
    return jnp.maximum(i * (tt // HALO) - 1, 0)


def _next_idx(i, tt, t):
    return jnp.minimum((i + 1) * (tt // HALO), t // HALO - 1)


def _ffn_act_fwd(u3, cw, cb, *, name):
    _, t, f = u3.shape
    tt = _pick(t, (512, 256, 128))
    tc = _pick(f, (512, 256, 128))
    width = cw.shape[0]

    def body(u_ref, up_ref, w_ref, b_ref, o_ref):
        first = pl.program_id(1) == 0
        pre = []
        for p in range(2):
            e = _extend(up_ref[p], u_ref[p], None, first, None)
            pre.append(_conv_causal(e, lambda k: w_ref[k, p:p + 1, :], width)[HALO:] + b_ref[p:p + 1, :])
        g, v = pre
        o_ref[...] = (g * _sigmoid(g) * v).astype(BF16)

    return pl.pallas_call(
        body, name=name, grid=(f // tc, t // tt),
        in_specs=[pl.BlockSpec((2, tt, tc), lambda j, i: (0, i, j)),
                  pl.BlockSpec((2, HALO, tc), lambda j, i: (0, _prev_idx(i, tt), j)),
                  pl.BlockSpec((width, 2, tc), lambda j, i: (0, 0, j)),
                  pl.BlockSpec((2, tc), lambda j, i: (0, j))],
        out_specs=pl.BlockSpec((tt, tc), lambda j, i: (i, j)),
        out_shape=jax.ShapeDtypeStruct((t, f), BF16),
        compiler_params=_params("parallel", "parallel"),
    )(u3, u3, cw, cb)


def _ffn_act_bwd(u3, da, cw, cb, *, name):
    _, t, f = u3.shape
    tt = _pick(t, (512, 256, 128))
    tc = _pick(f, (512, 256, 128))
    width = cw.shape[0]
    nt = t // tt
    ctr = slice(HALO, HALO + tt)

    def body(u_ref, up_ref, un_ref, da_ref, dan_ref, w_ref, b_ref, du_ref, dcw_ref, dcb_ref):
        i = pl.program_id(1)
        first, last = i == 0, i == nt - 1
        ext, pre = [], []
        for p in range(2):
            e = _extend(up_ref[p], u_ref[p], un_ref[p], first, last)
            ext.append(e)
            pre.append(_conv_causal(e, lambda k: w_ref[k, p:p + 1, :], width) + b_ref[p:p + 1, :])
        g, v = pre
        sg = _sigmoid(g)
        dae = _extend(jnp.zeros((HALO, tc), F32), da_ref[...], dan_ref[...], False, last)
        dpre = (dae * v * (sg * (1.0 + g * (1.0 - sg))), dae * (g * sg))

        @pl.when(first)
        def _():
            dcw_ref[...] = jnp.zeros_like(dcw_ref)
            dcb_ref[...] = jnp.zeros_like(dcb_ref)

        for p in range(2):
            du_ref[p] = _conv_anticausal(dpre[p], lambda k: w_ref[k, p:p + 1, :], width)[ctr].astype(BF16)
            dc = dpre[p][ctr]
            dcb_ref[p:p + 1, :] += jnp.sum(dc, axis=0, keepdims=True)
            for k in range(width):
                s = width - 1 - k
                xs = (ext[p] if s == 0 else pltpu.roll(ext[p], s, 0))[ctr]
                dcw_ref[k, p:p + 1, :] += jnp.sum(dc * xs, axis=0, keepdims=True)

    return pl.pallas_call(
        body, name=name, grid=(f // tc, nt),
        in_specs=[pl.BlockSpec((2, tt, tc), lambda j, i: (0, i, j)),
                  pl.BlockSpec((2, HALO, tc), lambda j, i: (0, _prev_idx(i, tt), j)),
                  pl.BlockSpec((2, HALO, tc), lambda j, i: (0, _next_idx(i, tt, t), j)),
                  pl.BlockSpec((tt, tc), lambda j, i: (i, j)),
                  pl.BlockSpec((HALO, tc), lambda j, i: (_next_idx(i, tt, t), j)),
                  pl.BlockSpec((width, 2, tc), lambda j, i: (0, 0, j)),
                  pl.BlockSpec((2, tc), lambda j, i: (0, j))],
        out_specs=[pl.BlockSpec((2, tt, tc), lambda j, i: (0, i, j)),
                   pl.BlockSpec((width, 2, tc), lambda j, i: (0, 0, j)),
                   pl.BlockSpec((2, tc), lambda j, i: (0, j))],
        out_shape=[jax.ShapeDtypeStruct((2, t, f), BF16), jax.ShapeDtypeStruct((width, 2, f), F32),
                   jax.ShapeDtypeStruct((2, f), F32)],
        compiler_params=_params("parallel", "arbitrary"),
    )(u3, u3, u3, da, da, cw, cb)


def _sc_act_fwd(p3, cw, *, name):
    _, t, c = p3.shape
    tt = _pick(t, (512, 256, 128))
    tc = _pick(c, (512, 256, 128))
    width = cw.shape[0]

    def body(p_ref, pp_ref, w_ref, o_ref):
        first = pl.program_id(1) == 0
        q = _extend(pp_ref[1], p_ref[1], None, first, None) * _extend(pp_ref[2], p_ref[2], None, first, None)
        cq = _conv_causal(q, lambda k: w_ref[k:k + 1, :], width)[HALO:]
        o_ref[...] = (p_ref[0].astype(F32) * cq).astype(BF16)

    return pl.pallas_call(
        body, name=name, grid=(c // tc, t // tt),
        in_specs=[pl.BlockSpec((3, tt, tc), lambda j, i: (0, i, j)),
                  pl.BlockSpec((3, HALO, tc), lambda j, i: (0, _prev_idx(i, tt), j)),
                  pl.BlockSpec((width, tc), lambda j, i: (0, j))],
        out_specs=pl.BlockSpec((tt, tc), lambda j, i: (i, j)),
        out_shape=jax.ShapeDtypeStruct((t, c), BF16),
        compiler_params=_params("parallel", "parallel"),
    )(p3, p3, cw)


def _sc_act_bwd(p3, da, cw, *, name):
    _, t, c = p3.shape
    tt = _pick(t, (512, 256, 128))
    tc = _pick(c, (512, 256, 128))
    width = cw.shape[0]
    nt = t // tt
    ctr = slice(HALO, HALO + tt)

    def body(p_ref, pp_ref, pn_ref, da_ref, dan_ref, w_ref, dp_ref, dcw_ref):
        i = pl.program_id(1)
        first, last = i == 0, i == nt - 1
        tap = lambda k: w_ref[k:k + 1, :]
        bg, cg, hh = (_extend(pp_ref[p], p_ref[p], pn_ref[p], first, last) for p in range(3))
        q = cg * hh
        cq = _conv_causal(q, tap, width)
        dae = _extend(jnp.zeros((HALO, tc), F32), da_ref[...], dan_ref[...], False, last)
        dcq = dae * bg
        dq = _conv_anticausal(dcq, tap, width)[ctr]
        dp_ref[0] = (dae * cq)[ctr].astype(BF16)
        dp_ref[1] = (dq * hh[ctr]).astype(BF16)
        dp_ref[2] = (dq * cg[ctr]).astype(BF16)

        @pl.when(first)
        def _():
            dcw_ref[...] = jnp.zeros_like(dcw_ref)

        dc = dcq[ctr]
        for k in range(width):
            s = width - 1 - k
            qs = (q if s == 0 else pltpu.roll(q, s, 0))[ctr]
            dcw_ref[k:k + 1, :] += jnp.sum(dc * qs, axis=0, keepdims=True)

    return pl.pallas_call(
        body, name=name, grid=(c // tc, nt),
        in_specs=[pl.BlockSpec((3, tt, tc), lambda j, i: (0, i, j)),
                  pl.BlockSpec((3, HALO, tc), lambda j, i: (0, _prev_idx(i, tt), j)),
                  pl.BlockSpec((3, HALO, tc), lambda j, i: (0, _next_idx(i, tt, t), j)),
                  pl.BlockSpec((tt, tc), lambda j, i: (i, j)),
                  pl.BlockSpec((HALO, tc), lambda j, i: (_next_idx(i, tt, t), j)),
                  pl.BlockSpec((width, tc), lambda j, i: (0, j))],
        out_specs=[pl.BlockSpec((3, tt, tc), lambda j, i: (0, i, j)),
                   pl.BlockSpec((width, tc), lambda j, i: (0, j))],
        out_shape=[jax.ShapeDtypeStruct((3, t, c), BF16), jax.ShapeDtypeStruct((width, c), F32)],
        compiler_params=_params("parallel", "arbitrary"),
    )(p3, p3, p3, da, da, cw)


def _ssd_conv_fwd(zx, cw, cb, col0, *, name):
    t = zx.shape[0]
    width, c = cw.shape
    tt = _pick(t, (512, 256, 128))
    tc = _pick(math.gcd(c, col0), (512, 256, 128))
    off = col0 // tc

    def body(x_ref, xp_ref, w_ref, b_ref, o_ref):
        first = pl.program_id(1) == 0
        e = _extend(xp_ref[...], x_ref[...], None, first, None)
        pre = _conv_causal(e, lambda k: w_ref[k:k + 1, :], width)[HALO:] + b_ref[...]
        o_ref[...] = (pre * _sigmoid(pre)).astype(BF16)

    return pl.pallas_call(
        body, name=name, grid=(c // tc, t // tt),
        in_specs=[pl.BlockSpec((tt, tc), lambda j, i: (i, off + j)),
                  pl.BlockSpec((HALO, tc), lambda j, i: (_prev_idx(i, tt), off + j)),
                  pl.BlockSpec((width, tc), lambda j, i: (0, j)),
                  pl.BlockSpec((1, tc), lambda j, i: (0, j))],
        out_specs=pl.BlockSpec((tt, tc), lambda j, i: (i, j)),
        out_shape=jax.ShapeDtypeStruct((t, c), BF16),
        compiler_params=_params("parallel", "parallel"),
    )(zx, zx, cw, cb)


def _ssd_conv_bwd(zx, dxc, cw, cb, dzx, col0, woff, *, name):
    t = zx.shape[0]
    width = cw.shape[0]
    c = dxc.shape[1]
    tt = _pick(t, (512, 256, 128))
    tc = _pick(math.gcd(math.gcd(c, col0), woff) if woff else math.gcd(c, col0), (512, 256, 128))
    nt = t // tt
    xoff, wo = (col0 + woff) // tc, woff // tc
    ctr = slice(HALO, HALO + tt)

    def body(x_ref, xp_ref, xn_ref, d_ref, dn_ref, w_ref, b_ref, dzx_in, dzx_ref, dcw_ref, dcb_ref):
        del dzx_in
        i = pl.program_id(1)
        first, last = i == 0, i == nt - 1
        tap = lambda k: w_ref[k:k + 1, :]
        e = _extend(xp_ref[...], x_ref[...], xn_ref[...], first, last)
        pre = _conv_causal(e, tap, width) + b_ref[...]
        sg = _sigmoid(pre)
        de = _extend(jnp.zeros((HALO, tc), F32), d_ref[...], dn_ref[...], False, last)
        dpre = de * (sg * (1.0 + pre * (1.0 - sg)))
        dzx_ref[...] = _conv_anticausal(dpre, tap, width)[ctr].astype(BF16)

        @pl.when(first)
        def _():
            dcw_ref[...] = jnp.zeros_like(dcw_ref)
            dcb_ref[...] = jnp.zeros_like(dcb_ref)

        dc = dpre[ctr]
        dcb_ref[...] += jnp.sum(dc, axis=0, keepdims=True)
        for k in range(width):
            s = width - 1 - k
            xs = (e if s == 0 else pltpu.roll(e, s, 0))[ctr]
            dcw_ref[k:k + 1, :] += jnp.sum(dc * xs, axis=0, keepdims=True)

    return pl.pallas_call(
        body, name=name, grid=(c // tc, nt),
        in_specs=[pl.BlockSpec((tt, tc), lambda j, i: (i, xoff + j)),
                  pl.BlockSpec((HALO, tc), lambda j, i: (_prev_idx(i, tt), xoff + j)),
                  pl.BlockSpec((HALO, tc), lambda j, i: (_next_idx(i, tt, t), xoff + j)),
                  pl.BlockSpec((tt, tc), lambda j, i: (i, j)),
                  pl.BlockSpec((HALO, tc), lambda j, i: (_next_idx(i, tt, t), j)),
                  pl.BlockSpec((width, tc), lambda j, i: (0, wo + j)),
                  pl.BlockSpec((1, tc), lambda j, i: (0, wo + j)),
                  pl.BlockSpec(memory_space=pl.ANY)],
        out_specs=[pl.BlockSpec((tt, tc), lambda j, i: (i, xoff + j)),
                   pl.BlockSpec((width, tc), lambda j, i: (0, j)),
                   pl.BlockSpec((1, tc), lambda j, i: (0, j))],
        out_shape=[jax.ShapeDtypeStruct(dzx.shape, dzx.dtype), jax.ShapeDtypeStruct((width, c), F32),
                   jax.ShapeDtypeStruct((1, c), F32)],
        input_output_aliases={7: 0},
        compiler_params=_params("parallel", "arbitrary"),
    )(zx, zx, zx, dxc, dxc, cw, cb, dzx)


def _ssd_put_ddt(ddt_g, dzx, col, *, name):
    g, t, _ = ddt_g.shape
    tt = _pick(t, (512, 256, 128))

    def body(d_ref, dzx_in, dzx_ref):
        del dzx_in
        dzx_ref[...] = jnp.sum(d_ref[...], axis=0).astype(BF16)

    return pl.pallas_call(
        body, name=name, grid=(t // tt,),
        in_specs=[pl.BlockSpec((g, tt, LANE), lambda i: (0, i, 0)), pl.BlockSpec(memory_space=pl.ANY)],
        out_specs=pl.BlockSpec((tt, LANE), lambda i: (i, col)),
        out_shape=jax.ShapeDtypeStruct(dzx.shape, dzx.dtype),
        input_output_aliases={1: 0},
        compiler_params=_params("parallel"),
    )(ddt_g, dzx)


def _dot(a, b, mode):
    return lax.dot_general(a, b, _DIMS[mode], preferred_element_type=F32)


def _dot_exact(m01, v, mode="nn"):
    hi = v.astype(BF16)
    r1 = v - hi.astype(F32)
    mid = r1.astype(BF16)
    lo = (r1 - mid.astype(F32)).astype(BF16)
    return _dot(m01, hi, mode) + _dot(m01, mid, mode) + _dot(m01, lo, mode)


def _softplus(x):
    return jnp.maximum(x, 0.0) + jnp.log(1.0 + jnp.exp(-jnp.abs(x)))


def _head_vectors(g, dt_raw, bias, alog):
    n = CHUNK
    dt = _softplus(dt_raw + bias)
    a = -jnp.exp(alog)
    tri = (lax.broadcasted_iota(jnp.int32, (n, n), 0) >= lax.broadcasted_iota(jnp.int32, (n, n), 1)).astype(BF16)
    cs = _dot_exact(tri, dt * a)
    return dt, a, cs, cs.T


def _col(v, lane_ids, h):
    return jnp.sum(jnp.where(lane_ids == h, v, 0.0), axis=1, keepdims=True)


def _row(vt, sub_ids, h):
    return jnp.sum(jnp.where(sub_ids == h, vt, 0.0), axis=0, keepdims=True)


def _ssd_specs(di, bc, nc, rev):
    cidx = (lambda c: nc - 1 - c) if rev else (lambda c: c)
    wide = lambda off: pl.BlockSpec((CHUNK, GROUP_W), lambda g, c: (cidx(c), off + g))
    lane = lambda off: pl.BlockSpec((CHUNK, LANE), lambda g, c: (cidx(c), off + g))
    fixed = lambda off: pl.BlockSpec((CHUNK, LANE), lambda g, c: (cidx(c), off))
    vec = pl.BlockSpec((1, LANE), lambda g, c: (0, 0))
    gvec = pl.BlockSpec((1, GROUP_W), lambda g, c: (0, g))
    state = pl.BlockSpec((None, None, 4, PAIR, STATE), lambda g, c: (g, cidx(c), 0, 0, 0))
    return wide, lane, fixed, vec, gvec, state


def _ssd_fwd(xbc, zx, bias, alog, dexp, nw, *, name):
    t = xbc.shape[0]
    di = nw.shape[1]
    bc = (xbc.shape[1] - di) // 2
    ng, nc = di // GROUP_W, t // CHUNK
    wide, lane, fixed, vec, gvec, state = _ssd_specs(di, bc, nc, rev=False)

    def body(xs_ref, b_ref, c_ref, dt_ref, z_ref, bias_ref, alog_ref, dexp_ref, nw_ref,
             yn_ref, y_ref, st_ref, s_scr):
        g, c = pl.program_id(0), pl.program_id(1)

        @pl.when(c == 0)
        def _():
            s_scr[...] = jnp.zeros_like(s_scr)

        n = CHUNK
        dt, a, cs, cst = _head_vectors(g, dt_ref[...].astype(F32), bias_ref[...], alog_ref[...])
        lane_ids = lax.broadcasted_iota(jnp.int32, (n, LANE), 1)
        sub_ids = lax.broadcasted_iota(jnp.int32, (LANE, n), 0)
        causal = lax.broadcasted_iota(jnp.int32, (n, n), 0) >= lax.broadcasted_iota(jnp.int32, (n, n), 1)
        half = lax.broadcasted_iota(jnp.int32, (1, PAIR), 1) < HEAD_DIM
        half_rows = lax.broadcasted_iota(jnp.int32, (PAIR, 1), 0) < HEAD_DIM
        bm, cm = b_ref[...], c_ref[...]
        gm = _dot(cm, bm, "nt")
        x = xs_ref[...].astype(F32)
        ys = []
        for q in range(4):
            h0 = g * 8 + 2 * q
            col = [_col(cs, lane_ids, h0 + e) for e in range(2)]
            row = [_row(cst, sub_ids, h0 + e) for e in range(2)]
            dtc = [_col(dt, lane_ids, h0 + e) for e in range(2)]
            last = [col[e][n - 1:n, :] for e in range(2)]
            xd = x[:, q * PAIR:(q + 1) * PAIR] * jnp.where(half, dtc[0], dtc[1])
            xd_bf = xd.astype(BF16)
            yd = []
            for e in range(2):
                lm = jnp.exp(jnp.where(causal, col[e] - row[e], -1e30))
                yd.append(_dot((gm * lm).astype(BF16), xd_bf, "nn"))
            s = s_scr[q]
            st_ref[q] = s
            ecs = jnp.where(half, jnp.exp(col[0]), jnp.exp(col[1]))
            dte = jnp.where(half, jnp.exp(last[0] - col[0]), jnp.exp(last[1] - col[1]))
            yoff = ecs * _dot(cm, s.astype(BF16), "nt")
            snew = _dot((xd * dte).astype(BF16), bm, "tn")
            s_scr[q] = s * jnp.where(half_rows, jnp.exp(last[0]), jnp.exp(last[1])) + snew
            ys.append(jnp.where(half, yd[0], yd[1]) + yoff)
        y = jnp.concatenate(ys, axis=1) + dexp_ref[...] * x
        y_ref[...] = y.astype(BF16)
        z = z_ref[...].astype(F32)
        yg = y * (z * _sigmoid(z))
        r = lax.rsqrt(jnp.mean(yg * yg, axis=1, keepdims=True) + EPS)
        yn_ref[...] = (yg * r * nw_ref[...]).astype(BF16)

    dtcol = (2 * di + 2 * bc) // LANE
    return pl.pallas_call(
        body, name=name, grid=(ng, nc),
        in_specs=[wide(0), lane(di // LANE), lane((di + bc) // LANE), fixed(dtcol), wide(0),
                  vec, vec, gvec, gvec],
        out_specs=[wide(0), wide(0), state],
        out_shape=[jax.ShapeDtypeStruct((t, di), BF16), jax.ShapeDtypeStruct((t, di), BF16),
                   jax.ShapeDtypeStruct((ng, nc, 4, PAIR, STATE), F32)],
        scratch_shapes=[pltpu.VMEM((4, PAIR, STATE), F32)],
        compiler_params=_params("parallel", "arbitrary"),
    )(xbc, xbc, xbc, zx, zx, bias, alog, dexp, nw)


def _ssd_bwd(dyn, y, xbc, zx, states, bias, alog, dexp, nw, *, name):
    t = xbc.shape[0]
    di = nw.shape[1]
    bc = (xbc.shape[1] - di) // 2
    ng, nc = di // GROUP_W, t // CHUNK
    wide, lane, fixed, vec, gvec, state = _ssd_specs(di, bc, nc, rev=True)
    acc = lambda w: pl.BlockSpec((None, 8, w), lambda g, c: (g, 0, 0))

    def body(dyn_ref, y_ref, z_ref, nw_ref, xs_ref, b_ref, c_ref, dt_ref, bias_ref, alog_ref, dexp_ref, st_ref,
             dz_ref, dxs_ref, db_ref, dc_ref, ddt_ref, small_ref, dnw_ref, ddexp_ref, ds_scr):
        g, c = pl.program_id(0), pl.program_id(1)

        @pl.when(c == 0)
        def _():
            ds_scr[...] = jnp.zeros_like(ds_scr)
            small_ref[...] = jnp.zeros_like(small_ref)
            dnw_ref[...] = jnp.zeros_like(dnw_ref)
            ddexp_ref[...] = jnp.zeros_like(ddexp_ref)

        n = CHUNK
        yv = y_ref[...].astype(F32)
        z = z_ref[...].astype(F32)
        sz = _sigmoid(z)
        silu = z * sz
        yg = yv * silu
        r = lax.rsqrt(jnp.mean(yg * yg, axis=1, keepdims=True) + EPS)
        yhat = yg * r
        dynv = dyn_ref[...].astype(F32)
        dnw_ref[0:1, :] += jnp.sum(dynv * yhat, axis=0, keepdims=True)
        dyhat = dynv * nw_ref[...]
        dyg = r * (dyhat - yhat * jnp.mean(dyhat * yhat, axis=1, keepdims=True))
        dz_ref[...] = (dyg * yv * (sz * (1.0 + z * (1.0 - sz)))).astype(BF16)
        dy = dyg * silu

        dt_in = dt_ref[...].astype(F32) + bias_ref[...]
        dt, a, cs, cst = _head_vectors(g, dt_ref[...].astype(F32), bias_ref[...], alog_ref[...])
        lane_ids = lax.broadcasted_iota(jnp.int32, (n, LANE), 1)
        sub_ids = lax.broadcasted_iota(jnp.int32, (LANE, n), 0)
        ri = lax.broadcasted_iota(jnp.int32, (n, n), 0)
        ci = lax.broadcasted_iota(jnp.int32, (n, n), 1)
        causal, causal_t = ri >= ci, ci >= ri
        is_last = lax.broadcasted_iota(jnp.int32, (n, 1), 0) == n - 1
        half = lax.broadcasted_iota(jnp.int32, (1, PAIR), 1) < HEAD_DIM
        half_rows = lax.broadcasted_iota(jnp.int32, (PAIR, 1), 0) < HEAD_DIM
        bm, cm = b_ref[...], c_ref[...]
        bf = bm.astype(F32)
        gm, gmt = _dot(cm, bm, "nt"), _dot(bm, cm, "nt")
        x = xs_ref[...].astype(F32)
        dexp = dexp_ref[...]

        dg_sum = jnp.zeros((n, n), F32)
        dgt_sum = jnp.zeros((n, n), F32)
        db_off = jnp.zeros((n, STATE), F32)
        dc_off = jnp.zeros((n, STATE), F32)
        dcs_blk = jnp.zeros((n, LANE), F32)
        ddt_blk = jnp.zeros((n, LANE), F32)
        dxs = []
        for q in range(4):
            h0 = g * 8 + 2 * q
            sl = slice(q * PAIR, (q + 1) * PAIR)
            col = [_col(cs, lane_ids, h0 + e) for e in range(2)]
            row = [_row(cst, sub_ids, h0 + e) for e in range(2)]
            dtc = [_col(dt, lane_ids, h0 + e) for e in range(2)]
            last = [col[e][n - 1:n, :] for e in range(2)]
            xp, dyp = x[:, sl], dy[:, sl]
            dtp = jnp.where(half, dtc[0], dtc[1])
            xd = xp * dtp
            xd_bf, dyp_bf = xd.astype(BF16), dyp.astype(BF16)
            ecs = jnp.where(half, jnp.exp(col[0]), jnp.exp(col[1]))
            dte = jnp.where(half, jnp.exp(last[0] - col[0]), jnp.exp(last[1] - col[1]))
            s, ds = st_ref[q], ds_scr[q]
            s_bf, ds_bf = s.astype(BF16), ds.astype(BF16)
            yoff = ecs * _dot(cm, s_bf, "nt")
            edy_bf = (ecs * dyp).astype(BF16)
            dc_off += _dot(edy_bf, s_bf, "nn")
            bds = _dot(bm, ds_bf, "nt")
            sds = s * ds
            zs = []
            for e in range(2):
                msk = half if e == 0 else jnp.logical_not(half)
                msk_rows = half_rows if e == 0 else jnp.logical_not(half_rows)
                lm = jnp.exp(jnp.where(causal, col[e] - row[e], -1e30))
                lmt = jnp.exp(jnp.where(causal_t, row[e] - col[e], -1e30))
                dym_bf = jnp.where(msk, dyp, 0.0).astype(BF16)
                xdm_bf = jnp.where(msk, xd, 0.0).astype(BF16)
                dm = _dot(dym_bf, xd_bf, "nt")
                dmt = _dot(xdm_bf, dyp_bf, "nt")
                m, mt = gm * lm, gmt * lmt
                dcs = jnp.sum(dm * m, axis=1, keepdims=True) - jnp.sum(dmt * mt, axis=1, keepdims=True)
                dg_sum += dm * lm
                dgt_sum += dmt * lmt
                zs.append(_dot(mt.astype(BF16), dyp_bf, "nn"))
                we = _dot(xdm_bf, ds_bf, "nn")
                dte_col = jnp.exp(last[e] - col[e])
                te = dte_col * jnp.sum(we * bf, axis=1, keepdims=True)
                db_off += dte_col * we
                dcs += jnp.sum(jnp.where(msk, dyp * yoff, 0.0), axis=1, keepdims=True) - te
                tail = jnp.exp(last[e]) * jnp.sum(jnp.where(msk_rows, sds, 0.0), keepdims=True) \
                    + jnp.sum(te, keepdims=True)
                dcs += jnp.where(is_last, tail, 0.0)
                dcs_blk += jnp.where(lane_ids == h0 + e, dcs, 0.0)
            dxd = jnp.where(half, zs[0], zs[1]) + dte * bds
            dxs.append(dxd * dtp + dexp[:, sl] * dyp)
            ddexp_ref[0:1, sl] += jnp.sum(dyp * xp, axis=0, keepdims=True)
            rs = dxd * xp
            for e in range(2):
                msk = half if e == 0 else jnp.logical_not(half)
                ddt_blk += jnp.where(lane_ids == h0 + e, jnp.sum(jnp.where(msk, rs, 0.0), axis=1, keepdims=True), 0.0)
            ds_scr[q] = ds * jnp.where(half_rows, jnp.exp(last[0]), jnp.exp(last[1])) + _dot(edy_bf, cm, "tn")

        dxs_ref[...] = jnp.concatenate(dxs, axis=1).astype(BF16)
        dc_ref[...] = (_dot(dg_sum.astype(BF16), bm, "nn") + dc_off).astype(BF16)
        db_ref[...] = (_dot(dgt_sum.astype(BF16), cm, "nn") + db_off).astype(BF16)
        upper = (ri <= ci).astype(BF16)
        dda = _dot_exact(upper, dcs_blk)
        ddt = dda * a + ddt_blk
        small_ref[0:1, :] += jnp.sum(dda * dt, axis=0, keepdims=True) * a
        ddt_raw = ddt * _sigmoid(dt_in)
        small_ref[1:2, :] += jnp.sum(ddt_raw, axis=0, keepdims=True)
        ddt_ref[...] = ddt_raw

    dtcol = (2 * di + 2 * bc) // LANE
    tot = 2 * di + 2 * bc + LANE
    return pl.pallas_call(
        body, name=name, grid=(ng, nc),
        in_specs=[wide(0), wide(0), wide(0), gvec, wide(0), lane(di // LANE), lane((di + bc) // LANE),
                  fixed(dtcol), vec, vec, gvec, state],
        out_specs=[wide(0), wide(0), lane(0), lane(0),
                   pl.BlockSpec((None, CHUNK, LANE), lambda g, c: (g, nc - 1 - c, 0)),
                   acc(LANE), acc(GROUP_W), acc(GROUP_W)],
        out_shape=[jax.ShapeDtypeStruct((t, tot), BF16), jax.ShapeDtypeStruct((t, di), BF16),
                   jax.ShapeDtypeStruct((t, bc), BF16), jax.ShapeDtypeStruct((t, bc), BF16),
                   jax.ShapeDtypeStruct((ng, t, LANE), F32), jax.ShapeDtypeStruct((ng, 8, LANE), F32),
                   jax.ShapeDtypeStruct((ng, 8, GROUP_W), F32), jax.ShapeDtypeStruct((ng, 8, GROUP_W), F32)],
        scratch_shapes=[pltpu.VMEM((4, PAIR, STATE), F32)],
        compiler_params=_params("parallel", "arbitrary"),
    )(dyn, y, zx, nw, xbc, xbc, xbc, zx, bias, alog, dexp, states)


HBM_ANY = pl.BlockSpec(memory_space=pl.ANY)


def _place():
    x, y, c = lax.axis_index("x"), lax.axis_index("y"), lax.axis_index("c")
    chips = [(1 - x, y), (x, 1 - y), (1 - x, 1 - y)]
    return x, y, c, chips


def _all_gather(arrs, *, name, inplace=False):
    n = len(arrs)

    def body(*refs):
        ins, outs = refs[:n], refs[n:2 * n]
        send, recv, loc = refs[2 * n:]
        x, y, c, chips = _place()
        me, sib = (x, y, c), (x, y, 1 - c)

        def blk(a, p):
            return outs[a].at[4 * p[0] + 2 * p[1] + p[2]]

        def cp(a, k, block, to, src=None):
            return pltpu.make_async_remote_copy(
                src_ref=blk(a, block) if src is None else src, dst_ref=blk(a, block),
                send_sem=send.at[a * 7 + k], recv_sem=recv.at[a * 7 + k], device_id=to, device_id_type=MESH)

        src = [None if inplace else ins[a] for a in range(n)]
        mine = [] if inplace else [pltpu.make_async_copy(ins[a], blk(a, me), loc.at[a]) for a in range(n)]
        for m in mine:
            m.start()
        started = []
        for a in range(n):
            started.append(cp(a, 0, me, sib, src=src[a]))
            started += [cp(a, 1 + j, me, (*chip, c), src=src[a]) for j, chip in enumerate(chips)]
        for s in started:
            s.start()
        for j, chip in enumerate(chips):
            for a in range(n):
                cp(a, 1 + j, (*chip, c), me).wait_recv()
                fwd = cp(a, 4 + j, (*chip, c), sib)
                fwd.start()
                started.append(fwd)
        for a in range(n):
            cp(a, 0, sib, me).wait_recv()
            for j, chip in enumerate(chips):
                cp(a, 4 + j, (*chip, 1 - c), me).wait_recv()
        for s in started:
            s.wait_send()
        for m in mine:
            m.wait()

    return pl.pallas_call(
        body, name=name,
        in_specs=[HBM_ANY] * n, out_specs=[HBM_ANY] * n,
        out_shape=[jax.ShapeDtypeStruct(a.shape if inplace else (N_DEV,) + a.shape, a.dtype) for a in arrs],
        input_output_aliases={a: a for a in range(n)} if inplace else {},
        scratch_shapes=[pltpu.SemaphoreType.DMA((7 * n,)), pltpu.SemaphoreType.DMA((7 * n,)),
                        pltpu.SemaphoreType.DMA((n,))],
    )(*arrs)


HBM_SPEC = pl.BlockSpec(memory_space=pltpu.HBM)
SEM_SPEC = pl.BlockSpec(memory_space=pltpu.SEMAPHORE)
SPLIT_EFFECT = pltpu.SideEffectType.DATAFLOW_SIDE_EFFECTING


def _split_start(arrs, plan, n_copies, after, *, name):
    m = len(arrs)

    def body(*refs):
        send, recv, token = refs[m + 1], refs[m + 2], refs[-1]
        for i, (src, dst, to) in enumerate(plan(refs[:m])):
            pltpu.make_async_remote_copy(src_ref=src, dst_ref=dst, send_sem=send.at[i], recv_sem=recv.at[i],
                                         device_id=to, device_id_type=MESH).start()
        token[...] = jnp.zeros_like(token)

    outs = pl.pallas_call(
        body, name=name,
        out_shape=(pltpu.SemaphoreType.DMA((n_copies,)), pltpu.SemaphoreType.DMA((n_copies,)),
                   *[pltpu.HBM(a.shape, a.dtype) for a in arrs], jax.ShapeDtypeStruct((8, LANE), F32)),
        in_specs=[HBM_SPEC] * m + [HBM_ANY],
        out_specs=(SEM_SPEC, SEM_SPEC, *[HBM_SPEC] * m, pl.BlockSpec(memory_space=pltpu.VMEM)),
        input_output_aliases={i: 2 + i for i in range(m)},
        compiler_params=pltpu.CompilerParams(has_side_effects=SPLIT_EFFECT),
    )(*[pltpu.with_memory_space_constraint(a, pltpu.HBM) for a in arrs], after)
    return outs[0], outs[1], list(outs[2:2 + m]), outs[-1]


def _split_wait(arrs, send, recv, after, plan, *, name):
    m = len(arrs)

    def body(*refs):
        send_ref, recv_ref = refs[m], refs[m + 1]
        for i, (src, dst, to) in enumerate(plan(refs[:m])):
            cp = pltpu.make_async_remote_copy(src_ref=src, dst_ref=dst, send_sem=send_ref.at[i],
                                              recv_sem=recv_ref.at[i], device_id=to, device_id_type=MESH)
            cp.wait_send()
            cp.wait_recv()

    outs = pl.pallas_call(
        body, name=name,
        out_shape=[pltpu.HBM(a.shape, a.dtype) for a in arrs],
        in_specs=[HBM_SPEC] * m + [SEM_SPEC, SEM_SPEC, HBM_ANY], out_specs=[HBM_SPEC] * m,
        input_output_aliases={i: i for i in range(m)},
        compiler_params=pltpu.CompilerParams(has_side_effects=SPLIT_EFFECT),
    )(*arrs, send, recv, after)
    return list(outs)


def _dev(p):
    return 4 * p[0] + 2 * p[1] + p[2]


def _plan_gather_ici(bufs):
    x, y, c, chips = _place()
    me = _dev((x, y, c))
    peers = [(x, y, 1 - c)] + [(*chip, c) for chip in chips]
    return [(b.at[me], b.at[me], p) for b in bufs for p in peers]


def _plan_gather_d2d(bufs):
    x, y, c, chips = _place()
    return [(b.at[_dev((*chip, c))], b.at[_dev((*chip, c))], (x, y, 1 - c)) for b in bufs for chip in chips]


def _plan_pair(refs):
    n = len(refs) // 2
    x, y, c, _ = _place()
    return [(refs[a].at[2 * k + 1 - c], refs[n + a].at[k], (x, y, 1 - c)) for a in range(n) for k in range(N_CHIP)]


def _plan_chip(refs):
    n = len(refs) // 2
    x, y, c, chips = _place()
    return [(refs[a].at[2 * chip[0] + chip[1]], refs[n + a].at[j], (*chip, c))
            for a in range(n) for j, chip in enumerate(chips)]


def _land(shape, dtype):
    return lax.empty(shape, dtype)


def _with_tokens(v, *tokens):
    for t in tokens:
        if t is not None:
            v = v + t[0, 0].astype(v.dtype)
    return v


def _pair_exchange(grads, *, name):
    n = len(grads)

    def body(*refs):
        ins, gots = refs[:n], refs[n:2 * n]
        send, recv = refs[2 * n:]
        x, y, c, _ = _place()
        copies = []
        for a in range(n):
            for k in range(N_CHIP):
                copies.append(pltpu.make_async_remote_copy(
                    src_ref=ins[a].at[2 * k + 1 - c], dst_ref=gots[a].at[k],
                    send_sem=send.at[a * N_CHIP + k], recv_sem=recv.at[a * N_CHIP + k],
                    device_id=(x, y, 1 - c), device_id_type=MESH))
        for cpy in copies:
            cpy.start()
        for cpy in copies:
            cpy.wait()

    return pl.pallas_call(
        body, name=name,
        in_specs=[HBM_ANY] * n, out_specs=[HBM_ANY] * n,
        out_shape=[jax.ShapeDtypeStruct((N_CHIP,) + g.shape[1:], g.dtype) for g in grads],
        scratch_shapes=[pltpu.SemaphoreType.DMA((N_CHIP * n,)), pltpu.SemaphoreType.DMA((N_CHIP * n,))],
    )(*grads)


def _chip_exchange(sums, *, name):
    n = len(sums)

    def body(*refs):
        ins, outs = refs[:n], refs[n:2 * n]
        send, recv = refs[2 * n:]
        x, y, c, chips = _place()
        copies = []
        for a in range(n):
            for j, chip in enumerate(chips):
                copies.append(pltpu.make_async_remote_copy(
                    src_ref=ins[a].at[2 * chip[0] + chip[1]], dst_ref=outs[a].at[j],
                    send_sem=send.at[a * 3 + j], recv_sem=recv.at[a * 3 + j],
                    device_id=(*chip, c), device_id_type=MESH))
        for cpy in copies:
            cpy.start()
        for cpy in copies:
            cpy.wait()

    return pl.pallas_call(
        body, name=name,
        in_specs=[HBM_ANY] * n, out_specs=[HBM_ANY] * n,
        out_shape=[jax.ShapeDtypeStruct((N_CHIP - 1,) + s.shape[1:], s.dtype) for s in sums],
        scratch_shapes=[pltpu.SemaphoreType.DMA((3 * n,)), pltpu.SemaphoreType.DMA((3 * n,))],
    )(*sums)


def _add_pair(grad, got, core, *, name):
    k, r, c = got.shape
    tr = _pick(r, (256, 128, 64, 32, 16))

    def body(core_ref, a_ref, b_ref, o_ref):
        del core_ref
        o_ref[...] = (a_ref[...].astype(F32) + b_ref[...].astype(F32)).astype(BF16)

    spec = pl.BlockSpec((None, tr, c), lambda q, i, core_ref: (q, i, 0))
    return pl.pallas_call(
        body, name=name,
        grid_spec=pltpu.PrefetchScalarGridSpec(
            num_scalar_prefetch=1, grid=(k, r // tr),
            in_specs=[pl.BlockSpec((None, tr, c), lambda q, i, core_ref: (2 * q + core_ref[0], i, 0)), spec],
            out_specs=spec),
        out_shape=jax.ShapeDtypeStruct(got.shape, BF16), compiler_params=_params("parallel", "parallel"),
    )(core, grad, got)


def _all_reduce_small(v, *, name):
    r = v.shape[0]

    def body(v_ref, o_ref, buf, send, recv):
        x, y, c, _ = _place()
        me = 4 * x + 2 * y + c
        buf[me] = v_ref[...]
        copies = []
        for rel in range(1, N_DEV):
            fx, fy, fc = rel >> 2 & 1, rel >> 1 & 1, rel & 1
            peer = ((1 - x) if fx else x, (1 - y) if fy else y, (1 - c) if fc else c)
            copies.append(pltpu.make_async_remote_copy(
                src_ref=v_ref, dst_ref=buf.at[me], send_sem=send.at[rel - 1], recv_sem=recv.at[rel - 1],
                device_id=peer, device_id_type=MESH))
        for cpy in copies:
            cpy.start()
        for cpy in copies:
            cpy.wait()
        acc = buf[0]
        for d in range(1, N_DEV):
            acc = acc + buf[d]
        o_ref[...] = acc

    return pl.pallas_call(
        body, name=name,
        in_specs=[pl.BlockSpec(memory_space=pltpu.VMEM)], out_specs=pl.BlockSpec(memory_space=pltpu.VMEM),
        out_shape=jax.ShapeDtypeStruct(v.shape, F32),
        scratch_shapes=[pltpu.VMEM((N_DEV, r, LANE), F32), pltpu.SemaphoreType.DMA((N_DEV - 1,)),
                        pltpu.SemaphoreType.DMA((N_DEV - 1,))],
        compiler_params=pltpu.CompilerParams(vmem_limit_bytes=VMEM_LIMIT),
    )(v)


def _adamw_math(w, g, m, v):
    m = ADAM_B1 * m + (1.0 - ADAM_B1) * g
    v = ADAM_B2 * v + (1.0 - ADAM_B2) * (g * g)
    m_hat = m / (1.0 - ADAM_B1 ** ADAM_STEP)
    v_hat = v / (1.0 - ADAM_B2 ** ADAM_STEP)
    delta = -ADAM_LR * (m_hat / (jnp.sqrt(v_hat) + ADAM_EPS) + ADAM_WD * w)
    return delta, m, v


def _adamw_layer(w, m, v, sums, recv, chip, layer, prev, after, *, name):
    nl, r, c = w.shape
    tr = _pick(r, (256, 128, 64, 32, 16))

    def body(chip_ref, w_ref, m_ref, v_ref, s_ref, p_ref, *rest):
        del chip_ref
        g_ref, d_ref, mo_ref, vo_ref = rest[-4:]
        g = s_ref[...].astype(F32)
        for k in range(N_CHIP - 1):
            g = g + p_ref[k].astype(F32)
        delta, mn, vn = _adamw_math(w_ref[...], g, m_ref[...], v_ref[...])
        g_ref[...] = g
        d_ref[...] = delta
        mo_ref[...] = mn
        vo_ref[...] = vn

    lay = pl.BlockSpec((None, tr, c), lambda i, chip_ref: (layer, i, 0))
    ins = [w, m, v, sums, recv, after] + (list(prev) if prev is not None else [])
    in_specs = [lay, lay, lay, pl.BlockSpec((None, tr, c), lambda i, chip_ref: (chip_ref[0], i, 0)),
                pl.BlockSpec((N_CHIP - 1, tr, c), lambda i, chip_ref: (0, i, 0)), HBM_ANY]
    in_specs += [HBM_ANY] * (4 if prev is not None else 0)
    return pl.pallas_call(
        body, name=name,
        grid_spec=pltpu.PrefetchScalarGridSpec(
            num_scalar_prefetch=1, grid=(r // tr,), in_specs=in_specs, out_specs=[lay] * 4),
        out_shape=[jax.ShapeDtypeStruct(w.shape, F32)] * 4,
        input_output_aliases={7 + q: q for q in range(4)} if prev is not None else {},
        compiler_params=_params("parallel"),
    )(chip, *ins)


def _adamw_small(w, g, m, v, *, name):
    def body(w_ref, g_ref, m_ref, v_ref, d_ref, mo_ref, vo_ref):
        d_ref[...], mo_ref[...], vo_ref[...] = _adamw_math(w_ref[...], g_ref[...], m_ref[...], v_ref[...])

    vm = pl.BlockSpec(memory_space=pltpu.VMEM)
    return pl.pallas_call(
        body, name=name, in_specs=[vm] * 4, out_specs=[vm] * 3,
        out_shape=[jax.ShapeDtypeStruct(w.shape, F32)] * 3,
        compiler_params=pltpu.CompilerParams(vmem_limit_bytes=VMEM_LIMIT),
    )(w, g, m, v)


def _pack(arrs):
    flat = jnp.concatenate([a.reshape(-1).astype(F32) for a in arrs])
    pad = (-flat.shape[0]) % (8 * LANE)
    return jnp.pad(flat, (0, pad)).reshape(-1, LANE)


def _unpack(packed, shapes):
    flat = packed.reshape(-1)
    out, off = [], 0
    for s in shapes:
        size = math.prod(s)
        out.append(flat[off:off + size].reshape(s))
        off += size
    return out


WEIGHTS = ['mix_norm_w', 'ffn_norm_w', 'final_norm_w', 'ssd_w_in', 'ssd_conv_w', 'ssd_conv_b', 'ssd_dt_bias',
           'ssd_a_log', 'ssd_d', 'ssd_norm_w', 'ssd_w_out', 'sc_w_in', 'sc_conv_w', 'sc_w_out', 'ffn_w_up',
           'ffn_conv_w', 'ffn_conv_b', 'ffn_w_down']
BIG = ('ssd_w_in', 'ssd_w_out', 'sc_w_in', 'sc_w_out', 'ffn_w_up', 'ffn_w_down')
SHARDED_SMALL = ('ssd_conv_w', 'sc_conv_w', 'ffn_conv_w')


def _lane_pad(v):
    return jnp.pad(v.astype(F32), (0, LANE - v.shape[0])).reshape(1, LANE)


def _gather_cols(g):
    return jnp.moveaxis(g, 0, -2).reshape(g.shape[1:-1] + (N_DEV * g.shape[-1],))


class _Gather:
    def __init__(self, bufs, tag):
        self.bufs, self.tag = bufs, tag

    def start_ici(self, after):
        self.sems = _split_start(self.bufs, _plan_gather_ici, 4 * len(self.bufs), after, name=f"ag_ici_start_{self.tag}")
        return self.sems[3]

    def hand_on(self, after):
        send, recv, bufs, _ = self.sems
        bufs = _split_wait(bufs, send, recv, after, _plan_gather_ici, name=f"ag_ici_wait_{self.tag}")
        self.sems = _split_start(bufs, _plan_gather_d2d, 3 * len(bufs), after, name=f"ag_d2d_start_{self.tag}")
        return self.sems[3]

    def finish(self, after):
        send, recv, bufs, _ = self.sems
        return _split_wait(bufs, send, recv, after, _plan_gather_d2d, name=f"ag_d2d_wait_{self.tag}")


class _Scatter:
    def __init__(self, grads, core, tag):
        self.grads, self.core, self.tag = grads, core, tag

    def start_pair(self, after):
        lands = [_land((N_CHIP,) + g.shape[1:], g.dtype) for g in self.grads]
        self.sems = _split_start(self.grads + lands, _plan_pair, N_CHIP * len(lands), after,
                                 name=f"rs_pair_start_{self.tag}")
        return self.sems[3]

    def start_chip(self, after):
        n = len(self.grads)
        send, recv, arrs, _ = self.sems
        arrs = _split_wait(arrs, send, recv, after, _plan_pair, name=f"rs_pair_wait_{self.tag}")
        self.sums = [_add_pair(g, o, self.core, name=f"rs_add_{self.tag}{a}")
                     for a, (g, o) in enumerate(zip(arrs[:n], arrs[n:]))]
        lands = [_land((N_CHIP - 1,) + s.shape[1:], s.dtype) for s in self.sums]
        self.sems = _split_start(self.sums + lands, _plan_chip, (N_CHIP - 1) * n, after,
                                 name=f"rs_chip_start_{self.tag}")
        return self.sems[3]

    def finish(self, after):
        n = len(self.grads)
        send, recv, arrs, _ = self.sems
        arrs = _split_wait(arrs, send, recv, after, _plan_chip, name=f"rs_chip_wait_{self.tag}")
        return list(zip(arrs[:n], arrs[n:]))


def kernel(x, mix_norm_w, ffn_norm_w, final_norm_w, ssd_w_in, ssd_conv_w, ssd_conv_b, ssd_dt_bias, ssd_a_log, ssd_d, ssd_norm_w, ssd_w_out, sc_w_in, sc_conv_w, sc_w_out, ffn_w_up, ffn_conv_w, ffn_conv_b, ffn_w_down, loss_target, m_mix_norm_w, m_ffn_norm_w, m_final_norm_w, m_ssd_w_in, m_ssd_conv_w, m_ssd_conv_b, m_ssd_dt_bias, m_ssd_a_log, m_ssd_d, m_ssd_norm_w, m_ssd_w_out, m_sc_w_in, m_sc_conv_w, m_sc_w_out, m_ffn_w_up, m_ffn_conv_w, m_ffn_conv_b, m_ffn_w_down, v_mix_norm_w, v_ffn_norm_w, v_final_norm_w, v_ssd_w_in, v_ssd_conv_w, v_ssd_conv_b, v_ssd_dt_bias, v_ssd_a_log, v_ssd_d, v_ssd_norm_w, v_ssd_w_out, v_sc_w_in, v_sc_conv_w, v_sc_w_out, v_ffn_w_up, v_ffn_conv_w, v_ffn_conv_b, v_ffn_w_down):
    args = locals()
    wt = {n: args[n] for n in WEIGHTS}
    mom = {n: args["m_" + n] for n in WEIGHTS}
    var = {n: args["v_" + n] for n in WEIGHTS}

    t, d = x.shape[-2], x.shape[-1]
    cur = x.reshape(t, d)
    target = loss_target.reshape(t, d)
    depth = mix_norm_w.shape[0]
    n_ssd, n_sc = ssd_w_in.shape[0], sc_w_in.shape[0]
    heads = ssd_dt_bias.shape[1]
    di = ssd_norm_w.shape[1]
    conv_dim = ssd_conv_b.shape[1]
    bc = (conv_dim - di) // 2
    in_dim = N_DEV * ssd_w_in.shape[2]
    in_pad = di + conv_dim + LANE
    ff = ffn_w_down.shape[1] * N_DEV
    me = 4 * lax.axis_index("x") + 2 * lax.axis_index("y") + lax.axis_index("c")
    me_s = me.astype(jnp.int32).reshape(1)
    core_s = lax.axis_index("c").astype(jnp.int32).reshape(1)
    chip_s = (2 * lax.axis_index("x") + lax.axis_index("y")).astype(jnp.int32).reshape(1)

    names_of = {"ssd": ('ssd_w_in', 'ssd_w_out'), "sc": ('sc_w_in', 'sc_w_out'), "ffn": ('ffn_w_up', 'ffn_w_down')}
    order = []
    for i in range(depth):
        order += [("ssd" if i % 2 == 0 else "sc", i // 2), ("ffn", i)]
    gathers = [_Gather([_cast_layer(wt[n], idx, me_s, name=f"cast_{n}{idx}") for n in names_of[kind]], f"{kind}{idx}")
               for kind, idx in order]
    conv_full = [_gather_cols(g) for g in _all_gather([wt[n] for n in SHARDED_SMALL], name="ag_conv")]
    tok_a = gathers[0].start_ici(conv_full[0])
    tok_b = gathers[0].hand_on(tok_a)
    tok_c = gathers[1].start_ici(tok_b)
    weights = [None] * len(order)
    weights[0] = gathers[0].finish(tok_c)
    ssd_cw, sc_cw, ffn_cw = conv_full
    ffn_cw = ffn_cw.reshape(depth, ffn_cw.shape[1], 2, ff)
    ffn_cb = ffn_conv_b.reshape(depth, 2, ff)
    dexp = jnp.repeat(ssd_d.astype(F32), HEAD_DIM, axis=1)

    n_sub = len(order)
    full = {n: [None] * wt[n].shape[0] for n in BIG}

    def prefetch(s, after):
        return gathers[s + 2].start_ici(after) if s + 2 < n_sub else None

    def hand_on(s, after):
        return gathers[s + 1].hand_on(after) if s + 1 < n_sub else None

    def arrive(s, after):
        if s + 1 < n_sub:
            weights[s + 1] = gathers[s + 1].finish(after)
            use(s + 1)

    def use(s):
        kind, idx = order[s]
        g_in, g_out = weights[s]
        if kind != "ffn":
            g_in = jnp.swapaxes(g_in, 0, 1).reshape(d, -1)
        if kind == "ssd":
            g_in = jnp.pad(g_in, ((0, 0), (0, in_pad - in_dim)))
        n_in, n_out = names_of[kind]
        full[n_in][idx], full[n_out][idx] = g_in, g_out.reshape(-1, d)

    use(0)
    saved = []
    for i in range(depth):
        j = i // 2
        s = 2 * i
        rec = {"x_mix": cur}
        tok = prefetch(s, cur)
        h = _rmsnorm_fwd(cur, _with_tokens(mix_norm_w[i], tok, tok_c if i == 0 else None), name=f"norm_mix{i}")
        rec["h_mix"] = h
        if i % 2 == 0:
            zx = _mm_nn(h, full['ssd_w_in'][j], out_dtype=BF16, name=f"ssd_in{j}")
            cb = _with_tokens(ssd_conv_b[j].reshape(1, conv_dim), hand_on(s, zx))
            xbc = _ssd_conv_fwd(zx, ssd_cw[j], cb, di, name=f"ssd_conv{j}")
            ssd_vecs = (_lane_pad(ssd_dt_bias[j]), _lane_pad(ssd_a_log[j]), dexp[j].reshape(1, di),
                        ssd_norm_w[j].reshape(1, di))
            yn, y, states = _ssd_fwd(xbc, zx, *ssd_vecs, name=f"ssd_core{j}")
            arrive(s, yn)
            cur = _mm_nn(yn, full['ssd_w_out'][j], res=cur, out_dtype=F32, name=f"ssd_out{j}")
            rec.update(zx=zx, xbc=xbc, yn=yn, y=y, states=states, vecs=ssd_vecs)
        else:
            p3 = _mm_nn(h, full['sc_w_in'][j], out_dtype=BF16, out_parts=3, name=f"sc_in{j}")
            act = _sc_act_fwd(p3, _with_tokens(sc_cw[j], hand_on(s, p3)), name=f"sc_act{j}")
            arrive(s, act)
            cur = _mm_nn(act, full['sc_w_out'][j], res=cur, out_dtype=F32, name=f"sc_out{j}")
            rec.update(p3=p3, act=act)
        s += 1
        rec["x_ffn"] = cur
        h = _rmsnorm_fwd(cur, _with_tokens(ffn_norm_w[i], prefetch(s, cur)), name=f"norm_ffn{i}")
        u3 = _lin_in_fwd(h, full['ffn_w_up'][i], 2, name=f"ffn_up{i}")
        act = _ffn_act_fwd(u3, ffn_cw[i], _with_tokens(ffn_cb[i], hand_on(s, u3)), name=f"ffn_act{i}")
        arrive(s, act)
        cur = _mm_nn(act, full['ffn_w_down'][i], res=cur, out_dtype=F32, name=f"ffn_down{i}")
        rec.update(h_ffn=h, u3=u3, ffn_act=act)
        saved.append(rec)

    dx, dxb, dw_final, loss8 = _loss_head(cur, final_norm_w, target, name="loss_head")

    small = {n: [None] * wt[n].shape[0] for n in WEIGHTS if n not in BIG and n != 'final_norm_w'}
    scatters = [None] * n_sub
    pending = None

    def chip_step(after):
        return pending.start_chip(after) if pending is not None else None

    for i in reversed(range(depth)):
        j = i // 2
        rec = saved[i]
        nb_up = ffn_w_up.shape[2]
        da = _mm_nt(dxb, full['ffn_w_down'][i], out_dtype=BF16, name=f"ffn_down_dx{i}")
        g_down = _mm_tn(rec["ffn_act"], dxb, out_dtype=BF16, name=f"ffn_down_dw{i}")
        du3, dcw, dcb = _ffn_act_bwd(rec["u3"], da, ffn_cw[i], _with_tokens(ffn_cb[i], chip_step(da)),
                                     name=f"ffn_act_bwd{i}")
        dh = _lin_in_dx(du3, full['ffn_w_up'][i], name=f"ffn_up_dx{i}")
        g_up = _lin_in_dw(rec["h_ffn"], du3, nb_up, name=f"ffn_up_dw{i}")
        pending = scatters[2 * i + 1] = _Scatter([g_up, g_down.reshape(N_DEV, ff // N_DEV, d)], core_s, f"ffn{i}")
        tok = pending.start_pair(dh)
        dx, dxb, dwn = _rmsnorm_bwd(dh, rec["x_ffn"], _with_tokens(ffn_norm_w[i], tok), dx, name=f"norm_ffn_bwd{i}")
        small['ffn_conv_w'][i] = dcw.reshape(dcw.shape[0], 2 * ff)
        small['ffn_conv_b'][i] = dcb.reshape(2 * ff)
        small['ffn_norm_w'][i] = dwn.sum(axis=0)

        if i % 2 == 0:
            zx, xbc = rec["zx"], rec["xbc"]
            cw, cb = ssd_cw[j], ssd_conv_b[j].reshape(1, conv_dim)
            dyn = _mm_nt(dxb, full['ssd_w_out'][j], out_dtype=BF16, name=f"ssd_out_dx{j}")
            g_out = _mm_tn(rec["yn"], dxb, out_dtype=BF16, name=f"ssd_out_dw{j}")
            bias_t = _with_tokens(rec["vecs"][0], chip_step(dyn))
            dzx, dxs, db, dc, ddt_g, vec_acc, dnw, ddexp = _ssd_bwd(
                dyn, rec["y"], xbc, zx, rec["states"], bias_t, *rec["vecs"][1:], name=f"ssd_core_bwd{j}")
            dzx, dcw_x, dcb_x = _ssd_conv_bwd(zx, dxs, cw, cb, dzx, di, 0, name=f"ssd_conv_bwd_x{j}")
            dzx, dcw_b, dcb_b = _ssd_conv_bwd(zx, db, cw, cb, dzx, di, di, name=f"ssd_conv_bwd_b{j}")
            dzx, dcw_c, dcb_c = _ssd_conv_bwd(zx, dc, cw, cb, dzx, di, di + bc, name=f"ssd_conv_bwd_c{j}")
            dzx = _ssd_put_ddt(ddt_g, dzx, (di + conv_dim) // LANE, name=f"ssd_put_ddt{j}")
            g_in = _mm_tn(rec["h_mix"], dzx, out_dtype=BF16, name=f"ssd_in_dw{j}")
            g_in = jnp.swapaxes(g_in[:, :in_dim].reshape(d, N_DEV, in_dim // N_DEV), 0, 1)
            dh = _mm_nt(dzx, full['ssd_w_in'][j], out_dtype=F32, name=f"ssd_in_dx{j}")
            small['ssd_conv_w'][j] = jnp.concatenate([dcw_x, dcw_b, dcw_c], axis=1)
            small['ssd_conv_b'][j] = jnp.concatenate([dcb_x, dcb_b, dcb_c], axis=1).reshape(conv_dim)
            small['ssd_a_log'][j] = vec_acc[:, 0, :heads].sum(axis=0)
            small['ssd_dt_bias'][j] = vec_acc[:, 1, :heads].sum(axis=0)
            small['ssd_d'][j] = ddexp[:, 0, :].reshape(heads, HEAD_DIM).sum(axis=1)
            small['ssd_norm_w'][j] = dnw[:, 0, :].reshape(di)
            g_out = g_out.reshape(N_DEV, di // N_DEV, d)
        else:
            nb_in = sc_w_in.shape[2]
            da = _mm_nt(dxb, full['sc_w_out'][j], out_dtype=BF16, name=f"sc_out_dx{j}")
            g_out = _mm_tn(rec["act"], dxb, out_dtype=BF16, name=f"sc_out_dw{j}")
            dp3, dcw = _sc_act_bwd(rec["p3"], da, _with_tokens(sc_cw[j], chip_step(da)), name=f"sc_act_bwd{j}")
            g_in = _mm_tn(rec["h_mix"], dp3, out_dtype=BF16, name=f"sc_in_dw{j}")
            g_in = jnp.swapaxes(g_in.reshape(d, N_DEV, nb_in), 0, 1)
            dh = _mm_nt(dp3, full['sc_w_in'][j], out_dtype=F32, name=f"sc_in_dx{j}")
            small['sc_conv_w'][j] = dcw
            g_out = g_out.reshape(N_DEV, g_out.shape[0] // N_DEV, d)
        pending = scatters[2 * i] = _Scatter([g_in, g_out], core_s, f"{order[2 * i][0]}{j}")
        tok = pending.start_pair(dh)
        dx, dxb, dwn = _rmsnorm_bwd(dh, rec["x_mix"], _with_tokens(mix_norm_w[i], tok), dx, name=f"norm_mix_bwd{i}")
        small['mix_norm_w'][i] = dwn.sum(axis=0)
    tok_last = chip_step(dx)

    small_names = [n for n in WEIGHTS if n not in BIG]
    partial = {n: jnp.stack(small[n]) for n in small}
    partial['final_norm_w'] = dw_final.sum(axis=0)
    full_shapes = [partial[n].shape for n in small_names]
    packed = _pack([loss8.sum().reshape(1)] + [partial[n] for n in small_names])
    total = _unpack(_all_reduce_small(packed, name="ar_small"), [(1,)] + full_shapes)
    loss = total[0].reshape(())
    grads = dict(zip(small_names, total[1:]))
    for n in SHARDED_SMALL:
        nb = wt[n].shape[-1]
        grads[n] = lax.dynamic_slice_in_dim(grads[n], me * nb, nb, axis=grads[n].ndim - 1)

    delta, new_m, new_v = {}, {}, {}
    shapes = [wt[n].shape for n in small_names]
    outs = _adamw_small(*[_pack([src[n] for n in small_names]) for src in (wt, grads, mom, var)], name="adamw_small")
    for dst, packed_out in zip((delta, new_m, new_v), outs):
        dst.update(zip(small_names, _unpack(packed_out, shapes)))
    parts = {n: [None] * wt[n].shape[0] for n in BIG}
    for s in range(1, n_sub):
        kind, idx = order[s]
        parts[names_of[kind][0]][idx], parts[names_of[kind][1]][idx] = scatters[s].finish(tok_last)
    first_in, first_out = names_of[order[0][0]]
    last_out = tok_last
    jobs = [(n, layer) for n in reversed(BIG) for layer in reversed(range(wt[n].shape[0]))]
    jobs.sort(key=lambda job: parts[job[0]][job[1]] is None)
    chain = {n: None for n in BIG}
    for n, layer in jobs:
        if parts[n][layer] is None:
            parts[first_in][0], parts[first_out][0] = scatters[0].finish(last_out)
        chain[n] = _adamw_layer(wt[n], mom[n], var[n], *parts[n][layer], chip_s, layer, chain[n], last_out,
                                name=f"adamw_{n}{layer}")
        last_out = chain[n][1]
    for n in BIG:
        grads[n], delta[n], new_m[n], new_v[n] = chain[n]

    return (loss, dx.reshape(x.shape), *[grads[n] for n in WEIGHTS], *[delta[n] for n in WEIGHTS],
            *[new_m[n] for n in WEIGHTS], *[new_v[n] for n in WEIGHTS])
```

```python
import functools
import math

import jax
import jax.numpy as jnp
from jax import lax
from jax.experimental import pallas as pl
from jax.experimental.pallas import tpu as pltpu

F32 = jnp.float32
BF16 = jnp.bfloat16
MESH = pl.DeviceIdType.MESH

N_DEV = 8
N_CHIP = 4
EPS = 1e-5
HEAD_DIM = 64
STATE = 128
CHUNK = 128
PAIR = 2 * HEAD_DIM
GROUP_W = 8 * HEAD_DIM
HALO = 16
LANE = 128
VMEM_LIMIT = 56 * 1024 * 1024

ADAM_LR = 0.001
ADAM_B1 = 0.9
ADAM_B2 = 0.999
ADAM_EPS = 1e-08
ADAM_WD = 0.01
ADAM_STEP = 10


def _pick(n, candidates):
    for c in candidates:
        if c <= n and n % c == 0:
            return c
    return n


def _params(*sem):
    return pltpu.CompilerParams(dimension_semantics=sem, vmem_limit_bytes=VMEM_LIMIT)


def _sigmoid(x):
    return 0.5 * jnp.tanh(0.5 * x) + 0.5


_DIMS = {
    "nn": (((1,), (0,)), ((), ())),
    "nt": (((1,), (1,)), ((), ())),
    "tn": (((0,), (0,)), ((), ())),
}


def _matmul(mode, a, b, *, grid, a_spec, b_spec, o_spec, out_shape, acc_shape, name, res=None, res_spec=None,
            part_fn=None):
    nk = grid[2]
    dims = _DIMS[mode]
    if part_fn is None:
        part_fn = lambda a_ref, b_ref: lax.dot_general(a_ref[...], b_ref[...], dims, preferred_element_type=F32)

    def body(*refs):
        if res is None:
            a_ref, b_ref, o_ref = refs[:3]
            r_ref, scratch = None, refs[3:]
        else:
            a_ref, b_ref, r_ref, o_ref = refs[:4]
            scratch = refs[4:]
        part = part_fn(a_ref, b_ref)

        def finish(acc):
            if r_ref is not None:
                acc = acc + r_ref[...]
            o_ref[...] = acc.astype(o_ref.dtype)

        if nk == 1:
            finish(part)
        else:
            acc_ref = scratch[0]
            k = pl.program_id(2)

            @pl.when(k == 0)
            def _():
                acc_ref[...] = part

            @pl.when(k > 0)
            def _():
                acc_ref[...] += part

            @pl.when(k == nk - 1)
            def _():
                finish(acc_ref[...])

    in_specs = [a_spec, b_spec] + ([res_spec] if res is not None else [])
    args = (a, b) + ((res,) if res is not None else ())
    return pl.pallas_call(
        body, name=name, grid=grid, in_specs=in_specs, out_specs=o_spec, out_shape=out_shape,
        scratch_shapes=[pltpu.VMEM(acc_shape, F32)] if nk > 1 else [],
        compiler_params=_params("parallel", "parallel", "arbitrary"),
    )(*args)


def _mm_nn(a, b, *, out_dtype, res=None, out_parts=1, name):
    m, kd = a.shape
    n = b.shape[1]
    c = n // out_parts
    tm = _pick(m, (512, 256, 128))
    tn = _pick(c, (1152, 1024, 512, 384, 256, 128))
    tk = kd if kd <= 2048 else _pick(kd, (2816, 2048, 1024, 512, 256, 128))
    grid = (n // tn, m // tm, kd // tk)
    if out_parts == 1:
        o_spec = pl.BlockSpec((tm, tn), lambda j, i, k: (i, j))
        out_shape = jax.ShapeDtypeStruct((m, n), out_dtype)
    else:
        o_spec = _stacked_spec(tm, tn, c, lambda j, i, k: (i, j))
        out_shape = jax.ShapeDtypeStruct((out_parts, m, c), out_dtype)
    return _matmul(
        "nn", a, b, res=res, grid=grid, name=name,
        a_spec=pl.BlockSpec((tm, tk), lambda j, i, k: (i, k)),
        b_spec=pl.BlockSpec((tk, tn), lambda j, i, k: (k, j)),
        res_spec=pl.BlockSpec((tm, tn), lambda j, i, k: (i, j)),
        o_spec=o_spec, out_shape=out_shape, acc_shape=(tm, tn))


def _stacked_spec(rows, width, c, row_col):
    per = c // width

    def index(j, i, k):
        r, q = row_col(j, i, k)
        return q // per, r, q % per

    return pl.BlockSpec((None, rows, width), index)


def _mm_nt(a, b, *, out_dtype, name):
    stacked = a.ndim == 3
    m = a.shape[-2]
    n, kd = b.shape
    c = a.shape[-1]
    tm = _pick(m, (512, 256, 128))
    tn = _pick(n, (1408, 1024, 512, 256, 128))
    tk = c if c <= 2048 else _pick(c, (3456, 2816, 2048, 1024, 512, 384, 256, 128))
    grid = (n // tn, m // tm, kd // tk)
    a_spec = (_stacked_spec(tm, tk, c, lambda j, i, k: (i, k)) if stacked
              else pl.BlockSpec((tm, tk), lambda j, i, k: (i, k)))
    return _matmul(
        "nt", a, b, grid=grid, name=name, a_spec=a_spec,
        b_spec=pl.BlockSpec((tn, tk), lambda j, i, k: (j, k)),
        o_spec=pl.BlockSpec((tm, tn), lambda j, i, k: (i, j)),
        out_shape=jax.ShapeDtypeStruct((m, n), out_dtype), acc_shape=(tm, tn))


def _mm_tn(a, b, *, out_dtype, name):
    stacked = b.ndim == 3
    kd, m = a.shape
    c = b.shape[-1]
    n = c * (b.shape[0] if stacked else 1)
    tm = _pick(m, (512, 256, 128))
    tn = _pick(c, (1152, 1024, 512, 384, 256, 128))
    tk = _pick(kd, (2048, 1024, 512, 256, 128))
    grid = (n // tn, m // tm, kd // tk)
    b_spec = (_stacked_spec(tk, tn, c, lambda j, i, k: (k, j)) if stacked
              else pl.BlockSpec((tk, tn), lambda j, i, k: (k, j)))
    return _matmul(
        "tn", a, b, grid=grid, name=name,
        a_spec=pl.BlockSpec((tk, tm), lambda j, i, k: (k, i)), b_spec=b_spec,
        o_spec=pl.BlockSpec((tm, tn), lambda j, i, k: (i, j)),
        out_shape=jax.ShapeDtypeStruct((m, n), out_dtype), acc_shape=(tm, tn))


def _in_tile(nb, c):
    return math.gcd(nb, c)


def _lin_in_fwd(h, wg, parts, *, name):
    t, d = h.shape
    nb = wg.shape[2]
    c = N_DEV * nb // parts
    w = _in_tile(nb, c)
    nbw, cw = nb // w, c // w
    tm = _pick(t, (1024,) if w < 512 else (512, 256, 128))
    grid = (N_DEV * nbw, t // tm, 1)
    return _matmul(
        "nn", h, wg, grid=grid, name=name,
        a_spec=pl.BlockSpec((tm, d), lambda j, i, k: (i, 0)),
        b_spec=pl.BlockSpec((None, d, w), lambda j, i, k: (j // nbw, 0, j % nbw)),
        o_spec=pl.BlockSpec((None, tm, w), lambda j, i, k: (j // cw, i, j % cw)),
        out_shape=jax.ShapeDtypeStruct((parts, t, c), BF16), acc_shape=(tm, w))


def _lin_in_dx(dact, wg, *, name):
    parts, t, c = dact.shape
    d, nb = wg.shape[1], wg.shape[2]
    per = c // (2 * nb)
    tm = _pick(t, (512, 256, 128))
    tn = _pick(d, (1024, 512, 256, 128))
    grid = (d // tn, t // tm, N_DEV // 2)

    def pair(a_ref, b_ref):
        return (lax.dot_general(a_ref[:, :nb], b_ref[0], _DIMS["nt"], preferred_element_type=F32)
                + lax.dot_general(a_ref[:, nb:], b_ref[1], _DIMS["nt"], preferred_element_type=F32))

    return _matmul(
        "nt", dact, wg, grid=grid, name=name, part_fn=pair,
        a_spec=pl.BlockSpec((None, tm, 2 * nb), lambda j, i, k: (k // per, i, k % per)),
        b_spec=pl.BlockSpec((2, tn, nb), lambda j, i, k: (k, j, 0)),
        o_spec=pl.BlockSpec((tm, tn), lambda j, i, k: (i, j)),
        out_shape=jax.ShapeDtypeStruct((t, d), F32), acc_shape=(tm, tn))


def _lin_in_dw(h, dact, nb, *, name):
    t, d = h.shape
    parts, _, c = dact.shape
    w = _in_tile(nb, c)
    nbw, cw = nb // w, c // w
    tm = _pick(d, (1024,) if w < 512 else (512, 256, 128))
    tk = _pick(t, (2048, 1024, 512, 256, 128))
    grid = (N_DEV * nbw, d // tm, t // tk)
    return _matmul(
        "tn", h, dact, grid=grid, name=name,
        a_spec=pl.BlockSpec((tk, tm), lambda j, i, k: (k, i)),
        b_spec=pl.BlockSpec((None, tk, w), lambda j, i, k: (j // cw, k, j % cw)),
        o_spec=pl.BlockSpec((None, tm, w), lambda j, i, k: (j // nbw, i, j % nbw)),
        out_shape=jax.ShapeDtypeStruct((N_DEV, d, nb), BF16), acc_shape=(tm, w))


def _fold8(v):
    rows, c = v.shape
    return v.reshape(rows // 8, 8, c).sum(axis=0)


def _accumulate(ref, val, first):
    @pl.when(first)
    def _():
        ref[...] = val

    @pl.when(jnp.logical_not(first))
    def _():
        ref[...] += val


def _cast_layer(w_stack, layer, me, *, name):
    _, r, c = w_stack.shape
    tr = _pick(r, (256, 128, 64, 32, 16))

    def body(me_ref, w_ref, o_ref):
        del me_ref
        o_ref[...] = w_ref[...].astype(BF16)

    return pl.pallas_call(
        body, name=name,
        grid_spec=pltpu.PrefetchScalarGridSpec(
            num_scalar_prefetch=1, grid=(r // tr,),
            in_specs=[pl.BlockSpec((None, tr, c), lambda i, me_ref: (layer, i, 0))],
            out_specs=pl.BlockSpec((None, tr, c), lambda i, me_ref: (me_ref[0], i, 0))),
        out_shape=jax.ShapeDtypeStruct((N_DEV, r, c), BF16),
        compiler_params=_params("parallel"),
    )(me, w_stack)


def _rmsnorm_fwd(x, w, *, name):
    t, d = x.shape
    tt = _pick(t, (256, 128))

    def body(x_ref, w_ref, o_ref):
        xv = x_ref[...]
        r = lax.rsqrt(jnp.mean(xv * xv, axis=1, keepdims=True) + EPS)
        o_ref[...] = (xv * r * w_ref[...]).astype(BF16)

    return pl.pallas_call(
        body, name=name, grid=(t // tt,),
        in_specs=[pl.BlockSpec((tt, d), lambda i: (i, 0)), pl.BlockSpec((1, d), lambda i: (0, 0))],
        out_specs=pl.BlockSpec((tt, d), lambda i: (i, 0)),
        out_shape=jax.ShapeDtypeStruct((t, d), BF16),
        compiler_params=_params("parallel"),
    )(x, w.reshape(1, d))


def _rmsnorm_bwd(dh, x, w, dres, *, name):
    t, d = x.shape
    tt = _pick(t, (256, 128))

    def body(dh_ref, x_ref, w_ref, dres_ref, dx_ref, dxb_ref, dw_ref):
        xv = x_ref[...]
        r = lax.rsqrt(jnp.mean(xv * xv, axis=1, keepdims=True) + EPS)
        xhat = xv * r
        dhv = dh_ref[...].astype(F32)
        dxhat = dhv * w_ref[...]
        dx = dres_ref[...] + r * (dxhat - xhat * jnp.mean(dxhat * xhat, axis=1, keepdims=True))
        dx_ref[...] = dx
        dxb_ref[...] = dx.astype(BF16)
        _accumulate(dw_ref, _fold8(dhv * xhat), pl.program_id(0) == 0)

    row = pl.BlockSpec((tt, d), lambda i: (i, 0))
    return pl.pallas_call(
        body, name=name, grid=(t // tt,),
        in_specs=[row, row, pl.BlockSpec((1, d), lambda i: (0, 0)), row],
        out_specs=[row, row, pl.BlockSpec((8, d), lambda i: (0, 0))],
        out_shape=[jax.ShapeDtypeStruct((t, d), F32), jax.ShapeDtypeStruct((t, d), BF16),
                   jax.ShapeDtypeStruct((8, d), F32)],
        compiler_params=_params("arbitrary"),
    )(dh, x, w.reshape(1, d), dres)


def _loss_head(x, w, target, *, name):
    t, d = x.shape
    tt = _pick(t, (256, 128))

    def body(x_ref, w_ref, tg_ref, dx_ref, dxb_ref, dw_ref, ls_ref):
        xv = x_ref[...]
        wv = w_ref[...]
        r = lax.rsqrt(jnp.mean(xv * xv, axis=1, keepdims=True) + EPS)
        xhat = xv * r
        err = xhat * wv - tg_ref[...]
        dy = err * (1.0 / d)
        dxhat = dy * wv
        dx = r * (dxhat - xhat * jnp.mean(dxhat * xhat, axis=1, keepdims=True))
        dx_ref[...] = dx
        dxb_ref[...] = dx.astype(BF16)
        first = pl.program_id(0) == 0
        _accumulate(dw_ref, _fold8(dy * xhat), first)
        _accumulate(ls_ref, _fold8(err * err) * (0.5 / d), first)

    row = pl.BlockSpec((tt, d), lambda i: (i, 0))
    acc = pl.BlockSpec((8, d), lambda i: (0, 0))
    return pl.pallas_call(
        body, name=name, grid=(t // tt,),
        in_specs=[row, pl.BlockSpec((1, d), lambda i: (0, 0)), row],
        out_specs=[row, row, acc, acc],
        out_shape=[jax.ShapeDtypeStruct((t, d), F32), jax.ShapeDtypeStruct((t, d), BF16),
                   jax.ShapeDtypeStruct((8, d), F32), jax.ShapeDtypeStruct((8, d), F32)],
        compiler_params=_params("arbitrary"),
    )(x, w.reshape(1, d), target)


def _conv_causal(e, tap, width):
    acc = None
    for k in range(width):
        s = width - 1 - k
        term = (e if s == 0 else pltpu.roll(e, s, 0)) * tap(k)
        acc = term if acc is None else acc + term
    return acc


def _conv_anticausal(e, tap, width):
    rows = e.shape[0]
    acc = None
    for k in range(width):
        s = width - 1 - k
        term = (e if s == 0 else pltpu.roll(e, rows - s, 0)) * tap(k)
        acc = term if acc is None else acc + term
    return acc


def _extend(prev, cur, nxt, first, last):
    parts = []
    if prev is not None:
        parts.append(jnp.where(first, 0.0, prev.astype(F32)))
    parts.append(cur.astype(F32))
    if nxt is not None:
        parts.append(jnp.where(last, 0.0, nxt.astype(F32)))
    return jnp.concatenate(parts, axis=0)


def _prev_idx(i, tt):
    return jnp.maximum(i * (tt // HALO) - 1, 0)


def _next_idx(i, tt, t):
    return jnp.minimum((i + 1) * (tt // HALO), t // HALO - 1)


def _ffn_act_fwd(u3, cw, cb, *, name):
    _, t, f = u3.shape
    tt = _pick(t, (512, 256, 128))
    tc = _pick(f, (512, 256, 128))
    width = cw.shape[0]

    def body(u_ref, up_ref, w_ref, b_ref, o_ref, pre_ref):
        first = pl.program_id(1) == 0
        pre = []
        for p in range(2):
            e = _extend(up_ref[p], u_ref[p], None, first, None)
            pre.append(_conv_causal(e, lambda k: w_ref[k, p:p + 1, :], width)[HALO:] + b_ref[p:p + 1, :])
            pre_ref[p] = pre[p].astype(BF16)
        g, v = pre
        o_ref[...] = (g * _sigmoid(g) * v).astype(BF16)

    return pl.pallas_call(
        body, name=name, grid=(f // tc, t // tt),
        in_specs=[pl.BlockSpec((2, tt, tc), lambda j, i: (0, i, j)),
                  pl.BlockSpec((2, HALO, tc), lambda j, i: (0, _prev_idx(i, tt), j)),
                  pl.BlockSpec((width, 2, tc), lambda j, i: (0, 0, j)),
                  pl.BlockSpec((2, tc), lambda j, i: (0, j))],
        out_specs=[pl.BlockSpec((tt, tc), lambda j, i: (i, j)), pl.BlockSpec((2, tt, tc), lambda j, i: (0, i, j))],
        out_shape=[jax.ShapeDtypeStruct((t, f), BF16), jax.ShapeDtypeStruct((2, t, f), BF16)],
        compiler_params=_params("parallel", "parallel"),
    )(u3, u3, cw, cb)


def _ffn_act_bwd(u3, pre3, da, cw, *, name):
    _, t, f = u3.shape
    tt = _pick(t, (512, 256, 128))
    tc = _pick(f, (512, 256, 128))
    width = cw.shape[0]
    nt = t // tt
    rows = tt + HALO

    def body(u_ref, pre_ref, pren_ref, da_ref, dan_ref, w_ref, du_ref, dcw_ref, dcb_ref):
        i = pl.program_id(1)
        first, last = i == 0, i == nt - 1
        g, v = (_extend(None, pre_ref[p], pren_ref[p], None, False) for p in range(2))
        dae = _extend(None, da_ref[...], dan_ref[...], None, last)
        sg = _sigmoid(g)
        dpre = (dae * v * (sg * (1.0 + g * (1.0 - sg))), dae * (g * sg))

        @pl.when(first)
        def _():
            dcw_ref[...] = jnp.zeros_like(dcw_ref)
            dcb_ref[...] = jnp.zeros_like(dcb_ref)

        for p in range(2):
            u = u_ref[p].astype(F32)
            du = None
            for k in range(width):
                s = width - 1 - k
                d = (dpre[p] if s == 0 else pltpu.roll(dpre[p], rows - s, 0))[:tt]
                term = d * w_ref[k, p:p + 1, :]
                du = term if du is None else du + term
                dcw_ref[k, p:p + 1, :] += jnp.sum(d * u, axis=0, keepdims=True)
                if s == 0:
                    dcb_ref[p:p + 1, :] += jnp.sum(d, axis=0, keepdims=True)
            du_ref[p] = du.astype(BF16)

    cur3 = pl.BlockSpec((2, tt, tc), lambda j, i: (0, i, j))
    return pl.pallas_call(
        body, name=name, grid=(f // tc, nt),
        in_specs=[cur3, cur3,
                  pl.BlockSpec((2, HALO, tc), lambda j, i: (0, _next_idx(i, tt, t), j)),
                  pl.BlockSpec((tt, tc), lambda j, i: (i, j)),
                  pl.BlockSpec((HALO, tc), lambda j, i: (_next_idx(i, tt, t), j)),
                  pl.BlockSpec((width, 2, tc), lambda j, i: (0, 0, j))],
        out_specs=[cur3,
                   pl.BlockSpec((width, 2, tc), lambda j, i: (0, 0, j)),
                   pl.BlockSpec((2, tc), lambda j, i: (0, j))],
        out_shape=[jax.ShapeDtypeStruct((2, t, f), BF16), jax.ShapeDtypeStruct((width, 2, f), F32),
                   jax.ShapeDtypeStruct((2, f), F32)],
        compiler_params=_params("parallel", "arbitrary"),
    )(u3, pre3, pre3, da, da, cw)


def _sc_act_fwd(p3, cw, *, name):
    _, t, c = p3.shape
    tt = _pick(t, (512, 256, 128))
    tc = _pick(c, (512, 256, 128))
    width = cw.shape[0]

    def body(p_ref, pp_ref, w_ref, o_ref):
        first = pl.program_id(1) == 0
        q = _extend(pp_ref[1], p_ref[1], None, first, None) * _extend(pp_ref[2], p_ref[2], None, first, None)
        cq = _conv_causal(q, lambda k: w_ref[k:k + 1, :], width)[HALO:]
        o_ref[...] = (p_ref[0].astype(F32) * cq).astype(BF16)

    return pl.pallas_call(
        body, name=name, grid=(c // tc, t // tt),
        in_specs=[pl.BlockSpec((3, tt, tc), lambda j, i: (0, i, j)),
                  pl.BlockSpec((3, HALO, tc), lambda j, i: (0, _prev_idx(i, tt), j)),
                  pl.BlockSpec((width, tc), lambda j, i: (0, j))],
        out_specs=pl.BlockSpec((tt, tc), lambda j, i: (i, j)),
        out_shape=jax.ShapeDtypeStruct((t, c), BF16),
        compiler_params=_params("parallel", "parallel"),
    )(p3, p3, cw)


def _sc_act_bwd(p3, da, cw, *, name):
    _, t, c = p3.shape
    tt = _pick(t, (512, 256, 128))
    tc = _pick(c, (512, 256, 128))
    width = cw.shape[0]
    nt = t // tt
    ctr = slice(HALO, HALO + tt)

    def body(p_ref, pp_ref, pn_ref, da_ref, dan_ref, w_ref, dp_ref, dcw_ref):
        i = pl.program_id(1)
        first, last = i == 0, i == nt - 1
        tap = lambda k: w_ref[k:k + 1, :]
        bg, cg, hh = (_extend(pp_ref[p], p_ref[p], pn_ref[p], first, last) for p in range(3))
        q = cg * hh
        cq = _conv_causal(q, tap, width)
        dae = _extend(jnp.zeros((HALO, tc), F32), da_ref[...], dan_ref[...], False, last)
        dcq = dae * bg
        dq = _conv_anticausal(dcq, tap, width)[ctr]
        dp_ref[0] = (dae * cq)[ctr].astype(BF16)
        dp_ref[1] = (dq * hh[ctr]).astype(BF16)
        dp_ref[2] = (dq * cg[ctr]).astype(BF16)

        @pl.when(first)
        def _():
            dcw_ref[...] = jnp.zeros_like(dcw_ref)

        dc = dcq[ctr]
        for k in range(width):
            s = width - 1 - k
            qs = (q if s == 0 else pltpu.roll(q, s, 0))[ctr]
            dcw_ref[k:k + 1, :] += jnp.sum(dc * qs, axis=0, keepdims=True)

    return pl.pallas_call(
        body, name=name, grid=(c // tc, nt),
        in_specs=[pl.BlockSpec((3, tt, tc), lambda j, i: (0, i, j)),
                  pl.BlockSpec((3, HALO, tc), lambda j, i: (0, _prev_idx(i, tt), j)),
                  pl.BlockSpec((3, HALO, tc), lambda j, i: (0, _next_idx(i, tt, t), j)),
                  pl.BlockSpec((tt, tc), lambda j, i: (i, j)),
                  pl.BlockSpec((HALO, tc), lambda j, i: (_next_idx(i, tt, t), j)),
                  pl.BlockSpec((width, tc), lambda j, i: (0, j))],
        out_specs=[pl.BlockSpec((3, tt, tc), lambda j, i: (0, i, j)),
                   pl.BlockSpec((width, tc), lambda j, i: (0, j))],
        out_shape=[jax.ShapeDtypeStruct((3, t, c), BF16), jax.ShapeDtypeStruct((width, c), F32)],
        compiler_params=_params("parallel", "arbitrary"),
    )(p3, p3, p3, da, da, cw)


def _ssd_conv_fwd(zx, cw, cb, col0, *, name):
    t = zx.shape[0]
    width, c = cw.shape
    tt = _pick(t, (512, 256, 128))
    tc = _pick(math.gcd(c, col0), (512, 256, 128))
    off = col0 // tc

    def body(x_ref, xp_ref, w_ref, b_ref, o_ref):
        first = pl.program_id(1) == 0
        e = _extend(xp_ref[...], x_ref[...], None, first, None)
        pre = _conv_causal(e, lambda k: w_ref[k:k + 1, :], width)[HALO:] + b_ref[...]
        o_ref[...] = (pre * _sigmoid(pre)).astype(BF16)

    return pl.pallas_call(
        body, name=name, grid=(c // tc, t // tt),
        in_specs=[pl.BlockSpec((tt, tc), lambda j, i: (i, off + j)),
                  pl.BlockSpec((HALO, tc), lambda j, i: (_prev_idx(i, tt), off + j)),
                  pl.BlockSpec((width, tc), lambda j, i: (0, j)),
                  pl.BlockSpec((1, tc), lambda j, i: (0, j))],
        out_specs=pl.BlockSpec((tt, tc), lambda j, i: (i, j)),
        out_shape=jax.ShapeDtypeStruct((t, c), BF16),
        compiler_params=_params("parallel", "parallel"),
    )(zx, zx, cw, cb)


def _ssd_conv_bwd(zx, dxc, cw, cb, dzx, col0, woff, *, name):
    t = zx.shape[0]
    width = cw.shape[0]
    c = dxc.shape[1]
    tt = _pick(t, (512, 256, 128))
    tc = _pick(math.gcd(math.gcd(c, col0), woff) if woff else math.gcd(c, col0), (512, 256, 128))
    nt = t // tt
    xoff, wo = (col0 + woff) // tc, woff // tc
    ctr = slice(HALO, HALO + tt)

    def body(x_ref, xp_ref, xn_ref, d_ref, dn_ref, w_ref, b_ref, dzx_in, dzx_ref, dcw_ref, dcb_ref):
        del dzx_in
        i = pl.program_id(1)
        first, last = i == 0, i == nt - 1
        tap = lambda k: w_ref[k:k + 1, :]
        e = _extend(xp_ref[...], x_ref[...], xn_ref[...], first, last)
        pre = _conv_causal(e, tap, width) + b_ref[...]
        sg = _sigmoid(pre)
        de = _extend(jnp.zeros((HALO, tc), F32), d_ref[...], dn_ref[...], False, last)
        dpre = de * (sg * (1.0 + pre * (1.0 - sg)))
        dzx_ref[...] = _conv_anticausal(dpre, tap, width)[ctr].astype(BF16)

        @pl.when(first)
        def _():
            dcw_ref[...] = jnp.zeros_like(dcw_ref)
            dcb_ref[...] = jnp.zeros_like(dcb_ref)

        dc = dpre[ctr]
        dcb_ref[...] += jnp.sum(dc, axis=0, keepdims=True)
        for k in range(width):
            s = width - 1 - k
            xs = (e if s == 0 else pltpu.roll(e, s, 0))[ctr]
            dcw_ref[k:k + 1, :] += jnp.sum(dc * xs, axis=0, keepdims=True)

    return pl.pallas_call(
        body, name=name, grid=(c // tc, nt),
        in_specs=[pl.BlockSpec((tt, tc), lambda j, i: (i, xoff + j)),
                  pl.BlockSpec((HALO, tc), lambda j, i: (_prev_idx(i, tt), xoff + j)),
                  pl.BlockSpec((HALO, tc), lambda j, i: (_next_idx(i, tt, t), xoff + j)),
                  pl.BlockSpec((tt, tc), lambda j, i: (i, j)),
                  pl.BlockSpec((HALO, tc), lambda j, i: (_next_idx(i, tt, t), j)),
                  pl.BlockSpec((width, tc), lambda j, i: (0, wo + j)),
                  pl.BlockSpec((1, tc), lambda j, i: (0, wo + j)),
                  pl.BlockSpec(memory_space=pl.ANY)],
        out_specs=[pl.BlockSpec((tt, tc), lambda j, i: (i, xoff + j)),
                   pl.BlockSpec((width, tc), lambda j, i: (0, j)),
                   pl.BlockSpec((1, tc), lambda j, i: (0, j))],
        out_shape=[jax.ShapeDtypeStruct(dzx.shape, dzx.dtype), jax.ShapeDtypeStruct((width, c), F32),
                   jax.ShapeDtypeStruct((1, c), F32)],
        input_output_aliases={7: 0},
        compiler_params=_params("parallel", "arbitrary"),
    )(zx, zx, zx, dxc, dxc, cw, cb, dzx)


def _ssd_put_ddt(ddt_g, dzx, col, *, name):
    g, t, _ = ddt_g.shape
    tt = _pick(t, (512, 256, 128))

    def body(d_ref, dzx_in, dzx_ref):
        del dzx_in
        dzx_ref[...] = jnp.sum(d_ref[...], axis=0).astype(BF16)

    return pl.pallas_call(
        body, name=name, grid=(t // tt,),
        in_specs=[pl.BlockSpec((g, tt, LANE), lambda i: (0, i, 0)), pl.BlockSpec(memory_space=pl.ANY)],
        out_specs=pl.BlockSpec((tt, LANE), lambda i: (i, col)),
        out_shape=jax.ShapeDtypeStruct(dzx.shape, dzx.dtype),
        input_output_aliases={1: 0},
        compiler_params=_params("parallel"),
    )(ddt_g, dzx)


def _dot(a, b, mode):
    return lax.dot_general(a, b, _DIMS[mode], preferred_element_type=F32)


def _dot_exact(m01, v, mode="nn"):
    hi = v.astype(BF16)
    r1 = v - hi.astype(F32)
    mid = r1.astype(BF16)
    lo = (r1 - mid.astype(F32)).astype(BF16)
    return _dot(m01, hi, mode) + _dot(m01, mid, mode) + _dot(m01, lo, mode)


def _softplus(x):
    return jnp.maximum(x, 0.0) + jnp.log(1.0 + jnp.exp(-jnp.abs(x)))


def _head_vectors(g, dt_raw, bias, alog):
    n = CHUNK
    dt = _softplus(dt_raw + bias)
    a = -jnp.exp(alog)
    tri = (lax.broadcasted_iota(jnp.int32, (n, n), 0) >= lax.broadcasted_iota(jnp.int32, (n, n), 1)).astype(BF16)
    cs = _dot_exact(tri, dt * a)
    return dt, a, cs, cs.T


def _col(v, lane_ids, h):
    return jnp.sum(jnp.where(lane_ids == h, v, 0.0), axis=1, keepdims=True)


def _row(vt, sub_ids, h):
    return jnp.sum(jnp.where(sub_ids == h, vt, 0.0), axis=0, keepdims=True)


def _ssd_specs(di, bc, nc, rev):
    cidx = (lambda c: nc - 1 - c) if rev else (lambda c: c)
    wide = lambda off: pl.BlockSpec((CHUNK, GROUP_W), lambda g, c: (cidx(c), off + g))
    lane = lambda off: pl.BlockSpec((CHUNK, LANE), lambda g, c: (cidx(c), off + g))
    fixed = lambda off: pl.BlockSpec((CHUNK, LANE), lambda g, c: (cidx(c), off))
    vec = pl.BlockSpec((1, LANE), lambda g, c: (0, 0))
    gvec = pl.BlockSpec((1, GROUP_W), lambda g, c: (0, g))
    state = pl.BlockSpec((None, None, 4, PAIR, STATE), lambda g, c: (g, cidx(c), 0, 0, 0))
    return wide, lane, fixed, vec, gvec, state


def _ssd_fwd(xbc, zx, bias, alog, dexp, nw, *, name):
    t = xbc.shape[0]
    di = nw.shape[1]
    bc = (xbc.shape[1] - di) // 2
    ng, nc = di // GROUP_W, t // CHUNK
    wide, lane, fixed, vec, gvec, state = _ssd_specs(di, bc, nc, rev=False)

    def body(xs_ref, b_ref, c_ref, dt_ref, z_ref, bias_ref, alog_ref, dexp_ref, nw_ref,
             yn_ref, y_ref, st_ref, s_scr):
        g, c = pl.program_id(0), pl.program_id(1)

        @pl.when(c == 0)
        def _():
            s_scr[...] = jnp.zeros_like(s_scr)

        n = CHUNK
        dt, a, cs, cst = _head_vectors(g, dt_ref[...].astype(F32), bias_ref[...], alog_ref[...])
        lane_ids = lax.broadcasted_iota(jnp.int32, (n, LANE), 1)
        sub_ids = lax.broadcasted_iota(jnp.int32, (LANE, n), 0)
        causal = lax.broadcasted_iota(jnp.int32, (n, n), 0) >= lax.broadcasted_iota(jnp.int32, (n, n), 1)
        half = lax.broadcasted_iota(jnp.int32, (1, PAIR), 1) < HEAD_DIM
        half_rows = lax.broadcasted_iota(jnp.int32, (PAIR, 1), 0) < HEAD_DIM
        bm, cm = b_ref[...], c_ref[...]
        gm = _dot(cm, bm, "nt")
        x = xs_ref[...].astype(F32)
        ys = []
        for q in range(4):
            h0 = g * 8 + 2 * q
            col = [_col(cs, lane_ids, h0 + e) for e in range(2)]
            row = [_row(cst, sub_ids, h0 + e) for e in range(2)]
            dtc = [_col(dt, lane_ids, h0 + e) for e in range(2)]
            last = [col[e][n - 1:n, :] for e in range(2)]
            xd = x[:, q * PAIR:(q + 1) * PAIR] * jnp.where(half, dtc[0], dtc[1])
            xd_bf = xd.astype(BF16)
            yd = []
            for e in range(2):
                lm = jnp.exp(jnp.where(causal, col[e] - row[e], -1e30))
                yd.append(_dot((gm * lm).astype(BF16), xd_bf, "nn"))
            s = s_scr[q]
            st_ref[q] = s
            ecs = jnp.where(half, jnp.exp(col[0]), jnp.exp(col[1]))
            dte = jnp.where(half, jnp.exp(last[0] - col[0]), jnp.exp(last[1] - col[1]))
            yoff = ecs * _dot(cm, s.astype(BF16), "nt")
            snew = _dot((xd * dte).astype(BF16), bm, "tn")
            s_scr[q] = s * jnp.where(half_rows, jnp.exp(last[0]), jnp.exp(last[1])) + snew
            ys.append(jnp.where(half, yd[0], yd[1]) + yoff)
        y = jnp.concatenate(ys, axis=1) + dexp_ref[...] * x
        y_ref[...] = y.astype(BF16)
        z = z_ref[...].astype(F32)
        yg = y * (z * _sigmoid(z))
        r = lax.rsqrt(jnp.mean(yg * yg, axis=1, keepdims=True) + EPS)
        yn_ref[...] = (yg * r * nw_ref[...]).astype(BF16)

    dtcol = (2 * di + 2 * bc) // LANE
    return pl.pallas_call(
        body, name=name, grid=(ng, nc),
        in_specs=[wide(0), lane(di // LANE), lane((di + bc) // LANE), fixed(dtcol), wide(0),
                  vec, vec, gvec, gvec],
        out_specs=[wide(0), wide(0), state],
        out_shape=[jax.ShapeDtypeStruct((t, di), BF16), jax.ShapeDtypeStruct((t, di), BF16),
                   jax.ShapeDtypeStruct((ng, nc, 4, PAIR, STATE), F32)],
        scratch_shapes=[pltpu.VMEM((4, PAIR, STATE), F32)],
        compiler_params=_params("parallel", "arbitrary"),
    )(xbc, xbc, xbc, zx, zx, bias, alog, dexp, nw)


def _ssd_bwd(dyn, y, xbc, zx, states, bias, alog, dexp, nw, *, name):
    t = xbc.shape[0]
    di = nw.shape[1]
    bc = (xbc.shape[1] - di) // 2
    ng, nc = di // GROUP_W, t // CHUNK
    wide, lane, fixed, vec, gvec, state = _ssd_specs(di, bc, nc, rev=True)
    acc = lambda w: pl.BlockSpec((None, 8, w), lambda g, c: (g, 0, 0))

    def body(dyn_ref, y_ref, z_ref, nw_ref, xs_ref, b_ref, c_ref, dt_ref, bias_ref, alog_ref, dexp_ref, st_ref,
             dz_ref, dxs_ref, db_ref, dc_ref, ddt_ref, small_ref, dnw_ref, ddexp_ref, ds_scr):
        g, c = pl.program_id(0), pl.program_id(1)

        @pl.when(c == 0)
        def _():
            ds_scr[...] = jnp.zeros_like(ds_scr)
            small_ref[...] = jnp.zeros_like(small_ref)
            dnw_ref[...] = jnp.zeros_like(dnw_ref)
            ddexp_ref[...] = jnp.zeros_like(ddexp_ref)

        n = CHUNK
        yv = y_ref[...].astype(F32)
        z = z_ref[...].astype(F32)
        sz = _sigmoid(z)
        silu = z * sz
        yg = yv * silu
        r = lax.rsqrt(jnp.mean(yg * yg, axis=1, keepdims=True) + EPS)
        yhat = yg * r
        dynv = dyn_ref[...].astype(F32)
        dnw_ref[0:1, :] += jnp.sum(dynv * yhat, axis=0, keepdims=True)
        dyhat = dynv * nw_ref[...]
        dyg = r * (dyhat - yhat * jnp.mean(dyhat * yhat, axis=1, keepdims=True))
        dz_ref[...] = (dyg * yv * (sz * (1.0 + z * (1.0 - sz)))).astype(BF16)
        dy = dyg * silu

        dt_in = dt_ref[...].astype(F32) + bias_ref[...]
        dt, a, cs, cst = _head_vectors(g, dt_ref[...].astype(F32), bias_ref[...], alog_ref[...])
        lane_ids = lax.broadcasted_iota(jnp.int32, (n, LANE), 1)
        sub_ids = lax.broadcasted_iota(jnp.int32, (LANE, n), 0)
        ri = lax.broadcasted_iota(jnp.int32, (n, n), 0)
        ci = lax.broadcasted_iota(jnp.int32, (n, n), 1)
        causal, causal_t = ri >= ci, ci >= ri
        is_last = lax.broadcasted_iota(jnp.int32, (n, 1), 0) == n - 1
        half = lax.broadcasted_iota(jnp.int32, (1, PAIR), 1) < HEAD_DIM
        half_rows = lax.broadcasted_iota(jnp.int32, (PAIR, 1), 0) < HEAD_DIM
        bm, cm = b_ref[...], c_ref[...]
        bf = bm.astype(F32)
        gm, gmt = _dot(cm, bm, "nt"), _dot(bm, cm, "nt")
        x = xs_ref[...].astype(F32)
        dexp = dexp_ref[...]

        dg_sum = jnp.zeros((n, n), F32)
        dgt_sum = jnp.zeros((n, n), F32)
        db_off = jnp.zeros((n, STATE), F32)
        dc_off = jnp.zeros((n, STATE), F32)
        dcs_blk = jnp.zeros((n, LANE), F32)
        ddt_blk = jnp.zeros((n, LANE), F32)
        dxs = []
        for q in range(4):
            h0 = g * 8 + 2 * q
            sl = slice(q * PAIR, (q + 1) * PAIR)
            col = [_col(cs, lane_ids, h0 + e) for e in range(2)]
            row = [_row(cst, sub_ids, h0 + e) for e in range(2)]
            dtc = [_col(dt, lane_ids, h0 + e) for e in range(2)]
            last = [col[e][n - 1:n, :] for e in range(2)]
            xp, dyp = x[:, sl], dy[:, sl]
            dtp = jnp.where(half, dtc[0], dtc[1])
            xd = xp * dtp
            xd_bf, dyp_bf = xd.astype(BF16), dyp.astype(BF16)
            ecs = jnp.where(half, jnp.exp(col[0]), jnp.exp(col[1]))
            dte = jnp.where(half, jnp.exp(last[0] - col[0]), jnp.exp(last[1] - col[1]))
            s, ds = st_ref[q], ds_scr[q]
            s_bf, ds_bf = s.astype(BF16), ds.astype(BF16)
            yoff = ecs * _dot(cm, s_bf, "nt")
            edy_bf = (ecs * dyp).astype(BF16)
            dc_off += _dot(edy_bf, s_bf, "nn")
            bds = _dot(bm, ds_bf, "nt")
            sds = s * ds
            zs = []
            for e in range(2):
                msk = half if e == 0 else jnp.logical_not(half)
                msk_rows = half_rows if e == 0 else jnp.logical_not(half_rows)
                lm = jnp.exp(jnp.where(causal, col[e] - row[e], -1e30))
                lmt = jnp.exp(jnp.where(causal_t, row[e] - col[e], -1e30))
                dym_bf = jnp.where(msk, dyp, 0.0).astype(BF16)
                xdm_bf = jnp.where(msk, xd, 0.0).astype(BF16)
                dm = _dot(dym_bf, xd_bf, "nt")
                dmt = _dot(xdm_bf, dyp_bf, "nt")
                m, mt = gm * lm, gmt * lmt
                dcs = jnp.sum(dm * m, axis=1, keepdims=True) - jnp.sum(dmt * mt, axis=1, keepdims=True)
                dg_sum += dm * lm
                dgt_sum += dmt * lmt
                zs.append(_dot(mt.astype(BF16), dyp_bf, "nn"))
                we = _dot(xdm_bf, ds_bf, "nn")
                dte_col = jnp.exp(last[e] - col[e])
                te = dte_col * jnp.sum(we * bf, axis=1, keepdims=True)
                db_off += dte_col * we
                dcs += jnp.sum(jnp.where(msk, dyp * yoff, 0.0), axis=1, keepdims=True) - te
                tail = jnp.exp(last[e]) * jnp.sum(jnp.where(msk_rows, sds, 0.0), keepdims=True) \
                    + jnp.sum(te, keepdims=True)
                dcs += jnp.where(is_last, tail, 0.0)
                dcs_blk += jnp.where(lane_ids == h0 + e, dcs, 0.0)
            dxd = jnp.where(half, zs[0], zs[1]) + dte * bds
            dxs.append(dxd * dtp + dexp[:, sl] * dyp)
            ddexp_ref[0:1, sl] += jnp.sum(dyp * xp, axis=0, keepdims=True)
            rs = dxd * xp
            for e in range(2):
                msk = half if e == 0 else jnp.logical_not(half)
                ddt_blk += jnp.where(lane_ids == h0 + e, jnp.sum(jnp.where(msk, rs, 0.0), axis=1, keepdims=True), 0.0)
            ds_scr[q] = ds * jnp.where(half_rows, jnp.exp(last[0]), jnp.exp(last[1])) + _dot(edy_bf, cm, "tn")

        dxs_ref[...] = jnp.concatenate(dxs, axis=1).astype(BF16)
        dc_ref[...] = (_dot(dg_sum.astype(BF16), bm, "nn") + dc_off).astype(BF16)
        db_ref[...] = (_dot(dgt_sum.astype(BF16), cm, "nn") + db_off).astype(BF16)
        upper = (ri <= ci).astype(BF16)
        dda = _dot_exact(upper, dcs_blk)
        ddt = dda * a + ddt_blk
        small_ref[0:1, :] += jnp.sum(dda * dt, axis=0, keepdims=True) * a
        ddt_raw = ddt * _sigmoid(dt_in)
        small_ref[1:2, :] += jnp.sum(ddt_raw, axis=0, keepdims=True)
        ddt_ref[...] = ddt_raw

    dtcol = (2 * di + 2 * bc) // LANE
    tot = 2 * di + 2 * bc + LANE
    return pl.pallas_call(
        body, name=name, grid=(ng, nc),
        in_specs=[wide(0), wide(0), wide(0), gvec, wide(0), lane(di // LANE), lane((di + bc) // LANE),
                  fixed(dtcol), vec, vec, gvec, state],
        out_specs=[wide(0), wide(0), lane(0), lane(0),
                   pl.BlockSpec((None, CHUNK, LANE), lambda g, c: (g, nc - 1 - c, 0)),
                   acc(LANE), acc(GROUP_W), acc(GROUP_W)],
        out_shape=[jax.ShapeDtypeStruct((t, tot), BF16), jax.ShapeDtypeStruct((t, di), BF16),
                   jax.ShapeDtypeStruct((t, bc), BF16), jax.ShapeDtypeStruct((t, bc), BF16),
                   jax.ShapeDtypeStruct((ng, t, LANE), F32), jax.ShapeDtypeStruct((ng, 8, LANE), F32),
                   jax.ShapeDtypeStruct((ng, 8, GROUP_W), F32), jax.ShapeDtypeStruct((ng, 8, GROUP_W), F32)],
        scratch_shapes=[pltpu.VMEM((4, PAIR, STATE), F32)],
        compiler_params=_params("parallel", "arbitrary"),
    )(dyn, y, zx, nw, xbc, xbc, xbc, zx, bias, alog, dexp, states)


HBM_ANY = pl.BlockSpec(memory_space=pl.ANY)


def _place():
    x, y, c = lax.axis_index("x"), lax.axis_index("y"), lax.axis_index("c")
    chips = [(1 - x, y), (x, 1 - y), (1 - x, 1 - y)]
    return x, y, c, chips


def _all_gather(arrs, *, name, inplace=False):
    n = len(arrs)

    def body(*refs):
        ins, outs = refs[:n], refs[n:2 * n]
        send, recv, loc = refs[2 * n:]
        x, y, c, chips = _place()
        me, sib = (x, y, c), (x, y, 1 - c)

        def blk(a, p):
            return outs[a].at[4 * p[0] + 2 * p[1] + p[2]]

        def cp(a, k, block, to, src=None):
            return pltpu.make_async_remote_copy(
                src_ref=blk(a, block) if src is None else src, dst_ref=blk(a, block),
                send_sem=send.at[a * 7 + k], recv_sem=recv.at[a * 7 + k], device_id=to, device_id_type=MESH)

        src = [None if inplace else ins[a] for a in range(n)]
        mine = [] if inplace else [pltpu.make_async_copy(ins[a], blk(a, me), loc.at[a]) for a in range(n)]
        for m in mine:
            m.start()
        started = []
        for a in range(n):
            started.append(cp(a, 0, me, sib, src=src[a]))
            started += [cp(a, 1 + j, me, (*chip, c), src=src[a]) for j, chip in enumerate(chips)]
        for s in started:
            s.start()
        for j, chip in enumerate(chips):
            for a in range(n):
                cp(a, 1 + j, (*chip, c), me).wait_recv()
                fwd = cp(a, 4 + j, (*chip, c), sib)
                fwd.start()
                started.append(fwd)
        for a in range(n):
            cp(a, 0, sib, me).wait_recv()
            for j, chip in enumerate(chips):
                cp(a, 4 + j, (*chip, 1 - c), me).wait_recv()
        for s in started:
            s.wait_send()
        for m in mine:
            m.wait()

    return pl.pallas_call(
        body, name=name,
        in_specs=[HBM_ANY] * n, out_specs=[HBM_ANY] * n,
        out_shape=[jax.ShapeDtypeStruct(a.shape if inplace else (N_DEV,) + a.shape, a.dtype) for a in arrs],
        input_output_aliases={a: a for a in range(n)} if inplace else {},
        scratch_shapes=[pltpu.SemaphoreType.DMA((7 * n,)), pltpu.SemaphoreType.DMA((7 * n,)),
                        pltpu.SemaphoreType.DMA((n,))],
    )(*arrs)


HBM_SPEC = pl.BlockSpec(memory_space=pltpu.HBM)
SEM_SPEC = pl.BlockSpec(memory_space=pltpu.SEMAPHORE)
SPLIT_EFFECT = pltpu.SideEffectType.DATAFLOW_SIDE_EFFECTING


def _split_start(arrs, plan, n_copies, after, *, name):
    m = len(arrs)

    def body(*refs):
        send, recv, token = refs[m + 1], refs[m + 2], refs[-1]
        for i, (src, dst, to) in enumerate(plan(refs[:m])):
            pltpu.make_async_remote_copy(src_ref=src, dst_ref=dst, send_sem=send.at[i], recv_sem=recv.at[i],
                                         device_id=to, device_id_type=MESH).start()
        token[...] = jnp.zeros_like(token)

    outs = pl.pallas_call(
        body, name=name,
        out_shape=(pltpu.SemaphoreType.DMA((n_copies,)), pltpu.SemaphoreType.DMA((n_copies,)),
                   *[pltpu.HBM(a.shape, a.dtype) for a in arrs], jax.ShapeDtypeStruct((8, LANE), F32)),
        in_specs=[HBM_SPEC] * m + [HBM_ANY],
        out_specs=(SEM_SPEC, SEM_SPEC, *[HBM_SPEC] * m, pl.BlockSpec(memory_space=pltpu.VMEM)),
        input_output_aliases={i: 2 + i for i in range(m)},
        compiler_params=pltpu.CompilerParams(has_side_effects=SPLIT_EFFECT),
    )(*[pltpu.with_memory_space_constraint(a, pltpu.HBM) for a in arrs], after)
    return outs[0], outs[1], list(outs[2:2 + m]), outs[-1]


def _split_wait(arrs, send, recv, after, plan, *, name):
    m = len(arrs)

    def body(*refs):
        send_ref, recv_ref = refs[m], refs[m + 1]
        for i, (src, dst, to) in enumerate(plan(refs[:m])):
            cp = pltpu.make_async_remote_copy(src_ref=src, dst_ref=dst, send_sem=send_ref.at[i],
                                              recv_sem=recv_ref.at[i], device_id=to, device_id_type=MESH)
            cp.wait_send()
            cp.wait_recv()

    outs = pl.pallas_call(
        body, name=name,
        out_shape=[pltpu.HBM(a.shape, a.dtype) for a in arrs],
        in_specs=[HBM_SPEC] * m + [SEM_SPEC, SEM_SPEC, HBM_ANY], out_specs=[HBM_SPEC] * m,
        input_output_aliases={i: i for i in range(m)},
        compiler_params=pltpu.CompilerParams(has_side_effects=SPLIT_EFFECT),
    )(*arrs, send, recv, after)
    return list(outs)


def _dev(p):
    return 4 * p[0] + 2 * p[1] + p[2]


def _plan_gather_ici(bufs):
    x, y, c, chips = _place()
    me = _dev((x, y, c))
    peers = [(x, y, 1 - c)] + [(*chip, c) for chip in chips]
    return [(b.at[me], b.at[me], p) for b in bufs for p in peers]


def _plan_gather_d2d(bufs):
    x, y, c, chips = _place()
    return [(b.at[_dev((*chip, c))], b.at[_dev((*chip, c))], (x, y, 1 - c)) for b in bufs for chip in chips]


def _plan_pair(refs):
    n = len(refs) // 2
    x, y, c, _ = _place()
    return [(refs[a].at[2 * k + 1 - c], refs[n + a].at[k], (x, y, 1 - c)) for a in range(n) for k in range(N_CHIP)]


def _plan_chip(refs):
    n = len(refs) // 2
    x, y, c, chips = _place()
    return [(refs[a].at[2 * chip[0] + chip[1]], refs[n + a].at[j], (*chip, c))
            for a in range(n) for j, chip in enumerate(chips)]


def _land(shape, dtype):
    return lax.empty(shape, dtype)


def _with_tokens(v, *tokens):
    for t in tokens:
        if t is not None:
            v = v + t[0, 0].astype(v.dtype)
    return v


def _pair_exchange(grads, *, name):
    n = len(grads)

    def body(*refs):
        ins, gots = refs[:n], refs[n:2 * n]
        send, recv = refs[2 * n:]
        x, y, c, _ = _place()
        copies = []
        for a in range(n):
            for k in range(N_CHIP):
                copies.append(pltpu.make_async_remote_copy(
                    src_ref=ins[a].at[2 * k + 1 - c], dst_ref=gots[a].at[k],
                    send_sem=send.at[a * N_CHIP + k], recv_sem=recv.at[a * N_CHIP + k],
                    device_id=(x, y, 1 - c), device_id_type=MESH))
        for cpy in copies:
            cpy.start()
        for cpy in copies:
            cpy.wait()

    return pl.pallas_call(
        body, name=name,
        in_specs=[HBM_ANY] * n, out_specs=[HBM_ANY] * n,
        out_shape=[jax.ShapeDtypeStruct((N_CHIP,) + g.shape[1:], g.dtype) for g in grads],
        scratch_shapes=[pltpu.SemaphoreType.DMA((N_CHIP * n,)), pltpu.SemaphoreType.DMA((N_CHIP * n,))],
    )(*grads)


def _chip_exchange(sums, *, name):
    n = len(sums)

    def body(*refs):
        ins, outs = refs[:n], refs[n:2 * n]
        send, recv = refs[2 * n:]
        x, y, c, chips = _place()
        copies = []
        for a in range(n):
            for j, chip in enumerate(chips):
                copies.append(pltpu.make_async_remote_copy(
                    src_ref=ins[a].at[2 * chip[0] + chip[1]], dst_ref=outs[a].at[j],
                    send_sem=send.at[a * 3 + j], recv_sem=recv.at[a * 3 + j],
                    device_id=(*chip, c), device_id_type=MESH))
        for cpy in copies:
            cpy.start()
        for cpy in copies:
            cpy.wait()

    return pl.pallas_call(
        body, name=name,
        in_specs=[HBM_ANY] * n, out_specs=[HBM_ANY] * n,
        out_shape=[jax.ShapeDtypeStruct((N_CHIP - 1,) + s.shape[1:], s.dtype) for s in sums],
        scratch_shapes=[pltpu.SemaphoreType.DMA((3 * n,)), pltpu.SemaphoreType.DMA((3 * n,))],
    )(*sums)


def _add_pair(grad, got, core, *, name):
    k, r, c = got.shape
    tr = _pick(r, (1024, 704, 512, 256, 128, 64, 32, 16))

    def body(core_ref, a_ref, b_ref, o_ref):
        del core_ref
        o_ref[...] = (a_ref[...].astype(F32) + b_ref[...].astype(F32)).astype(BF16)

    spec = pl.BlockSpec((None, tr, c), lambda q, i, core_ref: (q, i, 0))
    return pl.pallas_call(
        body, name=name,
        grid_spec=pltpu.PrefetchScalarGridSpec(
            num_scalar_prefetch=1, grid=(k, r // tr),
            in_specs=[pl.BlockSpec((None, tr, c), lambda q, i, core_ref: (2 * q + core_ref[0], i, 0)), spec],
            out_specs=spec),
        out_shape=jax.ShapeDtypeStruct(got.shape, BF16), compiler_params=_params("parallel", "parallel"),
    )(core, grad, got)


def _all_reduce_small(v, *, name):
    r = v.shape[0]

    def body(v_ref, o_ref, buf, send, recv):
        x, y, c, _ = _place()
        me = 4 * x + 2 * y + c
        buf[me] = v_ref[...]
        copies = []
        for rel in range(1, N_DEV):
            fx, fy, fc = rel >> 2 & 1, rel >> 1 & 1, rel & 1
            peer = ((1 - x) if fx else x, (1 - y) if fy else y, (1 - c) if fc else c)
            copies.append(pltpu.make_async_remote_copy(
                src_ref=v_ref, dst_ref=buf.at[me], send_sem=send.at[rel - 1], recv_sem=recv.at[rel - 1],
                device_id=peer, device_id_type=MESH))
        for cpy in copies:
            cpy.start()
        for cpy in copies:
            cpy.wait()
        acc = buf[0]
        for d in range(1, N_DEV):
            acc = acc + buf[d]
        o_ref[...] = acc

    return pl.pallas_call(
        body, name=name,
        in_specs=[pl.BlockSpec(memory_space=pltpu.VMEM)], out_specs=pl.BlockSpec(memory_space=pltpu.VMEM),
        out_shape=jax.ShapeDtypeStruct(v.shape, F32),
        scratch_shapes=[pltpu.VMEM((N_DEV, r, LANE), F32), pltpu.SemaphoreType.DMA((N_DEV - 1,)),
                        pltpu.SemaphoreType.DMA((N_DEV - 1,))],
        compiler_params=pltpu.CompilerParams(vmem_limit_bytes=VMEM_LIMIT),
    )(v)


def _adamw_math(w, g, m, v):
    m = ADAM_B1 * m + (1.0 - ADAM_B1) * g
    v = ADAM_B2 * v + (1.0 - ADAM_B2) * (g * g)
    m_hat = m / (1.0 - ADAM_B1 ** ADAM_STEP)
    v_hat = v / (1.0 - ADAM_B2 ** ADAM_STEP)
    delta = -ADAM_LR * (m_hat / (jnp.sqrt(v_hat) + ADAM_EPS) + ADAM_WD * w)
    return delta, m, v


def _adamw_layer(w, m, v, sums, recv, chip, layer, prev, after, *, name):
    nl, r, c = w.shape
    tr = _pick(r, (256, 128, 64, 32, 16))

    def body(chip_ref, w_ref, m_ref, v_ref, s_ref, p_ref, *rest):
        del chip_ref
        g_ref, d_ref, mo_ref, vo_ref = rest[-4:]
        g = s_ref[...].astype(F32)
        for k in range(N_CHIP - 1):
            g = g + p_ref[k].astype(F32)
        delta, mn, vn = _adamw_math(w_ref[...], g, m_ref[...], v_ref[...])
        g_ref[...] = g
        d_ref[...] = delta
        mo_ref[...] = mn
        vo_ref[...] = vn

    lay = pl.BlockSpec((None, tr, c), lambda i, chip_ref: (layer, i, 0))
    ins = [w, m, v, sums, recv, after] + (list(prev) if prev is not None else [])
    in_specs = [lay, lay, lay, pl.BlockSpec((None, tr, c), lambda i, chip_ref: (chip_ref[0], i, 0)),
                pl.BlockSpec((N_CHIP - 1, tr, c), lambda i, chip_ref: (0, i, 0)), HBM_ANY]
    in_specs += [HBM_ANY] * (4 if prev is not None else 0)
    return pl.pallas_call(
        body, name=name,
        grid_spec=pltpu.PrefetchScalarGridSpec(
            num_scalar_prefetch=1, grid=(r // tr,), in_specs=in_specs, out_specs=[lay] * 4),
        out_shape=[jax.ShapeDtypeStruct(w.shape, F32)] * 4,
        input_output_aliases={7 + q: q for q in range(4)} if prev is not None else {},
        compiler_params=_params("parallel"),
    )(chip, *ins)


def _adamw_small(w, g, m, v, *, name):
    def body(w_ref, g_ref, m_ref, v_ref, d_ref, mo_ref, vo_ref):
        d_ref[...], mo_ref[...], vo_ref[...] = _adamw_math(w_ref[...], g_ref[...], m_ref[...], v_ref[...])

    vm = pl.BlockSpec(memory_space=pltpu.VMEM)
    return pl.pallas_call(
        body, name=name, in_specs=[vm] * 4, out_specs=[vm] * 3,
        out_shape=[jax.ShapeDtypeStruct(w.shape, F32)] * 3,
        compiler_params=pltpu.CompilerParams(vmem_limit_bytes=VMEM_LIMIT),
    )(w, g, m, v)


def _pack(arrs):
    flat = jnp.concatenate([a.reshape(-1).astype(F32) for a in arrs])
    pad = (-flat.shape[0]) % (8 * LANE)
    return jnp.pad(flat, (0, pad)).reshape(-1, LANE)


def _unpack(packed, shapes):
    flat = packed.reshape(-1)
    out, off = [], 0
    for s in shapes:
        size = math.prod(s)
        out.append(flat[off:off + size].reshape(s))
        off += size
    return out


WEIGHTS = ['mix_norm_w', 'ffn_norm_w', 'final_norm_w', 'ssd_w_in', 'ssd_conv_w', 'ssd_conv_b', 'ssd_dt_bias',
           'ssd_a_log', 'ssd_d', 'ssd_norm_w', 'ssd_w_out', 'sc_w_in', 'sc_conv_w', 'sc_w_out', 'ffn_w_up',
           'ffn_conv_w', 'ffn_conv_b', 'ffn_w_down']
BIG = ('ssd_w_in', 'ssd_w_out', 'sc_w_in', 'sc_w_out', 'ffn_w_up', 'ffn_w_down')
SHARDED_SMALL = ('ssd_conv_w', 'sc_conv_w', 'ffn_conv_w')


def _lane_pad(v):
    return jnp.pad(v.astype(F32), (0, LANE - v.shape[0])).reshape(1, LANE)


def _gather_cols(g):
    return jnp.moveaxis(g, 0, -2).reshape(g.shape[1:-1] + (N_DEV * g.shape[-1],))


class _Gather:
    def __init__(self, bufs, tag):
        self.bufs, self.tag = bufs, tag

    def start_ici(self, after):
        self.sems = _split_start(self.bufs, _plan_gather_ici, 4 * len(self.bufs), after, name=f"ag_ici_start_{self.tag}")
        return self.sems[3]

    def hand_on(self, after):
        send, recv, bufs, _ = self.sems
        bufs = _split_wait(bufs, send, recv, after, _plan_gather_ici, name=f"ag_ici_wait_{self.tag}")
        self.sems = _split_start(bufs, _plan_gather_d2d, 3 * len(bufs), after, name=f"ag_d2d_start_{self.tag}")
        return self.sems[3]

    def finish(self, after):
        send, recv, bufs, _ = self.sems
        return _split_wait(bufs, send, recv, after, _plan_gather_d2d, name=f"ag_d2d_wait_{self.tag}")


class _Scatter:
    def __init__(self, grads, core, tag):
        self.grads, self.core, self.tag = grads, core, tag

    def start_pair(self, after):
        lands = [_land((N_CHIP,) + g.shape[1:], g.dtype) for g in self.grads]
        self.sems = _split_start(self.grads + lands, _plan_pair, N_CHIP * len(lands), after,
                                 name=f"rs_pair_start_{self.tag}")
        return self.sems[3]

    def start_chip(self, after):
        n = len(self.grads)
        send, recv, arrs, _ = self.sems
        arrs = _split_wait(arrs, send, recv, after, _plan_pair, name=f"rs_pair_wait_{self.tag}")
        self.sums = [_add_pair(g, o, self.core, name=f"rs_add_{self.tag}{a}")
                     for a, (g, o) in enumerate(zip(arrs[:n], arrs[n:]))]
        lands = [_land((N_CHIP - 1,) + s.shape[1:], s.dtype) for s in self.sums]
        self.sems = _split_start(self.sums + lands, _plan_chip, (N_CHIP - 1) * n, after,
                                 name=f"rs_chip_start_{self.tag}")
        return self.sems[3]

    def finish(self, after):
        n = len(self.grads)
        send, recv, arrs, _ = self.sems
        arrs = _split_wait(arrs, send, recv, after, _plan_chip, name=f"rs_chip_wait_{self.tag}")
        return list(zip(arrs[:n], arrs[n:]))


def kernel(x, mix_norm_w, ffn_norm_w, final_norm_w, ssd_w_in, ssd_conv_w, ssd_conv_b, ssd_dt_bias, ssd_a_log, ssd_d, ssd_norm_w, ssd_w_out, sc_w_in, sc_conv_w, sc_w_out, ffn_w_up, ffn_conv_w, ffn_conv_b, ffn_w_down, loss_target, m_mix_norm_w, m_ffn_norm_w, m_final_norm_w, m_ssd_w_in, m_ssd_conv_w, m_ssd_conv_b, m_ssd_dt_bias, m_ssd_a_log, m_ssd_d, m_ssd_norm_w, m_ssd_w_out, m_sc_w_in, m_sc_conv_w, m_sc_w_out, m_ffn_w_up, m_ffn_conv_w, m_ffn_conv_b, m_ffn_w_down, v_mix_norm_w, v_ffn_norm_w, v_final_norm_w, v_ssd_w_in, v_ssd_conv_w, v_ssd_conv_b, v_ssd_dt_bias, v_ssd_a_log, v_ssd_d, v_ssd_norm_w, v_ssd_w_out, v_sc_w_in, v_sc_conv_w, v_sc_w_out, v_ffn_w_up, v_ffn_conv_w, v_ffn_conv_b, v_ffn_w_down):
    args = locals()
    wt = {n: args[n] for n in WEIGHTS}
    mom = {n: args["m_" + n] for n in WEIGHTS}
    var = {n: args["v_" + n] for n in WEIGHTS}

    t, d = x.shape[-2], x.shape[-1]
    cur = x.reshape(t, d)
    target = loss_target.reshape(t, d)
    depth = mix_norm_w.shape[0]
    n_ssd, n_sc = ssd_w_in.shape[0], sc_w_in.shape[0]
    heads = ssd_dt_bias.shape[1]
    di = ssd_norm_w.shape[1]
    conv_dim = ssd_conv_b.shape[1]
    bc = (conv_dim - di) // 2
    in_dim = N_DEV * ssd_w_in.shape[2]
    in_pad = di + conv_dim + LANE
    ff = ffn_w_down.shape[1] * N_DEV
    me = 4 * lax.axis_index("x") + 2 * lax.axis_index("y") + lax.axis_index("c")
    me_s = me.astype(jnp.int32).reshape(1)
    core_s = lax.axis_index("c").astype(jnp.int32).reshape(1)
    chip_s = (2 * lax.axis_index("x") + lax.axis_index("y")).astype(jnp.int32).reshape(1)

    names_of = {"ssd": ('ssd_w_in', 'ssd_w_out'), "sc": ('sc_w_in', 'sc_w_out'), "ffn": ('ffn_w_up', 'ffn_w_down')}
    order = []
    for i in range(depth):
        order += [("ssd" if i % 2 == 0 else "sc", i // 2), ("ffn", i)]
    gathers = [_Gather([_cast_layer(wt[n], idx, me_s, name=f"cast_{n}{idx}") for n in names_of[kind]], f"{kind}{idx}")
               for kind, idx in order]
    conv_full = [_gather_cols(g) for g in _all_gather([wt[n] for n in SHARDED_SMALL], name="ag_conv")]
    tok_a = gathers[0].start_ici(conv_full[0])
    tok_b = gathers[0].hand_on(tok_a)
    tok_c = gathers[1].start_ici(tok_b)
    weights = [None] * len(order)
    weights[0] = gathers[0].finish(tok_c)
    ssd_cw, sc_cw, ffn_cw = conv_full
    ffn_cw = ffn_cw.reshape(depth, ffn_cw.shape[1], 2, ff)
    ffn_cb = ffn_conv_b.reshape(depth, 2, ff)
    dexp = jnp.repeat(ssd_d.astype(F32), HEAD_DIM, axis=1)

    n_sub = len(order)
    full = {n: [None] * wt[n].shape[0] for n in BIG}

    def prefetch(s, after):
        return gathers[s + 2].start_ici(after) if s + 2 < n_sub else None

    def hand_on(s, after):
        return gathers[s + 1].hand_on(after) if s + 1 < n_sub else None

    def arrive(s, after):
        if s + 1 < n_sub:
            weights[s + 1] = gathers[s + 1].finish(after)
            use(s + 1)

    def use(s):
        kind, idx = order[s]
        g_in, g_out = weights[s]
        if kind != "ffn":
            g_in = jnp.swapaxes(g_in, 0, 1).reshape(d, -1)
        if kind == "ssd":
            g_in = jnp.pad(g_in, ((0, 0), (0, in_pad - in_dim)))
        n_in, n_out = names_of[kind]
        full[n_in][idx], full[n_out][idx] = g_in, g_out.reshape(-1, d)

    use(0)
    saved = []
    for i in range(depth):
        j = i // 2
        s = 2 * i
        rec = {"x_mix": cur}
        tok = prefetch(s, cur)
        h = _rmsnorm_fwd(cur, _with_tokens(mix_norm_w[i], tok, tok_c if i == 0 else None), name=f"norm_mix{i}")
        rec["h_mix"] = h
        if i % 2 == 0:
            zx = _mm_nn(h, full['ssd_w_in'][j], out_dtype=BF16, name=f"ssd_in{j}")
            cb = _with_tokens(ssd_conv_b[j].reshape(1, conv_dim), hand_on(s, zx))
            xbc = _ssd_conv_fwd(zx, ssd_cw[j], cb, di, name=f"ssd_conv{j}")
            ssd_vecs = (_lane_pad(ssd_dt_bias[j]), _lane_pad(ssd_a_log[j]), dexp[j].reshape(1, di),
                        ssd_norm_w[j].reshape(1, di))
            yn, y, states = _ssd_fwd(xbc, zx, *ssd_vecs, name=f"ssd_core{j}")
            arrive(s, yn)
            cur = _mm_nn(yn, full['ssd_w_out'][j], res=cur, out_dtype=F32, name=f"ssd_out{j}")
            rec.update(zx=zx, xbc=xbc, yn=yn, y=y, states=states, vecs=ssd_vecs)
        else:
            p3 = _mm_nn(h, full['sc_w_in'][j], out_dtype=BF16, out_parts=3, name=f"sc_in{j}")
            act = _sc_act_fwd(p3, _with_tokens(sc_cw[j], hand_on(s, p3)), name=f"sc_act{j}")
            arrive(s, act)
            cur = _mm_nn(act, full['sc_w_out'][j], res=cur, out_dtype=F32, name=f"sc_out{j}")
            rec.update(p3=p3, act=act)
        s += 1
        rec["x_ffn"] = cur
        h = _rmsnorm_fwd(cur, _with_tokens(ffn_norm_w[i], prefetch(s, cur)), name=f"norm_ffn{i}")
        u3 = _lin_in_fwd(h, full['ffn_w_up'][i], 2, name=f"ffn_up{i}")
        act, pre3 = _ffn_act_fwd(u3, ffn_cw[i], _with_tokens(ffn_cb[i], hand_on(s, u3)), name=f"ffn_act{i}")
        arrive(s, act)
        cur = _mm_nn(act, full['ffn_w_down'][i], res=cur, out_dtype=F32, name=f"ffn_down{i}")
        rec.update(h_ffn=h, u3=u3, pre3=pre3, ffn_act=act)
        saved.append(rec)

    dx, dxb, dw_final, loss8 = _loss_head(cur, final_norm_w, target, name="loss_head")

    small = {n: [None] * wt[n].shape[0] for n in WEIGHTS if n not in BIG and n != 'final_norm_w'}
    scatters = [None] * n_sub
    pending = None

    def chip_step(after):
        return pending.start_chip(after) if pending is not None else None

    for i in reversed(range(depth)):
        j = i // 2
        rec = saved[i]
        nb_up = ffn_w_up.shape[2]
        da = _mm_nt(dxb, full['ffn_w_down'][i], out_dtype=BF16, name=f"ffn_down_dx{i}")
        g_down = _mm_tn(rec["ffn_act"], dxb, out_dtype=BF16, name=f"ffn_down_dw{i}")
        du3, dcw, dcb = _ffn_act_bwd(rec["u3"], rec["pre3"], da, _with_tokens(ffn_cw[i], chip_step(da)),
                                     name=f"ffn_act_bwd{i}")
        g_up = _lin_in_dw(rec["h_ffn"], du3, nb_up, name=f"ffn_up_dw{i}")
        dh = _lin_in_dx(du3, full['ffn_w_up'][i], name=f"ffn_up_dx{i}")
        pending = scatters[2 * i + 1] = _Scatter([g_up, g_down.reshape(N_DEV, ff // N_DEV, d)], core_s, f"ffn{i}")
        tok = pending.start_pair(dh)
        dx, dxb, dwn = _rmsnorm_bwd(dh, rec["x_ffn"], _with_tokens(ffn_norm_w[i], tok), dx, name=f"norm_ffn_bwd{i}")
        small['ffn_conv_w'][i] = dcw.reshape(dcw.shape[0], 2 * ff)
        small['ffn_conv_b'][i] = dcb.reshape(2 * ff)
        small['ffn_norm_w'][i] = dwn.sum(axis=0)

        if i % 2 == 0:
            zx, xbc = rec["zx"], rec["xbc"]
            cw, cb = ssd_cw[j], ssd_conv_b[j].reshape(1, conv_dim)
            dyn = _mm_nt(dxb, full['ssd_w_out'][j], out_dtype=BF16, name=f"ssd_out_dx{j}")
            g_out = _mm_tn(rec["yn"], dxb, out_dtype=BF16, name=f"ssd_out_dw{j}")
            bias_t = _with_tokens(rec["vecs"][0], chip_step(dyn))
            dzx, dxs, db, dc, ddt_g, vec_acc, dnw, ddexp = _ssd_bwd(
                dyn, rec["y"], xbc, zx, rec["states"], bias_t, *rec["vecs"][1:], name=f"ssd_core_bwd{j}")
            dzx, dcw_x, dcb_x = _ssd_conv_bwd(zx, dxs, cw, cb, dzx, di, 0, name=f"ssd_conv_bwd_x{j}")
            dzx, dcw_b, dcb_b = _ssd_conv_bwd(zx, db, cw, cb, dzx, di, di, name=f"ssd_conv_bwd_b{j}")
            dzx, dcw_c, dcb_c = _ssd_conv_bwd(zx, dc, cw, cb, dzx, di, di + bc, name=f"ssd_conv_bwd_c{j}")
            dzx = _ssd_put_ddt(ddt_g, dzx, (di + conv_dim) // LANE, name=f"ssd_put_ddt{j}")
            g_in = _mm_tn(rec["h_mix"], dzx, out_dtype=BF16, name=f"ssd_in_dw{j}")
            g_in = jnp.swapaxes(g_in[:, :in_dim].reshape(d, N_DEV, in_dim // N_DEV), 0, 1)
            dh = _mm_nt(dzx, full['ssd_w_in'][j], out_dtype=F32, name=f"ssd_in_dx{j}")
            small['ssd_conv_w'][j] = jnp.concatenate([dcw_x, dcw_b, dcw_c], axis=1)
            small['ssd_conv_b'][j] = jnp.concatenate([dcb_x, dcb_b, dcb_c], axis=1).reshape(conv_dim)
            small['ssd_a_log'][j] = vec_acc[:, 0, :heads].sum(axis=0)
            small['ssd_dt_bias'][j] = vec_acc[:, 1, :heads].sum(axis=0)
            small['ssd_d'][j] = ddexp[:, 0, :].reshape(heads, HEAD_DIM).sum(axis=1)
            small['ssd_norm_w'][j] = dnw[:, 0, :].reshape(di)
            g_out = g_out.reshape(N_DEV, di // N_DEV, d)
        else:
            nb_in = sc_w_in.shape[2]
            da = _mm_nt(dxb, full['sc_w_out'][j], out_dtype=BF16, name=f"sc_out_dx{j}")
            g_out = _mm_tn(rec["act"], dxb, out_dtype=BF16, name=f"sc_out_dw{j}")
            dp3, dcw = _sc_act_bwd(rec["p3"], da, _with_tokens(sc_cw[j], chip_step(da)), name=f"sc_act_bwd{j}")
            g_in = _mm_tn(rec["h_mix"], dp3, out_dtype=BF16, name=f"sc_in_dw{j}")
            g_in = jnp.swapaxes(g_in.reshape(d, N_DEV, nb_in), 0, 1)
            dh = _mm_nt(dp3, full['sc_w_in'][j], out_dtype=F32, name=f"sc_in_dx{j}")
            small['sc_conv_w'][j] = dcw
            g_out = g_out.reshape(N_DEV, g_out.shape[0] // N_DEV, d)
        pending = scatters[2 * i] = _Scatter([g_in, g_out], core_s, f"{order[2 * i][0]}{j}")
        tok = pending.start_pair(dh)
        dx, dxb, dwn = _rmsnorm_bwd(dh, rec["x_mix"], _with_tokens(mix_norm_w[i], tok), dx, name=f"norm_mix_bwd{i}")
        small['mix_norm_w'][i] = dwn.sum(axis=0)
    tok_last = chip_step(dx)

    small_names = [n for n in WEIGHTS if n not in BIG]
    partial = {n: jnp.stack(small[n]) for n in small}
    partial['final_norm_w'] = dw_final.sum(axis=0)
    full_shapes = [partial[n].shape for n in small_names]
    packed = _pack([loss8.sum().reshape(1)] + [partial[n] for n in small_names])
    total = _unpack(_all_reduce_small(packed, name="ar_small"), [(1,)] + full_shapes)
    loss = total[0].reshape(())
    grads = dict(zip(small_names, total[1:]))
    for n in SHARDED_SMALL:
        nb = wt[n].shape[-1]
        grads[n] = lax.dynamic_slice_in_dim(grads[n], me * nb, nb, axis=grads[n].ndim - 1)

    delta, new_m, new_v = {}, {}, {}
    shapes = [wt[n].shape for n in small_names]
    outs = _adamw_small(*[_pack([src[n] for n in small_names]) for src in (wt, grads, mom, var)], name="adamw_small")
    for dst, packed_out in zip((delta, new_m, new_v), outs):
        dst.update(zip(small_names, _unpack(packed_out, shapes)))
    parts = {n: [None] * wt[n].shape[0] for n in BIG}
    for s in range(1, n_sub):
        kind, idx = order[s]
        parts[names_of[kind][0]][idx], parts[names_of[kind][1]][idx] = scatters[s].finish(tok_last)
    first_in, first_out = names_of[order[0][0]]
    last_out = tok_last
    jobs = [(n, layer) for n in reversed(BIG) for layer in reversed(range(wt[n].shape[0]))]
    jobs.sort(key=lambda job: parts[job[0]][job[1]] is None)
    chain = {n: None for n in BIG}
    for n, layer in jobs:
        if parts[n][layer] is None:
            parts[first_in][0], parts[first_out][0] = scatters[0].finish(last_out)
        chain[n] = _adamw_layer(wt[n], mom[n], var[n], *parts[n][layer], chip_s, layer, chain[n], last_out,
                                name=f"adamw_{n}{layer}")
        last_out = chain[n][1]
    for n in BIG:
        grads[n], delta[n], new_m[n], new_v[n] = chain[n]

    return (loss, dx.reshape(x.shape), *[grads[n] for n in WEIGHTS], *[delta[n] for n in WEIGHTS],
            *[new_m[n] for n in WEIGHTS], *[new_v[n] for n in WEIGHTS])
```

```python
import functools
import math

import jax
import jax.numpy as jnp
from jax import lax
from jax.experimental import pallas as pl
from jax.experimental.pallas import tpu as pltpu

F32 = jnp.float32
BF16 = jnp.bfloat16
MESH = pl.DeviceIdType.MESH

N_DEV = 8
N_CHIP = 4
EPS = 1e-5
HEAD_DIM = 64
STATE = 128
CHUNK = 128
PAIR = 2 * HEAD_DIM
GROUP_W = 8 * HEAD_DIM
HALO = 16
LANE = 128
VMEM_LIMIT = 56 * 1024 * 1024

ADAM_LR = 0.001
ADAM_B1 = 0.9
ADAM_B2 = 0.999
ADAM_EPS = 1e-08
ADAM_WD = 0.01
ADAM_STEP = 10


def _pick(n, candidates):
    for c in candidates:
        if c <= n and n % c == 0:
            return c
    return n


def _params(*sem):
    return pltpu.CompilerParams(dimension_semantics=sem, vmem_limit_bytes=VMEM_LIMIT)


def _sigmoid(x):
    return 0.5 * jnp.tanh(0.5 * x) + 0.5


_DIMS = {
    "nn": (((1,), (0,)), ((), ())),
    "nt": (((1,), (1,)), ((), ())),
    "tn": (((0,), (0,)), ((), ())),
}


def _matmul(mode, a, b, *, grid, a_spec, b_spec, o_spec, out_shape, acc_shape, name, res=None, res_spec=None,
            part_fn=None):
    nk = grid[2]
    dims = _DIMS[mode]
    if part_fn is None:
        part_fn = lambda a_ref, b_ref: lax.dot_general(a_ref[...], b_ref[...], dims, preferred_element_type=F32)

    def body(*refs):
        if res is None:
            a_ref, b_ref, o_ref = refs[:3]
            r_ref, scratch = None, refs[3:]
        else:
            a_ref, b_ref, r_ref, o_ref = refs[:4]
            scratch = refs[4:]
        part = part_fn(a_ref, b_ref)

        def finish(acc):
            if r_ref is not None:
                acc = acc + r_ref[...]
            o_ref[...] = acc.astype(o_ref.dtype)

        if nk == 1:
            finish(part)
        else:
            acc_ref = scratch[0]
            k = pl.program_id(2)

            @pl.when(k == 0)
            def _():
                acc_ref[...] = part

            @pl.when(k > 0)
            def _():
                acc_ref[...] += part

            @pl.when(k == nk - 1)
            def _():
                finish(acc_ref[...])

    in_specs = [a_spec, b_spec] + ([res_spec] if res is not None else [])
    args = (a, b) + ((res,) if res is not None else ())
    return pl.pallas_call(
        body, name=name, grid=grid, in_specs=in_specs, out_specs=o_spec, out_shape=out_shape,
        scratch_shapes=[pltpu.VMEM(acc_shape, F32)] if nk > 1 else [],
        compiler_params=_params("parallel", "parallel", "arbitrary"),
    )(*args)


def _mm_nn(a, b, *, out_dtype, res=None, out_parts=1, name):
    m, kd = a.shape
    n = b.shape[1]
    c = n // out_parts
    tm = _pick(m, (512, 256, 128))
    tn = _pick(c, (1152, 1024, 512, 384, 256, 128))
    tk = kd if kd <= 2048 else _pick(kd, (2816, 2048, 1024, 512, 256, 128))
    grid = (n // tn, m // tm, kd // tk)
    if out_parts == 1:
        o_spec = pl.BlockSpec((tm, tn), lambda j, i, k: (i, j))
        out_shape = jax.ShapeDtypeStruct((m, n), out_dtype)
    else:
        o_spec = _stacked_spec(tm, tn, c, lambda j, i, k: (i, j))
        out_shape = jax.ShapeDtypeStruct((out_parts, m, c), out_dtype)
    return _matmul(
        "nn", a, b, res=res, grid=grid, name=name,
        a_spec=pl.BlockSpec((tm, tk), lambda j, i, k: (i, k)),
        b_spec=pl.BlockSpec((tk, tn), lambda j, i, k: (k, j)),
        res_spec=pl.BlockSpec((tm, tn), lambda j, i, k: (i, j)),
        o_spec=o_spec, out_shape=out_shape, acc_shape=(tm, tn))


def _stacked_spec(rows, width, c, row_col):
    per = c // width

    def index(j, i, k):
        r, q = row_col(j, i, k)
        return q // per, r, q % per

    return pl.BlockSpec((None, rows, width), index)


def _mm_nt(a, b, *, out_dtype, name):
    stacked = a.ndim == 3
    m = a.shape[-2]
    n, kd = b.shape
    c = a.shape[-1]
    tm = _pick(m, (512, 256, 128))
    tn = _pick(n, (1408, 1024, 512, 256, 128))
    tk = c if c <= 2048 else _pick(c, (3456, 2816, 2048, 1024, 512, 384, 256, 128))
    grid = (n // tn, m // tm, kd // tk)
    a_spec = (_stacked_spec(tm, tk, c, lambda j, i, k: (i, k)) if stacked
              else pl.BlockSpec((tm, tk), lambda j, i, k: (i, k)))
    return _matmul(
        "nt", a, b, grid=grid, name=name, a_spec=a_spec,
        b_spec=pl.BlockSpec((tn, tk), lambda j, i, k: (j, k)),
        o_spec=pl.BlockSpec((tm, tn), lambda j, i, k: (i, j)),
        out_shape=jax.ShapeDtypeStruct((m, n), out_dtype), acc_shape=(tm, tn))


def _mm_tn(a, b, *, out_dtype, name):
    stacked = b.ndim == 3
    kd, m = a.shape
    c = b.shape[-1]
    n = c * (b.shape[0] if stacked else 1)
    tm = _pick(m, (512, 256, 128))
    tn = _pick(c, (1152, 1024, 512, 384, 256, 128))
    tk = _pick(kd, (2048, 1024, 512, 256, 128))
    grid = (n // tn, m // tm, kd // tk)
    b_spec = (_stacked_spec(tk, tn, c, lambda j, i, k: (k, j)) if stacked
              else pl.BlockSpec((tk, tn), lambda j, i, k: (k, j)))
    return _matmul(
        "tn", a, b, grid=grid, name=name,
        a_spec=pl.BlockSpec((tk, tm), lambda j, i, k: (k, i)), b_spec=b_spec,
        o_spec=pl.BlockSpec((tm, tn), lambda j, i, k: (i, j)),
        out_shape=jax.ShapeDtypeStruct((m, n), out_dtype), acc_shape=(tm, tn))


def _in_tile(nb, c):
    return math.gcd(nb, c)


def _lin_in_fwd(h, wg, parts, *, name):
    t, d = h.shape
    nb = wg.shape[2]
    c = N_DEV * nb // parts
    w = _in_tile(nb, c)
    nbw, cw = nb // w, c // w
    tm = _pick(t, (1024,) if w < 512 else (512, 256, 128))
    grid = (N_DEV * nbw, t // tm, 1)
    return _matmul(
        "nn", h, wg, grid=grid, name=name,
        a_spec=pl.BlockSpec((tm, d), lambda j, i, k: (i, 0)),
        b_spec=pl.BlockSpec((None, d, w), lambda j, i, k: (j // nbw, 0, j % nbw)),
        o_spec=pl.BlockSpec((None, tm, w), lambda j, i, k: (j // cw, i, j % cw)),
        out_shape=jax.ShapeDtypeStruct((parts, t, c), BF16), acc_shape=(tm, w))


def _lin_in_dx(dact, wg, *, name):
    parts, t, c = dact.shape
    d, nb = wg.shape[1], wg.shape[2]
    per = c // (2 * nb)
    tm = _pick(t, (512, 256, 128))
    tn = _pick(d, (1024, 512, 256, 128))
    grid = (d // tn, t // tm, N_DEV // 2)

    def pair(a_ref, b_ref):
        return (lax.dot_general(a_ref[:, :nb], b_ref[0], _DIMS["nt"], preferred_element_type=F32)
                + lax.dot_general(a_ref[:, nb:], b_ref[1], _DIMS["nt"], preferred_element_type=F32))

    return _matmul(
        "nt", dact, wg, grid=grid, name=name, part_fn=pair,
        a_spec=pl.BlockSpec((None, tm, 2 * nb), lambda j, i, k: (k // per, i, k % per)),
        b_spec=pl.BlockSpec((2, tn, nb), lambda j, i, k: (k, j, 0)),
        o_spec=pl.BlockSpec((tm, tn), lambda j, i, k: (i, j)),
        out_shape=jax.ShapeDtypeStruct((t, d), F32), acc_shape=(tm, tn))


def _lin_in_dw(h, dact, nb, *, name):
    t, d = h.shape
    parts, _, c = dact.shape
    w = _in_tile(nb, c)
    nbw, cw = nb // w, c // w
    tm = _pick(d, (1024,) if w < 512 else (512, 256, 128))
    tk = _pick(t, (2048, 1024, 512, 256, 128))
    grid = (N_DEV * nbw, d // tm, t // tk)
    return _matmul(
        "tn", h, dact, grid=grid, name=name,
        a_spec=pl.BlockSpec((tk, tm), lambda j, i, k: (k, i)),
        b_spec=pl.BlockSpec((None, tk, w), lambda j, i, k: (j // cw, k, j % cw)),
        o_spec=pl.BlockSpec((None, tm, w), lambda j, i, k: (j // nbw, i, j % nbw)),
        out_shape=jax.ShapeDtypeStruct((N_DEV, d, nb), BF16), acc_shape=(tm, w))


def _fold8(v):
    rows, c = v.shape
    return v.reshape(rows // 8, 8, c).sum(axis=0)


def _accumulate(ref, val, first):
    @pl.when(first)
    def _():
        ref[...] = val

    @pl.when(jnp.logical_not(first))
    def _():
        ref[...] += val


def _tile2(r, c, rows=(256, 128, 64, 32, 16)):
    tr = _pick(r, rows)
    if tr < r or r <= rows[0]:
        return tr, c
    return r, _pick(c, (256, 128))


def _cast_layer(w_stack, layer, me, after, *, name):
    _, r, c = w_stack.shape
    tr, tc = _tile2(r, c)

    def body(me_ref, w_ref, after_ref, o_ref):
        del me_ref, after_ref
        o_ref[...] = w_ref[...].astype(BF16)

    return pl.pallas_call(
        body, name=name,
        grid_spec=pltpu.PrefetchScalarGridSpec(
            num_scalar_prefetch=1, grid=(r // tr, c // tc),
            in_specs=[pl.BlockSpec((None, tr, tc), lambda i, j, me_ref: (layer, i, j)), HBM_ANY],
            out_specs=pl.BlockSpec((None, tr, tc), lambda i, j, me_ref: (me_ref[0], i, j))),
        out_shape=jax.ShapeDtypeStruct((N_DEV, r, c), BF16),
        compiler_params=_params("parallel", "parallel"),
    )(me, w_stack, after)


def _rmsnorm_fwd(x, w, *, name):
    t, d = x.shape
    tt = _pick(t, (256, 128))

    def body(x_ref, w_ref, o_ref):
        xv = x_ref[...]
        r = lax.rsqrt(jnp.mean(xv * xv, axis=1, keepdims=True) + EPS)
        o_ref[...] = (xv * r * w_ref[...]).astype(BF16)

    return pl.pallas_call(
        body, name=name, grid=(t // tt,),
        in_specs=[pl.BlockSpec((tt, d), lambda i: (i, 0)), pl.BlockSpec((1, d), lambda i: (0, 0))],
        out_specs=pl.BlockSpec((tt, d), lambda i: (i, 0)),
        out_shape=jax.ShapeDtypeStruct((t, d), BF16),
        compiler_params=_params("parallel"),
    )(x, w.reshape(1, d))


def _rmsnorm_bwd(dh, x, w, dres, *, name):
    t, d = x.shape
    tt = _pick(t, (256, 128))

    def body(dh_ref, x_ref, w_ref, dres_ref, dx_ref, dxb_ref, dw_ref):
        xv = x_ref[...]
        r = lax.rsqrt(jnp.mean(xv * xv, axis=1, keepdims=True) + EPS)
        xhat = xv * r
        dhv = dh_ref[...].astype(F32)
        dxhat = dhv * w_ref[...]
        dx = dres_ref[...] + r * (dxhat - xhat * jnp.mean(dxhat * xhat, axis=1, keepdims=True))
        dx_ref[...] = dx
        dxb_ref[...] = dx.astype(BF16)
        _accumulate(dw_ref, _fold8(dhv * xhat), pl.program_id(0) == 0)

    row = pl.BlockSpec((tt, d), lambda i: (i, 0))
    return pl.pallas_call(
        body, name=name, grid=(t // tt,),
        in_specs=[row, row, pl.BlockSpec((1, d), lambda i: (0, 0)), row],
        out_specs=[row, row, pl.BlockSpec((8, d), lambda i: (0, 0))],
        out_shape=[jax.ShapeDtypeStruct((t, d), F32), jax.ShapeDtypeStruct((t, d), BF16),
                   jax.ShapeDtypeStruct((8, d), F32)],
        compiler_params=_params("arbitrary"),
    )(dh, x, w.reshape(1, d), dres)


def _loss_head(x, w, target, *, name):
    t, d = x.shape
    tt = _pick(t, (256, 128))

    def body(x_ref, w_ref, tg_ref, dx_ref, dxb_ref, dw_ref, ls_ref):
        xv = x_ref[...]
        wv = w_ref[...]
        r = lax.rsqrt(jnp.mean(xv * xv, axis=1, keepdims=True) + EPS)
        xhat = xv * r
        err = xhat * wv - tg_ref[...]
        dy = err * (1.0 / d)
        dxhat = dy * wv
        dx = r * (dxhat - xhat * jnp.mean(dxhat * xhat, axis=1, keepdims=True))
        dx_ref[...] = dx
        dxb_ref[...] = dx.astype(BF16)
        first = pl.program_id(0) == 0
        _accumulate(dw_ref, _fold8(dy * xhat), first)
        _accumulate(ls_ref, _fold8(err * err) * (0.5 / d), first)

    row = pl.BlockSpec((tt, d), lambda i: (i, 0))
    acc = pl.BlockSpec((8, d), lambda i: (0, 0))
    return pl.pallas_call(
        body, name=name, grid=(t // tt,),
        in_specs=[row, pl.BlockSpec((1, d), lambda i: (0, 0)), row],
        out_specs=[row, row, acc, acc],
        out_shape=[jax.ShapeDtypeStruct((t, d), F32), jax.ShapeDtypeStruct((t, d), BF16),
                   jax.ShapeDtypeStruct((8, d), F32), jax.ShapeDtypeStruct((8, d), F32)],
        compiler_params=_params("arbitrary"),
    )(x, w.reshape(1, d), target)


def _conv_causal(e, tap, width):
    acc = None
    for k in range(width):
        s = width - 1 - k
        term = (e if s == 0 else pltpu.roll(e, s, 0)) * tap(k)
        acc = term if acc is None else acc + term
    return acc


def _conv_anticausal(e, tap, width):
    rows = e.shape[0]
    acc = None
    for k in range(width):
        s = width - 1 - k
        term = (e if s == 0 else pltpu.roll(e, rows - s, 0)) * tap(k)
        acc = term if acc is None else acc + term
    return acc


def _extend(prev, cur, nxt, first, last):
    parts = []
    if prev is not None:
        parts.append(jnp.where(first, 0.0, prev.astype(F32)))
    parts.append(cur.astype(F32))
    if nxt is not None:
        parts.append(jnp.where(last, 0.0, nxt.astype(F32)))
    return jnp.concatenate(parts, axis=0)


def _prev_idx(i, tt):
    return jnp.maximum(i * (tt // HALO) - 1, 0)


def _next_idx(i, tt, t):
    return jnp.minimum((i + 1) * (tt // HALO), t // HALO - 1)


def _ffn_act_fwd(u3, cw, cb, *, name):
    _, t, f = u3.shape
    tt = _pick(t, (512, 256, 128))
    tc = _pick(f, (512, 256, 128))
    width = cw.shape[0]

    def body(u_ref, up_ref, w_ref, b_ref, o_ref, pre_ref):
        first = pl.program_id(1) == 0
        pre = []
        for p in range(2):
            e = _extend(up_ref[p], u_ref[p], None, first, None)
            pre.append(_conv_causal(e, lambda k: w_ref[k, p:p + 1, :], width)[HALO:] + b_ref[p:p + 1, :])
            pre_ref[p] = pre[p].astype(BF16)
        g, v = pre
        o_ref[...] = (g * _sigmoid(g) * v).astype(BF16)

    return pl.pallas_call(
        body, name=name, grid=(f // tc, t // tt),
        in_specs=[pl.BlockSpec((2, tt, tc), lambda j, i: (0, i, j)),
                  pl.BlockSpec((2, HALO, tc), lambda j, i: (0, _prev_idx(i, tt), j)),
                  pl.BlockSpec((width, 2, tc), lambda j, i: (0, 0, j)),
                  pl.BlockSpec((2, tc), lambda j, i: (0, j))],
        out_specs=[pl.BlockSpec((tt, tc), lambda j, i: (i, j)), pl.BlockSpec((2, tt, tc), lambda j, i: (0, i, j))],
        out_shape=[jax.ShapeDtypeStruct((t, f), BF16), jax.ShapeDtypeStruct((2, t, f), BF16)],
        compiler_params=_params("parallel", "parallel"),
    )(u3, u3, cw, cb)


def _ffn_act_bwd(u3, pre3, da, cw, *, name):
    _, t, f = u3.shape
    tt = _pick(t, (512, 256, 128))
    tc = _pick(f, (512, 256, 128))
    width = cw.shape[0]
    nt = t // tt
    rows = tt + HALO

    def body(u_ref, pre_ref, pren_ref, da_ref, dan_ref, w_ref, du_ref, dcw_ref, dcb_ref):
        i = pl.program_id(1)
        first, last = i == 0, i == nt - 1
        g, v = (_extend(None, pre_ref[p], pren_ref[p], None, False) for p in range(2))
        dae = _extend(None, da_ref[...], dan_ref[...], None, last)
        sg = _sigmoid(g)
        dpre = (dae * v * (sg * (1.0 + g * (1.0 - sg))), dae * (g * sg))

        @pl.when(first)
        def _():
            dcw_ref[...] = jnp.zeros_like(dcw_ref)
            dcb_ref[...] = jnp.zeros_like(dcb_ref)

        for p in range(2):
            u = u_ref[p].astype(F32)
            du = None
            for k in range(width):
                s = width - 1 - k
                d = (dpre[p] if s == 0 else pltpu.roll(dpre[p], rows - s, 0))[:tt]
                term = d * w_ref[k, p:p + 1, :]
                du = term if du is None else du + term
                dcw_ref[k, p:p + 1, :] += jnp.sum(d * u, axis=0, keepdims=True)
                if s == 0:
                    dcb_ref[p:p + 1, :] += jnp.sum(d, axis=0, keepdims=True)
            du_ref[p] = du.astype(BF16)

    cur3 = pl.BlockSpec((2, tt, tc), lambda j, i: (0, i, j))
    return pl.pallas_call(
        body, name=name, grid=(f // tc, nt),
        in_specs=[cur3, cur3,
                  pl.BlockSpec((2, HALO, tc), lambda j, i: (0, _next_idx(i, tt, t), j)),
                  pl.BlockSpec((tt, tc), lambda j, i: (i, j)),
                  pl.BlockSpec((HALO, tc), lambda j, i: (_next_idx(i, tt, t), j)),
                  pl.BlockSpec((width, 2, tc), lambda j, i: (0, 0, j))],
        out_specs=[cur3,
                   pl.BlockSpec((width, 2, tc), lambda j, i: (0, 0, j)),
                   pl.BlockSpec((2, tc), lambda j, i: (0, j))],
        out_shape=[jax.ShapeDtypeStruct((2, t, f), BF16), jax.ShapeDtypeStruct((width, 2, f), F32),
                   jax.ShapeDtypeStruct((2, f), F32)],
        compiler_params=_params("parallel", "arbitrary"),
    )(u3, pre3, pre3, da, da, cw)


def _sc_act_fwd(p3, cw, *, name):
    _, t, c = p3.shape
    tt = _pick(t, (512, 256, 128))
    tc = _pick(c, (512, 256, 128))
    width = cw.shape[0]

    def body(p_ref, pp_ref, w_ref, o_ref):
        first = pl.program_id(1) == 0
        q = _extend(pp_ref[1], p_ref[1], None, first, None) * _extend(pp_ref[2], p_ref[2], None, first, None)
        cq = _conv_causal(q, lambda k: w_ref[k:k + 1, :], width)[HALO:]
        o_ref[...] = (p_ref[0].astype(F32) * cq).astype(BF16)

    return pl.pallas_call(
        body, name=name, grid=(c // tc, t // tt),
        in_specs=[pl.BlockSpec((3, tt, tc), lambda j, i: (0, i, j)),
                  pl.BlockSpec((3, HALO, tc), lambda j, i: (0, _prev_idx(i, tt), j)),
                  pl.BlockSpec((width, tc), lambda j, i: (0, j))],
        out_specs=pl.BlockSpec((tt, tc), lambda j, i: (i, j)),
        out_shape=jax.ShapeDtypeStruct((t, c), BF16),
        compiler_params=_params("parallel", "parallel"),
    )(p3, p3, cw)


def _sc_act_bwd(p3, da, cw, *, name):
    _, t, c = p3.shape
    tt = _pick(t, (512, 256, 128))
    tc = _pick(c, (512, 256, 128))
    width = cw.shape[0]
    nt = t // tt
    ctr = slice(HALO, HALO + tt)

    def body(p_ref, pp_ref, pn_ref, da_ref, dan_ref, w_ref, dp_ref, dcw_ref):
        i = pl.program_id(1)
        first, last = i == 0, i == nt - 1
        tap = lambda k: w_ref[k:k + 1, :]
        bg, cg, hh = (_extend(pp_ref[p], p_ref[p], pn_ref[p], first, last) for p in range(3))
        q = cg * hh
        cq = _conv_causal(q, tap, width)
        dae = _extend(jnp.zeros((HALO, tc), F32), da_ref[...], dan_ref[...], False, last)
        dcq = dae * bg
        dq = _conv_anticausal(dcq, tap, width)[ctr]
        dp_ref[0] = (dae * cq)[ctr].astype(BF16)
        dp_ref[1] = (dq * hh[ctr]).astype(BF16)
        dp_ref[2] = (dq * cg[ctr]).astype(BF16)

        @pl.when(first)
        def _():
            dcw_ref[...] = jnp.zeros_like(dcw_ref)

        dc = dcq[ctr]
        for k in range(width):
            s = width - 1 - k
            qs = (q if s == 0 else pltpu.roll(q, s, 0))[ctr]
            dcw_ref[k:k + 1, :] += jnp.sum(dc * qs, axis=0, keepdims=True)

    return pl.pallas_call(
        body, name=name, grid=(c // tc, nt),
        in_specs=[pl.BlockSpec((3, tt, tc), lambda j, i: (0, i, j)),
                  pl.BlockSpec((3, HALO, tc), lambda j, i: (0, _prev_idx(i, tt), j)),
                  pl.BlockSpec((3, HALO, tc), lambda j, i: (0, _next_idx(i, tt, t), j)),
                  pl.BlockSpec((tt, tc), lambda j, i: (i, j)),
                  pl.BlockSpec((HALO, tc), lambda j, i: (_next_idx(i, tt, t), j)),
                  pl.BlockSpec((width, tc), lambda j, i: (0, j))],
        out_specs=[pl.BlockSpec((3, tt, tc), lambda j, i: (0, i, j)),
                   pl.BlockSpec((width, tc), lambda j, i: (0, j))],
        out_shape=[jax.ShapeDtypeStruct((3, t, c), BF16), jax.ShapeDtypeStruct((width, c), F32)],
        compiler_params=_params("parallel", "arbitrary"),
    )(p3, p3, p3, da, da, cw)


def _ssd_conv_fwd(zx, cw, cb, col0, *, name):
    t = zx.shape[0]
    width, c = cw.shape
    tt = _pick(t, (512, 256, 128))
    tc = _pick(math.gcd(c, col0), (512, 256, 128))
    off = col0 // tc

    def body(x_ref, xp_ref, w_ref, b_ref, o_ref):
        first = pl.program_id(1) == 0
        e = _extend(xp_ref[...], x_ref[...], None, first, None)
        pre = _conv_causal(e, lambda k: w_ref[k:k + 1, :], width)[HALO:] + b_ref[...]
        o_ref[...] = (pre * _sigmoid(pre)).astype(BF16)

    return pl.pallas_call(
        body, name=name, grid=(c // tc, t // tt),
        in_specs=[pl.BlockSpec((tt, tc), lambda j, i: (i, off + j)),
                  pl.BlockSpec((HALO, tc), lambda j, i: (_prev_idx(i, tt), off + j)),
                  pl.BlockSpec((width, tc), lambda j, i: (0, j)),
                  pl.BlockSpec((1, tc), lambda j, i: (0, j))],
        out_specs=pl.BlockSpec((tt, tc), lambda j, i: (i, j)),
        out_shape=jax.ShapeDtypeStruct((t, c), BF16),
        compiler_params=_params("parallel", "parallel"),
    )(zx, zx, cw, cb)


def _ssd_conv_bwd(zx, dxc, cw, cb, dzx, col0, woff, *, name):
    t = zx.shape[0]
    width = cw.shape[0]
    c = dxc.shape[1]
    tt = _pick(t, (512, 256, 128))
    tc = _pick(math.gcd(math.gcd(c, col0), woff) if woff else math.gcd(c, col0), (512, 256, 128))
    nt = t // tt
    xoff, wo = (col0 + woff) // tc, woff // tc
    ctr = slice(HALO, HALO + tt)

    def body(x_ref, xp_ref, xn_ref, d_ref, dn_ref, w_ref, b_ref, dzx_in, dzx_ref, dcw_ref, dcb_ref):
        del dzx_in
        i = pl.program_id(1)
        first, last = i == 0, i == nt - 1
        tap = lambda k: w_ref[k:k + 1, :]
        e = _extend(xp_ref[...], x_ref[...], xn_ref[...], first, last)
        pre = _conv_causal(e, tap, width) + b_ref[...]
        sg = _sigmoid(pre)
        de = _extend(jnp.zeros((HALO, tc), F32), d_ref[...], dn_ref[...], False, last)
        dpre = de * (sg * (1.0 + pre * (1.0 - sg)))
        dzx_ref[...] = _conv_anticausal(dpre, tap, width)[ctr].astype(BF16)

        @pl.when(first)
        def _():
            dcw_ref[...] = jnp.zeros_like(dcw_ref)
            dcb_ref[...] = jnp.zeros_like(dcb_ref)

        dc = dpre[ctr]
        dcb_ref[...] += jnp.sum(dc, axis=0, keepdims=True)
        for k in range(width):
            s = width - 1 - k
            xs = (e if s == 0 else pltpu.roll(e, s, 0))[ctr]
            dcw_ref[k:k + 1, :] += jnp.sum(dc * xs, axis=0, keepdims=True)

    return pl.pallas_call(
        body, name=name, grid=(c // tc, nt),
        in_specs=[pl.BlockSpec((tt, tc), lambda j, i: (i, xoff + j)),
                  pl.BlockSpec((HALO, tc), lambda j, i: (_prev_idx(i, tt), xoff + j)),
                  pl.BlockSpec((HALO, tc), lambda j, i: (_next_idx(i, tt, t), xoff + j)),
                  pl.BlockSpec((tt, tc), lambda j, i: (i, j)),
                  pl.BlockSpec((HALO, tc), lambda j, i: (_next_idx(i, tt, t), j)),
                  pl.BlockSpec((width, tc), lambda j, i: (0, wo + j)),
                  pl.BlockSpec((1, tc), lambda j, i: (0, wo + j)),
                  pl.BlockSpec(memory_space=pl.ANY)],
        out_specs=[pl.BlockSpec((tt, tc), lambda j, i: (i, xoff + j)),
                   pl.BlockSpec((width, tc), lambda j, i: (0, j)),
                   pl.BlockSpec((1, tc), lambda j, i: (0, j))],
        out_shape=[jax.ShapeDtypeStruct(dzx.shape, dzx.dtype), jax.ShapeDtypeStruct((width, c), F32),
                   jax.ShapeDtypeStruct((1, c), F32)],
        input_output_aliases={7: 0},
        compiler_params=_params("parallel", "arbitrary"),
    )(zx, zx, zx, dxc, dxc, cw, cb, dzx)


def _ssd_put_ddt(ddt_g, dzx, col, *, name):
    g, t, _ = ddt_g.shape
    tt = _pick(t, (512, 256, 128))

    def body(d_ref, dzx_in, dzx_ref):
        del dzx_in
        dzx_ref[...] = jnp.sum(d_ref[...], axis=0).astype(BF16)

    return pl.pallas_call(
        body, name=name, grid=(t // tt,),
        in_specs=[pl.BlockSpec((g, tt, LANE), lambda i: (0, i, 0)), pl.BlockSpec(memory_space=pl.ANY)],
        out_specs=pl.BlockSpec((tt, LANE), lambda i: (i, col)),
        out_shape=jax.ShapeDtypeStruct(dzx.shape, dzx.dtype),
        input_output_aliases={1: 0},
        compiler_params=_params("parallel"),
    )(ddt_g, dzx)


def _dot(a, b, mode):
    return lax.dot_general(a, b, _DIMS[mode], preferred_element_type=F32)


def _dot_exact(m01, v, mode="nn"):
    hi = v.astype(BF16)
    r1 = v - hi.astype(F32)
    mid = r1.astype(BF16)
    lo = (r1 - mid.astype(F32)).astype(BF16)
    return _dot(m01, hi, mode) + _dot(m01, mid, mode) + _dot(m01, lo, mode)


def _softplus(x):
    return jnp.maximum(x, 0.0) + jnp.log(1.0 + jnp.exp(-jnp.abs(x)))


def _head_vectors(g, dt_raw, bias, alog):
    n = CHUNK
    dt = _softplus(dt_raw + bias)
    a = -jnp.exp(alog)
    tri = (lax.broadcasted_iota(jnp.int32, (n, n), 0) >= lax.broadcasted_iota(jnp.int32, (n, n), 1)).astype(BF16)
    cs = _dot_exact(tri, dt * a)
    return dt, a, cs, cs.T


def _col(v, lane_ids, h):
    return jnp.sum(jnp.where(lane_ids == h, v, 0.0), axis=1, keepdims=True)


def _row(vt, sub_ids, h):
    return jnp.sum(jnp.where(sub_ids == h, vt, 0.0), axis=0, keepdims=True)


def _ssd_specs(di, bc, nc, rev):
    cidx = (lambda c: nc - 1 - c) if rev else (lambda c: c)
    wide = lambda off: pl.BlockSpec((CHUNK, GROUP_W), lambda g, c: (cidx(c), off + g))
    lane = lambda off: pl.BlockSpec((CHUNK, LANE), lambda g, c: (cidx(c), off + g))
    fixed = lambda off: pl.BlockSpec((CHUNK, LANE), lambda g, c: (cidx(c), off))
    vec = pl.BlockSpec((1, LANE), lambda g, c: (0, 0))
    gvec = pl.BlockSpec((1, GROUP_W), lambda g, c: (0, g))
    state = pl.BlockSpec((None, None, 4, PAIR, STATE), lambda g, c: (g, cidx(c), 0, 0, 0))
    return wide, lane, fixed, vec, gvec, state


def _ssd_fwd(xbc, zx, bias, alog, dexp, nw, *, name):
    t = xbc.shape[0]
    di = nw.shape[1]
    bc = (xbc.shape[1] - di) // 2
    ng, nc = di // GROUP_W, t // CHUNK
    wide, lane, fixed, vec, gvec, state = _ssd_specs(di, bc, nc, rev=False)

    def body(xs_ref, b_ref, c_ref, dt_ref, z_ref, bias_ref, alog_ref, dexp_ref, nw_ref,
             yn_ref, y_ref, st_ref, s_scr):
        g, c = pl.program_id(0), pl.program_id(1)

        @pl.when(c == 0)
        def _():
            s_scr[...] = jnp.zeros_like(s_scr)

        n = CHUNK
        dt, a, cs, cst = _head_vectors(g, dt_ref[...].astype(F32), bias_ref[...], alog_ref[...])
        lane_ids = lax.broadcasted_iota(jnp.int32, (n, LANE), 1)
        sub_ids = lax.broadcasted_iota(jnp.int32, (LANE, n), 0)
        causal = lax.broadcasted_iota(jnp.int32, (n, n), 0) >= lax.broadcasted_iota(jnp.int32, (n, n), 1)
        half = lax.broadcasted_iota(jnp.int32, (1, PAIR), 1) < HEAD_DIM
        half_rows = lax.broadcasted_iota(jnp.int32, (PAIR, 1), 0) < HEAD_DIM
        bm, cm = b_ref[...], c_ref[...]
        gm = _dot(cm, bm, "nt")
        x = xs_ref[...].astype(F32)
        ys = []
        for q in range(4):
            h0 = g * 8 + 2 * q
            col = [_col(cs, lane_ids, h0 + e) for e in range(2)]
            row = [_row(cst, sub_ids, h0 + e) for e in range(2)]
            dtc = [_col(dt, lane_ids, h0 + e) for e in range(2)]
            last = [col[e][n - 1:n, :] for e in range(2)]
            xd = x[:, q * PAIR:(q + 1) * PAIR] * jnp.where(half, dtc[0], dtc[1])
            xd_bf = xd.astype(BF16)
            yd = []
            for e in range(2):
                lm = jnp.exp(jnp.where(causal, col[e] - row[e], -1e30))
                yd.append(_dot((gm * lm).astype(BF16), xd_bf, "nn"))
            s = s_scr[q]
            st_ref[q] = s
            ecs = jnp.where(half, jnp.exp(col[0]), jnp.exp(col[1]))
            dte = jnp.where(half, jnp.exp(last[0] - col[0]), jnp.exp(last[1] - col[1]))
            yoff = ecs * _dot(cm, s.astype(BF16), "nt")
            snew = _dot((xd * dte).astype(BF16), bm, "tn")
            s_scr[q] = s * jnp.where(half_rows, jnp.exp(last[0]), jnp.exp(last[1])) + snew
            ys.append(jnp.where(half, yd[0], yd[1]) + yoff)
        y = jnp.concatenate(ys, axis=1) + dexp_ref[...] * x
        y_ref[...] = y.astype(BF16)
        z = z_ref[...].astype(F32)
        yg = y * (z * _sigmoid(z))
        r = lax.rsqrt(jnp.mean(yg * yg, axis=1, keepdims=True) + EPS)
        yn_ref[...] = (yg * r * nw_ref[...]).astype(BF16)

    dtcol = (2 * di + 2 * bc) // LANE
    return pl.pallas_call(
        body, name=name, grid=(ng, nc),
        in_specs=[wide(0), lane(di // LANE), lane((di + bc) // LANE), fixed(dtcol), wide(0),
                  vec, vec, gvec, gvec],
        out_specs=[wide(0), wide(0), state],
        out_shape=[jax.ShapeDtypeStruct((t, di), BF16), jax.ShapeDtypeStruct((t, di), BF16),
                   jax.ShapeDtypeStruct((ng, nc, 4, PAIR, STATE), F32)],
        scratch_shapes=[pltpu.VMEM((4, PAIR, STATE), F32)],
        compiler_params=_params("parallel", "arbitrary"),
    )(xbc, xbc, xbc, zx, zx, bias, alog, dexp, nw)


def _ssd_bwd(dyn, y, xbc, zx, states, bias, alog, dexp, nw, *, name):
    t = xbc.shape[0]
    di = nw.shape[1]
    bc = (xbc.shape[1] - di) // 2
    ng, nc = di // GROUP_W, t // CHUNK
    wide, lane, fixed, vec, gvec, state = _ssd_specs(di, bc, nc, rev=True)
    acc = lambda w: pl.BlockSpec((None, 8, w), lambda g, c: (g, 0, 0))

    def body(dyn_ref, y_ref, z_ref, nw_ref, xs_ref, b_ref, c_ref, dt_ref, bias_ref, alog_ref, dexp_ref, st_ref,
             dz_ref, dxs_ref, db_ref, dc_ref, ddt_ref, small_ref, dnw_ref, ddexp_ref, ds_scr):
        g, c = pl.program_id(0), pl.program_id(1)

        @pl.when(c == 0)
        def _():
            ds_scr[...] = jnp.zeros_like(ds_scr)
            small_ref[...] = jnp.zeros_like(small_ref)
            dnw_ref[...] = jnp.zeros_like(dnw_ref)
            ddexp_ref[...] = jnp.zeros_like(ddexp_ref)

        n = CHUNK
        yv = y_ref[...].astype(F32)
        z = z_ref[...].astype(F32)
        sz = _sigmoid(z)
        silu = z * sz
        yg = yv * silu
        r = lax.rsqrt(jnp.mean(yg * yg, axis=1, keepdims=True) + EPS)
        yhat = yg * r
        dynv = dyn_ref[...].astype(F32)
        dnw_ref[0:1, :] += jnp.sum(dynv * yhat, axis=0, keepdims=True)
        dyhat = dynv * nw_ref[...]
        dyg = r * (dyhat - yhat * jnp.mean(dyhat * yhat, axis=1, keepdims=True))
        dz_ref[...] = (dyg * yv * (sz * (1.0 + z * (1.0 - sz)))).astype(BF16)
        dy = dyg * silu

        dt_in = dt_ref[...].astype(F32) + bias_ref[...]
        dt, a, cs, cst = _head_vectors(g, dt_ref[...].astype(F32), bias_ref[...], alog_ref[...])
        lane_ids = lax.broadcasted_iota(jnp.int32, (n, LANE), 1)
        sub_ids = lax.broadcasted_iota(jnp.int32, (LANE, n), 0)
        ri = lax.broadcasted_iota(jnp.int32, (n, n), 0)
        ci = lax.broadcasted_iota(jnp.int32, (n, n), 1)
        causal, causal_t = ri >= ci, ci >= ri
        is_last = lax.broadcasted_iota(jnp.int32, (n, 1), 0) == n - 1
        half = lax.broadcasted_iota(jnp.int32, (1, PAIR), 1) < HEAD_DIM
        half_rows = lax.broadcasted_iota(jnp.int32, (PAIR, 1), 0) < HEAD_DIM
        bm, cm = b_ref[...], c_ref[...]
        bf = bm.astype(F32)
        gm, gmt = _dot(cm, bm, "nt"), _dot(bm, cm, "nt")
        x = xs_ref[...].astype(F32)
        dexp = dexp_ref[...]

        dg_sum = jnp.zeros((n, n), F32)
        dgt_sum = jnp.zeros((n, n), F32)
        db_off = jnp.zeros((n, STATE), F32)
        dc_off = jnp.zeros((n, STATE), F32)
        dcs_blk = jnp.zeros((n, LANE), F32)
        ddt_blk = jnp.zeros((n, LANE), F32)
        dxs = []
        for q in range(4):
            h0 = g * 8 + 2 * q
            sl = slice(q * PAIR, (q + 1) * PAIR)
            col = [_col(cs, lane_ids, h0 + e) for e in range(2)]
            row = [_row(cst, sub_ids, h0 + e) for e in range(2)]
            dtc = [_col(dt, lane_ids, h0 + e) for e in range(2)]
            last = [col[e][n - 1:n, :] for e in range(2)]
            xp, dyp = x[:, sl], dy[:, sl]
            dtp = jnp.where(half, dtc[0], dtc[1])
            xd = xp * dtp
            xd_bf, dyp_bf = xd.astype(BF16), dyp.astype(BF16)
            ecs = jnp.where(half, jnp.exp(col[0]), jnp.exp(col[1]))
            dte = jnp.where(half, jnp.exp(last[0] - col[0]), jnp.exp(last[1] - col[1]))
            s, ds = st_ref[q], ds_scr[q]
            s_bf, ds_bf = s.astype(BF16), ds.astype(BF16)
            yoff = ecs * _dot(cm, s_bf, "nt")
            edy_bf = (ecs * dyp).astype(BF16)
            dc_off += _dot(edy_bf, s_bf, "nn")
            bds = _dot(bm, ds_bf, "nt")
            sds = s * ds
            zs = []
            for e in range(2):
                msk = half if e == 0 else jnp.logical_not(half)
                msk_rows = half_rows if e == 0 else jnp.logical_not(half_rows)
                lm = jnp.exp(jnp.where(causal, col[e] - row[e], -1e30))
                lmt = jnp.exp(jnp.where(causal_t, row[e] - col[e], -1e30))
                dym_bf = jnp.where(msk, dyp, 0.0).astype(BF16)
                xdm_bf = jnp.where(msk, xd, 0.0).astype(BF16)
                dm = _dot(dym_bf, xd_bf, "nt")
                dmt = _dot(xdm_bf, dyp_bf, "nt")
                m, mt = gm * lm, gmt * lmt
                dcs = jnp.sum(dm * m, axis=1, keepdims=True) - jnp.sum(dmt * mt, axis=1, keepdims=True)
                dg_sum += dm * lm
                dgt_sum += dmt * lmt
                zs.append(_dot(mt.astype(BF16), dyp_bf, "nn"))
                we = _dot(xdm_bf, ds_bf, "nn")
                dte_col = jnp.exp(last[e] - col[e])
                te = dte_col * jnp.sum(we * bf, axis=1, keepdims=True)
                db_off += dte_col * we
                dcs += jnp.sum(jnp.where(msk, dyp * yoff, 0.0), axis=1, keepdims=True) - te
                tail = jnp.exp(last[e]) * jnp.sum(jnp.where(msk_rows, sds, 0.0), keepdims=True) \
                    + jnp.sum(te, keepdims=True)
                dcs += jnp.where(is_last, tail, 0.0)
                dcs_blk += jnp.where(lane_ids == h0 + e, dcs, 0.0)
            dxd = jnp.where(half, zs[0], zs[1]) + dte * bds
            dxs.append(dxd * dtp + dexp[:, sl] * dyp)
            ddexp_ref[0:1, sl] += jnp.sum(dyp * xp, axis=0, keepdims=True)
            rs = dxd * xp
            for e in range(2):
                msk = half if e == 0 else jnp.logical_not(half)
                ddt_blk += jnp.where(lane_ids == h0 + e, jnp.sum(jnp.where(msk, rs, 0.0), axis=1, keepdims=True), 0.0)
            ds_scr[q] = ds * jnp.where(half_rows, jnp.exp(last[0]), jnp.exp(last[1])) + _dot(edy_bf, cm, "tn")

        dxs_ref[...] = jnp.concatenate(dxs, axis=1).astype(BF16)
        dc_ref[...] = (_dot(dg_sum.astype(BF16), bm, "nn") + dc_off).astype(BF16)
        db_ref[...] = (_dot(dgt_sum.astype(BF16), cm, "nn") + db_off).astype(BF16)
        upper = (ri <= ci).astype(BF16)
        dda = _dot_exact(upper, dcs_blk)
        ddt = dda * a + ddt_blk
        small_ref[0:1, :] += jnp.sum(dda * dt, axis=0, keepdims=True) * a
        ddt_raw = ddt * _sigmoid(dt_in)
        small_ref[1:2, :] += jnp.sum(ddt_raw, axis=0, keepdims=True)
        ddt_ref[...] = ddt_raw

    dtcol = (2 * di + 2 * bc) // LANE
    tot = 2 * di + 2 * bc + LANE
    return pl.pallas_call(
        body, name=name, grid=(ng, nc),
        in_specs=[wide(0), wide(0), wide(0), gvec, wide(0), lane(di // LANE), lane((di + bc) // LANE),
                  fixed(dtcol), vec, vec, gvec, state],
        out_specs=[wide(0), wide(0), lane(0), lane(0),
                   pl.BlockSpec((None, CHUNK, LANE), lambda g, c: (g, nc - 1 - c, 0)),
                   acc(LANE), acc(GROUP_W), acc(GROUP_W)],
        out_shape=[jax.ShapeDtypeStruct((t, tot), BF16), jax.ShapeDtypeStruct((t, di), BF16),
                   jax.ShapeDtypeStruct((t, bc), BF16), jax.ShapeDtypeStruct((t, bc), BF16),
                   jax.ShapeDtypeStruct((ng, t, LANE), F32), jax.ShapeDtypeStruct((ng, 8, LANE), F32),
                   jax.ShapeDtypeStruct((ng, 8, GROUP_W), F32), jax.ShapeDtypeStruct((ng, 8, GROUP_W), F32)],
        scratch_shapes=[pltpu.VMEM((4, PAIR, STATE), F32)],
        compiler_params=_params("parallel", "arbitrary"),
    )(dyn, y, zx, nw, xbc, xbc, xbc, zx, bias, alog, dexp, states)


HBM_ANY = pl.BlockSpec(memory_space=pl.ANY)


def _place():
    x, y, c = lax.axis_index("x"), lax.axis_index("y"), lax.axis_index("c")
    chips = [(1 - x, y), (x, 1 - y), (1 - x, 1 - y)]
    return x, y, c, chips


def _all_gather(arrs, *, name, inplace=False):
    n = len(arrs)

    def body(*refs):
        ins, outs = refs[:n], refs[n:2 * n]
        send, recv, loc = refs[2 * n:]
        x, y, c, chips = _place()
        me, sib = (x, y, c), (x, y, 1 - c)

        def blk(a, p):
            return outs[a].at[4 * p[0] + 2 * p[1] + p[2]]

        def cp(a, k, block, to, src=None):
            return pltpu.make_async_remote_copy(
                src_ref=blk(a, block) if src is None else src, dst_ref=blk(a, block),
                send_sem=send.at[a * 7 + k], recv_sem=recv.at[a * 7 + k], device_id=to, device_id_type=MESH)

        src = [None if inplace else ins[a] for a in range(n)]
        mine = [] if inplace else [pltpu.make_async_copy(ins[a], blk(a, me), loc.at[a]) for a in range(n)]
        for m in mine:
            m.start()
        started = []
        for a in range(n):
            started.append(cp(a, 0, me, sib, src=src[a]))
            started += [cp(a, 1 + j, me, (*chip, c), src=src[a]) for j, chip in enumerate(chips)]
        for s in started:
            s.start()
        for j, chip in enumerate(chips):
            for a in range(n):
                cp(a, 1 + j, (*chip, c), me).wait_recv()
                fwd = cp(a, 4 + j, (*chip, c), sib)
                fwd.start()
                started.append(fwd)
        for a in range(n):
            cp(a, 0, sib, me).wait_recv()
            for j, chip in enumerate(chips):
                cp(a, 4 + j, (*chip, 1 - c), me).wait_recv()
        for s in started:
            s.wait_send()
        for m in mine:
            m.wait()

    return pl.pallas_call(
        body, name=name,
        in_specs=[HBM_ANY] * n, out_specs=[HBM_ANY] * n,
        out_shape=[jax.ShapeDtypeStruct(a.shape if inplace else (N_DEV,) + a.shape, a.dtype) for a in arrs],
        input_output_aliases={a: a for a in range(n)} if inplace else {},
        scratch_shapes=[pltpu.SemaphoreType.DMA((7 * n,)), pltpu.SemaphoreType.DMA((7 * n,)),
                        pltpu.SemaphoreType.DMA((n,))],
    )(*arrs)


HBM_SPEC = pl.BlockSpec(memory_space=pltpu.HBM)
SEM_SPEC = pl.BlockSpec(memory_space=pltpu.SEMAPHORE)
SPLIT_EFFECT = pltpu.SideEffectType.DATAFLOW_SIDE_EFFECTING


def _split_start(arrs, plan, n_copies, after, *, name):
    m = len(arrs)

    def body(*refs):
        send, recv, token = refs[m + 1], refs[m + 2], refs[-1]
        for i, (src, dst, to) in enumerate(plan(refs[:m])):
            pltpu.make_async_remote_copy(src_ref=src, dst_ref=dst, send_sem=send.at[i], recv_sem=recv.at[i],
                                         device_id=to, device_id_type=MESH).start()
        token[...] = jnp.zeros_like(token)

    outs = pl.pallas_call(
        body, name=name,
        out_shape=(pltpu.SemaphoreType.DMA((n_copies,)), pltpu.SemaphoreType.DMA((n_copies,)),
                   *[pltpu.HBM(a.shape, a.dtype) for a in arrs], jax.ShapeDtypeStruct((8, LANE), F32)),
        in_specs=[HBM_SPEC] * m + [HBM_ANY],
        out_specs=(SEM_SPEC, SEM_SPEC, *[HBM_SPEC] * m, pl.BlockSpec(memory_space=pltpu.VMEM)),
        input_output_aliases={i: 2 + i for i in range(m)},
        compiler_params=pltpu.CompilerParams(has_side_effects=SPLIT_EFFECT),
    )(*[pltpu.with_memory_space_constraint(a, pltpu.HBM) for a in arrs], after)
    return outs[0], outs[1], list(outs[2:2 + m]), outs[-1]


def _split_wait(arrs, send, recv, after, plan, *, name):
    m = len(arrs)

    def body(*refs):
        send_ref, recv_ref = refs[m], refs[m + 1]
        for i, (src, dst, to) in enumerate(plan(refs[:m])):
            cp = pltpu.make_async_remote_copy(src_ref=src, dst_ref=dst, send_sem=send_ref.at[i],
                                              recv_sem=recv_ref.at[i], device_id=to, device_id_type=MESH)
            cp.wait_send()
            cp.wait_recv()

    outs = pl.pallas_call(
        body, name=name,
        out_shape=[pltpu.HBM(a.shape, a.dtype) for a in arrs],
        in_specs=[HBM_SPEC] * m + [SEM_SPEC, SEM_SPEC, HBM_ANY], out_specs=[HBM_SPEC] * m,
        input_output_aliases={i: i for i in range(m)},
        compiler_params=pltpu.CompilerParams(has_side_effects=SPLIT_EFFECT),
    )(*arrs, send, recv, after)
    return list(outs)


def _dev(p):
    return 4 * p[0] + 2 * p[1] + p[2]


def _plan_gather_ici(bufs):
    x, y, c, chips = _place()
    me = _dev((x, y, c))
    peers = [(x, y, 1 - c)] + [(*chip, c) for chip in chips]
    return [(b.at[me], b.at[me], p) for b in bufs for p in peers]


def _plan_gather_d2d(bufs):
    x, y, c, chips = _place()
    return [(b.at[_dev((*chip, c))], b.at[_dev((*chip, c))], (x, y, 1 - c)) for b in bufs for chip in chips]


def _plan_pair(refs):
    n = len(refs) // 2
    x, y, c, _ = _place()
    return [(refs[a].at[2 * k + 1 - c], refs[n + a].at[k], (x, y, 1 - c)) for a in range(n) for k in range(N_CHIP)]


def _plan_chip(refs):
    n = len(refs) // 2
    x, y, c, chips = _place()
    return [(refs[a].at[2 * chip[0] + chip[1]], refs[n + a].at[j], (*chip, c))
            for a in range(n) for j, chip in enumerate(chips)]


def _land(shape, dtype):
    return lax.empty(shape, dtype)


def _with_tokens(v, *tokens):
    for t in tokens:
        if t is not None:
            v = v + t[0, 0].astype(v.dtype)
    return v


def _pair_exchange(grads, *, name):
    n = len(grads)

    def body(*refs):
        ins, gots = refs[:n], refs[n:2 * n]
        send, recv = refs[2 * n:]
        x, y, c, _ = _place()
        copies = []
        for a in range(n):
            for k in range(N_CHIP):
                copies.append(pltpu.make_async_remote_copy(
                    src_ref=ins[a].at[2 * k + 1 - c], dst_ref=gots[a].at[k],
                    send_sem=send.at[a * N_CHIP + k], recv_sem=recv.at[a * N_CHIP + k],
                    device_id=(x, y, 1 - c), device_id_type=MESH))
        for cpy in copies:
            cpy.start()
        for cpy in copies:
            cpy.wait()

    return pl.pallas_call(
        body, name=name,
        in_specs=[HBM_ANY] * n, out_specs=[HBM_ANY] * n,
        out_shape=[jax.ShapeDtypeStruct((N_CHIP,) + g.shape[1:], g.dtype) for g in grads],
        scratch_shapes=[pltpu.SemaphoreType.DMA((N_CHIP * n,)), pltpu.SemaphoreType.DMA((N_CHIP * n,))],
    )(*grads)


def _chip_exchange(sums, *, name):
    n = len(sums)

    def body(*refs):
        ins, outs = refs[:n], refs[n:2 * n]
        send, recv = refs[2 * n:]
        x, y, c, chips = _place()
        copies = []
        for a in range(n):
            for j, chip in enumerate(chips):
                copies.append(pltpu.make_async_remote_copy(
                    src_ref=ins[a].at[2 * chip[0] + chip[1]], dst_ref=outs[a].at[j],
                    send_sem=send.at[a * 3 + j], recv_sem=recv.at[a * 3 + j],
                    device_id=(*chip, c), device_id_type=MESH))
        for cpy in copies:
            cpy.start()
        for cpy in copies:
            cpy.wait()

    return pl.pallas_call(
        body, name=name,
        in_specs=[HBM_ANY] * n, out_specs=[HBM_ANY] * n,
        out_shape=[jax.ShapeDtypeStruct((N_CHIP - 1,) + s.shape[1:], s.dtype) for s in sums],
        scratch_shapes=[pltpu.SemaphoreType.DMA((3 * n,)), pltpu.SemaphoreType.DMA((3 * n,))],
    )(*sums)


def _add_pair(grad, got, core, *, name):
    k, r, c = got.shape
    tr, tc = _tile2(r, c, rows=(1024, 704, 512, 256, 128, 64, 32, 16))

    def body(core_ref, a_ref, b_ref, o_ref):
        del core_ref
        o_ref[...] = (a_ref[...].astype(F32) + b_ref[...].astype(F32)).astype(BF16)

    spec = pl.BlockSpec((None, tr, tc), lambda q, i, j, core_ref: (q, i, j))
    return pl.pallas_call(
        body, name=name,
        grid_spec=pltpu.PrefetchScalarGridSpec(
            num_scalar_prefetch=1, grid=(k, r // tr, c // tc),
            in_specs=[pl.BlockSpec((None, tr, tc), lambda q, i, j, core_ref: (2 * q + core_ref[0], i, j)), spec],
            out_specs=spec),
        out_shape=jax.ShapeDtypeStruct(got.shape, BF16),
        compiler_params=_params("parallel", "parallel", "parallel"),
    )(core, grad, got)


def _all_reduce_small(v, *, name):
    r = v.shape[0]

    def body(v_ref, o_ref, buf, send, recv):
        x, y, c, _ = _place()
        me = 4 * x + 2 * y + c
        buf[me] = v_ref[...]
        copies = []
        for rel in range(1, N_DEV):
            fx, fy, fc = rel >> 2 & 1, rel >> 1 & 1, rel & 1
            peer = ((1 - x) if fx else x, (1 - y) if fy else y, (1 - c) if fc else c)
            copies.append(pltpu.make_async_remote_copy(
                src_ref=v_ref, dst_ref=buf.at[me], send_sem=send.at[rel - 1], recv_sem=recv.at[rel - 1],
                device_id=peer, device_id_type=MESH))
        for cpy in copies:
            cpy.start()
        for cpy in copies:
            cpy.wait()
        acc = buf[0]
        for d in range(1, N_DEV):
            acc = acc + buf[d]
        o_ref[...] = acc

    return pl.pallas_call(
        body, name=name,
        in_specs=[pl.BlockSpec(memory_space=pltpu.VMEM)], out_specs=pl.BlockSpec(memory_space=pltpu.VMEM),
        out_shape=jax.ShapeDtypeStruct(v.shape, F32),
        scratch_shapes=[pltpu.VMEM((N_DEV, r, LANE), F32), pltpu.SemaphoreType.DMA((N_DEV - 1,)),
                        pltpu.SemaphoreType.DMA((N_DEV - 1,))],
        compiler_params=pltpu.CompilerParams(vmem_limit_bytes=VMEM_LIMIT),
    )(v)


def _adamw_math(w, g, m, v):
    m = ADAM_B1 * m + (1.0 - ADAM_B1) * g
    v = ADAM_B2 * v + (1.0 - ADAM_B2) * (g * g)
    m_hat = m / (1.0 - ADAM_B1 ** ADAM_STEP)
    v_hat = v / (1.0 - ADAM_B2 ** ADAM_STEP)
    delta = -ADAM_LR * (m_hat / (jnp.sqrt(v_hat) + ADAM_EPS) + ADAM_WD * w)
    return delta, m, v


def _adamw_layer(w, m, v, sums, recv, chip, layer, prev, after, *, name):
    nl, r, c = w.shape
    tr, tc = _tile2(r, c)

    def body(chip_ref, w_ref, m_ref, v_ref, s_ref, p_ref, *rest):
        del chip_ref
        g_ref, d_ref, mo_ref, vo_ref = rest[-4:]
        g = s_ref[...].astype(F32)
        for k in range(N_CHIP - 1):
            g = g + p_ref[k].astype(F32)
        delta, mn, vn = _adamw_math(w_ref[...], g, m_ref[...], v_ref[...])
        g_ref[...] = g
        d_ref[...] = delta
        mo_ref[...] = mn
        vo_ref[...] = vn

    lay = pl.BlockSpec((None, tr, tc), lambda i, j, chip_ref: (layer, i, j))
    ins = [w, m, v, sums, recv, after] + (list(prev) if prev is not None else [])
    in_specs = [lay, lay, lay, pl.BlockSpec((None, tr, tc), lambda i, j, chip_ref: (chip_ref[0], i, j)),
                pl.BlockSpec((N_CHIP - 1, tr, tc), lambda i, j, chip_ref: (0, i, j)), HBM_ANY]
    in_specs += [HBM_ANY] * (4 if prev is not None else 0)
    return pl.pallas_call(
        body, name=name,
        grid_spec=pltpu.PrefetchScalarGridSpec(
            num_scalar_prefetch=1, grid=(r // tr, c // tc), in_specs=in_specs, out_specs=[lay] * 4),
        out_shape=[jax.ShapeDtypeStruct(w.shape, F32)] * 4,
        input_output_aliases={7 + q: q for q in range(4)} if prev is not None else {},
        compiler_params=_params("parallel", "parallel"),
    )(chip, *ins)


def _adamw_small(w, g, m, v, *, name):
    def body(w_ref, g_ref, m_ref, v_ref, d_ref, mo_ref, vo_ref):
        d_ref[...], mo_ref[...], vo_ref[...] = _adamw_math(w_ref[...], g_ref[...], m_ref[...], v_ref[...])

    vm = pl.BlockSpec(memory_space=pltpu.VMEM)
    return pl.pallas_call(
        body, name=name, in_specs=[vm] * 4, out_specs=[vm] * 3,
        out_shape=[jax.ShapeDtypeStruct(w.shape, F32)] * 3,
        compiler_params=pltpu.CompilerParams(vmem_limit_bytes=VMEM_LIMIT),
    )(w, g, m, v)


def _pack(arrs):
    flat = jnp.concatenate([a.reshape(-1).astype(F32) for a in arrs])
    pad = (-flat.shape[0]) % (8 * LANE)
    return jnp.pad(flat, (0, pad)).reshape(-1, LANE)


def _unpack(packed, shapes):
    flat = packed.reshape(-1)
    out, off = [], 0
    for s in shapes:
        size = math.prod(s)
        out.append(flat[off:off + size].reshape(s))
        off += size
    return out


WEIGHTS = ['mix_norm_w', 'ffn_norm_w', 'final_norm_w', 'ssd_w_in', 'ssd_conv_w', 'ssd_conv_b', 'ssd_dt_bias',
           'ssd_a_log', 'ssd_d', 'ssd_norm_w', 'ssd_w_out', 'sc_w_in', 'sc_conv_w', 'sc_w_out', 'ffn_w_up',
           'ffn_conv_w', 'ffn_conv_b', 'ffn_w_down']
BIG = ('ssd_w_in', 'ssd_w_out', 'sc_w_in', 'sc_w_out', 'ffn_w_up', 'ffn_w_down')
SHARDED_SMALL = ('ssd_conv_w', 'sc_conv_w', 'ffn_conv_w')


def _lane_pad(v):
    return jnp.pad(v.astype(F32), (0, LANE - v.shape[0])).reshape(1, LANE)


def _gather_cols(g):
    return jnp.moveaxis(g, 0, -2).reshape(g.shape[1:-1] + (N_DEV * g.shape[-1],))


class _Gather:
    def __init__(self, bufs, tag):
        self.bufs, self.tag = bufs, tag

    def start_ici(self, after):
        self.sems = _split_start(self.bufs, _plan_gather_ici, 4 * len(self.bufs), after, name=f"ag_ici_start_{self.tag}")
        return self.sems[3]

    def hand_on(self, after):
        send, recv, bufs, _ = self.sems
        bufs = _split_wait(bufs, send, recv, after, _plan_gather_ici, name=f"ag_ici_wait_{self.tag}")
        self.sems = _split_start(bufs, _plan_gather_d2d, 3 * len(bufs), after, name=f"ag_d2d_start_{self.tag}")
        return self.sems[3]

    def finish(self, after):
        send, recv, bufs, _ = self.sems
        return _split_wait(bufs, send, recv, after, _plan_gather_d2d, name=f"ag_d2d_wait_{self.tag}")


class _Scatter:
    def __init__(self, grads, core, tag):
        self.grads, self.core, self.tag = grads, core, tag

    def start_pair(self, after):
        lands = [_land((N_CHIP,) + g.shape[1:], g.dtype) for g in self.grads]
        self.sems = _split_start(self.grads + lands, _plan_pair, N_CHIP * len(lands), after,
                                 name=f"rs_pair_start_{self.tag}")
        return self.sems[3]

    def start_chip(self, after):
        n = len(self.grads)
        send, recv, arrs, _ = self.sems
        arrs = _split_wait(arrs, send, recv, after, _plan_pair, name=f"rs_pair_wait_{self.tag}")
        self.sums = [_add_pair(g, o, self.core, name=f"rs_add_{self.tag}{a}")
                     for a, (g, o) in enumerate(zip(arrs[:n], arrs[n:]))]
        lands = [_land((N_CHIP - 1,) + s.shape[1:], s.dtype) for s in self.sums]
        self.sems = _split_start(self.sums + lands, _plan_chip, (N_CHIP - 1) * n, after,
                                 name=f"rs_chip_start_{self.tag}")
        return self.sems[3]

    def finish(self, after):
        n = len(self.grads)
        send, recv, arrs, _ = self.sems
        arrs = _split_wait(arrs, send, recv, after, _plan_chip, name=f"rs_chip_wait_{self.tag}")
        return list(zip(arrs[:n], arrs[n:]))


def kernel(x, mix_norm_w, ffn_norm_w, final_norm_w, ssd_w_in, ssd_conv_w, ssd_conv_b, ssd_dt_bias, ssd_a_log, ssd_d, ssd_norm_w, ssd_w_out, sc_w_in, sc_conv_w, sc_w_out, ffn_w_up, ffn_conv_w, ffn_conv_b, ffn_w_down, loss_target, m_mix_norm_w, m_ffn_norm_w, m_final_norm_w, m_ssd_w_in, m_ssd_conv_w, m_ssd_conv_b, m_ssd_dt_bias, m_ssd_a_log, m_ssd_d, m_ssd_norm_w, m_ssd_w_out, m_sc_w_in, m_sc_conv_w, m_sc_w_out, m_ffn_w_up, m_ffn_conv_w, m_ffn_conv_b, m_ffn_w_down, v_mix_norm_w, v_ffn_norm_w, v_final_norm_w, v_ssd_w_in, v_ssd_conv_w, v_ssd_conv_b, v_ssd_dt_bias, v_ssd_a_log, v_ssd_d, v_ssd_norm_w, v_ssd_w_out, v_sc_w_in, v_sc_conv_w, v_sc_w_out, v_ffn_w_up, v_ffn_conv_w, v_ffn_conv_b, v_ffn_w_down):
    args = locals()
    wt = {n: args[n] for n in WEIGHTS}
    mom = {n: args["m_" + n] for n in WEIGHTS}
    var = {n: args["v_" + n] for n in WEIGHTS}
    for src in (wt, mom, var):
        src['ssd_w_in'] = jnp.swapaxes(src['ssd_w_in'], 1, 2)

    t, d = x.shape[-2], x.shape[-1]
    cur = x.reshape(t, d)
    target = loss_target.reshape(t, d)
    depth = mix_norm_w.shape[0]
    n_ssd, n_sc = ssd_w_in.shape[0], sc_w_in.shape[0]
    heads = ssd_dt_bias.shape[1]
    di = ssd_norm_w.shape[1]
    conv_dim = ssd_conv_b.shape[1]
    bc = (conv_dim - di) // 2
    in_dim = N_DEV * ssd_w_in.shape[2]
    in_pad = di + conv_dim + LANE
    ff = ffn_w_down.shape[1] * N_DEV
    me = 4 * lax.axis_index("x") + 2 * lax.axis_index("y") + lax.axis_index("c")
    me_s = me.astype(jnp.int32).reshape(1)
    core_s = lax.axis_index("c").astype(jnp.int32).reshape(1)
    chip_s = (2 * lax.axis_index("x") + lax.axis_index("y")).astype(jnp.int32).reshape(1)

    names_of = {"ssd": ('ssd_w_in', 'ssd_w_out'), "sc": ('sc_w_in', 'sc_w_out'), "ffn": ('ffn_w_up', 'ffn_w_down')}
    order = []
    for i in range(depth):
        order += [("ssd" if i % 2 == 0 else "sc", i // 2), ("ffn", i)]
    def make_gather(s, after):
        kind, idx = order[s]
        bufs = []
        for n in names_of[kind]:
            after = _cast_layer(wt[n], idx, me_s, after, name=f"cast_{n}{idx}")
            bufs.append(after)
        return _Gather(bufs, f"{kind}{idx}"), after

    conv_full = [_gather_cols(g) for g in _all_gather([wt[n] for n in SHARDED_SMALL], name="ag_conv")]
    first, last_cast = make_gather(0, conv_full[0])
    gathers = [first]
    tok_a = last_cast = first.start_ici(last_cast)
    for s in range(1, len(order)):
        g, last_cast = make_gather(s, last_cast)
        gathers.append(g)
    tok_b = gathers[0].hand_on(last_cast)
    tok_c = gathers[1].start_ici(tok_b)
    weights = [None] * len(order)
    weights[0] = gathers[0].finish(tok_c)
    ssd_cw, sc_cw, ffn_cw = conv_full
    ffn_cw = ffn_cw.reshape(depth, ffn_cw.shape[1], 2, ff)
    ffn_cb = ffn_conv_b.reshape(depth, 2, ff)
    dexp = jnp.repeat(ssd_d.astype(F32), HEAD_DIM, axis=1)

    n_sub = len(order)
    full = {n: [None] * wt[n].shape[0] for n in BIG}

    def prefetch(s, after):
        return gathers[s + 2].start_ici(after) if s + 2 < n_sub else None

    def hand_on(s, after):
        return gathers[s + 1].hand_on(after) if s + 1 < n_sub else None

    def arrive(s, after):
        if s + 1 < n_sub:
            weights[s + 1] = gathers[s + 1].finish(after)
            use(s + 1)

    def use(s):
        kind, idx = order[s]
        g_in, g_out = weights[s]
        if kind == "sc":
            g_in = jnp.swapaxes(g_in, 0, 1).reshape(d, -1)
        if kind == "ssd":
            g_in = jnp.pad(g_in.reshape(in_dim, d).T, ((0, 0), (0, in_pad - in_dim)))
        n_in, n_out = names_of[kind]
        full[n_in][idx], full[n_out][idx] = g_in, g_out.reshape(-1, d)

    use(0)
    saved = []
    for i in range(depth):
        j = i // 2
        s = 2 * i
        rec = {"x_mix": cur}
        tok = prefetch(s, cur)
        h = _rmsnorm_fwd(cur, _with_tokens(mix_norm_w[i], tok, tok_c if i == 0 else None), name=f"norm_mix{i}")
        rec["h_mix"] = h
        if i % 2 == 0:
            zx = _mm_nn(h, full['ssd_w_in'][j], out_dtype=BF16, name=f"ssd_in{j}")
            cb = _with_tokens(ssd_conv_b[j].reshape(1, conv_dim), hand_on(s, zx))
            xbc = _ssd_conv_fwd(zx, ssd_cw[j], cb, di, name=f"ssd_conv{j}")
            ssd_vecs = (_lane_pad(ssd_dt_bias[j]), _lane_pad(ssd_a_log[j]), dexp[j].reshape(1, di),
                        ssd_norm_w[j].reshape(1, di))
            yn, y, states = _ssd_fwd(xbc, zx, *ssd_vecs, name=f"ssd_core{j}")
            arrive(s, yn)
            cur = _mm_nn(yn, full['ssd_w_out'][j], res=cur, out_dtype=F32, name=f"ssd_out{j}")
            rec.update(zx=zx, xbc=xbc, yn=yn, y=y, states=states, vecs=ssd_vecs)
        else:
            p3 = _mm_nn(h, full['sc_w_in'][j], out_dtype=BF16, out_parts=3, name=f"sc_in{j}")
            act = _sc_act_fwd(p3, _with_tokens(sc_cw[j], hand_on(s, p3)), name=f"sc_act{j}")
            arrive(s, act)
            cur = _mm_nn(act, full['sc_w_out'][j], res=cur, out_dtype=F32, name=f"sc_out{j}")
            rec.update(p3=p3, act=act)
        s += 1
        rec["x_ffn"] = cur
        h = _rmsnorm_fwd(cur, _with_tokens(ffn_norm_w[i], prefetch(s, cur)), name=f"norm_ffn{i}")
        u3 = _lin_in_fwd(h, full['ffn_w_up'][i], 2, name=f"ffn_up{i}")
        act, pre3 = _ffn_act_fwd(u3, ffn_cw[i], _with_tokens(ffn_cb[i], hand_on(s, u3)), name=f"ffn_act{i}")
        arrive(s, act)
        cur = _mm_nn(act, full['ffn_w_down'][i], res=cur, out_dtype=F32, name=f"ffn_down{i}")
        rec.update(h_ffn=h, u3=u3, pre3=pre3, ffn_act=act)
        saved.append(rec)

    dx, dxb, dw_final, loss8 = _loss_head(cur, final_norm_w, target, name="loss_head")

    small = {n: [None] * wt[n].shape[0] for n in WEIGHTS if n not in BIG and n != 'final_norm_w'}
    scatters = [None] * n_sub
    pending = None

    def chip_step(after):
        return pending.start_chip(after) if pending is not None else None

    for i in reversed(range(depth)):
        j = i // 2
        rec = saved[i]
        nb_up = ffn_w_up.shape[2]
        da = _mm_nt(dxb, full['ffn_w_down'][i], out_dtype=BF16, name=f"ffn_down_dx{i}")
        g_down = _mm_tn(rec["ffn_act"], dxb, out_dtype=BF16, name=f"ffn_down_dw{i}")
        du3, dcw, dcb = _ffn_act_bwd(rec["u3"], rec["pre3"], da, _with_tokens(ffn_cw[i], chip_step(da)),
                                     name=f"ffn_act_bwd{i}")
        g_up = _lin_in_dw(rec["h_ffn"], du3, nb_up, name=f"ffn_up_dw{i}")
        dh = _lin_in_dx(du3, full['ffn_w_up'][i], name=f"ffn_up_dx{i}")
        pending = scatters[2 * i + 1] = _Scatter([g_up, g_down.reshape(N_DEV, ff // N_DEV, d)], core_s, f"ffn{i}")
        tok = pending.start_pair(dh)
        dx, dxb, dwn = _rmsnorm_bwd(dh, rec["x_ffn"], _with_tokens(ffn_norm_w[i], tok), dx, name=f"norm_ffn_bwd{i}")
        small['ffn_conv_w'][i] = dcw.reshape(dcw.shape[0], 2 * ff)
        small['ffn_conv_b'][i] = dcb.reshape(2 * ff)
        small['ffn_norm_w'][i] = dwn.sum(axis=0)

        if i % 2 == 0:
            zx, xbc = rec["zx"], rec["xbc"]
            cw, cb = ssd_cw[j], ssd_conv_b[j].reshape(1, conv_dim)
            dyn = _mm_nt(dxb, full['ssd_w_out'][j], out_dtype=BF16, name=f"ssd_out_dx{j}")
            g_out = _mm_tn(rec["yn"], dxb, out_dtype=BF16, name=f"ssd_out_dw{j}")
            bias_t = _with_tokens(rec["vecs"][0], chip_step(dyn))
            dzx, dxs, db, dc, ddt_g, vec_acc, dnw, ddexp = _ssd_bwd(
                dyn, rec["y"], xbc, zx, rec["states"], bias_t, *rec["vecs"][1:], name=f"ssd_core_bwd{j}")
            dzx, dcw_x, dcb_x = _ssd_conv_bwd(zx, dxs, cw, cb, dzx, di, 0, name=f"ssd_conv_bwd_x{j}")
            dzx, dcw_b, dcb_b = _ssd_conv_bwd(zx, db, cw, cb, dzx, di, di, name=f"ssd_conv_bwd_b{j}")
            dzx, dcw_c, dcb_c = _ssd_conv_bwd(zx, dc, cw, cb, dzx, di, di + bc, name=f"ssd_conv_bwd_c{j}")
            dzx = _ssd_put_ddt(ddt_g, dzx, (di + conv_dim) // LANE, name=f"ssd_put_ddt{j}")
            g_in = _mm_tn(rec["h_mix"], dzx, out_dtype=BF16, name=f"ssd_in_dw{j}")
            g_in = g_in[:, :in_dim].T.reshape(N_DEV, in_dim // N_DEV, d)
            dh = _mm_nt(dzx, full['ssd_w_in'][j], out_dtype=F32, name=f"ssd_in_dx{j}")
            small['ssd_conv_w'][j] = jnp.concatenate([dcw_x, dcw_b, dcw_c], axis=1)
            small['ssd_conv_b'][j] = jnp.concatenate([dcb_x, dcb_b, dcb_c], axis=1).reshape(conv_dim)
            small['ssd_a_log'][j] = vec_acc[:, 0, :heads].sum(axis=0)
            small['ssd_dt_bias'][j] = vec_acc[:, 1, :heads].sum(axis=0)
            small['ssd_d'][j] = ddexp[:, 0, :].reshape(heads, HEAD_DIM).sum(axis=1)
            small['ssd_norm_w'][j] = dnw[:, 0, :].reshape(di)
            g_out = g_out.reshape(N_DEV, di // N_DEV, d)
        else:
            nb_in = sc_w_in.shape[2]
            da = _mm_nt(dxb, full['sc_w_out'][j], out_dtype=BF16, name=f"sc_out_dx{j}")
            g_out = _mm_tn(rec["act"], dxb, out_dtype=BF16, name=f"sc_out_dw{j}")
            dp3, dcw = _sc_act_bwd(rec["p3"], da, _with_tokens(sc_cw[j], chip_step(da)), name=f"sc_act_bwd{j}")
            g_in = _mm_tn(rec["h_mix"], dp3, out_dtype=BF16, name=f"sc_in_dw{j}")
            g_in = jnp.swapaxes(g_in.reshape(d, N_DEV, nb_in), 0, 1)
            dh = _mm_nt(dp3, full['sc_w_in'][j], out_dtype=F32, name=f"sc_in_dx{j}")
            small['sc_conv_w'][j] = dcw
            g_out = g_out.reshape(N_DEV, g_out.shape[0] // N_DEV, d)
        pending = scatters[2 * i] = _Scatter([g_in, g_out], core_s, f"{order[2 * i][0]}{j}")
        tok = pending.start_pair(dh)
        dx, dxb, dwn = _rmsnorm_bwd(dh, rec["x_mix"], _with_tokens(mix_norm_w[i], tok), dx, name=f"norm_mix_bwd{i}")
        small['mix_norm_w'][i] = dwn.sum(axis=0)
    tok_last = chip_step(dx)

    small_names = [n for n in WEIGHTS if n not in BIG]
    partial = {n: jnp.stack(small[n]) for n in small}
    partial['final_norm_w'] = dw_final.sum(axis=0)
    full_shapes = [partial[n].shape for n in small_names]
    packed = _pack([loss8.sum().reshape(1)] + [partial[n] for n in small_names])
    total = _unpack(_all_reduce_small(packed, name="ar_small"), [(1,)] + full_shapes)
    loss = total[0].reshape(())
    grads = dict(zip(small_names, total[1:]))
    for n in SHARDED_SMALL:
        nb = wt[n].shape[-1]
        grads[n] = lax.dynamic_slice_in_dim(grads[n], me * nb, nb, axis=grads[n].ndim - 1)

    delta, new_m, new_v = {}, {}, {}
    shapes = [wt[n].shape for n in small_names]
    outs = _adamw_small(*[_pack([src[n] for n in small_names]) for src in (wt, grads, mom, var)], name="adamw_small")
    for dst, packed_out in zip((delta, new_m, new_v), outs):
        dst.update(zip(small_names, _unpack(packed_out, shapes)))
    parts = {n: [None] * wt[n].shape[0] for n in BIG}
    for s in range(1, n_sub):
        kind, idx = order[s]
        parts[names_of[kind][0]][idx], parts[names_of[kind][1]][idx] = scatters[s].finish(tok_last)
    first_in, first_out = names_of[order[0][0]]
    last_out = tok_last
    jobs = [(n, layer) for n in reversed(BIG) for layer in reversed(range(wt[n].shape[0]))]
    jobs.sort(key=lambda job: parts[job[0]][job[1]] is None)
    chain = {n: None for n in BIG}
    for n, layer in jobs:
        if parts[n][layer] is None:
            parts[first_in][0], parts[first_out][0] = scatters[0].finish(last_out)
        chain[n] = _adamw_layer(wt[n], mom[n], var[n], *parts[n][layer], chip_s, layer, chain[n], last_out,
                                name=f"adamw_{n}{layer}")
        last_out = chain[n][1]
    for n in BIG:
        grads[n], delta[n], new_m[n], new_v[n] = chain[n]
    for dst in (grads, delta, new_m, new_v):
        dst['ssd_w_in'] = jnp.swapaxes(dst['ssd_w_in'], 1, 2)

    return (loss, dx.reshape(x.shape), *[grads[n] for n in WEIGHTS], *[delta[n] for n in WEIGHTS],
            *[new_m[n] for n in WEIGHTS], *[new_v[n] for n in WEIGHTS])
```

```python
import functools
import math

import jax
import jax.numpy as jnp
from jax import lax
from jax.experimental import pallas as pl
from jax.experimental.pallas import tpu as pltpu

F32 = jnp.float32
BF16 = jnp.bfloat16
MESH = pl.DeviceIdType.MESH

N_DEV = 8
N_CHIP = 4
EPS = 1e-5
HEAD_DIM = 64
STATE = 128
CHUNK = 128
PAIR = 2 * HEAD_DIM
GROUP_W = 8 * HEAD_DIM
HALO = 16
LANE = 128
VMEM_LIMIT = 56 * 1024 * 1024

ADAM_LR = 0.001
ADAM_B1 = 0.9
ADAM_B2 = 0.999
ADAM_EPS = 1e-08
ADAM_WD = 0.01
ADAM_STEP = 10


def _pick(n, candidates):
    for c in candidates:
        if c <= n and n % c == 0:
            return c
    return n


OPERAND_VMEM = 36 * 1024 * 1024


def _pick_k(kd, other, candidates):
    for c in (kd,) + tuple(candidates):
        if c <= kd and kd % c == 0 and 2 * 2 * other * c <= OPERAND_VMEM:
            return c
    return kd


def _params(*sem):
    return pltpu.CompilerParams(dimension_semantics=sem, vmem_limit_bytes=VMEM_LIMIT)


def _sigmoid(x):
    return 0.5 * jnp.tanh(0.5 * x) + 0.5


_DIMS = {
    "nn": (((1,), (0,)), ((), ())),
    "nt": (((1,), (1,)), ((), ())),
    "tn": (((0,), (0,)), ((), ())),
}


def _matmul(mode, a, b, *, grid, a_spec, b_spec, o_spec, out_shape, acc_shape, name, res=None, res_spec=None,
            part_fn=None):
    nk = grid[2]
    dims = _DIMS[mode]
    if part_fn is None:
        part_fn = lambda a_ref, b_ref: lax.dot_general(a_ref[...], b_ref[...], dims, preferred_element_type=F32)

    def body(*refs):
        if res is None:
            a_ref, b_ref, o_ref = refs[:3]
            r_ref, scratch = None, refs[3:]
        else:
            a_ref, b_ref, r_ref, o_ref = refs[:4]
            scratch = refs[4:]
        part = part_fn(a_ref, b_ref)

        def finish(acc):
            if r_ref is not None:
                acc = acc + r_ref[...]
            o_ref[...] = acc.astype(o_ref.dtype)

        if nk == 1:
            finish(part)
        else:
            acc_ref = scratch[0]
            k = pl.program_id(2)

            @pl.when(k == 0)
            def _():
                acc_ref[...] = part

            @pl.when(k > 0)
            def _():
                acc_ref[...] += part

            @pl.when(k == nk - 1)
            def _():
                finish(acc_ref[...])

    in_specs = [a_spec, b_spec] + ([res_spec] if res is not None else [])
    args = (a, b) + ((res,) if res is not None else ())
    return pl.pallas_call(
        body, name=name, grid=grid, in_specs=in_specs, out_specs=o_spec, out_shape=out_shape,
        scratch_shapes=[pltpu.VMEM(acc_shape, F32)] if nk > 1 else [],
        compiler_params=_params("parallel", "parallel", "arbitrary"),
    )(*args)


def _mm_nn(a, b, *, out_dtype, res=None, out_parts=1, name):
    m, kd = a.shape
    n = b.shape[1]
    c = n // out_parts
    tm = _pick(m, (512, 256, 128))
    tn = _pick(c, (1152, 1024, 512, 384, 256, 128))
    tk = _pick_k(kd, tm + tn, (2816, 2048, 1024, 512, 256, 128))
    grid = (n // tn, m // tm, kd // tk)
    if out_parts == 1:
        o_spec = pl.BlockSpec((tm, tn), lambda j, i, k: (i, j))
        out_shape = jax.ShapeDtypeStruct((m, n), out_dtype)
    else:
        o_spec = _stacked_spec(tm, tn, c, lambda j, i, k: (i, j))
        out_shape = jax.ShapeDtypeStruct((out_parts, m, c), out_dtype)
    return _matmul(
        "nn", a, b, res=res, grid=grid, name=name,
        a_spec=pl.BlockSpec((tm, tk), lambda j, i, k: (i, k)),
        b_spec=pl.BlockSpec((tk, tn), lambda j, i, k: (k, j)),
        res_spec=pl.BlockSpec((tm, tn), lambda j, i, k: (i, j)),
        o_spec=o_spec, out_shape=out_shape, acc_shape=(tm, tn))


def _stacked_spec(rows, width, c, row_col):
    per = c // width

    def index(j, i, k):
        r, q = row_col(j, i, k)
        return q // per, r, q % per

    return pl.BlockSpec((None, rows, width), index)


def _mm_nt(a, b, *, out_dtype, name):
    stacked = a.ndim == 3
    m = a.shape[-2]
    n, kd = b.shape
    c = a.shape[-1]
    tm = _pick(m, (512, 256, 128))
    tn = _pick(n, (1408, 1024, 512, 256, 128))
    tk = _pick_k(c, tm + tn, (3456, 2816, 2048, 1024, 512, 384, 256, 128))
    grid = (n // tn, m // tm, kd // tk)
    a_spec = (_stacked_spec(tm, tk, c, lambda j, i, k: (i, k)) if stacked
              else pl.BlockSpec((tm, tk), lambda j, i, k: (i, k)))
    return _matmul(
        "nt", a, b, grid=grid, name=name, a_spec=a_spec,
        b_spec=pl.BlockSpec((tn, tk), lambda j, i, k: (j, k)),
        o_spec=pl.BlockSpec((tm, tn), lambda j, i, k: (i, j)),
        out_shape=jax.ShapeDtypeStruct((m, n), out_dtype), acc_shape=(tm, tn))


def _mm_tn(a, b, *, out_dtype, name):
    stacked = b.ndim == 3
    kd, m = a.shape
    c = b.shape[-1]
    n = c * (b.shape[0] if stacked else 1)
    tm = _pick(m, (512, 256, 128))
    tn = _pick(c, (1152, 1024, 512, 384, 256, 128))
    tk = _pick_k(kd, tm + tn, (2048, 1024, 512, 256, 128))
    grid = (n // tn, m // tm, kd // tk)
    b_spec = (_stacked_spec(tk, tn, c, lambda j, i, k: (k, j)) if stacked
              else pl.BlockSpec((tk, tn), lambda j, i, k: (k, j)))
    return _matmul(
        "tn", a, b, grid=grid, name=name,
        a_spec=pl.BlockSpec((tk, tm), lambda j, i, k: (k, i)), b_spec=b_spec,
        o_spec=pl.BlockSpec((tm, tn), lambda j, i, k: (i, j)),
        out_shape=jax.ShapeDtypeStruct((m, n), out_dtype), acc_shape=(tm, tn))


def _in_tile(nb, c):
    return math.gcd(nb, c)


def _lin_in_fwd(h, wg, parts, *, name):
    t, d = h.shape
    nb = wg.shape[2]
    c = N_DEV * nb // parts
    w = _in_tile(nb, c)
    nbw, cw = nb // w, c // w
    tm = _pick(t, (1024,) if w < 512 else (512, 256, 128))
    grid = (N_DEV * nbw, t // tm, 1)
    return _matmul(
        "nn", h, wg, grid=grid, name=name,
        a_spec=pl.BlockSpec((tm, d), lambda j, i, k: (i, 0)),
        b_spec=pl.BlockSpec((None, d, w), lambda j, i, k: (j // nbw, 0, j % nbw)),
        o_spec=pl.BlockSpec((None, tm, w), lambda j, i, k: (j // cw, i, j % cw)),
        out_shape=jax.ShapeDtypeStruct((parts, t, c), BF16), acc_shape=(tm, w))


def _lin_in_dx(dact, wg, *, name):
    parts, t, c = dact.shape
    d, nb = wg.shape[1], wg.shape[2]
    tm = _pick(t, (512, 256, 128))
    tn = _pick(d, (1024, 512, 256, 128))
    group = _pick_k(c, tm + tn, (2 * nb, nb)) // nb
    per = c // (group * nb)
    grid = (d // tn, t // tm, N_DEV // group)

    def blocks(a_ref, b_ref):
        acc = None
        for q in range(group):
            part = lax.dot_general(a_ref[:, q * nb:(q + 1) * nb], b_ref[q], _DIMS["nt"], preferred_element_type=F32)
            acc = part if acc is None else acc + part
        return acc

    return _matmul(
        "nt", dact, wg, grid=grid, name=name, part_fn=blocks,
        a_spec=pl.BlockSpec((None, tm, group * nb), lambda j, i, k: (k // per, i, k % per)),
        b_spec=pl.BlockSpec((group, tn, nb), lambda j, i, k: (k, j, 0)),
        o_spec=pl.BlockSpec((tm, tn), lambda j, i, k: (i, j)),
        out_shape=jax.ShapeDtypeStruct((t, d), F32), acc_shape=(tm, tn))


def _lin_in_dw(h, dact, nb, *, name):
    t, d = h.shape
    parts, _, c = dact.shape
    w = _in_tile(nb, c)
    nbw, cw = nb // w, c // w
    tm = _pick(d, (512, 256, 128))
    tk = _pick_k(t, tm + w, (2048, 1024, 512, 256, 128))
    grid = (N_DEV * nbw, d // tm, t // tk)
    return _matmul(
        "tn", h, dact, grid=grid, name=name,
        a_spec=pl.BlockSpec((tk, tm), lambda j, i, k: (k, i)),
        b_spec=pl.BlockSpec((None, tk, w), lambda j, i, k: (j // cw, k, j % cw)),
        o_spec=pl.BlockSpec((None, tm, w), lambda j, i, k: (j // nbw, i, j % nbw)),
        out_shape=jax.ShapeDtypeStruct((N_DEV, d, nb), BF16), acc_shape=(tm, w))


def _fold8(v):
    rows, c = v.shape
    return v.reshape(rows // 8, 8, c).sum(axis=0)


def _accumulate(ref, val, first):
    @pl.when(first)
    def _():
        ref[...] = val

    @pl.when(jnp.logical_not(first))
    def _():
        ref[...] += val


def _tile2(r, c, rows=(256, 128, 64, 32, 16)):
    tr = _pick(r, rows)
    if tr < r or r <= rows[0]:
        return tr, c
    return r, _pick(c, (256, 128))


def _cast_layer(w_stack, layer, me, after, *, name):
    _, r, c = w_stack.shape
    tr, tc = _tile2(r, c)

    def body(me_ref, w_ref, after_ref, o_ref):
        del me_ref, after_ref
        o_ref[...] = w_ref[...].astype(BF16)

    return pl.pallas_call(
        body, name=name,
        grid_spec=pltpu.PrefetchScalarGridSpec(
            num_scalar_prefetch=1, grid=(r // tr, c // tc),
            in_specs=[pl.BlockSpec((None, tr, tc), lambda i, j, me_ref: (layer, i, j)), HBM_ANY],
            out_specs=pl.BlockSpec((None, tr, tc), lambda i, j, me_ref: (me_ref[0], i, j))),
        out_shape=jax.ShapeDtypeStruct((N_DEV, r, c), BF16),
        compiler_params=_params("parallel", "parallel"),
    )(me, w_stack, after)


def _rmsnorm_fwd(x, w, *, name):
    t, d = x.shape
    tt = _pick(t, (256, 128))

    def body(x_ref, w_ref, o_ref):
        xv = x_ref[...]
        r = lax.rsqrt(jnp.mean(xv * xv, axis=1, keepdims=True) + EPS)
        o_ref[...] = (xv * r * w_ref[...]).astype(BF16)

    return pl.pallas_call(
        body, name=name, grid=(t // tt,),
        in_specs=[pl.BlockSpec((tt, d), lambda i: (i, 0)), pl.BlockSpec((1, d), lambda i: (0, 0))],
        out_specs=pl.BlockSpec((tt, d), lambda i: (i, 0)),
        out_shape=jax.ShapeDtypeStruct((t, d), BF16),
        compiler_params=_params("parallel"),
    )(x, w.reshape(1, d))


def _rmsnorm_bwd(dh, x, w, dres, *, name):
    t, d = x.shape
    tt = _pick(t, (256, 128))

    def body(dh_ref, x_ref, w_ref, dres_ref, dx_ref, dxb_ref, dw_ref):
        xv = x_ref[...]
        r = lax.rsqrt(jnp.mean(xv * xv, axis=1, keepdims=True) + EPS)
        xhat = xv * r
        dhv = dh_ref[...].astype(F32)
        dxhat = dhv * w_ref[...]
        dx = dres_ref[...] + r * (dxhat - xhat * jnp.mean(dxhat * xhat, axis=1, keepdims=True))
        dx_ref[...] = dx
        dxb_ref[...] = dx.astype(BF16)
        _accumulate(dw_ref, _fold8(dhv * xhat), pl.program_id(0) == 0)

    row = pl.BlockSpec((tt, d), lambda i: (i, 0))
    return pl.pallas_call(
        body, name=name, grid=(t // tt,),
        in_specs=[row, row, pl.BlockSpec((1, d), lambda i: (0, 0)), row],
        out_specs=[row, row, pl.BlockSpec((8, d), lambda i: (0, 0))],
        out_shape=[jax.ShapeDtypeStruct((t, d), F32), jax.ShapeDtypeStruct((t, d), BF16),
                   jax.ShapeDtypeStruct((8, d), F32)],
        compiler_params=_params("arbitrary"),
    )(dh, x, w.reshape(1, d), dres)


def _loss_head(x, w, target, *, name):
    t, d = x.shape
    tt = _pick(t, (256, 128))

    def body(x_ref, w_ref, tg_ref, dx_ref, dxb_ref, dw_ref, ls_ref):
        xv = x_ref[...]
        wv = w_ref[...]
        r = lax.rsqrt(jnp.mean(xv * xv, axis=1, keepdims=True) + EPS)
        xhat = xv * r
        err = xhat * wv - tg_ref[...]
        dy = err * (1.0 / d)
        dxhat = dy * wv
        dx = r * (dxhat - xhat * jnp.mean(dxhat * xhat, axis=1, keepdims=True))
        dx_ref[...] = dx
        dxb_ref[...] = dx.astype(BF16)
        first = pl.program_id(0) == 0
        _accumulate(dw_ref, _fold8(dy * xhat), first)
        _accumulate(ls_ref, _fold8(err * err) * (0.5 / d), first)

    row = pl.BlockSpec((tt, d), lambda i: (i, 0))
    acc = pl.BlockSpec((8, d), lambda i: (0, 0))
    return pl.pallas_call(
        body, name=name, grid=(t // tt,),
        in_specs=[row, pl.BlockSpec((1, d), lambda i: (0, 0)), row],
        out_specs=[row, row, acc, acc],
        out_shape=[jax.ShapeDtypeStruct((t, d), F32), jax.ShapeDtypeStruct((t, d), BF16),
                   jax.ShapeDtypeStruct((8, d), F32), jax.ShapeDtypeStruct((8, d), F32)],
        compiler_params=_params("arbitrary"),
    )(x, w.reshape(1, d), target)


def _conv_causal(e, tap, width):
    acc = None
    for k in range(width):
        s = width - 1 - k
        term = (e if s == 0 else pltpu.roll(e, s, 0)) * tap(k)
        acc = term if acc is None else acc + term
    return acc


def _conv_anticausal(e, tap, width):
    rows = e.shape[0]
    acc = None
    for k in range(width):
        s = width - 1 - k
        term = (e if s == 0 else pltpu.roll(e, rows - s, 0)) * tap(k)
        acc = term if acc is None else acc + term
    return acc


def _extend(prev, cur, nxt, first, last):
    parts = []
    if prev is not None:
        parts.append(jnp.where(first, 0.0, prev.astype(F32)))
    parts.append(cur.astype(F32))
    if nxt is not None:
        parts.append(jnp.where(last, 0.0, nxt.astype(F32)))
    return jnp.concatenate(parts, axis=0)


def _prev_idx(i, tt):
    return jnp.maximum(i * (tt // HALO) - 1, 0)


def _next_idx(i, tt, t):
    return jnp.minimum((i + 1) * (tt // HALO), t // HALO - 1)


def _ffn_act_fwd(u3, cw, cb, *, name):
    _, t, f = u3.shape
    tt = _pick(t, (512, 256, 128))
    tc = _pick(f, (512, 256, 128))
    width = cw.shape[0]

    def body(u_ref, up_ref, w_ref, b_ref, o_ref, pre_ref):
        first = pl.program_id(1) == 0
        pre = []
        for p in range(2):
            e = _extend(up_ref[p], u_ref[p], None, first, None)
            pre.append(_conv_causal(e, lambda k: w_ref[k, p:p + 1, :], width)[HALO:] + b_ref[p:p + 1, :])
            pre_ref[p] = pre[p].astype(BF16)
        g, v = pre
        o_ref[...] = (g * _sigmoid(g) * v).astype(BF16)

    return pl.pallas_call(
        body, name=name, grid=(f // tc, t // tt),
        in_specs=[pl.BlockSpec((2, tt, tc), lambda j, i: (0, i, j)),
                  pl.BlockSpec((2, HALO, tc), lambda j, i: (0, _prev_idx(i, tt), j)),
                  pl.BlockSpec((width, 2, tc), lambda j, i: (0, 0, j)),
                  pl.BlockSpec((2, tc), lambda j, i: (0, j))],
        out_specs=[pl.BlockSpec((tt, tc), lambda j, i: (i, j)), pl.BlockSpec((2, tt, tc), lambda j, i: (0, i, j))],
        out_shape=[jax.ShapeDtypeStruct((t, f), BF16), jax.ShapeDtypeStruct((2, t, f), BF16)],
        compiler_params=_params("parallel", "parallel"),
    )(u3, u3, cw, cb)


def _ffn_act_bwd(u3, pre3, da, cw, *, name):
    _, t, f = u3.shape
    tt = _pick(t, (512, 256, 128))
    tc = _pick(f, (512, 256, 128))
    width = cw.shape[0]
    nt = t // tt
    rows = tt + HALO

    def body(u_ref, pre_ref, pren_ref, da_ref, dan_ref, w_ref, du_ref, dcw_ref, dcb_ref):
        i = pl.program_id(1)
        first, last = i == 0, i == nt - 1
        g, v = (_extend(None, pre_ref[p], pren_ref[p], None, False) for p in range(2))
        dae = _extend(None, da_ref[...], dan_ref[...], None, last)
        sg = _sigmoid(g)
        dpre = (dae * v * (sg * (1.0 + g * (1.0 - sg))), dae * (g * sg))

        @pl.when(first)
        def _():
            dcw_ref[...] = jnp.zeros_like(dcw_ref)
            dcb_ref[...] = jnp.zeros_like(dcb_ref)

        for p in range(2):
            u = u_ref[p].astype(F32)
            du = None
            for k in range(width):
                s = width - 1 - k
                d = (dpre[p] if s == 0 else pltpu.roll(dpre[p], rows - s, 0))[:tt]
                term = d * w_ref[k, p:p + 1, :]
                du = term if du is None else du + term
                dcw_ref[k, p:p + 1, :] += jnp.sum(d * u, axis=0, keepdims=True)
                if s == 0:
                    dcb_ref[p:p + 1, :] += jnp.sum(d, axis=0, keepdims=True)
            du_ref[p] = du.astype(BF16)

    cur3 = pl.BlockSpec((2, tt, tc), lambda j, i: (0, i, j))
    return pl.pallas_call(
        body, name=name, grid=(f // tc, nt),
        in_specs=[cur3, cur3,
                  pl.BlockSpec((2, HALO, tc), lambda j, i: (0, _next_idx(i, tt, t), j)),
                  pl.BlockSpec((tt, tc), lambda j, i: (i, j)),
                  pl.BlockSpec((HALO, tc), lambda j, i: (_next_idx(i, tt, t), j)),
                  pl.BlockSpec((width, 2, tc), lambda j, i: (0, 0, j))],
        out_specs=[cur3,
                   pl.BlockSpec((width, 2, tc), lambda j, i: (0, 0, j)),
                   pl.BlockSpec((2, tc), lambda j, i: (0, j))],
        out_shape=[jax.ShapeDtypeStruct((2, t, f), BF16), jax.ShapeDtypeStruct((width, 2, f), F32),
                   jax.ShapeDtypeStruct((2, f), F32)],
        compiler_params=_params("parallel", "arbitrary"),
    )(u3, pre3, pre3, da, da, cw)


def _sc_act_fwd(p3, cw, *, name):
    _, t, c = p3.shape
    tt = _pick(t, (512, 256, 128))
    tc = _pick(c, (512, 256, 128))
    width = cw.shape[0]

    def body(p_ref, pp_ref, w_ref, o_ref):
        first = pl.program_id(1) == 0
        q = _extend(pp_ref[1], p_ref[1], None, first, None) * _extend(pp_ref[2], p_ref[2], None, first, None)
        cq = _conv_causal(q, lambda k: w_ref[k:k + 1, :], width)[HALO:]
        o_ref[...] = (p_ref[0].astype(F32) * cq).astype(BF16)

    return pl.pallas_call(
        body, name=name, grid=(c // tc, t // tt),
        in_specs=[pl.BlockSpec((3, tt, tc), lambda j, i: (0, i, j)),
                  pl.BlockSpec((3, HALO, tc), lambda j, i: (0, _prev_idx(i, tt), j)),
                  pl.BlockSpec((width, tc), lambda j, i: (0, j))],
        out_specs=pl.BlockSpec((tt, tc), lambda j, i: (i, j)),
        out_shape=jax.ShapeDtypeStruct((t, c), BF16),
        compiler_params=_params("parallel", "parallel"),
    )(p3, p3, cw)


def _sc_act_bwd(p3, da, cw, *, name):
    _, t, c = p3.shape
    tt = _pick(t, (512, 256, 128))
    tc = _pick(c, (512, 256, 128))
    width = cw.shape[0]
    nt = t // tt
    ctr = slice(HALO, HALO + tt)

    def body(p_ref, pp_ref, pn_ref, da_ref, dan_ref, w_ref, dp_ref, dcw_ref):
        i = pl.program_id(1)
        first, last = i == 0, i == nt - 1
        tap = lambda k: w_ref[k:k + 1, :]
        bg, cg, hh = (_extend(pp_ref[p], p_ref[p], pn_ref[p], first, last) for p in range(3))
        q = cg * hh
        cq = _conv_causal(q, tap, width)
        dae = _extend(jnp.zeros((HALO, tc), F32), da_ref[...], dan_ref[...], False, last)
        dcq = dae * bg
        dq = _conv_anticausal(dcq, tap, width)[ctr]
        dp_ref[0] = (dae * cq)[ctr].astype(BF16)
        dp_ref[1] = (dq * hh[ctr]).astype(BF16)
        dp_ref[2] = (dq * cg[ctr]).astype(BF16)

        @pl.when(first)
        def _():
            dcw_ref[...] = jnp.zeros_like(dcw_ref)

        dc = dcq[ctr]
        for k in range(width):
            s = width - 1 - k
            qs = (q if s == 0 else pltpu.roll(q, s, 0))[ctr]
            dcw_ref[k:k + 1, :] += jnp.sum(dc * qs, axis=0, keepdims=True)

    return pl.pallas_call(
        body, name=name, grid=(c // tc, nt),
        in_specs=[pl.BlockSpec((3, tt, tc), lambda j, i: (0, i, j)),
                  pl.BlockSpec((3, HALO, tc), lambda j, i: (0, _prev_idx(i, tt), j)),
                  pl.BlockSpec((3, HALO, tc), lambda j, i: (0, _next_idx(i, tt, t), j)),
                  pl.BlockSpec((tt, tc), lambda j, i: (i, j)),
                  pl.BlockSpec((HALO, tc), lambda j, i: (_next_idx(i, tt, t), j)),
                  pl.BlockSpec((width, tc), lambda j, i: (0, j))],
        out_specs=[pl.BlockSpec((3, tt, tc), lambda j, i: (0, i, j)),
                   pl.BlockSpec((width, tc), lambda j, i: (0, j))],
        out_shape=[jax.ShapeDtypeStruct((3, t, c), BF16), jax.ShapeDtypeStruct((width, c), F32)],
        compiler_params=_params("parallel", "arbitrary"),
    )(p3, p3, p3, da, da, cw)


def _ssd_conv_fwd(zx, cw, cb, col0, *, name):
    t = zx.shape[0]
    width, c = cw.shape
    tt = _pick(t, (512, 256, 128))
    tc = _pick(math.gcd(c, col0), (512, 256, 128))
    off = col0 // tc

    def body(x_ref, xp_ref, w_ref, b_ref, o_ref):
        first = pl.program_id(1) == 0
        e = _extend(xp_ref[...], x_ref[...], None, first, None)
        pre = _conv_causal(e, lambda k: w_ref[k:k + 1, :], width)[HALO:] + b_ref[...]
        o_ref[...] = (pre * _sigmoid(pre)).astype(BF16)

    return pl.pallas_call(
        body, name=name, grid=(c // tc, t // tt),
        in_specs=[pl.BlockSpec((tt, tc), lambda j, i: (i, off + j)),
                  pl.BlockSpec((HALO, tc), lambda j, i: (_prev_idx(i, tt), off + j)),
                  pl.BlockSpec((width, tc), lambda j, i: (0, j)),
                  pl.BlockSpec((1, tc), lambda j, i: (0, j))],
        out_specs=pl.BlockSpec((tt, tc), lambda j, i: (i, j)),
        out_shape=jax.ShapeDtypeStruct((t, c), BF16),
        compiler_params=_params("parallel", "parallel"),
    )(zx, zx, cw, cb)


def _ssd_conv_bwd(zx, dxc, cw, cb, dzx, col0, woff, *, name):
    t = zx.shape[0]
    width = cw.shape[0]
    c = dxc.shape[1]
    tt = _pick(t, (512, 256, 128))
    tc = _pick(math.gcd(math.gcd(c, col0), woff) if woff else math.gcd(c, col0), (512, 256, 128))
    nt = t // tt
    xoff, wo = (col0 + woff) // tc, woff // tc
    ctr = slice(HALO, HALO + tt)

    def body(x_ref, xp_ref, xn_ref, d_ref, dn_ref, w_ref, b_ref, dzx_in, dzx_ref, dcw_ref, dcb_ref):
        del dzx_in
        i = pl.program_id(1)
        first, last = i == 0, i == nt - 1
        tap = lambda k: w_ref[k:k + 1, :]
        e = _extend(xp_ref[...], x_ref[...], xn_ref[...], first, last)
        pre = _conv_causal(e, tap, width) + b_ref[...]
        sg = _sigmoid(pre)
        de = _extend(jnp.zeros((HALO, tc), F32), d_ref[...], dn_ref[...], False, last)
        dpre = de * (sg * (1.0 + pre * (1.0 - sg)))
        dzx_ref[...] = _conv_anticausal(dpre, tap, width)[ctr].astype(BF16)

        @pl.when(first)
        def _():
            dcw_ref[...] = jnp.zeros_like(dcw_ref)
            dcb_ref[...] = jnp.zeros_like(dcb_ref)

        dc = dpre[ctr]
        dcb_ref[...] += jnp.sum(dc, axis=0, keepdims=True)
        for k in range(width):
            s = width - 1 - k
            xs = (e if s == 0 else pltpu.roll(e, s, 0))[ctr]
            dcw_ref[k:k + 1, :] += jnp.sum(dc * xs, axis=0, keepdims=True)

    return pl.pallas_call(
        body, name=name, grid=(c // tc, nt),
        in_specs=[pl.BlockSpec((tt, tc), lambda j, i: (i, xoff + j)),
                  pl.BlockSpec((HALO, tc), lambda j, i: (_prev_idx(i, tt), xoff + j)),
                  pl.BlockSpec((HALO, tc), lambda j, i: (_next_idx(i, tt, t), xoff + j)),
                  pl.BlockSpec((tt, tc), lambda j, i: (i, j)),
                  pl.BlockSpec((HALO, tc), lambda j, i: (_next_idx(i, tt, t), j)),
                  pl.BlockSpec((width, tc), lambda j, i: (0, wo + j)),
                  pl.BlockSpec((1, tc), lambda j, i: (0, wo + j)),
                  pl.BlockSpec(memory_space=pl.ANY)],
        out_specs=[pl.BlockSpec((tt, tc), lambda j, i: (i, xoff + j)),
                   pl.BlockSpec((width, tc), lambda j, i: (0, j)),
                   pl.BlockSpec((1, tc), lambda j, i: (0, j))],
        out_shape=[jax.ShapeDtypeStruct(dzx.shape, dzx.dtype), jax.ShapeDtypeStruct((width, c), F32),
                   jax.ShapeDtypeStruct((1, c), F32)],
        input_output_aliases={7: 0},
        compiler_params=_params("parallel", "arbitrary"),
    )(zx, zx, zx, dxc, dxc, cw, cb, dzx)


def _ssd_put_ddt(ddt_g, dzx, col, *, name):
    g, t, _ = ddt_g.shape
    tt = _pick(t, (512, 256, 128))

    def body(d_ref, dzx_in, dzx_ref):
        del dzx_in
        dzx_ref[...] = jnp.sum(d_ref[...], axis=0).astype(BF16)

    return pl.pallas_call(
        body, name=name, grid=(t // tt,),
        in_specs=[pl.BlockSpec((g, tt, LANE), lambda i: (0, i, 0)), pl.BlockSpec(memory_space=pl.ANY)],
        out_specs=pl.BlockSpec((tt, LANE), lambda i: (i, col)),
        out_shape=jax.ShapeDtypeStruct(dzx.shape, dzx.dtype),
        input_output_aliases={1: 0},
        compiler_params=_params("parallel"),
    )(ddt_g, dzx)


def _dot(a, b, mode):
    return lax.dot_general(a, b, _DIMS[mode], preferred_element_type=F32)


def _dot_exact(m01, v, mode="nn"):
    hi = v.astype(BF16)
    r1 = v - hi.astype(F32)
    mid = r1.astype(BF16)
    lo = (r1 - mid.astype(F32)).astype(BF16)
    return _dot(m01, hi, mode) + _dot(m01, mid, mode) + _dot(m01, lo, mode)


def _softplus(x):
    return jnp.maximum(x, 0.0) + jnp.log(1.0 + jnp.exp(-jnp.abs(x)))


def _head_vectors(g, dt_raw, bias, alog):
    n = CHUNK
    dt = _softplus(dt_raw + bias)
    a = -jnp.exp(alog)
    tri = (lax.broadcasted_iota(jnp.int32, (n, n), 0) >= lax.broadcasted_iota(jnp.int32, (n, n), 1)).astype(BF16)
    cs = _dot_exact(tri, dt * a)
    return dt, a, cs, cs.T


def _col(v, lane_ids, h):
    return jnp.sum(jnp.where(lane_ids == h, v, 0.0), axis=1, keepdims=True)


def _row(vt, sub_ids, h):
    return jnp.sum(jnp.where(sub_ids == h, vt, 0.0), axis=0, keepdims=True)


def _ssd_specs(di, bc, nc, rev):
    cidx = (lambda c: nc - 1 - c) if rev else (lambda c: c)
    wide = lambda off: pl.BlockSpec((CHUNK, GROUP_W), lambda g, c: (cidx(c), off + g))
    lane = lambda off: pl.BlockSpec((CHUNK, LANE), lambda g, c: (cidx(c), off + g))
    fixed = lambda off: pl.BlockSpec((CHUNK, LANE), lambda g, c: (cidx(c), off))
    vec = pl.BlockSpec((1, LANE), lambda g, c: (0, 0))
    gvec = pl.BlockSpec((1, GROUP_W), lambda g, c: (0, g))
    state = pl.BlockSpec((None, None, 4, PAIR, STATE), lambda g, c: (g, cidx(c), 0, 0, 0))
    return wide, lane, fixed, vec, gvec, state


def _ssd_fwd(xbc, zx, bias, alog, dexp, nw, *, name):
    t = xbc.shape[0]
    di = nw.shape[1]
    bc = (xbc.shape[1] - di) // 2
    ng, nc = di // GROUP_W, t // CHUNK
    wide, lane, fixed, vec, gvec, state = _ssd_specs(di, bc, nc, rev=False)

    def body(xs_ref, b_ref, c_ref, dt_ref, z_ref, bias_ref, alog_ref, dexp_ref, nw_ref,
             yn_ref, y_ref, st_ref, s_scr):
        g, c = pl.program_id(0), pl.program_id(1)

        @pl.when(c == 0)
        def _():
            s_scr[...] = jnp.zeros_like(s_scr)

        n = CHUNK
        dt, a, cs, cst = _head_vectors(g, dt_ref[...].astype(F32), bias_ref[...], alog_ref[...])
        lane_ids = lax.broadcasted_iota(jnp.int32, (n, LANE), 1)
        sub_ids = lax.broadcasted_iota(jnp.int32, (LANE, n), 0)
        causal = lax.broadcasted_iota(jnp.int32, (n, n), 0) >= lax.broadcasted_iota(jnp.int32, (n, n), 1)
        half = lax.broadcasted_iota(jnp.int32, (1, PAIR), 1) < HEAD_DIM
        half_rows = lax.broadcasted_iota(jnp.int32, (PAIR, 1), 0) < HEAD_DIM
        bm, cm = b_ref[...], c_ref[...]
        gm = _dot(cm, bm, "nt")
        x = xs_ref[...].astype(F32)
        ys = []
        for q in range(4):
            h0 = g * 8 + 2 * q
            col = [_col(cs, lane_ids, h0 + e) for e in range(2)]
            row = [_row(cst, sub_ids, h0 + e) for e in range(2)]
            dtc = [_col(dt, lane_ids, h0 + e) for e in range(2)]
            last = [col[e][n - 1:n, :] for e in range(2)]
            xd = x[:, q * PAIR:(q + 1) * PAIR] * jnp.where(half, dtc[0], dtc[1])
            xd_bf = xd.astype(BF16)
            yd = []
            for e in range(2):
                lm = jnp.exp(jnp.where(causal, col[e] - row[e], -1e30))
                yd.append(_dot((gm * lm).astype(BF16), xd_bf, "nn"))
            s = s_scr[q]
            st_ref[q] = s
            ecs = jnp.where(half, jnp.exp(col[0]), jnp.exp(col[1]))
            dte = jnp.where(half, jnp.exp(last[0] - col[0]), jnp.exp(last[1] - col[1]))
            yoff = ecs * _dot(cm, s.astype(BF16), "nt")
            snew = _dot((xd * dte).astype(BF16), bm, "tn")
            s_scr[q] = s * jnp.where(half_rows, jnp.exp(last[0]), jnp.exp(last[1])) + snew
            ys.append(jnp.where(half, yd[0], yd[1]) + yoff)
        y = jnp.concatenate(ys, axis=1) + dexp_ref[...] * x
        y_ref[...] = y.astype(BF16)
        z = z_ref[...].astype(F32)
        yg = y * (z * _sigmoid(z))
        r = lax.rsqrt(jnp.mean(yg * yg, axis=1, keepdims=True) + EPS)
        yn_ref[...] = (yg * r * nw_ref[...]).astype(BF16)

    dtcol = (2 * di + 2 * bc) // LANE
    return pl.pallas_call(
        body, name=name, grid=(ng, nc),
        in_specs=[wide(0), lane(di // LANE), lane((di + bc) // LANE), fixed(dtcol), wide(0),
                  vec, vec, gvec, gvec],
        out_specs=[wide(0), wide(0), state],
        out_shape=[jax.ShapeDtypeStruct((t, di), BF16), jax.ShapeDtypeStruct((t, di), BF16),
                   jax.ShapeDtypeStruct((ng, nc, 4, PAIR, STATE), F32)],
        scratch_shapes=[pltpu.VMEM((4, PAIR, STATE), F32)],
        compiler_params=_params("parallel", "arbitrary"),
    )(xbc, xbc, xbc, zx, zx, bias, alog, dexp, nw)


def _ssd_bwd(dyn, y, xbc, zx, states, bias, alog, dexp, nw, *, name):
    t = xbc.shape[0]
    di = nw.shape[1]
    bc = (xbc.shape[1] - di) // 2
    ng, nc = di // GROUP_W, t // CHUNK
    wide, lane, fixed, vec, gvec, state = _ssd_specs(di, bc, nc, rev=True)
    acc = lambda w: pl.BlockSpec((None, 8, w), lambda g, c: (g, 0, 0))

    def body(dyn_ref, y_ref, z_ref, nw_ref, xs_ref, b_ref, c_ref, dt_ref, bias_ref, alog_ref, dexp_ref, st_ref,
             dz_ref, dxs_ref, db_ref, dc_ref, ddt_ref, small_ref, dnw_ref, ddexp_ref, ds_scr):
        g, c = pl.program_id(0), pl.program_id(1)

        @pl.when(c == 0)
        def _():
            ds_scr[...] = jnp.zeros_like(ds_scr)
            small_ref[...] = jnp.zeros_like(small_ref)
            dnw_ref[...] = jnp.zeros_like(dnw_ref)
            ddexp_ref[...] = jnp.zeros_like(ddexp_ref)

        n = CHUNK
        yv = y_ref[...].astype(F32)
        z = z_ref[...].astype(F32)
        sz = _sigmoid(z)
        silu = z * sz
        yg = yv * silu
        r = lax.rsqrt(jnp.mean(yg * yg, axis=1, keepdims=True) + EPS)
        yhat = yg * r
        dynv = dyn_ref[...].astype(F32)
        dnw_ref[0:1, :] += jnp.sum(dynv * yhat, axis=0, keepdims=True)
        dyhat = dynv * nw_ref[...]
        dyg = r * (dyhat - yhat * jnp.mean(dyhat * yhat, axis=1, keepdims=True))
        dz_ref[...] = (dyg * yv * (sz * (1.0 + z * (1.0 - sz)))).astype(BF16)
        dy = dyg * silu

        dt_in = dt_ref[...].astype(F32) + bias_ref[...]
        dt, a, cs, cst = _head_vectors(g, dt_ref[...].astype(F32), bias_ref[...], alog_ref[...])
        lane_ids = lax.broadcasted_iota(jnp.int32, (n, LANE), 1)
        sub_ids = lax.broadcasted_iota(jnp.int32, (LANE, n), 0)
        ri = lax.broadcasted_iota(jnp.int32, (n, n), 0)
        ci = lax.broadcasted_iota(jnp.int32, (n, n), 1)
        causal, causal_t = ri >= ci, ci >= ri
        is_last = lax.broadcasted_iota(jnp.int32, (n, 1), 0) == n - 1
        half = lax.broadcasted_iota(jnp.int32, (1, PAIR), 1) < HEAD_DIM
        half_rows = lax.broadcasted_iota(jnp.int32, (PAIR, 1), 0) < HEAD_DIM
        bm, cm = b_ref[...], c_ref[...]
        bf = bm.astype(F32)
        gm, gmt = _dot(cm, bm, "nt"), _dot(bm, cm, "nt")
        x = xs_ref[...].astype(F32)
        dexp = dexp_ref[...]

        dg_sum = jnp.zeros((n, n), F32)
        dgt_sum = jnp.zeros((n, n), F32)
        db_off = jnp.zeros((n, STATE), F32)
        dc_off = jnp.zeros((n, STATE), F32)
        dcs_blk = jnp.zeros((n, LANE), F32)
        ddt_blk = jnp.zeros((n, LANE), F32)
        dxs = []
        for q in range(4):
            h0 = g * 8 + 2 * q
            sl = slice(q * PAIR, (q + 1) * PAIR)
            col = [_col(cs, lane_ids, h0 + e) for e in range(2)]
            row = [_row(cst, sub_ids, h0 + e) for e in range(2)]
            dtc = [_col(dt, lane_ids, h0 + e) for e in range(2)]
            last = [col[e][n - 1:n, :] for e in range(2)]
            xp, dyp = x[:, sl], dy[:, sl]
            dtp = jnp.where(half, dtc[0], dtc[1])
            xd = xp * dtp
            xd_bf, dyp_bf = xd.astype(BF16), dyp.astype(BF16)
            ecs = jnp.where(half, jnp.exp(col[0]), jnp.exp(col[1]))
            dte = jnp.where(half, jnp.exp(last[0] - col[0]), jnp.exp(last[1] - col[1]))
            s, ds = st_ref[q], ds_scr[q]
            s_bf, ds_bf = s.astype(BF16), ds.astype(BF16)
            yoff = ecs * _dot(cm, s_bf, "nt")
            edy_bf = (ecs * dyp).astype(BF16)
            dc_off += _dot(edy_bf, s_bf, "nn")
            bds = _dot(bm, ds_bf, "nt")
            sds = s * ds
            zs = []
            for e in range(2):
                msk = half if e == 0 else jnp.logical_not(half)
                msk_rows = half_rows if e == 0 else jnp.logical_not(half_rows)
                lm = jnp.exp(jnp.where(causal, col[e] - row[e], -1e30))
                lmt = jnp.exp(jnp.where(causal_t, row[e] - col[e], -1e30))
                dym_bf = jnp.where(msk, dyp, 0.0).astype(BF16)
                xdm_bf = jnp.where(msk, xd, 0.0).astype(BF16)
                dm = _dot(dym_bf, xd_bf, "nt")
                dmt = _dot(xdm_bf, dyp_bf, "nt")
                m, mt = gm * lm, gmt * lmt
                dcs = jnp.sum(dm * m, axis=1, keepdims=True) - jnp.sum(dmt * mt, axis=1, keepdims=True)
                dg_sum += dm * lm
                dgt_sum += dmt * lmt
                zs.append(_dot(mt.astype(BF16), dyp_bf, "nn"))
                we = _dot(xdm_bf, ds_bf, "nn")
                dte_col = jnp.exp(last[e] - col[e])
                te = dte_col * jnp.sum(we * bf, axis=1, keepdims=True)
                db_off += dte_col * we
                dcs += jnp.sum(jnp.where(msk, dyp * yoff, 0.0), axis=1, keepdims=True) - te
                tail = jnp.exp(last[e]) * jnp.sum(jnp.where(msk_rows, sds, 0.0), keepdims=True) \
                    + jnp.sum(te, keepdims=True)
                dcs += jnp.where(is_last, tail, 0.0)
                dcs_blk += jnp.where(lane_ids == h0 + e, dcs, 0.0)
            dxd = jnp.where(half, zs[0], zs[1]) + dte * bds
            dxs.append(dxd * dtp + dexp[:, sl] * dyp)
            ddexp_ref[0:1, sl] += jnp.sum(dyp * xp, axis=0, keepdims=True)
            rs = dxd * xp
            for e in range(2):
                msk = half if e == 0 else jnp.logical_not(half)
                ddt_blk += jnp.where(lane_ids == h0 + e, jnp.sum(jnp.where(msk, rs, 0.0), axis=1, keepdims=True), 0.0)
            ds_scr[q] = ds * jnp.where(half_rows, jnp.exp(last[0]), jnp.exp(last[1])) + _dot(edy_bf, cm, "tn")

        dxs_ref[...] = jnp.concatenate(dxs, axis=1).astype(BF16)
        dc_ref[...] = (_dot(dg_sum.astype(BF16), bm, "nn") + dc_off).astype(BF16)
        db_ref[...] = (_dot(dgt_sum.astype(BF16), cm, "nn") + db_off).astype(BF16)
        upper = (ri <= ci).astype(BF16)
        dda = _dot_exact(upper, dcs_blk)
        ddt = dda * a + ddt_blk
        small_ref[0:1, :] += jnp.sum(dda * dt, axis=0, keepdims=True) * a
        ddt_raw = ddt * _sigmoid(dt_in)
        small_ref[1:2, :] += jnp.sum(ddt_raw, axis=0, keepdims=True)
        ddt_ref[...] = ddt_raw

    dtcol = (2 * di + 2 * bc) // LANE
    tot = 2 * di + 2 * bc + LANE
    return pl.pallas_call(
        body, name=name, grid=(ng, nc),
        in_specs=[wide(0), wide(0), wide(0), gvec, wide(0), lane(di // LANE), lane((di + bc) // LANE),
                  fixed(dtcol), vec, vec, gvec, state],
        out_specs=[wide(0), wide(0), lane(0), lane(0),
                   pl.BlockSpec((None, CHUNK, LANE), lambda g, c: (g, nc - 1 - c, 0)),
                   acc(LANE), acc(GROUP_W), acc(GROUP_W)],
        out_shape=[jax.ShapeDtypeStruct((t, tot), BF16), jax.ShapeDtypeStruct((t, di), BF16),
                   jax.ShapeDtypeStruct((t, bc), BF16), jax.ShapeDtypeStruct((t, bc), BF16),
                   jax.ShapeDtypeStruct((ng, t, LANE), F32), jax.ShapeDtypeStruct((ng, 8, LANE), F32),
                   jax.ShapeDtypeStruct((ng, 8, GROUP_W), F32), jax.ShapeDtypeStruct((ng, 8, GROUP_W), F32)],
        scratch_shapes=[pltpu.VMEM((4, PAIR, STATE), F32)],
        compiler_params=_params("parallel", "arbitrary"),
    )(dyn, y, zx, nw, xbc, xbc, xbc, zx, bias, alog, dexp, states)


HBM_ANY = pl.BlockSpec(memory_space=pl.ANY)


def _place():
    x, y, c = lax.axis_index("x"), lax.axis_index("y"), lax.axis_index("c")
    chips = [(1 - x, y), (x, 1 - y), (1 - x, 1 - y)]
    return x, y, c, chips


def _all_gather(arrs, *, name, inplace=False):
    n = len(arrs)

    def body(*refs):
        ins, outs = refs[:n], refs[n:2 * n]
        send, recv, loc = refs[2 * n:]
        x, y, c, chips = _place()
        me, sib = (x, y, c), (x, y, 1 - c)

        def blk(a, p):
            return outs[a].at[4 * p[0] + 2 * p[1] + p[2]]

        def cp(a, k, block, to, src=None):
            return pltpu.make_async_remote_copy(
                src_ref=blk(a, block) if src is None else src, dst_ref=blk(a, block),
                send_sem=send.at[a * 7 + k], recv_sem=recv.at[a * 7 + k], device_id=to, device_id_type=MESH)

        src = [None if inplace else ins[a] for a in range(n)]
        mine = [] if inplace else [pltpu.make_async_copy(ins[a], blk(a, me), loc.at[a]) for a in range(n)]
        for m in mine:
            m.start()
        started = []
        for a in range(n):
            started.append(cp(a, 0, me, sib, src=src[a]))
            started += [cp(a, 1 + j, me, (*chip, c), src=src[a]) for j, chip in enumerate(chips)]
        for s in started:
            s.start()
        for j, chip in enumerate(chips):
            for a in range(n):
                cp(a, 1 + j, (*chip, c), me).wait_recv()
                fwd = cp(a, 4 + j, (*chip, c), sib)
                fwd.start()
                started.append(fwd)
        for a in range(n):
            cp(a, 0, sib, me).wait_recv()
            for j, chip in enumerate(chips):
                cp(a, 4 + j, (*chip, 1 - c), me).wait_recv()
        for s in started:
            s.wait_send()
        for m in mine:
            m.wait()

    return pl.pallas_call(
        body, name=name,
        in_specs=[HBM_ANY] * n, out_specs=[HBM_ANY] * n,
        out_shape=[jax.ShapeDtypeStruct(a.shape if inplace else (N_DEV,) + a.shape, a.dtype) for a in arrs],
        input_output_aliases={a: a for a in range(n)} if inplace else {},
        scratch_shapes=[pltpu.SemaphoreType.DMA((7 * n,)), pltpu.SemaphoreType.DMA((7 * n,)),
                        pltpu.SemaphoreType.DMA((n,))],
    )(*arrs)


HBM_SPEC = pl.BlockSpec(memory_space=pltpu.HBM)
SEM_SPEC = pl.BlockSpec(memory_space=pltpu.SEMAPHORE)
SPLIT_EFFECT = pltpu.SideEffectType.DATAFLOW_SIDE_EFFECTING


def _split_start(arrs, plan, n_copies, after, *, name):
    m = len(arrs)

    def body(*refs):
        send, recv, token = refs[m + 1], refs[m + 2], refs[-1]
        for i, (src, dst, to) in enumerate(plan(refs[:m])):
            pltpu.make_async_remote_copy(src_ref=src, dst_ref=dst, send_sem=send.at[i], recv_sem=recv.at[i],
                                         device_id=to, device_id_type=MESH).start()
        token[...] = jnp.zeros_like(token)

    outs = pl.pallas_call(
        body, name=name,
        out_shape=(pltpu.SemaphoreType.DMA((n_copies,)), pltpu.SemaphoreType.DMA((n_copies,)),
                   *[pltpu.HBM(a.shape, a.dtype) for a in arrs], jax.ShapeDtypeStruct((8, LANE), F32)),
        in_specs=[HBM_SPEC] * m + [HBM_ANY],
        out_specs=(SEM_SPEC, SEM_SPEC, *[HBM_SPEC] * m, pl.BlockSpec(memory_space=pltpu.VMEM)),
        input_output_aliases={i: 2 + i for i in range(m)},
        compiler_params=pltpu.CompilerParams(has_side_effects=SPLIT_EFFECT),
    )(*[pltpu.with_memory_space_constraint(a, pltpu.HBM) for a in arrs], after)
    return outs[0], outs[1], list(outs[2:2 + m]), outs[-1]


def _split_wait(arrs, send, recv, after, plan, *, name):
    m = len(arrs)

    def body(*refs):
        send_ref, recv_ref = refs[m], refs[m + 1]
        for i, (src, dst, to) in enumerate(plan(refs[:m])):
            cp = pltpu.make_async_remote_copy(src_ref=src, dst_ref=dst, send_sem=send_ref.at[i],
                                              recv_sem=recv_ref.at[i], device_id=to, device_id_type=MESH)
            cp.wait_send()
            cp.wait_recv()

    outs = pl.pallas_call(
        body, name=name,
        out_shape=[pltpu.HBM(a.shape, a.dtype) for a in arrs],
        in_specs=[HBM_SPEC] * m + [SEM_SPEC, SEM_SPEC, HBM_ANY], out_specs=[HBM_SPEC] * m,
        input_output_aliases={i: i for i in range(m)},
        compiler_params=pltpu.CompilerParams(has_side_effects=SPLIT_EFFECT),
    )(*arrs, send, recv, after)
    return list(outs)


def _dev(p):
    return 4 * p[0] + 2 * p[1] + p[2]


def _plan_gather_ici(bufs):
    x, y, c, chips = _place()
    me = _dev((x, y, c))
    peers = [(x, y, 1 - c)] + [(*chip, c) for chip in chips]
    return [(b.at[me], b.at[me], p) for b in bufs for p in peers]


def _plan_gather_d2d(bufs):
    x, y, c, chips = _place()
    return [(b.at[_dev((*chip, c))], b.at[_dev((*chip, c))], (x, y, 1 - c)) for b in bufs for chip in chips]


def _plan_pair(refs):
    n = len(refs) // 2
    x, y, c, _ = _place()
    return [(refs[a].at[2 * k + 1 - c], refs[n + a].at[k], (x, y, 1 - c)) for a in range(n) for k in range(N_CHIP)]


def _plan_chip(refs):
    n = len(refs) // 2
    x, y, c, chips = _place()
    return [(refs[a].at[2 * chip[0] + chip[1]], refs[n + a].at[j], (*chip, c))
            for a in range(n) for j, chip in enumerate(chips)]


def _land(shape, dtype):
    return lax.empty(shape, dtype)


def _with_tokens(v, *tokens):
    for t in tokens:
        if t is not None:
            v = v + t[0, 0].astype(v.dtype)
    return v


def _add_pair(grad, got, core, *, name):
    k, r, c = got.shape
    tr, tc = _tile2(r, c, rows=(1024, 704, 512, 256, 128, 64, 32, 16))

    def body(core_ref, a_ref, b_ref, o_ref):
        del core_ref
        o_ref[...] = (a_ref[...].astype(F32) + b_ref[...].astype(F32)).astype(BF16)

    spec = pl.BlockSpec((None, tr, tc), lambda q, i, j, core_ref: (q, i, j))
    return pl.pallas_call(
        body, name=name,
        grid_spec=pltpu.PrefetchScalarGridSpec(
            num_scalar_prefetch=1, grid=(k, r // tr, c // tc),
            in_specs=[pl.BlockSpec((None, tr, tc), lambda q, i, j, core_ref: (2 * q + core_ref[0], i, j)), spec],
            out_specs=spec),
        out_shape=jax.ShapeDtypeStruct(got.shape, BF16),
        compiler_params=_params("parallel", "parallel", "parallel"),
    )(core, grad, got)


def _all_reduce_small(v, *, name):
    r = v.shape[0]

    def body(v_ref, o_ref, buf, send, recv):
        x, y, c, _ = _place()
        me = 4 * x + 2 * y + c
        buf[me] = v_ref[...]
        copies = []
        for rel in range(1, N_DEV):
            fx, fy, fc = rel >> 2 & 1, rel >> 1 & 1, rel & 1
            peer = ((1 - x) if fx else x, (1 - y) if fy else y, (1 - c) if fc else c)
            copies.append(pltpu.make_async_remote_copy(
                src_ref=v_ref, dst_ref=buf.at[me], send_sem=send.at[rel - 1], recv_sem=recv.at[rel - 1],
                device_id=peer, device_id_type=MESH))
        for cpy in copies:
            cpy.start()
        for cpy in copies:
            cpy.wait()
        acc = buf[0]
        for d in range(1, N_DEV):
            acc = acc + buf[d]
        o_ref[...] = acc

    return pl.pallas_call(
        body, name=name,
        in_specs=[pl.BlockSpec(memory_space=pltpu.VMEM)], out_specs=pl.BlockSpec(memory_space=pltpu.VMEM),
        out_shape=jax.ShapeDtypeStruct(v.shape, F32),
        scratch_shapes=[pltpu.VMEM((N_DEV, r, LANE), F32), pltpu.SemaphoreType.DMA((N_DEV - 1,)),
                        pltpu.SemaphoreType.DMA((N_DEV - 1,))],
        compiler_params=pltpu.CompilerParams(vmem_limit_bytes=VMEM_LIMIT),
    )(v)


def _adamw_math(w, g, m, v):
    m = ADAM_B1 * m + (1.0 - ADAM_B1) * g
    v = ADAM_B2 * v + (1.0 - ADAM_B2) * (g * g)
    m_hat = m / (1.0 - ADAM_B1 ** ADAM_STEP)
    v_hat = v / (1.0 - ADAM_B2 ** ADAM_STEP)
    delta = -ADAM_LR * (m_hat / (jnp.sqrt(v_hat) + ADAM_EPS) + ADAM_WD * w)
    return delta, m, v


def _adamw_layer(w, m, v, sums, recv, chip, layer, prev, after, *, name):
    nl, r, c = w.shape
    tr, tc = _tile2(r, c)

    def body(chip_ref, w_ref, m_ref, v_ref, s_ref, p_ref, *rest):
        del chip_ref
        g_ref, d_ref, mo_ref, vo_ref = rest[-4:]
        g = s_ref[...].astype(F32)
        for k in range(N_CHIP - 1):
            g = g + p_ref[k].astype(F32)
        delta, mn, vn = _adamw_math(w_ref[...], g, m_ref[...], v_ref[...])
        g_ref[...] = g
        d_ref[...] = delta
        mo_ref[...] = mn
        vo_ref[...] = vn

    lay = pl.BlockSpec((None, tr, tc), lambda i, j, chip_ref: (layer, i, j))
    ins = [w, m, v, sums, recv, after] + (list(prev) if prev is not None else [])
    in_specs = [lay, lay, lay, pl.BlockSpec((None, tr, tc), lambda i, j, chip_ref: (chip_ref[0], i, j)),
                pl.BlockSpec((N_CHIP - 1, tr, tc), lambda i, j, chip_ref: (0, i, j)), HBM_ANY]
    in_specs += [HBM_ANY] * (4 if prev is not None else 0)
    return pl.pallas_call(
        body, name=name,
        grid_spec=pltpu.PrefetchScalarGridSpec(
            num_scalar_prefetch=1, grid=(r // tr, c // tc), in_specs=in_specs, out_specs=[lay] * 4),
        out_shape=[jax.ShapeDtypeStruct(w.shape, F32)] * 4,
        input_output_aliases={7 + q: q for q in range(4)} if prev is not None else {},
        compiler_params=_params("parallel", "parallel"),
    )(chip, *ins)


def _adamw_small(w, g, m, v, *, name):
    def body(w_ref, g_ref, m_ref, v_ref, d_ref, mo_ref, vo_ref):
        d_ref[...], mo_ref[...], vo_ref[...] = _adamw_math(w_ref[...], g_ref[...], m_ref[...], v_ref[...])

    vm = pl.BlockSpec(memory_space=pltpu.VMEM)
    return pl.pallas_call(
        body, name=name, in_specs=[vm] * 4, out_specs=[vm] * 3,
        out_shape=[jax.ShapeDtypeStruct(w.shape, F32)] * 3,
        compiler_params=pltpu.CompilerParams(vmem_limit_bytes=VMEM_LIMIT),
    )(w, g, m, v)


def _pack(arrs):
    flat = jnp.concatenate([a.reshape(-1).astype(F32) for a in arrs])
    pad = (-flat.shape[0]) % (8 * LANE)
    return jnp.pad(flat, (0, pad)).reshape(-1, LANE)


def _unpack(packed, shapes):
    flat = packed.reshape(-1)
    out, off = [], 0
    for s in shapes:
        size = math.prod(s)
        out.append(flat[off:off + size].reshape(s))
        off += size
    return out


WEIGHTS = ['mix_norm_w', 'ffn_norm_w', 'final_norm_w', 'ssd_w_in', 'ssd_conv_w', 'ssd_conv_b', 'ssd_dt_bias',
           'ssd_a_log', 'ssd_d', 'ssd_norm_w', 'ssd_w_out', 'sc_w_in', 'sc_conv_w', 'sc_w_out', 'ffn_w_up',
           'ffn_conv_w', 'ffn_conv_b', 'ffn_w_down']
BIG = ('ssd_w_in', 'ssd_w_out', 'sc_w_in', 'sc_w_out', 'ffn_w_up', 'ffn_w_down')
SHARDED_SMALL = ('ssd_conv_w', 'sc_conv_w', 'ffn_conv_w')


def _lane_pad(v):
    return jnp.pad(v.astype(F32), (0, LANE - v.shape[0])).reshape(1, LANE)


def _gather_cols(g):
    return jnp.moveaxis(g, 0, -2).reshape(g.shape[1:-1] + (N_DEV * g.shape[-1],))


class _Gather:
    def __init__(self, bufs, tag):
        self.bufs, self.tag = bufs, tag

    def start_ici(self, after):
        self.sems = _split_start(self.bufs, _plan_gather_ici, 4 * len(self.bufs), after, name=f"ag_ici_start_{self.tag}")
        return self.sems[3]

    def hand_on(self, after):
        send, recv, bufs, _ = self.sems
        bufs = _split_wait(bufs, send, recv, after, _plan_gather_ici, name=f"ag_ici_wait_{self.tag}")
        self.sems = _split_start(bufs, _plan_gather_d2d, 3 * len(bufs), after, name=f"ag_d2d_start_{self.tag}")
        return self.sems[3]

    def finish(self, after):
        send, recv, bufs, _ = self.sems
        return _split_wait(bufs, send, recv, after, _plan_gather_d2d, name=f"ag_d2d_wait_{self.tag}")


class _Scatter:
    def __init__(self, grads, core, tag):
        self.grads, self.core, self.tag = grads, core, tag

    def start_pair(self, after):
        lands = [_land((N_CHIP,) + g.shape[1:], g.dtype) for g in self.grads]
        self.sems = _split_start(self.grads + lands, _plan_pair, N_CHIP * len(lands), after,
                                 name=f"rs_pair_start_{self.tag}")
        return self.sems[3]

    def start_chip(self, after):
        n = len(self.grads)
        send, recv, arrs, _ = self.sems
        arrs = _split_wait(arrs, send, recv, after, _plan_pair, name=f"rs_pair_wait_{self.tag}")
        self.sums = [_add_pair(g, o, self.core, name=f"rs_add_{self.tag}{a}")
                     for a, (g, o) in enumerate(zip(arrs[:n], arrs[n:]))]
        lands = [_land((N_CHIP - 1,) + s.shape[1:], s.dtype) for s in self.sums]
        self.sems = _split_start(self.sums + lands, _plan_chip, (N_CHIP - 1) * n, after,
                                 name=f"rs_chip_start_{self.tag}")
        return self.sems[3]

    def finish(self, after):
        n = len(self.grads)
        send, recv, arrs, _ = self.sems
        arrs = _split_wait(arrs, send, recv, after, _plan_chip, name=f"rs_chip_wait_{self.tag}")
        return list(zip(arrs[:n], arrs[n:]))


def kernel(x, mix_norm_w, ffn_norm_w, final_norm_w, ssd_w_in, ssd_conv_w, ssd_conv_b, ssd_dt_bias, ssd_a_log, ssd_d, ssd_norm_w, ssd_w_out, sc_w_in, sc_conv_w, sc_w_out, ffn_w_up, ffn_conv_w, ffn_conv_b, ffn_w_down, loss_target, m_mix_norm_w, m_ffn_norm_w, m_final_norm_w, m_ssd_w_in, m_ssd_conv_w, m_ssd_conv_b, m_ssd_dt_bias, m_ssd_a_log, m_ssd_d, m_ssd_norm_w, m_ssd_w_out, m_sc_w_in, m_sc_conv_w, m_sc_w_out, m_ffn_w_up, m_ffn_conv_w, m_ffn_conv_b, m_ffn_w_down, v_mix_norm_w, v_ffn_norm_w, v_final_norm_w, v_ssd_w_in, v_ssd_conv_w, v_ssd_conv_b, v_ssd_dt_bias, v_ssd_a_log, v_ssd_d, v_ssd_norm_w, v_ssd_w_out, v_sc_w_in, v_sc_conv_w, v_sc_w_out, v_ffn_w_up, v_ffn_conv_w, v_ffn_conv_b, v_ffn_w_down):
    args = locals()
    wt = {n: args[n] for n in WEIGHTS}
    mom = {n: args["m_" + n] for n in WEIGHTS}
    var = {n: args["v_" + n] for n in WEIGHTS}
    for src in (wt, mom, var):
        src['ssd_w_in'] = jnp.swapaxes(src['ssd_w_in'], 1, 2)

    t, d = x.shape[-2], x.shape[-1]
    cur = x.reshape(t, d)
    target = loss_target.reshape(t, d)
    depth = mix_norm_w.shape[0]
    n_ssd, n_sc = ssd_w_in.shape[0], sc_w_in.shape[0]
    heads = ssd_dt_bias.shape[1]
    di = ssd_norm_w.shape[1]
    conv_dim = ssd_conv_b.shape[1]
    bc = (conv_dim - di) // 2
    in_dim = N_DEV * ssd_w_in.shape[2]
    in_pad = di + conv_dim + LANE
    ff = ffn_w_down.shape[1] * N_DEV
    me = 4 * lax.axis_index("x") + 2 * lax.axis_index("y") + lax.axis_index("c")
    me_s = me.astype(jnp.int32).reshape(1)
    core_s = lax.axis_index("c").astype(jnp.int32).reshape(1)
    chip_s = (2 * lax.axis_index("x") + lax.axis_index("y")).astype(jnp.int32).reshape(1)

    names_of = {"ssd": ('ssd_w_in', 'ssd_w_out'), "sc": ('sc_w_in', 'sc_w_out'), "ffn": ('ffn_w_up', 'ffn_w_down')}
    order = []
    for i in range(depth):
        order += [("ssd" if i % 2 == 0 else "sc", i // 2), ("ffn", i)]
    def make_gather(s, after):
        kind, idx = order[s]
        bufs = []
        for n in names_of[kind]:
            after = _cast_layer(wt[n], idx, me_s, after, name=f"cast_{n}{idx}")
            bufs.append(after)
        return _Gather(bufs, f"{kind}{idx}"), after

    conv_full = [_gather_cols(g) for g in _all_gather([wt[n] for n in SHARDED_SMALL], name="ag_conv")]
    first, last_cast = make_gather(0, conv_full[0])
    gathers = [first]
    tok_a = last_cast = first.start_ici(last_cast)
    for s in range(1, len(order)):
        g, last_cast = make_gather(s, last_cast)
        gathers.append(g)
    tok_b = gathers[0].hand_on(last_cast)
    tok_c = gathers[1].start_ici(tok_b)
    weights = [None] * len(order)
    weights[0] = gathers[0].finish(tok_c)
    ssd_cw, sc_cw, ffn_cw = conv_full
    ffn_cw = ffn_cw.reshape(depth, ffn_cw.shape[1], 2, ff)
    ffn_cb = ffn_conv_b.reshape(depth, 2, ff)
    dexp = jnp.repeat(ssd_d.astype(F32), HEAD_DIM, axis=1)

    n_sub = len(order)
    full = {n: [None] * wt[n].shape[0] for n in BIG}

    def prefetch(s, after):
        return gathers[s + 2].start_ici(after) if s + 2 < n_sub else None

    def hand_on(s, after):
        return gathers[s + 1].hand_on(after) if s + 1 < n_sub else None

    def arrive(s, after):
        if s + 1 < n_sub:
            weights[s + 1] = gathers[s + 1].finish(after)
            use(s + 1)

    def use(s):
        kind, idx = order[s]
        g_in, g_out = weights[s]
        if kind == "sc":
            g_in = jnp.swapaxes(g_in, 0, 1).reshape(d, -1)
        if kind == "ssd":
            g_in = jnp.pad(g_in.reshape(in_dim, d).T, ((0, 0), (0, in_pad - in_dim)))
        n_in, n_out = names_of[kind]
        full[n_in][idx], full[n_out][idx] = g_in, g_out.reshape(-1, d)

    use(0)
    saved = []
    for i in range(depth):
        j = i // 2
        s = 2 * i
        rec = {"x_mix": cur}
        tok = prefetch(s, cur)
        h = _rmsnorm_fwd(cur, _with_tokens(mix_norm_w[i], tok, tok_c if i == 0 else None), name=f"norm_mix{i}")
        rec["h_mix"] = h
        if i % 2 == 0:
            zx = _mm_nn(h, full['ssd_w_in'][j], out_dtype=BF16, name=f"ssd_in{j}")
            cb = _with_tokens(ssd_conv_b[j].reshape(1, conv_dim), hand_on(s, zx))
            xbc = _ssd_conv_fwd(zx, ssd_cw[j], cb, di, name=f"ssd_conv{j}")
            ssd_vecs = (_lane_pad(ssd_dt_bias[j]), _lane_pad(ssd_a_log[j]), dexp[j].reshape(1, di),
                        ssd_norm_w[j].reshape(1, di))
            yn, y, states = _ssd_fwd(xbc, zx, *ssd_vecs, name=f"ssd_core{j}")
            arrive(s, yn)
            cur = _mm_nn(yn, full['ssd_w_out'][j], res=cur, out_dtype=F32, name=f"ssd_out{j}")
            rec.update(zx=zx, xbc=xbc, yn=yn, y=y, states=states, vecs=ssd_vecs)
        else:
            p3 = _mm_nn(h, full['sc_w_in'][j], out_dtype=BF16, out_parts=3, name=f"sc_in{j}")
            act = _sc_act_fwd(p3, _with_tokens(sc_cw[j], hand_on(s, p3)), name=f"sc_act{j}")
            arrive(s, act)
            cur = _mm_nn(act, full['sc_w_out'][j], res=cur, out_dtype=F32, name=f"sc_out{j}")
            rec.update(p3=p3, act=act)
        s += 1
        rec["x_ffn"] = cur
        h = _rmsnorm_fwd(cur, _with_tokens(ffn_norm_w[i], prefetch(s, cur)), name=f"norm_ffn{i}")
        u3 = _lin_in_fwd(h, full['ffn_w_up'][i], 2, name=f"ffn_up{i}")
        act, pre3 = _ffn_act_fwd(u3, ffn_cw[i], _with_tokens(ffn_cb[i], hand_on(s, u3)), name=f"ffn_act{i}")
        arrive(s, act)
        cur = _mm_nn(act, full['ffn_w_down'][i], res=cur, out_dtype=F32, name=f"ffn_down{i}")
        rec.update(h_ffn=h, u3=u3, pre3=pre3, ffn_act=act)
        saved.append(rec)

    dx, dxb, dw_final, loss8 = _loss_head(cur, final_norm_w, target, name="loss_head")

    small = {n: [None] * wt[n].shape[0] for n in WEIGHTS if n not in BIG and n != 'final_norm_w'}
    scatters = [None] * n_sub
    pending = None

    def chip_step(after):
        return pending.start_chip(after) if pending is not None else None

    for i in reversed(range(depth)):
        j = i // 2
        rec = saved[i]
        nb_up = ffn_w_up.shape[2]
        da = _mm_nt(dxb, full['ffn_w_down'][i], out_dtype=BF16, name=f"ffn_down_dx{i}")
        g_down = _mm_tn(rec["ffn_act"], dxb, out_dtype=BF16, name=f"ffn_down_dw{i}")
        du3, dcw, dcb = _ffn_act_bwd(rec["u3"], rec["pre3"], da, _with_tokens(ffn_cw[i], chip_step(da)),
                                     name=f"ffn_act_bwd{i}")
        g_up = _lin_in_dw(rec["h_ffn"], du3, nb_up, name=f"ffn_up_dw{i}")
        dh = _lin_in_dx(du3, full['ffn_w_up'][i], name=f"ffn_up_dx{i}")
        pending = scatters[2 * i + 1] = _Scatter([g_up, g_down.reshape(N_DEV, ff // N_DEV, d)], core_s, f"ffn{i}")
        tok = pending.start_pair(dh)
        dx, dxb, dwn = _rmsnorm_bwd(dh, rec["x_ffn"], _with_tokens(ffn_norm_w[i], tok), dx, name=f"norm_ffn_bwd{i}")
        small['ffn_conv_w'][i] = dcw.reshape(dcw.shape[0], 2 * ff)
        small['ffn_conv_b'][i] = dcb.reshape(2 * ff)
        small['ffn_norm_w'][i] = dwn.sum(axis=0)

        if i % 2 == 0:
            zx, xbc = rec["zx"], rec["xbc"]
            cw, cb = ssd_cw[j], ssd_conv_b[j].reshape(1, conv_dim)
            dyn = _mm_nt(dxb, full['ssd_w_out'][j], out_dtype=BF16, name=f"ssd_out_dx{j}")
            g_out = _mm_tn(rec["yn"], dxb, out_dtype=BF16, name=f"ssd_out_dw{j}")
            bias_t = _with_tokens(rec["vecs"][0], chip_step(dyn))
            dzx, dxs, db, dc, ddt_g, vec_acc, dnw, ddexp = _ssd_bwd(
                dyn, rec["y"], xbc, zx, rec["states"], bias_t, *rec["vecs"][1:], name=f"ssd_core_bwd{j}")
            dzx, dcw_x, dcb_x = _ssd_conv_bwd(zx, dxs, cw, cb, dzx, di, 0, name=f"ssd_conv_bwd_x{j}")
            dzx, dcw_b, dcb_b = _ssd_conv_bwd(zx, db, cw, cb, dzx, di, di, name=f"ssd_conv_bwd_b{j}")
            dzx, dcw_c, dcb_c = _ssd_conv_bwd(zx, dc, cw, cb, dzx, di, di + bc, name=f"ssd_conv_bwd_c{j}")
            dzx = _ssd_put_ddt(ddt_g, dzx, (di + conv_dim) // LANE, name=f"ssd_put_ddt{j}")
            g_in = _mm_tn(rec["h_mix"], dzx, out_dtype=BF16, name=f"ssd_in_dw{j}")
            g_in = g_in[:, :in_dim].T.reshape(N_DEV, in_dim // N_DEV, d)
            dh = _mm_nt(dzx, full['ssd_w_in'][j], out_dtype=F32, name=f"ssd_in_dx{j}")
            small['ssd_conv_w'][j] = jnp.concatenate([dcw_x, dcw_b, dcw_c], axis=1)
            small['ssd_conv_b'][j] = jnp.concatenate([dcb_x, dcb_b, dcb_c], axis=1).reshape(conv_dim)
            small['ssd_a_log'][j] = vec_acc[:, 0, :heads].sum(axis=0)
            small['ssd_dt_bias'][j] = vec_acc[:, 1, :heads].sum(axis=0)
            small['ssd_d'][j] = ddexp[:, 0, :].reshape(heads, HEAD_DIM).sum(axis=1)
            small['ssd_norm_w'][j] = dnw[:, 0, :].reshape(di)
            g_out = g_out.reshape(N_DEV, di // N_DEV, d)
        else:
            nb_in = sc_w_in.shape[2]
            da = _mm_nt(dxb, full['sc_w_out'][j], out_dtype=BF16, name=f"sc_out_dx{j}")
            g_out = _mm_tn(rec["act"], dxb, out_dtype=BF16, name=f"sc_out_dw{j}")
            dp3, dcw = _sc_act_bwd(rec["p3"], da, _with_tokens(sc_cw[j], chip_step(da)), name=f"sc_act_bwd{j}")
            g_in = _mm_tn(rec["h_mix"], dp3, out_dtype=BF16, name=f"sc_in_dw{j}")
            g_in = jnp.swapaxes(g_in.reshape(d, N_DEV, nb_in), 0, 1)
            dh = _mm_nt(dp3, full['sc_w_in'][j], out_dtype=F32, name=f"sc_in_dx{j}")
            small['sc_conv_w'][j] = dcw
            g_out = g_out.reshape(N_DEV, g_out.shape[0] // N_DEV, d)
        pending = scatters[2 * i] = _Scatter([g_in, g_out], core_s, f"{order[2 * i][0]}{j}")
        tok = pending.start_pair(dh)
        dx, dxb, dwn = _rmsnorm_bwd(dh, rec["x_mix"], _with_tokens(mix_norm_w[i], tok), dx, name=f"norm_mix_bwd{i}")
        small['mix_norm_w'][i] = dwn.sum(axis=0)
    tok_last = chip_step(dx)

    small_names = [n for n in WEIGHTS if n not in BIG]
    partial = {n: jnp.stack(small[n]) for n in small}
    partial['final_norm_w'] = dw_final.sum(axis=0)
    full_shapes = [partial[n].shape for n in small_names]
    packed = _pack([loss8.sum().reshape(1)] + [partial[n] for n in small_names])
    total = _unpack(_all_reduce_small(packed, name="ar_small"), [(1,)] + full_shapes)
    loss = total[0].reshape(())
    grads = dict(zip(small_names, total[1:]))
    for n in SHARDED_SMALL:
        nb = wt[n].shape[-1]
        grads[n] = lax.dynamic_slice_in_dim(grads[n], me * nb, nb, axis=grads[n].ndim - 1)

    delta, new_m, new_v = {}, {}, {}
    shapes = [wt[n].shape for n in small_names]
    outs = _adamw_small(*[_pack([src[n] for n in small_names]) for src in (wt, grads, mom, var)], name="adamw_small")
    for dst, packed_out in zip((delta, new_m, new_v), outs):
        dst.update(zip(small_names, _unpack(packed_out, shapes)))
    parts = {n: [None] * wt[n].shape[0] for n in BIG}
    for s in range(1, n_sub):
        kind, idx = order[s]
        parts[names_of[kind][0]][idx], parts[names_of[kind][1]][idx] = scatters[s].finish(tok_last)
    first_in, first_out = names_of[order[0][0]]
    last_out = tok_last
    jobs = [(n, layer) for n in reversed(BIG) for layer in reversed(range(wt[n].shape[0]))]
    jobs.sort(key=lambda job: parts[job[0]][job[1]] is None)
    chain = {n: None for n in BIG}
    for n, layer in jobs:
        if parts[n][layer] is None:
            parts[first_in][0], parts[first_out][0] = scatters[0].finish(last_out)
        chain[n] = _adamw_layer(wt[n], mom[n], var[n], *parts[n][layer], chip_s, layer, chain[n], last_out,
                                name=f"adamw_{n}{layer}")
        last_out = chain[n][1]
    for n in BIG:
        grads[n], delta[n], new_m[n], new_v[n] = chain[n]
    for dst in (grads, delta, new_m, new_v):
        dst['ssd_w_in'] = jnp.swapaxes(dst['ssd_w_in'], 1, 2)

    return (loss, dx.reshape(x.shape), *[grads[n] for n in WEIGHTS], *[delta[n] for n in WEIGHTS],
            *[new_m[n] for n in WEIGHTS], *[new_v[n] for n in WEIGHTS])
```

```python
import functools
import math

import jax
import jax.numpy as jnp
from jax import lax
from jax.experimental import pallas as pl
from jax.experimental.pallas import tpu as pltpu

F32 = jnp.float32
BF16 = jnp.bfloat16
MESH = pl.DeviceIdType.MESH

N_DEV = 8
N_CHIP = 4
EPS = 1e-5
HEAD_DIM = 64
STATE = 128
CHUNK = 128
PAIR = 2 * HEAD_DIM
GROUP_W = 8 * HEAD_DIM
HALO = 16
LANE = 128
VMEM_LIMIT = 56 * 1024 * 1024

ADAM_LR = 0.001
ADAM_B1 = 0.9
ADAM_B2 = 0.999
ADAM_EPS = 1e-08
ADAM_WD = 0.01
ADAM_STEP = 10


def _pick(n, candidates):
    for c in candidates:
        if c <= n and n % c == 0:
            return c
    return n


OPERAND_VMEM = 36 * 1024 * 1024


def _pick_k(kd, other, candidates):
    for c in (kd,) + tuple(candidates):
        if c <= kd and kd % c == 0 and 2 * 2 * other * c <= OPERAND_VMEM:
            return c
    return kd


def _params(*sem):
    return pltpu.CompilerParams(dimension_semantics=sem, vmem_limit_bytes=VMEM_LIMIT)


def _sigmoid(x):
    return 0.5 * jnp.tanh(0.5 * x) + 0.5


_DIMS = {
    "nn": (((1,), (0,)), ((), ())),
    "nt": (((1,), (1,)), ((), ())),
    "tn": (((0,), (0,)), ((), ())),
}


def _matmul(mode, a, b, *, grid, a_spec, b_spec, o_spec, out_shape, acc_shape, name, res=None, res_spec=None,
            part_fn=None):
    nk = grid[2]
    dims = _DIMS[mode]
    if part_fn is None:
        part_fn = lambda a_ref, b_ref: lax.dot_general(a_ref[...], b_ref[...], dims, preferred_element_type=F32)

    def body(*refs):
        if res is None:
            a_ref, b_ref, o_ref = refs[:3]
            r_ref, scratch = None, refs[3:]
        else:
            a_ref, b_ref, r_ref, o_ref = refs[:4]
            scratch = refs[4:]
        part = part_fn(a_ref, b_ref)

        def finish(acc):
            if r_ref is not None:
                acc = acc + r_ref[...]
            o_ref[...] = acc.astype(o_ref.dtype)

        if nk == 1:
            finish(part)
        else:
            acc_ref = scratch[0]
            k = pl.program_id(2)

            @pl.when(k == 0)
            def _():
                acc_ref[...] = part

            @pl.when(k > 0)
            def _():
                acc_ref[...] += part

            @pl.when(k == nk - 1)
            def _():
                finish(acc_ref[...])

    in_specs = [a_spec, b_spec] + ([res_spec] if res is not None else [])
    args = (a, b) + ((res,) if res is not None else ())
    return pl.pallas_call(
        body, name=name, grid=grid, in_specs=in_specs, out_specs=o_spec, out_shape=out_shape,
        scratch_shapes=[pltpu.VMEM(acc_shape, F32)] if nk > 1 else [],
        compiler_params=_params("parallel", "parallel", "arbitrary"),
    )(*args)


def _mm_nn(a, b, *, out_dtype, res=None, out_parts=1, name):
    m, kd = a.shape
    n = b.shape[1]
    c = n // out_parts
    tm = _pick(m, (512, 256, 128))
    tn = _pick(c, (1152, 1024, 512, 384, 256, 128))
    tk = _pick_k(kd, tm + tn, (2816, 2048, 1024, 512, 256, 128))
    grid = (n // tn, m // tm, kd // tk)
    if out_parts == 1:
        o_spec = pl.BlockSpec((tm, tn), lambda j, i, k: (i, j))
        out_shape = jax.ShapeDtypeStruct((m, n), out_dtype)
    else:
        o_spec = _stacked_spec(tm, tn, c, lambda j, i, k: (i, j))
        out_shape = jax.ShapeDtypeStruct((out_parts, m, c), out_dtype)
    return _matmul(
        "nn", a, b, res=res, grid=grid, name=name,
        a_spec=pl.BlockSpec((tm, tk), lambda j, i, k: (i, k)),
        b_spec=pl.BlockSpec((tk, tn), lambda j, i, k: (k, j)),
        res_spec=pl.BlockSpec((tm, tn), lambda j, i, k: (i, j)),
        o_spec=o_spec, out_shape=out_shape, acc_shape=(tm, tn))


def _stacked_spec(rows, width, c, row_col):
    per = c // width

    def index(j, i, k):
        r, q = row_col(j, i, k)
        return q // per, r, q % per

    return pl.BlockSpec((None, rows, width), index)


def _mm_nt(a, b, *, out_dtype, name):
    stacked = a.ndim == 3
    m = a.shape[-2]
    n, kd = b.shape
    c = a.shape[-1]
    tm = _pick(m, (512, 256, 128))
    tn = _pick(n, (1408, 1024, 512, 256, 128))
    tk = _pick_k(c, tm + tn, (3456, 2816, 2048, 1024, 512, 384, 256, 128))
    grid = (n // tn, m // tm, kd // tk)
    a_spec = (_stacked_spec(tm, tk, c, lambda j, i, k: (i, k)) if stacked
              else pl.BlockSpec((tm, tk), lambda j, i, k: (i, k)))
    return _matmul(
        "nt", a, b, grid=grid, name=name, a_spec=a_spec,
        b_spec=pl.BlockSpec((tn, tk), lambda j, i, k: (j, k)),
        o_spec=pl.BlockSpec((tm, tn), lambda j, i, k: (i, j)),
        out_shape=jax.ShapeDtypeStruct((m, n), out_dtype), acc_shape=(tm, tn))


def _mm_tn(a, b, *, out_dtype, name):
    stacked = b.ndim == 3
    kd, m = a.shape
    c = b.shape[-1]
    n = c * (b.shape[0] if stacked else 1)
    tm = _pick(m, (512, 256, 128))
    tn = _pick(c, (1152, 1024, 512, 384, 256, 128))
    tk = _pick_k(kd, tm + tn, (2048, 1024, 512, 256, 128))
    grid = (n // tn, m // tm, kd // tk)
    b_spec = (_stacked_spec(tk, tn, c, lambda j, i, k: (k, j)) if stacked
              else pl.BlockSpec((tk, tn), lambda j, i, k: (k, j)))
    return _matmul(
        "tn", a, b, grid=grid, name=name,
        a_spec=pl.BlockSpec((tk, tm), lambda j, i, k: (k, i)), b_spec=b_spec,
        o_spec=pl.BlockSpec((tm, tn), lambda j, i, k: (i, j)),
        out_shape=jax.ShapeDtypeStruct((m, n), out_dtype), acc_shape=(tm, tn))


def _in_tile(nb, c):
    return math.gcd(nb, c)


def _lin_in_fwd(h, wg, parts, *, name):
    t, d = h.shape
    nb = wg.shape[2]
    c = N_DEV * nb // parts
    w = _in_tile(nb, c)
    nbw, cw = nb // w, c // w
    tm = _pick(t, (1024,) if w < 512 else (512, 256, 128))
    grid = (N_DEV * nbw, t // tm, 1)
    return _matmul(
        "nn", h, wg, grid=grid, name=name,
        a_spec=pl.BlockSpec((tm, d), lambda j, i, k: (i, 0)),
        b_spec=pl.BlockSpec((None, d, w), lambda j, i, k: (j // nbw, 0, j % nbw)),
        o_spec=pl.BlockSpec((None, tm, w), lambda j, i, k: (j // cw, i, j % cw)),
        out_shape=jax.ShapeDtypeStruct((parts, t, c), BF16), acc_shape=(tm, w))


def _lin_in_dx(dact, wg, *, name):
    parts, t, c = dact.shape
    d, nb = wg.shape[1], wg.shape[2]
    tm = _pick(t, (512, 256, 128))
    tn = _pick(d, (1024, 512, 256, 128))
    group = _pick_k(c, tm + tn, (2 * nb, nb)) // nb
    per = c // (group * nb)
    grid = (d // tn, t // tm, N_DEV // group)

    def blocks(a_ref, b_ref):
        acc = None
        for q in range(group):
            part = lax.dot_general(a_ref[:, q * nb:(q + 1) * nb], b_ref[q], _DIMS["nt"], preferred_element_type=F32)
            acc = part if acc is None else acc + part
        return acc

    return _matmul(
        "nt", dact, wg, grid=grid, name=name, part_fn=blocks,
        a_spec=pl.BlockSpec((None, tm, group * nb), lambda j, i, k: (k // per, i, k % per)),
        b_spec=pl.BlockSpec((group, tn, nb), lambda j, i, k: (k, j, 0)),
        o_spec=pl.BlockSpec((tm, tn), lambda j, i, k: (i, j)),
        out_shape=jax.ShapeDtypeStruct((t, d), F32), acc_shape=(tm, tn))


def _lin_in_dw(h, dact, nb, *, name):
    t, d = h.shape
    parts, _, c = dact.shape
    w = _in_tile(nb, c)
    nbw, cw = nb // w, c // w
    tm = _pick(d, (512, 256, 128))
    tk = _pick_k(t, tm + w, (2048, 1024, 512, 256, 128))
    grid = (N_DEV * nbw, d // tm, t // tk)
    return _matmul(
        "tn", h, dact, grid=grid, name=name,
        a_spec=pl.BlockSpec((tk, tm), lambda j, i, k: (k, i)),
        b_spec=pl.BlockSpec((None, tk, w), lambda j, i, k: (j // cw, k, j % cw)),
        o_spec=pl.BlockSpec((None, tm, w), lambda j, i, k: (j // nbw, i, j % nbw)),
        out_shape=jax.ShapeDtypeStruct((N_DEV, d, nb), BF16), acc_shape=(tm, w))


def _fold8(v):
    rows, c = v.shape
    return v.reshape(rows // 8, 8, c).sum(axis=0)


def _accumulate(ref, val, first):
    @pl.when(first)
    def _():
        ref[...] = val

    @pl.when(jnp.logical_not(first))
    def _():
        ref[...] += val


def _tile2(r, c, rows=(256, 128, 64, 32, 16)):
    tr = _pick(r, rows)
    if tr < r or r <= rows[0]:
        return tr, c
    return r, _pick(c, (256, 128))


def _cast_layer(w_stack, layer, me, after, *, name):
    _, r, c = w_stack.shape
    tr, tc = _tile2(r, c)

    def body(me_ref, w_ref, after_ref, o_ref):
        del me_ref, after_ref
        o_ref[...] = w_ref[...].astype(BF16)

    return pl.pallas_call(
        body, name=name,
        grid_spec=pltpu.PrefetchScalarGridSpec(
            num_scalar_prefetch=1, grid=(r // tr, c // tc),
            in_specs=[pl.BlockSpec((None, tr, tc), lambda i, j, me_ref: (layer, i, j)), HBM_ANY],
            out_specs=pl.BlockSpec((None, tr, tc), lambda i, j, me_ref: (me_ref[0], i, j))),
        out_shape=jax.ShapeDtypeStruct((N_DEV, r, c), BF16),
        compiler_params=_params("parallel", "parallel"),
    )(me, w_stack, after)


def _rmsnorm_fwd(x, w, *, name):
    t, d = x.shape
    tt = _pick(t, (256, 128))

    def body(x_ref, w_ref, o_ref):
        xv = x_ref[...]
        r = lax.rsqrt(jnp.mean(xv * xv, axis=1, keepdims=True) + EPS)
        o_ref[...] = (xv * r * w_ref[...]).astype(BF16)

    return pl.pallas_call(
        body, name=name, grid=(t // tt,),
        in_specs=[pl.BlockSpec((tt, d), lambda i: (i, 0)), pl.BlockSpec((1, d), lambda i: (0, 0))],
        out_specs=pl.BlockSpec((tt, d), lambda i: (i, 0)),
        out_shape=jax.ShapeDtypeStruct((t, d), BF16),
        compiler_params=_params("parallel"),
    )(x, w.reshape(1, d))


def _rmsnorm_bwd(dh, x, w, dres, *, name):
    t, d = x.shape
    tt = _pick(t, (256, 128))

    def body(dh_ref, x_ref, w_ref, dres_ref, dx_ref, dxb_ref, dw_ref):
        xv = x_ref[...]
        r = lax.rsqrt(jnp.mean(xv * xv, axis=1, keepdims=True) + EPS)
        xhat = xv * r
        dhv = dh_ref[...].astype(F32)
        dxhat = dhv * w_ref[...]
        dx = dres_ref[...] + r * (dxhat - xhat * jnp.mean(dxhat * xhat, axis=1, keepdims=True))
        dx_ref[...] = dx
        dxb_ref[...] = dx.astype(BF16)
        _accumulate(dw_ref, _fold8(dhv * xhat), pl.program_id(0) == 0)

    row = pl.BlockSpec((tt, d), lambda i: (i, 0))
    return pl.pallas_call(
        body, name=name, grid=(t // tt,),
        in_specs=[row, row, pl.BlockSpec((1, d), lambda i: (0, 0)), row],
        out_specs=[row, row, pl.BlockSpec((8, d), lambda i: (0, 0))],
        out_shape=[jax.ShapeDtypeStruct((t, d), F32), jax.ShapeDtypeStruct((t, d), BF16),
                   jax.ShapeDtypeStruct((8, d), F32)],
        compiler_params=_params("arbitrary"),
    )(dh, x, w.reshape(1, d), dres)


def _loss_head(x, w, target, *, name):
    t, d = x.shape
    tt = _pick(t, (256, 128))

    def body(x_ref, w_ref, tg_ref, dx_ref, dxb_ref, dw_ref, ls_ref):
        xv = x_ref[...]
        wv = w_ref[...]
        r = lax.rsqrt(jnp.mean(xv * xv, axis=1, keepdims=True) + EPS)
        xhat = xv * r
        err = xhat * wv - tg_ref[...]
        dy = err * (1.0 / d)
        dxhat = dy * wv
        dx = r * (dxhat - xhat * jnp.mean(dxhat * xhat, axis=1, keepdims=True))
        dx_ref[...] = dx
        dxb_ref[...] = dx.astype(BF16)
        first = pl.program_id(0) == 0
        _accumulate(dw_ref, _fold8(dy * xhat), first)
        _accumulate(ls_ref, _fold8(err * err) * (0.5 / d), first)

    row = pl.BlockSpec((tt, d), lambda i: (i, 0))
    acc = pl.BlockSpec((8, d), lambda i: (0, 0))
    return pl.pallas_call(
        body, name=name, grid=(t // tt,),
        in_specs=[row, pl.BlockSpec((1, d), lambda i: (0, 0)), row],
        out_specs=[row, row, acc, acc],
        out_shape=[jax.ShapeDtypeStruct((t, d), F32), jax.ShapeDtypeStruct((t, d), BF16),
                   jax.ShapeDtypeStruct((8, d), F32), jax.ShapeDtypeStruct((8, d), F32)],
        compiler_params=_params("arbitrary"),
    )(x, w.reshape(1, d), target)


def _conv_causal(e, tap, width):
    acc = None
    for k in range(width):
        s = width - 1 - k
        term = (e if s == 0 else pltpu.roll(e, s, 0)) * tap(k)
        acc = term if acc is None else acc + term
    return acc


def _conv_anticausal(e, tap, width):
    rows = e.shape[0]
    acc = None
    for k in range(width):
        s = width - 1 - k
        term = (e if s == 0 else pltpu.roll(e, rows - s, 0)) * tap(k)
        acc = term if acc is None else acc + term
    return acc


def _extend(prev, cur, nxt, first, last):
    parts = []
    if prev is not None:
        parts.append(jnp.where(first, 0.0, prev.astype(F32)))
    parts.append(cur.astype(F32))
    if nxt is not None:
        parts.append(jnp.where(last, 0.0, nxt.astype(F32)))
    return jnp.concatenate(parts, axis=0)


def _prev_idx(i, tt):
    return jnp.maximum(i * (tt // HALO) - 1, 0)


def _next_idx(i, tt, t):
    return jnp.minimum((i + 1) * (tt // HALO), t // HALO - 1)


def _ffn_act_fwd(u3, cw, cb, *, name):
    _, t, f = u3.shape
    tt = _pick(t, (512, 256, 128))
    tc = _pick(f, (512, 256, 128))
    width = cw.shape[0]

    def body(u_ref, up_ref, w_ref, b_ref, o_ref, pre_ref):
        first = pl.program_id(1) == 0
        pre = []
        for p in range(2):
            e = _extend(up_ref[p], u_ref[p], None, first, None)
            pre.append(_conv_causal(e, lambda k: w_ref[k, p:p + 1, :], width)[HALO:] + b_ref[p:p + 1, :])
            pre_ref[p] = pre[p].astype(BF16)
        g, v = pre
        o_ref[...] = (g * _sigmoid(g) * v).astype(BF16)

    return pl.pallas_call(
        body, name=name, grid=(f // tc, t // tt),
        in_specs=[pl.BlockSpec((2, tt, tc), lambda j, i: (0, i, j)),
                  pl.BlockSpec((2, HALO, tc), lambda j, i: (0, _prev_idx(i, tt), j)),
                  pl.BlockSpec((width, 2, tc), lambda j, i: (0, 0, j)),
                  pl.BlockSpec((2, tc), lambda j, i: (0, j))],
        out_specs=[pl.BlockSpec((tt, tc), lambda j, i: (i, j)), pl.BlockSpec((2, tt, tc), lambda j, i: (0, i, j))],
        out_shape=[jax.ShapeDtypeStruct((t, f), BF16), jax.ShapeDtypeStruct((2, t, f), BF16)],
        compiler_params=_params("parallel", "parallel"),
    )(u3, u3, cw, cb)


def _ffn_act_bwd(u3, pre3, da, cw, *, name):
    _, t, f = u3.shape
    tt = _pick(t, (512, 256, 128))
    tc = _pick(f, (512, 256, 128))
    width = cw.shape[0]
    nt = t // tt
    rows = tt + HALO

    def body(u_ref, pre_ref, pren_ref, da_ref, dan_ref, w_ref, du_ref, dcw_ref, dcb_ref):
        i = pl.program_id(1)
        first, last = i == 0, i == nt - 1
        g, v = (_extend(None, pre_ref[p], pren_ref[p], None, False) for p in range(2))
        dae = _extend(None, da_ref[...], dan_ref[...], None, last)
        sg = _sigmoid(g)
        dpre = (dae * v * (sg * (1.0 + g * (1.0 - sg))), dae * (g * sg))

        @pl.when(first)
        def _():
            dcw_ref[...] = jnp.zeros_like(dcw_ref)
            dcb_ref[...] = jnp.zeros_like(dcb_ref)

        for p in range(2):
            u = u_ref[p].astype(F32)
            du = None
            for k in range(width):
                s = width - 1 - k
                d = (dpre[p] if s == 0 else pltpu.roll(dpre[p], rows - s, 0))[:tt]
                term = d * w_ref[k, p:p + 1, :]
                du = term if du is None else du + term
                dcw_ref[k, p:p + 1, :] += jnp.sum(d * u, axis=0, keepdims=True)
                if s == 0:
                    dcb_ref[p:p + 1, :] += jnp.sum(d, axis=0, keepdims=True)
            du_ref[p] = du.astype(BF16)

    cur3 = pl.BlockSpec((2, tt, tc), lambda j, i: (0, i, j))
    return pl.pallas_call(
        body, name=name, grid=(f // tc, nt),
        in_specs=[cur3, cur3,
                  pl.BlockSpec((2, HALO, tc), lambda j, i: (0, _next_idx(i, tt, t), j)),
                  pl.BlockSpec((tt, tc), lambda j, i: (i, j)),
                  pl.BlockSpec((HALO, tc), lambda j, i: (_next_idx(i, tt, t), j)),
                  pl.BlockSpec((width, 2, tc), lambda j, i: (0, 0, j))],
        out_specs=[cur3,
                   pl.BlockSpec((width, 2, tc), lambda j, i: (0, 0, j)),
                   pl.BlockSpec((2, tc), lambda j, i: (0, j))],
        out_shape=[jax.ShapeDtypeStruct((2, t, f), BF16), jax.ShapeDtypeStruct((width, 2, f), F32),
                   jax.ShapeDtypeStruct((2, f), F32)],
        compiler_params=_params("parallel", "arbitrary"),
    )(u3, pre3, pre3, da, da, cw)


def _sc_act_fwd(p3, cw, *, name):
    _, t, c = p3.shape
    tt = _pick(t, (512, 256, 128))
    tc = _pick(c, (512, 256, 128))
    width = cw.shape[0]

    def body(p_ref, pp_ref, w_ref, o_ref):
        first = pl.program_id(1) == 0
        q = _extend(pp_ref[1], p_ref[1], None, first, None) * _extend(pp_ref[2], p_ref[2], None, first, None)
        cq = _conv_causal(q, lambda k: w_ref[k:k + 1, :], width)[HALO:]
        o_ref[...] = (p_ref[0].astype(F32) * cq).astype(BF16)

    return pl.pallas_call(
        body, name=name, grid=(c // tc, t // tt),
        in_specs=[pl.BlockSpec((3, tt, tc), lambda j, i: (0, i, j)),
                  pl.BlockSpec((3, HALO, tc), lambda j, i: (0, _prev_idx(i, tt), j)),
                  pl.BlockSpec((width, tc), lambda j, i: (0, j))],
        out_specs=pl.BlockSpec((tt, tc), lambda j, i: (i, j)),
        out_shape=jax.ShapeDtypeStruct((t, c), BF16),
        compiler_params=_params("parallel", "parallel"),
    )(p3, p3, cw)


def _sc_act_bwd(p3, da, cw, *, name):
    _, t, c = p3.shape
    tt = _pick(t, (512, 256, 128))
    tc = _pick(c, (512, 256, 128))
    width = cw.shape[0]
    nt = t // tt
    ctr = slice(HALO, HALO + tt)

    def body(p_ref, pp_ref, pn_ref, da_ref, dan_ref, w_ref, dp_ref, dcw_ref):
        i = pl.program_id(1)
        first, last = i == 0, i == nt - 1
        tap = lambda k: w_ref[k:k + 1, :]
        bg, cg, hh = (_extend(pp_ref[p], p_ref[p], pn_ref[p], first, last) for p in range(3))
        q = cg * hh
        cq = _conv_causal(q, tap, width)
        dae = _extend(jnp.zeros((HALO, tc), F32), da_ref[...], dan_ref[...], False, last)
        dcq = dae * bg
        dq = _conv_anticausal(dcq, tap, width)[ctr]
        dp_ref[0] = (dae * cq)[ctr].astype(BF16)
        dp_ref[1] = (dq * hh[ctr]).astype(BF16)
        dp_ref[2] = (dq * cg[ctr]).astype(BF16)

        @pl.when(first)
        def _():
            dcw_ref[...] = jnp.zeros_like(dcw_ref)

        dc = dcq[ctr]
        for k in range(width):
            s = width - 1 - k
            qs = (q if s == 0 else pltpu.roll(q, s, 0))[ctr]
            dcw_ref[k:k + 1, :] += jnp.sum(dc * qs, axis=0, keepdims=True)

    return pl.pallas_call(
        body, name=name, grid=(c // tc, nt),
        in_specs=[pl.BlockSpec((3, tt, tc), lambda j, i: (0, i, j)),
                  pl.BlockSpec((3, HALO, tc), lambda j, i: (0, _prev_idx(i, tt), j)),
                  pl.BlockSpec((3, HALO, tc), lambda j, i: (0, _next_idx(i, tt, t), j)),
                  pl.BlockSpec((tt, tc), lambda j, i: (i, j)),
                  pl.BlockSpec((HALO, tc), lambda j, i: (_next_idx(i, tt, t), j)),
                  pl.BlockSpec((width, tc), lambda j, i: (0, j))],
        out_specs=[pl.BlockSpec((3, tt, tc), lambda j, i: (0, i, j)),
                   pl.BlockSpec((width, tc), lambda j, i: (0, j))],
        out_shape=[jax.ShapeDtypeStruct((3, t, c), BF16), jax.ShapeDtypeStruct((width, c), F32)],
        compiler_params=_params("parallel", "arbitrary"),
    )(p3, p3, p3, da, da, cw)


def _ssd_conv_fwd(zx, cw, cb, col0, *, name):
    t = zx.shape[0]
    width, c = cw.shape
    tt = _pick(t, (512, 256, 128))
    tc = _pick(math.gcd(c, col0), (512, 256, 128))
    off = col0 // tc

    def body(x_ref, xp_ref, w_ref, b_ref, o_ref, pre_ref):
        first = pl.program_id(1) == 0
        e = _extend(xp_ref[...], x_ref[...], None, first, None)
        pre = _conv_causal(e, lambda k: w_ref[k:k + 1, :], width)[HALO:] + b_ref[...]
        pre_ref[...] = pre.astype(BF16)
        o_ref[...] = (pre * _sigmoid(pre)).astype(BF16)

    out = pl.BlockSpec((tt, tc), lambda j, i: (i, j))
    return pl.pallas_call(
        body, name=name, grid=(c // tc, t // tt),
        in_specs=[pl.BlockSpec((tt, tc), lambda j, i: (i, off + j)),
                  pl.BlockSpec((HALO, tc), lambda j, i: (_prev_idx(i, tt), off + j)),
                  pl.BlockSpec((width, tc), lambda j, i: (0, j)),
                  pl.BlockSpec((1, tc), lambda j, i: (0, j))],
        out_specs=[out, out],
        out_shape=[jax.ShapeDtypeStruct((t, c), BF16)] * 2,
        compiler_params=_params("parallel", "parallel"),
    )(zx, zx, cw, cb)


def _ssd_conv_bwd(zx, pre, dxc, cw, dzx, col0, woff, *, name):
    t = zx.shape[0]
    width = cw.shape[0]
    c = dxc.shape[1]
    tt = _pick(t, (512, 256, 128))
    tc = _pick(math.gcd(math.gcd(c, col0), woff) if woff else math.gcd(c, col0), (512, 256, 128))
    nt = t // tt
    xoff, wo = (col0 + woff) // tc, woff // tc
    rows = tt + HALO

    def body(x_ref, p_ref, pn_ref, d_ref, dn_ref, w_ref, dzx_in, dzx_ref, dcw_ref, dcb_ref):
        del dzx_in
        i = pl.program_id(1)
        first, last = i == 0, i == nt - 1
        pre_e = _extend(None, p_ref[...], pn_ref[...], None, False)
        de = _extend(None, d_ref[...], dn_ref[...], None, last)
        sg = _sigmoid(pre_e)
        dpre = de * (sg * (1.0 + pre_e * (1.0 - sg)))

        @pl.when(first)
        def _():
            dcw_ref[...] = jnp.zeros_like(dcw_ref)
            dcb_ref[...] = jnp.zeros_like(dcb_ref)

        xv = x_ref[...].astype(F32)
        dx = None
        for k in range(width):
            s = width - 1 - k
            d = (dpre if s == 0 else pltpu.roll(dpre, rows - s, 0))[:tt]
            term = d * w_ref[k:k + 1, :]
            dx = term if dx is None else dx + term
            dcw_ref[k:k + 1, :] += jnp.sum(d * xv, axis=0, keepdims=True)
            if s == 0:
                dcb_ref[...] += jnp.sum(d, axis=0, keepdims=True)
        dzx_ref[...] = dx.astype(BF16)

    return pl.pallas_call(
        body, name=name, grid=(c // tc, nt),
        in_specs=[pl.BlockSpec((tt, tc), lambda j, i: (i, xoff + j)),
                  pl.BlockSpec((tt, tc), lambda j, i: (i, wo + j)),
                  pl.BlockSpec((HALO, tc), lambda j, i: (_next_idx(i, tt, t), wo + j)),
                  pl.BlockSpec((tt, tc), lambda j, i: (i, j)),
                  pl.BlockSpec((HALO, tc), lambda j, i: (_next_idx(i, tt, t), j)),
                  pl.BlockSpec((width, tc), lambda j, i: (0, wo + j)),
                  pl.BlockSpec(memory_space=pl.ANY)],
        out_specs=[pl.BlockSpec((tt, tc), lambda j, i: (i, xoff + j)),
                   pl.BlockSpec((width, tc), lambda j, i: (0, j)),
                   pl.BlockSpec((1, tc), lambda j, i: (0, j))],
        out_shape=[jax.ShapeDtypeStruct(dzx.shape, dzx.dtype), jax.ShapeDtypeStruct((width, c), F32),
                   jax.ShapeDtypeStruct((1, c), F32)],
        input_output_aliases={6: 0},
        compiler_params=_params("parallel", "arbitrary"),
    )(zx, pre, pre, dxc, dxc, cw, dzx)


def _ssd_put_ddt(ddt_g, dzx, col, *, name):
    g, t, _ = ddt_g.shape
    tt = _pick(t, (512, 256, 128))

    def body(d_ref, dzx_in, dzx_ref):
        del dzx_in
        dzx_ref[...] = jnp.sum(d_ref[...], axis=0).astype(BF16)

    return pl.pallas_call(
        body, name=name, grid=(t // tt,),
        in_specs=[pl.BlockSpec((g, tt, LANE), lambda i: (0, i, 0)), pl.BlockSpec(memory_space=pl.ANY)],
        out_specs=pl.BlockSpec((tt, LANE), lambda i: (i, col)),
        out_shape=jax.ShapeDtypeStruct(dzx.shape, dzx.dtype),
        input_output_aliases={1: 0},
        compiler_params=_params("parallel"),
    )(ddt_g, dzx)


def _dot(a, b, mode):
    return lax.dot_general(a, b, _DIMS[mode], preferred_element_type=F32)


def _dot_exact(m01, v, mode="nn"):
    hi = v.astype(BF16)
    r1 = v - hi.astype(F32)
    mid = r1.astype(BF16)
    lo = (r1 - mid.astype(F32)).astype(BF16)
    return _dot(m01, hi, mode) + _dot(m01, mid, mode) + _dot(m01, lo, mode)


def _softplus(x):
    return jnp.maximum(x, 0.0) + jnp.log(1.0 + jnp.exp(-jnp.abs(x)))


def _head_vectors(g, dt_raw, bias, alog):
    n = CHUNK
    dt = _softplus(dt_raw + bias)
    a = -jnp.exp(alog)
    tri = (lax.broadcasted_iota(jnp.int32, (n, n), 0) >= lax.broadcasted_iota(jnp.int32, (n, n), 1)).astype(BF16)
    cs = _dot_exact(tri, dt * a)
    return dt, a, cs, cs.T


def _col(v, lane_ids, h):
    return jnp.sum(jnp.where(lane_ids == h, v, 0.0), axis=1, keepdims=True)


def _row(vt, sub_ids, h):
    return jnp.sum(jnp.where(sub_ids == h, vt, 0.0), axis=0, keepdims=True)


def _ssd_specs(di, bc, nc, rev):
    cidx = (lambda c: nc - 1 - c) if rev else (lambda c: c)
    wide = lambda off: pl.BlockSpec((CHUNK, GROUP_W), lambda g, c: (cidx(c), off + g))
    lane = lambda off: pl.BlockSpec((CHUNK, LANE), lambda g, c: (cidx(c), off + g))
    fixed = lambda off: pl.BlockSpec((CHUNK, LANE), lambda g, c: (cidx(c), off))
    vec = pl.BlockSpec((1, LANE), lambda g, c: (0, 0))
    gvec = pl.BlockSpec((1, GROUP_W), lambda g, c: (0, g))
    state = pl.BlockSpec((None, None, 4, PAIR, STATE), lambda g, c: (g, cidx(c), 0, 0, 0))
    return wide, lane, fixed, vec, gvec, state


def _ssd_fwd(xbc, zx, bias, alog, dexp, nw, *, name):
    t = xbc.shape[0]
    di = nw.shape[1]
    bc = (xbc.shape[1] - di) // 2
    ng, nc = di // GROUP_W, t // CHUNK
    wide, lane, fixed, vec, gvec, state = _ssd_specs(di, bc, nc, rev=False)

    def body(xs_ref, b_ref, c_ref, dt_ref, z_ref, bias_ref, alog_ref, dexp_ref, nw_ref,
             yn_ref, y_ref, st_ref, s_scr):
        g, c = pl.program_id(0), pl.program_id(1)

        @pl.when(c == 0)
        def _():
            s_scr[...] = jnp.zeros_like(s_scr)

        n = CHUNK
        dt, a, cs, cst = _head_vectors(g, dt_ref[...].astype(F32), bias_ref[...], alog_ref[...])
        lane_ids = lax.broadcasted_iota(jnp.int32, (n, LANE), 1)
        sub_ids = lax.broadcasted_iota(jnp.int32, (LANE, n), 0)
        causal = lax.broadcasted_iota(jnp.int32, (n, n), 0) >= lax.broadcasted_iota(jnp.int32, (n, n), 1)
        half = lax.broadcasted_iota(jnp.int32, (1, PAIR), 1) < HEAD_DIM
        half_rows = lax.broadcasted_iota(jnp.int32, (PAIR, 1), 0) < HEAD_DIM
        bm, cm = b_ref[...], c_ref[...]
        gm = _dot(cm, bm, "nt")
        x = xs_ref[...].astype(F32)
        ys = []
        for q in range(4):
            h0 = g * 8 + 2 * q
            col = [_col(cs, lane_ids, h0 + e) for e in range(2)]
            row = [_row(cst, sub_ids, h0 + e) for e in range(2)]
            dtc = [_col(dt, lane_ids, h0 + e) for e in range(2)]
            last = [col[e][n - 1:n, :] for e in range(2)]
            xd = x[:, q * PAIR:(q + 1) * PAIR] * jnp.where(half, dtc[0], dtc[1])
            xd_bf = xd.astype(BF16)
            yd = []
            for e in range(2):
                lm = jnp.exp(jnp.where(causal, col[e] - row[e], -1e30))
                yd.append(_dot((gm * lm).astype(BF16), xd_bf, "nn"))
            s = s_scr[q]
            st_ref[q] = s
            ecs = jnp.where(half, jnp.exp(col[0]), jnp.exp(col[1]))
            dte = jnp.where(half, jnp.exp(last[0] - col[0]), jnp.exp(last[1] - col[1]))
            yoff = ecs * _dot(cm, s.astype(BF16), "nt")
            snew = _dot((xd * dte).astype(BF16), bm, "tn")
            s_scr[q] = s * jnp.where(half_rows, jnp.exp(last[0]), jnp.exp(last[1])) + snew
            ys.append(jnp.where(half, yd[0], yd[1]) + yoff)
        y = jnp.concatenate(ys, axis=1) + dexp_ref[...] * x
        y_ref[...] = y.astype(BF16)
        z = z_ref[...].astype(F32)
        yg = y * (z * _sigmoid(z))
        r = lax.rsqrt(jnp.mean(yg * yg, axis=1, keepdims=True) + EPS)
        yn_ref[...] = (yg * r * nw_ref[...]).astype(BF16)

    dtcol = (2 * di + 2 * bc) // LANE
    return pl.pallas_call(
        body, name=name, grid=(ng, nc),
        in_specs=[wide(0), lane(di // LANE), lane((di + bc) // LANE), fixed(dtcol), wide(0),
                  vec, vec, gvec, gvec],
        out_specs=[wide(0), wide(0), state],
        out_shape=[jax.ShapeDtypeStruct((t, di), BF16), jax.ShapeDtypeStruct((t, di), BF16),
                   jax.ShapeDtypeStruct((ng, nc, 4, PAIR, STATE), F32)],
        scratch_shapes=[pltpu.VMEM((4, PAIR, STATE), F32)],
        compiler_params=_params("parallel", "arbitrary"),
    )(xbc, xbc, xbc, zx, zx, bias, alog, dexp, nw)


def _ssd_bwd(dyn, y, xbc, zx, states, bias, alog, dexp, nw, *, name):
    t = xbc.shape[0]
    di = nw.shape[1]
    bc = (xbc.shape[1] - di) // 2
    ng, nc = di // GROUP_W, t // CHUNK
    wide, lane, fixed, vec, gvec, state = _ssd_specs(di, bc, nc, rev=True)
    acc = lambda w: pl.BlockSpec((None, 8, w), lambda g, c: (g, 0, 0))

    def body(dyn_ref, y_ref, z_ref, nw_ref, xs_ref, b_ref, c_ref, dt_ref, bias_ref, alog_ref, dexp_ref, st_ref,
             dz_ref, dxs_ref, db_ref, dc_ref, ddt_ref, small_ref, dnw_ref, ddexp_ref, ds_scr):
        g, c = pl.program_id(0), pl.program_id(1)

        @pl.when(c == 0)
        def _():
            ds_scr[...] = jnp.zeros_like(ds_scr)
            small_ref[...] = jnp.zeros_like(small_ref)
            dnw_ref[...] = jnp.zeros_like(dnw_ref)
            ddexp_ref[...] = jnp.zeros_like(ddexp_ref)

        n = CHUNK
        yv = y_ref[...].astype(F32)
        z = z_ref[...].astype(F32)
        sz = _sigmoid(z)
        silu = z * sz
        yg = yv * silu
        r = lax.rsqrt(jnp.mean(yg * yg, axis=1, keepdims=True) + EPS)
        yhat = yg * r
        dynv = dyn_ref[...].astype(F32)
        dnw_ref[0:1, :] += jnp.sum(dynv * yhat, axis=0, keepdims=True)
        dyhat = dynv * nw_ref[...]
        dyg = r * (dyhat - yhat * jnp.mean(dyhat * yhat, axis=1, keepdims=True))
        dz_ref[...] = (dyg * yv * (sz * (1.0 + z * (1.0 - sz)))).astype(BF16)
        dy = dyg * silu

        dt_in = dt_ref[...].astype(F32) + bias_ref[...]
        dt, a, cs, cst = _head_vectors(g, dt_ref[...].astype(F32), bias_ref[...], alog_ref[...])
        lane_ids = lax.broadcasted_iota(jnp.int32, (n, LANE), 1)
        sub_ids = lax.broadcasted_iota(jnp.int32, (LANE, n), 0)
        ri = lax.broadcasted_iota(jnp.int32, (n, n), 0)
        ci = lax.broadcasted_iota(jnp.int32, (n, n), 1)
        causal, causal_t = ri >= ci, ci >= ri
        is_last = lax.broadcasted_iota(jnp.int32, (n, 1), 0) == n - 1
        half = lax.broadcasted_iota(jnp.int32, (1, PAIR), 1) < HEAD_DIM
        half_rows = lax.broadcasted_iota(jnp.int32, (PAIR, 1), 0) < HEAD_DIM
        bm, cm = b_ref[...], c_ref[...]
        bf = bm.astype(F32)
        gm, gmt = _dot(cm, bm, "nt"), _dot(bm, cm, "nt")
        x = xs_ref[...].astype(F32)
        dexp = dexp_ref[...]

        dg_sum = jnp.zeros((n, n), F32)
        dgt_sum = jnp.zeros((n, n), F32)
        db_off = jnp.zeros((n, STATE), F32)
        dc_off = jnp.zeros((n, STATE), F32)
        dcs_blk = jnp.zeros((n, LANE), F32)
        ddt_blk = jnp.zeros((n, LANE), F32)
        dxs = []
        for q in range(4):
            h0 = g * 8 + 2 * q
            sl = slice(q * PAIR, (q + 1) * PAIR)
            col = [_col(cs, lane_ids, h0 + e) for e in range(2)]
            row = [_row(cst, sub_ids, h0 + e) for e in range(2)]
            dtc = [_col(dt, lane_ids, h0 + e) for e in range(2)]
            last = [col[e][n - 1:n, :] for e in range(2)]
            xp, dyp = x[:, sl], dy[:, sl]
            dtp = jnp.where(half, dtc[0], dtc[1])
            xd = xp * dtp
            xd_bf, dyp_bf = xd.astype(BF16), dyp.astype(BF16)
            ecs = jnp.where(half, jnp.exp(col[0]), jnp.exp(col[1]))
            dte = jnp.where(half, jnp.exp(last[0] - col[0]), jnp.exp(last[1] - col[1]))
            s, ds = st_ref[q], ds_scr[q]
            s_bf, ds_bf = s.astype(BF16), ds.astype(BF16)
            yoff = ecs * _dot(cm, s_bf, "nt")
            edy_bf = (ecs * dyp).astype(BF16)
            dc_off += _dot(edy_bf, s_bf, "nn")
            bds = _dot(bm, ds_bf, "nt")
            sds = s * ds
            zs = []
            for e in range(2):
                msk = half if e == 0 else jnp.logical_not(half)
                msk_rows = half_rows if e == 0 else jnp.logical_not(half_rows)
                lm = jnp.exp(jnp.where(causal, col[e] - row[e], -1e30))
                lmt = jnp.exp(jnp.where(causal_t, row[e] - col[e], -1e30))
                dym_bf = jnp.where(msk, dyp, 0.0).astype(BF16)
                xdm_bf = jnp.where(msk, xd, 0.0).astype(BF16)
                dm = _dot(dym_bf, xd_bf, "nt")
                dmt = _dot(xdm_bf, dyp_bf, "nt")
                m, mt = gm * lm, gmt * lmt
                dcs = jnp.sum(dm * m, axis=1, keepdims=True) - jnp.sum(dmt * mt, axis=1, keepdims=True)
                dg_sum += dm * lm
                dgt_sum += dmt * lmt
                zs.append(_dot(mt.astype(BF16), dyp_bf, "nn"))
                we = _dot(xdm_bf, ds_bf, "nn")
                dte_col = jnp.exp(last[e] - col[e])
                te = dte_col * jnp.sum(we * bf, axis=1, keepdims=True)
                db_off += dte_col * we
                dcs += jnp.sum(jnp.where(msk, dyp * yoff, 0.0), axis=1, keepdims=True) - te
                tail = jnp.exp(last[e]) * jnp.sum(jnp.where(msk_rows, sds, 0.0), keepdims=True) \
                    + jnp.sum(te, keepdims=True)
                dcs += jnp.where(is_last, tail, 0.0)
                dcs_blk += jnp.where(lane_ids == h0 + e, dcs, 0.0)
            dxd = jnp.where(half, zs[0], zs[1]) + dte * bds
            dxs.append(dxd * dtp + dexp[:, sl] * dyp)
            ddexp_ref[0:1, sl] += jnp.sum(dyp * xp, axis=0, keepdims=True)
            rs = dxd * xp
            for e in range(2):
                msk = half if e == 0 else jnp.logical_not(half)
                ddt_blk += jnp.where(lane_ids == h0 + e, jnp.sum(jnp.where(msk, rs, 0.0), axis=1, keepdims=True), 0.0)
            ds_scr[q] = ds * jnp.where(half_rows, jnp.exp(last[0]), jnp.exp(last[1])) + _dot(edy_bf, cm, "tn")

        dxs_ref[...] = jnp.concatenate(dxs, axis=1).astype(BF16)
        dc_ref[...] = (_dot(dg_sum.astype(BF16), bm, "nn") + dc_off).astype(BF16)
        db_ref[...] = (_dot(dgt_sum.astype(BF16), cm, "nn") + db_off).astype(BF16)
        upper = (ri <= ci).astype(BF16)
        dda = _dot_exact(upper, dcs_blk)
        ddt = dda * a + ddt_blk
        small_ref[0:1, :] += jnp.sum(dda * dt, axis=0, keepdims=True) * a
        ddt_raw = ddt * _sigmoid(dt_in)
        small_ref[1:2, :] += jnp.sum(ddt_raw, axis=0, keepdims=True)
        ddt_ref[...] = ddt_raw

    dtcol = (2 * di + 2 * bc) // LANE
    tot = 2 * di + 2 * bc + LANE
    return pl.pallas_call(
        body, name=name, grid=(ng, nc),
        in_specs=[wide(0), wide(0), wide(0), gvec, wide(0), lane(di // LANE), lane((di + bc) // LANE),
                  fixed(dtcol), vec, vec, gvec, state],
        out_specs=[wide(0), wide(0), lane(0), lane(0),
                   pl.BlockSpec((None, CHUNK, LANE), lambda g, c: (g, nc - 1 - c, 0)),
                   acc(LANE), acc(GROUP_W), acc(GROUP_W)],
        out_shape=[jax.ShapeDtypeStruct((t, tot), BF16), jax.ShapeDtypeStruct((t, di), BF16),
                   jax.ShapeDtypeStruct((t, bc), BF16), jax.ShapeDtypeStruct((t, bc), BF16),
                   jax.ShapeDtypeStruct((ng, t, LANE), F32), jax.ShapeDtypeStruct((ng, 8, LANE), F32),
                   jax.ShapeDtypeStruct((ng, 8, GROUP_W), F32), jax.ShapeDtypeStruct((ng, 8, GROUP_W), F32)],
        scratch_shapes=[pltpu.VMEM((4, PAIR, STATE), F32)],
        compiler_params=_params("parallel", "arbitrary"),
    )(dyn, y, zx, nw, xbc, xbc, xbc, zx, bias, alog, dexp, states)


HBM_ANY = pl.BlockSpec(memory_space=pl.ANY)


def _place():
    x, y, c = lax.axis_index("x"), lax.axis_index("y"), lax.axis_index("c")
    chips = [(1 - x, y), (x, 1 - y), (1 - x, 1 - y)]
    return x, y, c, chips


def _all_gather(arrs, *, name, inplace=False):
    n = len(arrs)

    def body(*refs):
        ins, outs = refs[:n], refs[n:2 * n]
        send, recv, loc = refs[2 * n:]
        x, y, c, chips = _place()
        me, sib = (x, y, c), (x, y, 1 - c)

        def blk(a, p):
            return outs[a].at[4 * p[0] + 2 * p[1] + p[2]]

        def cp(a, k, block, to, src=None):
            return pltpu.make_async_remote_copy(
                src_ref=blk(a, block) if src is None else src, dst_ref=blk(a, block),
                send_sem=send.at[a * 7 + k], recv_sem=recv.at[a * 7 + k], device_id=to, device_id_type=MESH)

        src = [None if inplace else ins[a] for a in range(n)]
        mine = [] if inplace else [pltpu.make_async_copy(ins[a], blk(a, me), loc.at[a]) for a in range(n)]
        for m in mine:
            m.start()
        started = []
        for a in range(n):
            started.append(cp(a, 0, me, sib, src=src[a]))
            started += [cp(a, 1 + j, me, (*chip, c), src=src[a]) for j, chip in enumerate(chips)]
        for s in started:
            s.start()
        for j, chip in enumerate(chips):
            for a in range(n):
                cp(a, 1 + j, (*chip, c), me).wait_recv()
                fwd = cp(a, 4 + j, (*chip, c), sib)
                fwd.start()
                started.append(fwd)
        for a in range(n):
            cp(a, 0, sib, me).wait_recv()
            for j, chip in enumerate(chips):
                cp(a, 4 + j, (*chip, 1 - c), me).wait_recv()
        for s in started:
            s.wait_send()
        for m in mine:
            m.wait()

    return pl.pallas_call(
        body, name=name,
        in_specs=[HBM_ANY] * n, out_specs=[HBM_ANY] * n,
        out_shape=[jax.ShapeDtypeStruct(a.shape if inplace else (N_DEV,) + a.shape, a.dtype) for a in arrs],
        input_output_aliases={a: a for a in range(n)} if inplace else {},
        scratch_shapes=[pltpu.SemaphoreType.DMA((7 * n,)), pltpu.SemaphoreType.DMA((7 * n,)),
                        pltpu.SemaphoreType.DMA((n,))],
    )(*arrs)


HBM_SPEC = pl.BlockSpec(memory_space=pltpu.HBM)
SEM_SPEC = pl.BlockSpec(memory_space=pltpu.SEMAPHORE)
SPLIT_EFFECT = pltpu.SideEffectType.DATAFLOW_SIDE_EFFECTING


def _split_start(arrs, plan, n_copies, after, *, name):
    m = len(arrs)

    def body(*refs):
        send, recv, token = refs[m + 1], refs[m + 2], refs[-1]
        for i, (src, dst, to) in enumerate(plan(refs[:m])):
            pltpu.make_async_remote_copy(src_ref=src, dst_ref=dst, send_sem=send.at[i], recv_sem=recv.at[i],
                                         device_id=to, device_id_type=MESH).start()
        token[...] = jnp.zeros_like(token)

    outs = pl.pallas_call(
        body, name=name,
        out_shape=(pltpu.SemaphoreType.DMA((n_copies,)), pltpu.SemaphoreType.DMA((n_copies,)),
                   *[pltpu.HBM(a.shape, a.dtype) for a in arrs], jax.ShapeDtypeStruct((8, LANE), F32)),
        in_specs=[HBM_SPEC] * m + [HBM_ANY],
        out_specs=(SEM_SPEC, SEM_SPEC, *[HBM_SPEC] * m, pl.BlockSpec(memory_space=pltpu.VMEM)),
        input_output_aliases={i: 2 + i for i in range(m)},
        compiler_params=pltpu.CompilerParams(has_side_effects=SPLIT_EFFECT),
    )(*[pltpu.with_memory_space_constraint(a, pltpu.HBM) for a in arrs], after)
    return outs[0], outs[1], list(outs[2:2 + m]), outs[-1]


def _split_wait(arrs, send, recv, after, plan, *, name):
    m = len(arrs)

    def body(*refs):
        send_ref, recv_ref = refs[m], refs[m + 1]
        for i, (src, dst, to) in enumerate(plan(refs[:m])):
            cp = pltpu.make_async_remote_copy(src_ref=src, dst_ref=dst, send_sem=send_ref.at[i],
                                              recv_sem=recv_ref.at[i], device_id=to, device_id_type=MESH)
            cp.wait_send()
            cp.wait_recv()

    outs = pl.pallas_call(
        body, name=name,
        out_shape=[pltpu.HBM(a.shape, a.dtype) for a in arrs],
        in_specs=[HBM_SPEC] * m + [SEM_SPEC, SEM_SPEC, HBM_ANY], out_specs=[HBM_SPEC] * m,
        input_output_aliases={i: i for i in range(m)},
        compiler_params=pltpu.CompilerParams(has_side_effects=SPLIT_EFFECT),
    )(*arrs, send, recv, after)
    return list(outs)


def _dev(p):
    return 4 * p[0] + 2 * p[1] + p[2]


def _plan_gather_ici(bufs):
    x, y, c, chips = _place()
    me = _dev((x, y, c))
    peers = [(x, y, 1 - c)] + [(*chip, c) for chip in chips]
    return [(b.at[me], b.at[me], p) for b in bufs for p in peers]


def _plan_gather_d2d(bufs):
    x, y, c, chips = _place()
    return [(b.at[_dev((*chip, c))], b.at[_dev((*chip, c))], (x, y, 1 - c)) for b in bufs for chip in chips]


def _plan_pair(refs):
    n = len(refs) // 2
    x, y, c, _ = _place()
    return [(refs[a].at[2 * k + 1 - c], refs[n + a].at[k], (x, y, 1 - c)) for a in range(n) for k in range(N_CHIP)]


def _plan_chip(refs):
    n = len(refs) // 2
    x, y, c, chips = _place()
    return [(refs[a].at[2 * chip[0] + chip[1]], refs[n + a].at[j], (*chip, c))
            for a in range(n) for j, chip in enumerate(chips)]


def _land(shape, dtype):
    return lax.empty(shape, dtype)


def _with_tokens(v, *tokens):
    for t in tokens:
        if t is not None:
            v = v + t[0, 0].astype(v.dtype)
    return v


def _add_pair(grad, got, core, *, name):
    k, r, c = got.shape
    tr, tc = _tile2(r, c, rows=(1024, 704, 512, 256, 128, 64, 32, 16))

    def body(core_ref, a_ref, b_ref, o_ref):
        del core_ref
        o_ref[...] = (a_ref[...].astype(F32) + b_ref[...].astype(F32)).astype(BF16)

    spec = pl.BlockSpec((None, tr, tc), lambda q, i, j, core_ref: (q, i, j))
    return pl.pallas_call(
        body, name=name,
        grid_spec=pltpu.PrefetchScalarGridSpec(
            num_scalar_prefetch=1, grid=(k, r // tr, c // tc),
            in_specs=[pl.BlockSpec((None, tr, tc), lambda q, i, j, core_ref: (2 * q + core_ref[0], i, j)), spec],
            out_specs=spec),
        out_shape=jax.ShapeDtypeStruct(got.shape, BF16),
        compiler_params=_params("parallel", "parallel", "parallel"),
    )(core, grad, got)


def _all_reduce_small(v, *, name):
    r = v.shape[0]

    def body(v_ref, o_ref, buf, send, recv):
        x, y, c, _ = _place()
        me = 4 * x + 2 * y + c
        buf[me] = v_ref[...]
        copies = []
        for rel in range(1, N_DEV):
            fx, fy, fc = rel >> 2 & 1, rel >> 1 & 1, rel & 1
            peer = ((1 - x) if fx else x, (1 - y) if fy else y, (1 - c) if fc else c)
            copies.append(pltpu.make_async_remote_copy(
                src_ref=v_ref, dst_ref=buf.at[me], send_sem=send.at[rel - 1], recv_sem=recv.at[rel - 1],
                device_id=peer, device_id_type=MESH))
        for cpy in copies:
            cpy.start()
        for cpy in copies:
            cpy.wait()
        acc = buf[0]
        for d in range(1, N_DEV):
            acc = acc + buf[d]
        o_ref[...] = acc

    return pl.pallas_call(
        body, name=name,
        in_specs=[pl.BlockSpec(memory_space=pltpu.VMEM)], out_specs=pl.BlockSpec(memory_space=pltpu.VMEM),
        out_shape=jax.ShapeDtypeStruct(v.shape, F32),
        scratch_shapes=[pltpu.VMEM((N_DEV, r, LANE), F32), pltpu.SemaphoreType.DMA((N_DEV - 1,)),
                        pltpu.SemaphoreType.DMA((N_DEV - 1,))],
        compiler_params=pltpu.CompilerParams(vmem_limit_bytes=VMEM_LIMIT),
    )(v)


def _adamw_math(w, g, m, v):
    m = ADAM_B1 * m + (1.0 - ADAM_B1) * g
    v = ADAM_B2 * v + (1.0 - ADAM_B2) * (g * g)
    m_hat = m / (1.0 - ADAM_B1 ** ADAM_STEP)
    v_hat = v / (1.0 - ADAM_B2 ** ADAM_STEP)
    delta = -ADAM_LR * (m_hat / (jnp.sqrt(v_hat) + ADAM_EPS) + ADAM_WD * w)
    return delta, m, v


def _adamw_layer(w, m, v, sums, recv, chip, layer, prev, after, *, name):
    nl, r, c = w.shape
    tr, tc = _tile2(r, c)

    def body(chip_ref, w_ref, m_ref, v_ref, s_ref, p_ref, *rest):
        del chip_ref
        g_ref, d_ref, mo_ref, vo_ref, token_ref = rest[-5:]
        g = s_ref[...].astype(F32)
        for k in range(N_CHIP - 1):
            g = g + p_ref[k].astype(F32)
        delta, mn, vn = _adamw_math(w_ref[...], g, m_ref[...], v_ref[...])
        g_ref[...] = g
        d_ref[...] = delta
        mo_ref[...] = mn
        vo_ref[...] = vn
        token_ref[...] = jnp.zeros_like(token_ref)

    lay = pl.BlockSpec((None, tr, tc), lambda i, j, chip_ref: (layer, i, j))
    ins = [w, m, v, sums, recv, after] + (list(prev) if prev is not None else [])
    in_specs = [lay, lay, lay, pl.BlockSpec((None, tr, tc), lambda i, j, chip_ref: (chip_ref[0], i, j)),
                pl.BlockSpec((N_CHIP - 1, tr, tc), lambda i, j, chip_ref: (0, i, j)), HBM_ANY]
    in_specs += [HBM_ANY] * (4 if prev is not None else 0)
    token = pl.BlockSpec((8, LANE), lambda i, j, chip_ref: (0, 0))
    outs = pl.pallas_call(
        body, name=name,
        grid_spec=pltpu.PrefetchScalarGridSpec(
            num_scalar_prefetch=1, grid=(r // tr, c // tc), in_specs=in_specs, out_specs=[lay] * 4 + [token]),
        out_shape=[jax.ShapeDtypeStruct(w.shape, F32)] * 4 + [jax.ShapeDtypeStruct((8, LANE), F32)],
        input_output_aliases={7 + q: q for q in range(4)} if prev is not None else {},
        compiler_params=_params("arbitrary", "arbitrary"),
    )(chip, *ins)
    return outs[:4], outs[4]


def _adamw_small(w, g, m, v, *, name):
    def body(w_ref, g_ref, m_ref, v_ref, d_ref, mo_ref, vo_ref):
        d_ref[...], mo_ref[...], vo_ref[...] = _adamw_math(w_ref[...], g_ref[...], m_ref[...], v_ref[...])

    vm = pl.BlockSpec(memory_space=pltpu.VMEM)
    return pl.pallas_call(
        body, name=name, in_specs=[vm] * 4, out_specs=[vm] * 3,
        out_shape=[jax.ShapeDtypeStruct(w.shape, F32)] * 3,
        compiler_params=pltpu.CompilerParams(vmem_limit_bytes=VMEM_LIMIT),
    )(w, g, m, v)


def _pack(arrs):
    flat = jnp.concatenate([a.reshape(-1).astype(F32) for a in arrs])
    pad = (-flat.shape[0]) % (8 * LANE)
    return jnp.pad(flat, (0, pad)).reshape(-1, LANE)


def _unpack(packed, shapes):
    flat = packed.reshape(-1)
    out, off = [], 0
    for s in shapes:
        size = math.prod(s)
        out.append(flat[off:off + size].reshape(s))
        off += size
    return out


WEIGHTS = ['mix_norm_w', 'ffn_norm_w', 'final_norm_w', 'ssd_w_in', 'ssd_conv_w', 'ssd_conv_b', 'ssd_dt_bias',
           'ssd_a_log', 'ssd_d', 'ssd_norm_w', 'ssd_w_out', 'sc_w_in', 'sc_conv_w', 'sc_w_out', 'ffn_w_up',
           'ffn_conv_w', 'ffn_conv_b', 'ffn_w_down']
BIG = ('ssd_w_in', 'ssd_w_out', 'sc_w_in', 'sc_w_out', 'ffn_w_up', 'ffn_w_down')
SHARDED_SMALL = ('ssd_conv_w', 'sc_conv_w', 'ffn_conv_w')


def _lane_pad(v):
    return jnp.pad(v.astype(F32), (0, LANE - v.shape[0])).reshape(1, LANE)


def _gather_cols(g):
    return jnp.moveaxis(g, 0, -2).reshape(g.shape[1:-1] + (N_DEV * g.shape[-1],))


class _Gather:
    def __init__(self, bufs, tag):
        self.bufs, self.tag = bufs, tag

    def start_ici(self, after):
        self.sems = _split_start(self.bufs, _plan_gather_ici, 4 * len(self.bufs), after, name=f"ag_ici_start_{self.tag}")
        return self.sems[3]

    def hand_on(self, after):
        send, recv, bufs, _ = self.sems
        bufs = _split_wait(bufs, send, recv, after, _plan_gather_ici, name=f"ag_ici_wait_{self.tag}")
        self.sems = _split_start(bufs, _plan_gather_d2d, 3 * len(bufs), after, name=f"ag_d2d_start_{self.tag}")
        return self.sems[3]

    def finish(self, after):
        send, recv, bufs, _ = self.sems
        return _split_wait(bufs, send, recv, after, _plan_gather_d2d, name=f"ag_d2d_wait_{self.tag}")


class _Scatter:
    def __init__(self, grads, core, tag):
        self.grads, self.core, self.tag = grads, core, tag

    def start_pair(self, after):
        lands = [_land((N_CHIP,) + g.shape[1:], g.dtype) for g in self.grads]
        self.sems = _split_start(self.grads + lands, _plan_pair, N_CHIP * len(lands), after,
                                 name=f"rs_pair_start_{self.tag}")
        return self.sems[3]

    def start_chip(self, after):
        n = len(self.grads)
        send, recv, arrs, _ = self.sems
        arrs = _split_wait(arrs, send, recv, after, _plan_pair, name=f"rs_pair_wait_{self.tag}")
        self.sums = [_add_pair(g, o, self.core, name=f"rs_add_{self.tag}{a}")
                     for a, (g, o) in enumerate(zip(arrs[:n], arrs[n:]))]
        lands = [_land((N_CHIP - 1,) + s.shape[1:], s.dtype) for s in self.sums]
        self.sems = _split_start(self.sums + lands, _plan_chip, (N_CHIP - 1) * n, after,
                                 name=f"rs_chip_start_{self.tag}")
        return self.sems[3]

    def finish(self, after):
        n = len(self.grads)
        send, recv, arrs, _ = self.sems
        arrs = _split_wait(arrs, send, recv, after, _plan_chip, name=f"rs_chip_wait_{self.tag}")
        return list(zip(arrs[:n], arrs[n:]))


def kernel(x, mix_norm_w, ffn_norm_w, final_norm_w, ssd_w_in, ssd_conv_w, ssd_conv_b, ssd_dt_bias, ssd_a_log, ssd_d, ssd_norm_w, ssd_w_out, sc_w_in, sc_conv_w, sc_w_out, ffn_w_up, ffn_conv_w, ffn_conv_b, ffn_w_down, loss_target, m_mix_norm_w, m_ffn_norm_w, m_final_norm_w, m_ssd_w_in, m_ssd_conv_w, m_ssd_conv_b, m_ssd_dt_bias, m_ssd_a_log, m_ssd_d, m_ssd_norm_w, m_ssd_w_out, m_sc_w_in, m_sc_conv_w, m_sc_w_out, m_ffn_w_up, m_ffn_conv_w, m_ffn_conv_b, m_ffn_w_down, v_mix_norm_w, v_ffn_norm_w, v_final_norm_w, v_ssd_w_in, v_ssd_conv_w, v_ssd_conv_b, v_ssd_dt_bias, v_ssd_a_log, v_ssd_d, v_ssd_norm_w, v_ssd_w_out, v_sc_w_in, v_sc_conv_w, v_sc_w_out, v_ffn_w_up, v_ffn_conv_w, v_ffn_conv_b, v_ffn_w_down):
    args = locals()
    wt = {n: args[n] for n in WEIGHTS}
    mom = {n: args["m_" + n] for n in WEIGHTS}
    var = {n: args["v_" + n] for n in WEIGHTS}
    for src in (wt, mom, var):
        src['ssd_w_in'] = jnp.swapaxes(src['ssd_w_in'], 1, 2)

    t, d = x.shape[-2], x.shape[-1]
    cur = x.reshape(t, d)
    target = loss_target.reshape(t, d)
    depth = mix_norm_w.shape[0]
    n_ssd, n_sc = ssd_w_in.shape[0], sc_w_in.shape[0]
    heads = ssd_dt_bias.shape[1]
    di = ssd_norm_w.shape[1]
    conv_dim = ssd_conv_b.shape[1]
    bc = (conv_dim - di) // 2
    in_dim = N_DEV * ssd_w_in.shape[2]
    in_pad = di + conv_dim + LANE
    ff = ffn_w_down.shape[1] * N_DEV
    me = 4 * lax.axis_index("x") + 2 * lax.axis_index("y") + lax.axis_index("c")
    me_s = me.astype(jnp.int32).reshape(1)
    core_s = lax.axis_index("c").astype(jnp.int32).reshape(1)
    chip_s = (2 * lax.axis_index("x") + lax.axis_index("y")).astype(jnp.int32).reshape(1)

    names_of = {"ssd": ('ssd_w_in', 'ssd_w_out'), "sc": ('sc_w_in', 'sc_w_out'), "ffn": ('ffn_w_up', 'ffn_w_down')}
    order = []
    for i in range(depth):
        order += [("ssd" if i % 2 == 0 else "sc", i // 2), ("ffn", i)]
    def make_gather(s, after):
        kind, idx = order[s]
        bufs = []
        for n in names_of[kind]:
            after = _cast_layer(wt[n], idx, me_s, after, name=f"cast_{n}{idx}")
            bufs.append(after)
        return _Gather(bufs, f"{kind}{idx}"), after

    conv_full = [_gather_cols(g) for g in _all_gather([wt[n] for n in SHARDED_SMALL], name="ag_conv")]
    first, last_cast = make_gather(0, conv_full[0])
    gathers = [first]
    tok_a = last_cast = first.start_ici(last_cast)
    for s in range(1, len(order)):
        g, last_cast = make_gather(s, last_cast)
        gathers.append(g)
    tok_b = gathers[0].hand_on(last_cast)
    tok_c = gathers[1].start_ici(tok_b)
    weights = [None] * len(order)
    weights[0] = gathers[0].finish(tok_c)
    ssd_cw, sc_cw, ffn_cw = conv_full
    ffn_cw = ffn_cw.reshape(depth, ffn_cw.shape[1], 2, ff)
    ffn_cb = ffn_conv_b.reshape(depth, 2, ff)
    dexp = jnp.repeat(ssd_d.astype(F32), HEAD_DIM, axis=1)

    n_sub = len(order)
    full = {n: [None] * wt[n].shape[0] for n in BIG}

    def prefetch(s, after):
        return gathers[s + 2].start_ici(after) if s + 2 < n_sub else None

    def hand_on(s, after):
        return gathers[s + 1].hand_on(after) if s + 1 < n_sub else None

    def arrive(s, after):
        if s + 1 < n_sub:
            weights[s + 1] = gathers[s + 1].finish(after)
            use(s + 1)

    def use(s):
        kind, idx = order[s]
        g_in, g_out = weights[s]
        if kind == "sc":
            g_in = jnp.swapaxes(g_in, 0, 1).reshape(d, -1)
        if kind == "ssd":
            g_in = jnp.pad(g_in.reshape(in_dim, d).T, ((0, 0), (0, in_pad - in_dim)))
        n_in, n_out = names_of[kind]
        full[n_in][idx], full[n_out][idx] = g_in, g_out.reshape(-1, d)

    use(0)
    saved = []
    for i in range(depth):
        j = i // 2
        s = 2 * i
        rec = {"x_mix": cur}
        tok = prefetch(s, cur)
        h = _rmsnorm_fwd(cur, _with_tokens(mix_norm_w[i], tok, tok_c if i == 0 else None), name=f"norm_mix{i}")
        rec["h_mix"] = h
        if i % 2 == 0:
            zx = _mm_nn(h, full['ssd_w_in'][j], out_dtype=BF16, name=f"ssd_in{j}")
            cb = _with_tokens(ssd_conv_b[j].reshape(1, conv_dim), hand_on(s, zx))
            xbc, conv_pre = _ssd_conv_fwd(zx, ssd_cw[j], cb, di, name=f"ssd_conv{j}")
            ssd_vecs = (_lane_pad(ssd_dt_bias[j]), _lane_pad(ssd_a_log[j]), dexp[j].reshape(1, di),
                        ssd_norm_w[j].reshape(1, di))
            yn, y, states = _ssd_fwd(xbc, zx, *ssd_vecs, name=f"ssd_core{j}")
            arrive(s, yn)
            cur = _mm_nn(yn, full['ssd_w_out'][j], res=cur, out_dtype=F32, name=f"ssd_out{j}")
            rec.update(zx=zx, xbc=xbc, conv_pre=conv_pre, yn=yn, y=y, states=states, vecs=ssd_vecs)
        else:
            p3 = _mm_nn(h, full['sc_w_in'][j], out_dtype=BF16, out_parts=3, name=f"sc_in{j}")
            act = _sc_act_fwd(p3, _with_tokens(sc_cw[j], hand_on(s, p3)), name=f"sc_act{j}")
            arrive(s, act)
            cur = _mm_nn(act, full['sc_w_out'][j], res=cur, out_dtype=F32, name=f"sc_out{j}")
            rec.update(p3=p3, act=act)
        s += 1
        rec["x_ffn"] = cur
        h = _rmsnorm_fwd(cur, _with_tokens(ffn_norm_w[i], prefetch(s, cur)), name=f"norm_ffn{i}")
        u3 = _lin_in_fwd(h, full['ffn_w_up'][i], 2, name=f"ffn_up{i}")
        act, pre3 = _ffn_act_fwd(u3, ffn_cw[i], _with_tokens(ffn_cb[i], hand_on(s, u3)), name=f"ffn_act{i}")
        arrive(s, act)
        cur = _mm_nn(act, full['ffn_w_down'][i], res=cur, out_dtype=F32, name=f"ffn_down{i}")
        rec.update(h_ffn=h, u3=u3, pre3=pre3, ffn_act=act)
        saved.append(rec)

    dx, dxb, dw_final, loss8 = _loss_head(cur, final_norm_w, target, name="loss_head")

    small = {n: [None] * wt[n].shape[0] for n in WEIGHTS if n not in BIG and n != 'final_norm_w'}
    scatters = [None] * n_sub
    pending = None

    def chip_step(after):
        return pending.start_chip(after) if pending is not None else None

    for i in reversed(range(depth)):
        j = i // 2
        rec = saved[i]
        nb_up = ffn_w_up.shape[2]
        da = _mm_nt(dxb, full['ffn_w_down'][i], out_dtype=BF16, name=f"ffn_down_dx{i}")
        g_down = _mm_tn(rec["ffn_act"], dxb, out_dtype=BF16, name=f"ffn_down_dw{i}")
        du3, dcw, dcb = _ffn_act_bwd(rec["u3"], rec["pre3"], da, _with_tokens(ffn_cw[i], chip_step(da)),
                                     name=f"ffn_act_bwd{i}")
        g_up = _lin_in_dw(rec["h_ffn"], du3, nb_up, name=f"ffn_up_dw{i}")
        dh = _lin_in_dx(du3, full['ffn_w_up'][i], name=f"ffn_up_dx{i}")
        pending = scatters[2 * i + 1] = _Scatter([g_up, g_down.reshape(N_DEV, ff // N_DEV, d)], core_s, f"ffn{i}")
        tok = pending.start_pair(dh)
        dx, dxb, dwn = _rmsnorm_bwd(dh, rec["x_ffn"], _with_tokens(ffn_norm_w[i], tok), dx, name=f"norm_ffn_bwd{i}")
        small['ffn_conv_w'][i] = dcw.reshape(dcw.shape[0], 2 * ff)
        small['ffn_conv_b'][i] = dcb.reshape(2 * ff)
        small['ffn_norm_w'][i] = dwn.sum(axis=0)

        if i % 2 == 0:
            zx, xbc = rec["zx"], rec["xbc"]
            cw, cpre = ssd_cw[j], rec["conv_pre"]
            dyn = _mm_nt(dxb, full['ssd_w_out'][j], out_dtype=BF16, name=f"ssd_out_dx{j}")
            g_out = _mm_tn(rec["yn"], dxb, out_dtype=BF16, name=f"ssd_out_dw{j}")
            bias_t = _with_tokens(rec["vecs"][0], chip_step(dyn))
            dzx, dxs, db, dc, ddt_g, vec_acc, dnw, ddexp = _ssd_bwd(
                dyn, rec["y"], xbc, zx, rec["states"], bias_t, *rec["vecs"][1:], name=f"ssd_core_bwd{j}")
            dzx, dcw_x, dcb_x = _ssd_conv_bwd(zx, cpre, dxs, cw, dzx, di, 0, name=f"ssd_conv_bwd_x{j}")
            dzx, dcw_b, dcb_b = _ssd_conv_bwd(zx, cpre, db, cw, dzx, di, di, name=f"ssd_conv_bwd_b{j}")
            dzx, dcw_c, dcb_c = _ssd_conv_bwd(zx, cpre, dc, cw, dzx, di, di + bc, name=f"ssd_conv_bwd_c{j}")
            dzx = _ssd_put_ddt(ddt_g, dzx, (di + conv_dim) // LANE, name=f"ssd_put_ddt{j}")
            g_in = _mm_tn(rec["h_mix"], dzx, out_dtype=BF16, name=f"ssd_in_dw{j}")
            g_in = g_in[:, :in_dim].T.reshape(N_DEV, in_dim // N_DEV, d)
            dh = _mm_nt(dzx, full['ssd_w_in'][j], out_dtype=F32, name=f"ssd_in_dx{j}")
            small['ssd_conv_w'][j] = jnp.concatenate([dcw_x, dcw_b, dcw_c], axis=1)
            small['ssd_conv_b'][j] = jnp.concatenate([dcb_x, dcb_b, dcb_c], axis=1).reshape(conv_dim)
            small['ssd_a_log'][j] = vec_acc[:, 0, :heads].sum(axis=0)
            small['ssd_dt_bias'][j] = vec_acc[:, 1, :heads].sum(axis=0)
            small['ssd_d'][j] = ddexp[:, 0, :].reshape(heads, HEAD_DIM).sum(axis=1)
            small['ssd_norm_w'][j] = dnw[:, 0, :].reshape(di)
            g_out = g_out.reshape(N_DEV, di // N_DEV, d)
        else:
            nb_in = sc_w_in.shape[2]
            da = _mm_nt(dxb, full['sc_w_out'][j], out_dtype=BF16, name=f"sc_out_dx{j}")
            g_out = _mm_tn(rec["act"], dxb, out_dtype=BF16, name=f"sc_out_dw{j}")
            dp3, dcw = _sc_act_bwd(rec["p3"], da, _with_tokens(sc_cw[j], chip_step(da)), name=f"sc_act_bwd{j}")
            g_in = _mm_tn(rec["h_mix"], dp3, out_dtype=BF16, name=f"sc_in_dw{j}")
            g_in = jnp.swapaxes(g_in.reshape(d, N_DEV, nb_in), 0, 1)
            dh = _mm_nt(dp3, full['sc_w_in'][j], out_dtype=F32, name=f"sc_in_dx{j}")
            small['sc_conv_w'][j] = dcw
            g_out = g_out.reshape(N_DEV, g_out.shape[0] // N_DEV, d)
        pending = scatters[2 * i] = _Scatter([g_in, g_out], core_s, f"{order[2 * i][0]}{j}")
        tok = pending.start_pair(dh)
        dx, dxb, dwn = _rmsnorm_bwd(dh, rec["x_mix"], _with_tokens(mix_norm_w[i], tok), dx, name=f"norm_mix_bwd{i}")
        small['mix_norm_w'][i] = dwn.sum(axis=0)
    tok_last = chip_step(dx)

    small_names = [n for n in WEIGHTS if n not in BIG]
    partial = {n: jnp.stack(small[n]) for n in small}
    partial['final_norm_w'] = dw_final.sum(axis=0)
    full_shapes = [partial[n].shape for n in small_names]
    packed = _pack([loss8.sum().reshape(1)] + [partial[n] for n in small_names])
    total = _unpack(_all_reduce_small(packed, name="ar_small"), [(1,)] + full_shapes)
    loss = total[0].reshape(())
    grads = dict(zip(small_names, total[1:]))
    for n in SHARDED_SMALL:
        nb = wt[n].shape[-1]
        grads[n] = lax.dynamic_slice_in_dim(grads[n], me * nb, nb, axis=grads[n].ndim - 1)

    delta, new_m, new_v = {}, {}, {}
    shapes = [wt[n].shape for n in small_names]
    outs = _adamw_small(*[_pack([src[n] for n in small_names]) for src in (wt, grads, mom, var)], name="adamw_small")
    for dst, packed_out in zip((delta, new_m, new_v), outs):
        dst.update(zip(small_names, _unpack(packed_out, shapes)))
    parts = {n: [None] * wt[n].shape[0] for n in BIG}
    for s in range(1, n_sub):
        kind, idx = order[s]
        parts[names_of[kind][0]][idx], parts[names_of[kind][1]][idx] = scatters[s].finish(tok_last)
    first_in, first_out = names_of[order[0][0]]
    last_out = tok_last
    jobs = [(n, layer) for n in reversed(BIG) for layer in reversed(range(wt[n].shape[0]))]
    jobs.sort(key=lambda job: parts[job[0]][job[1]] is None)
    chain = {n: None for n in BIG}
    for n, layer in jobs:
        if parts[n][layer] is None:
            parts[first_in][0], parts[first_out][0] = scatters[0].finish(last_out)
        chain[n], last_out = _adamw_layer(wt[n], mom[n], var[n], *parts[n][layer], chip_s, layer, chain[n], last_out,
                                          name=f"adamw_{n}{layer}")
    for n in BIG:
        grads[n], delta[n], new_m[n], new_v[n] = chain[n]
    for dst in (grads, delta, new_m, new_v):
        dst['ssd_w_in'] = jnp.swapaxes(dst['ssd_w_in'], 1, 2)

    return (loss, dx.reshape(x.shape), *[grads[n] for n in WEIGHTS], *[delta[n] for n in WEIGHTS],
            *[new_m[n] for n in WEIGHTS], *[new_v[n] for n in WEIGHTS])
```

```python
import functools
import math

import jax
import jax.numpy as jnp
from jax import lax
from jax.experimental import pallas as pl
from jax.experimental.pallas import tpu as pltpu

F32 = jnp.float32
BF16 = jnp.bfloat16
MESH = pl.DeviceIdType.MESH

N_DEV = 8
N_CHIP = 4
EPS = 1e-5
HEAD_DIM = 64
STATE = 128
CHUNK = 128
PAIR = 2 * HEAD_DIM
GROUP_W = 8 * HEAD_DIM
HALO = 16
LANE = 128
VMEM_LIMIT = 56 * 1024 * 1024

ADAM_LR = 0.001
ADAM_B1 = 0.9
ADAM_B2 = 0.999
ADAM_EPS = 1e-08
ADAM_WD = 0.01
ADAM_STEP = 10


def _pick(n, candidates):
    for c in candidates:
        if c <= n and n % c == 0:
            return c
    return n


OPERAND_VMEM = 36 * 1024 * 1024


def _pick_k(kd, other, candidates):
    for c in (kd,) + tuple(candidates):
        if c <= kd and kd % c == 0 and 2 * 2 * other * c <= OPERAND_VMEM:
            return c
    return kd


def _params(*sem):
    return pltpu.CompilerParams(dimension_semantics=sem, vmem_limit_bytes=VMEM_LIMIT)


def _sigmoid(x):
    return 0.5 * jnp.tanh(0.5 * x) + 0.5


_DIMS = {
    "nn": (((1,), (0,)), ((), ())),
    "nt": (((1,), (1,)), ((), ())),
    "tn": (((0,), (0,)), ((), ())),
}


def _matmul(mode, a, b, *, grid, a_spec, b_spec, o_spec, out_shape, acc_shape, name, res=None, res_spec=None,
            part_fn=None):
    nk = grid[2]
    dims = _DIMS[mode]
    if part_fn is None:
        part_fn = lambda a_ref, b_ref: lax.dot_general(a_ref[...], b_ref[...], dims, preferred_element_type=F32)

    def body(*refs):
        if res is None:
            a_ref, b_ref, o_ref = refs[:3]
            r_ref, scratch = None, refs[3:]
        else:
            a_ref, b_ref, r_ref, o_ref = refs[:4]
            scratch = refs[4:]
        part = part_fn(a_ref, b_ref)

        def finish(acc):
            if r_ref is not None:
                acc = acc + r_ref[...]
            o_ref[...] = acc.astype(o_ref.dtype)

        if nk == 1:
            finish(part)
        else:
            acc_ref = scratch[0]
            k = pl.program_id(2)

            @pl.when(k == 0)
            def _():
                acc_ref[...] = part

            @pl.when(k > 0)
            def _():
                acc_ref[...] += part

            @pl.when(k == nk - 1)
            def _():
                finish(acc_ref[...])

    in_specs = [a_spec, b_spec] + ([res_spec] if res is not None else [])
    args = (a, b) + ((res,) if res is not None else ())
    return pl.pallas_call(
        body, name=name, grid=grid, in_specs=in_specs, out_specs=o_spec, out_shape=out_shape,
        scratch_shapes=[pltpu.VMEM(acc_shape, F32)] if nk > 1 else [],
        compiler_params=_params("parallel", "parallel", "arbitrary"),
    )(*args)


def _mm_nn(a, b, *, out_dtype, res=None, out_parts=1, name):
    m, kd = a.shape
    n = b.shape[1]
    c = n // out_parts
    tm = _pick(m, (512, 256, 128))
    tn = _pick(c, (1152, 1024, 512, 384, 256, 128))
    tk = _pick_k(kd, tm + tn, (2816, 2048, 1024, 512, 256, 128))
    grid = (n // tn, m // tm, kd // tk)
    if out_parts == 1:
        o_spec = pl.BlockSpec((tm, tn), lambda j, i, k: (i, j))
        out_shape = jax.ShapeDtypeStruct((m, n), out_dtype)
    else:
        o_spec = _stacked_spec(tm, tn, c, lambda j, i, k: (i, j))
        out_shape = jax.ShapeDtypeStruct((out_parts, m, c), out_dtype)
    return _matmul(
        "nn", a, b, res=res, grid=grid, name=name,
        a_spec=pl.BlockSpec((tm, tk), lambda j, i, k: (i, k)),
        b_spec=pl.BlockSpec((tk, tn), lambda j, i, k: (k, j)),
        res_spec=pl.BlockSpec((tm, tn), lambda j, i, k: (i, j)),
        o_spec=o_spec, out_shape=out_shape, acc_shape=(tm, tn))


def _stacked_spec(rows, width, c, row_col):
    per = c // width

    def index(j, i, k):
        r, q = row_col(j, i, k)
        return q // per, r, q % per

    return pl.BlockSpec((None, rows, width), index)


def _mm_nt(a, b, *, out_dtype, name):
    stacked = a.ndim == 3
    m = a.shape[-2]
    n, kd = b.shape
    c = a.shape[-1]
    tm = _pick(m, (512, 256, 128))
    tn = _pick(n, (1408, 1024, 512, 256, 128))
    tk = _pick_k(c, tm + tn, (3456, 2816, 2048, 1024, 512, 384, 256, 128))
    grid = (n // tn, m // tm, kd // tk)
    a_spec = (_stacked_spec(tm, tk, c, lambda j, i, k: (i, k)) if stacked
              else pl.BlockSpec((tm, tk), lambda j, i, k: (i, k)))
    return _matmul(
        "nt", a, b, grid=grid, name=name, a_spec=a_spec,
        b_spec=pl.BlockSpec((tn, tk), lambda j, i, k: (j, k)),
        o_spec=pl.BlockSpec((tm, tn), lambda j, i, k: (i, j)),
        out_shape=jax.ShapeDtypeStruct((m, n), out_dtype), acc_shape=(tm, tn))


def _mm_tn(a, b, *, out_dtype, name):
    stacked = b.ndim == 3
    kd, m = a.shape
    c = b.shape[-1]
    n = c * (b.shape[0] if stacked else 1)
    tm = _pick(m, (512, 256, 128))
    tn = _pick(c, (1152, 1024, 512, 384, 256, 128))
    tk = _pick_k(kd, tm + tn, (2048, 1024, 512, 256, 128))
    grid = (n // tn, m // tm, kd // tk)
    b_spec = (_stacked_spec(tk, tn, c, lambda j, i, k: (k, j)) if stacked
              else pl.BlockSpec((tk, tn), lambda j, i, k: (k, j)))
    return _matmul(
        "tn", a, b, grid=grid, name=name,
        a_spec=pl.BlockSpec((tk, tm), lambda j, i, k: (k, i)), b_spec=b_spec,
        o_spec=pl.BlockSpec((tm, tn), lambda j, i, k: (i, j)),
        out_shape=jax.ShapeDtypeStruct((m, n), out_dtype), acc_shape=(tm, tn))


def _in_tile(nb, c):
    return math.gcd(nb, c)


def _lin_in_fwd(h, wg, parts, *, name):
    t, d = h.shape
    nb = wg.shape[2]
    c = N_DEV * nb // parts
    w = _in_tile(nb, c)
    nbw, cw = nb // w, c // w
    tm = _pick(t, (1024,) if w < 512 else (512, 256, 128))
    grid = (N_DEV * nbw, t // tm, 1)
    return _matmul(
        "nn", h, wg, grid=grid, name=name,
        a_spec=pl.BlockSpec((tm, d), lambda j, i, k: (i, 0)),
        b_spec=pl.BlockSpec((None, d, w), lambda j, i, k: (j // nbw, 0, j % nbw)),
        o_spec=pl.BlockSpec((None, tm, w), lambda j, i, k: (j // cw, i, j % cw)),
        out_shape=jax.ShapeDtypeStruct((parts, t, c), BF16), acc_shape=(tm, w))


def _lin_in_dx(dact, wg, *, name):
    parts, t, c = dact.shape
    d, nb = wg.shape[1], wg.shape[2]
    tm = _pick(t, (512, 256, 128))
    tn = _pick(d, (1024, 512, 256, 128))
    group = _pick_k(c, tm + tn, (2 * nb, nb)) // nb
    per = c // (group * nb)
    grid = (d // tn, t // tm, N_DEV // group)

    def blocks(a_ref, b_ref):
        acc = None
        for q in range(group):
            part = lax.dot_general(a_ref[:, q * nb:(q + 1) * nb], b_ref[q], _DIMS["nt"], preferred_element_type=F32)
            acc = part if acc is None else acc + part
        return acc

    return _matmul(
        "nt", dact, wg, grid=grid, name=name, part_fn=blocks,
        a_spec=pl.BlockSpec((None, tm, group * nb), lambda j, i, k: (k // per, i, k % per)),
        b_spec=pl.BlockSpec((group, tn, nb), lambda j, i, k: (k, j, 0)),
        o_spec=pl.BlockSpec((tm, tn), lambda j, i, k: (i, j)),
        out_shape=jax.ShapeDtypeStruct((t, d), F32), acc_shape=(tm, tn))


def _lin_in_dw(h, dact, nb, *, name):
    t, d = h.shape
    parts, _, c = dact.shape
    w = _in_tile(nb, c)
    nbw, cw = nb // w, c // w
    tm = _pick(d, (512, 256, 128))
    tk = _pick_k(t, tm + w, (2048, 1024, 512, 256, 128))
    grid = (N_DEV * nbw, d // tm, t // tk)
    return _matmul(
        "tn", h, dact, grid=grid, name=name,
        a_spec=pl.BlockSpec((tk, tm), lambda j, i, k: (k, i)),
        b_spec=pl.BlockSpec((None, tk, w), lambda j, i, k: (j // cw, k, j % cw)),
        o_spec=pl.BlockSpec((None, tm, w), lambda j, i, k: (j // nbw, i, j % nbw)),
        out_shape=jax.ShapeDtypeStruct((N_DEV, d, nb), BF16), acc_shape=(tm, w))


def _fold8(v):
    rows, c = v.shape
    return v.reshape(rows // 8, 8, c).sum(axis=0)


def _accumulate(ref, val, first):
    @pl.when(first)
    def _():
        ref[...] = val

    @pl.when(jnp.logical_not(first))
    def _():
        ref[...] += val


def _tile2(r, c, rows=(256, 128, 64, 32, 16)):
    tr = _pick(r, rows)
    if tr < r or r <= rows[0]:
        return tr, c
    return r, _pick(c, (256, 128))


def _cast_layer(w_stack, layer, me, after, *, name):
    _, r, c = w_stack.shape
    tr, tc = _tile2(r, c)

    def body(me_ref, w_ref, after_ref, o_ref):
        del me_ref, after_ref
        o_ref[...] = w_ref[...].astype(BF16)

    return pl.pallas_call(
        body, name=name,
        grid_spec=pltpu.PrefetchScalarGridSpec(
            num_scalar_prefetch=1, grid=(r // tr, c // tc),
            in_specs=[pl.BlockSpec((None, tr, tc), lambda i, j, me_ref: (layer, i, j)), HBM_ANY],
            out_specs=pl.BlockSpec((None, tr, tc), lambda i, j, me_ref: (me_ref[0], i, j))),
        out_shape=jax.ShapeDtypeStruct((N_DEV, r, c), BF16),
        compiler_params=_params("parallel", "parallel"),
    )(me, w_stack, after)


def _rmsnorm_fwd(x, w, *, name):
    t, d = x.shape
    tt = _pick(t, (256, 128))

    def body(x_ref, w_ref, o_ref):
        xv = x_ref[...]
        r = lax.rsqrt(jnp.mean(xv * xv, axis=1, keepdims=True) + EPS)
        o_ref[...] = (xv * r * w_ref[...]).astype(BF16)

    return pl.pallas_call(
        body, name=name, grid=(t // tt,),
        in_specs=[pl.BlockSpec((tt, d), lambda i: (i, 0)), pl.BlockSpec((1, d), lambda i: (0, 0))],
        out_specs=pl.BlockSpec((tt, d), lambda i: (i, 0)),
        out_shape=jax.ShapeDtypeStruct((t, d), BF16),
        compiler_params=_params("parallel"),
    )(x, w.reshape(1, d))


def _rmsnorm_bwd(dh, x, w, dres, *, name):
    t, d = x.shape
    tt = _pick(t, (256, 128))

    def body(dh_ref, x_ref, w_ref, dres_ref, dx_ref, dxb_ref, dw_ref):
        xv = x_ref[...]
        r = lax.rsqrt(jnp.mean(xv * xv, axis=1, keepdims=True) + EPS)
        xhat = xv * r
        dhv = dh_ref[...].astype(F32)
        dxhat = dhv * w_ref[...]
        dx = dres_ref[...] + r * (dxhat - xhat * jnp.mean(dxhat * xhat, axis=1, keepdims=True))
        dx_ref[...] = dx
        dxb_ref[...] = dx.astype(BF16)
        _accumulate(dw_ref, _fold8(dhv * xhat), pl.program_id(0) == 0)

    row = pl.BlockSpec((tt, d), lambda i: (i, 0))
    return pl.pallas_call(
        body, name=name, grid=(t // tt,),
        in_specs=[row, row, pl.BlockSpec((1, d), lambda i: (0, 0)), row],
        out_specs=[row, row, pl.BlockSpec((8, d), lambda i: (0, 0))],
        out_shape=[jax.ShapeDtypeStruct((t, d), F32), jax.ShapeDtypeStruct((t, d), BF16),
                   jax.ShapeDtypeStruct((8, d), F32)],
        compiler_params=_params("arbitrary"),
    )(dh, x, w.reshape(1, d), dres)


def _loss_head(x, w, target, *, name):
    t, d = x.shape
    tt = _pick(t, (256, 128))

    def body(x_ref, w_ref, tg_ref, dx_ref, dxb_ref, dw_ref, ls_ref):
        xv = x_ref[...]
        wv = w_ref[...]
        r = lax.rsqrt(jnp.mean(xv * xv, axis=1, keepdims=True) + EPS)
        xhat = xv * r
        err = xhat * wv - tg_ref[...]
        dy = err * (1.0 / d)
        dxhat = dy * wv
        dx = r * (dxhat - xhat * jnp.mean(dxhat * xhat, axis=1, keepdims=True))
        dx_ref[...] = dx
        dxb_ref[...] = dx.astype(BF16)
        first = pl.program_id(0) == 0
        _accumulate(dw_ref, _fold8(dy * xhat), first)
        _accumulate(ls_ref, _fold8(err * err) * (0.5 / d), first)

    row = pl.BlockSpec((tt, d), lambda i: (i, 0))
    acc = pl.BlockSpec((8, d), lambda i: (0, 0))
    return pl.pallas_call(
        body, name=name, grid=(t // tt,),
        in_specs=[row, pl.BlockSpec((1, d), lambda i: (0, 0)), row],
        out_specs=[row, row, acc, acc],
        out_shape=[jax.ShapeDtypeStruct((t, d), F32), jax.ShapeDtypeStruct((t, d), BF16),
                   jax.ShapeDtypeStruct((8, d), F32), jax.ShapeDtypeStruct((8, d), F32)],
        compiler_params=_params("arbitrary"),
    )(x, w.reshape(1, d), target)


def _conv_causal(e, tap, width):
    acc = None
    for k in range(width):
        s = width - 1 - k
        term = (e if s == 0 else pltpu.roll(e, s, 0)) * tap(k)
        acc = term if acc is None else acc + term
    return acc


def _conv_anticausal(e, tap, width):
    rows = e.shape[0]
    acc = None
    for k in range(width):
        s = width - 1 - k
        term = (e if s == 0 else pltpu.roll(e, rows - s, 0)) * tap(k)
        acc = term if acc is None else acc + term
    return acc


def _extend(prev, cur, nxt, first, last):
    parts = []
    if prev is not None:
        parts.append(jnp.where(first, 0.0, prev.astype(F32)))
    parts.append(cur.astype(F32))
    if nxt is not None:
        parts.append(jnp.where(last, 0.0, nxt.astype(F32)))
    return jnp.concatenate(parts, axis=0)


def _prev_idx(i, tt):
    return jnp.maximum(i * (tt // HALO) - 1, 0)


def _next_idx(i, tt, t):
    return jnp.minimum((i + 1) * (tt // HALO), t // HALO - 1)


def _ffn_act_fwd(u3, cw, cb, *, name):
    _, t, f = u3.shape
    tt = _pick(t, (512, 256, 128))
    tc = _pick(f, (512, 256, 128))
    width = cw.shape[0]

    def body(u_ref, up_ref, w_ref, b_ref, o_ref, pre_ref):
        first = pl.program_id(1) == 0
        pre = []
        for p in range(2):
            e = _extend(up_ref[p], u_ref[p], None, first, None)
            pre.append(_conv_causal(e, lambda k: w_ref[k, p:p + 1, :], width)[HALO:] + b_ref[p:p + 1, :])
            pre_ref[p] = pre[p].astype(BF16)
        g, v = pre
        o_ref[...] = (g * _sigmoid(g) * v).astype(BF16)

    return pl.pallas_call(
        body, name=name, grid=(f // tc, t // tt),
        in_specs=[pl.BlockSpec((2, tt, tc), lambda j, i: (0, i, j)),
                  pl.BlockSpec((2, HALO, tc), lambda j, i: (0, _prev_idx(i, tt), j)),
                  pl.BlockSpec((width, 2, tc), lambda j, i: (0, 0, j)),
                  pl.BlockSpec((2, tc), lambda j, i: (0, j))],
        out_specs=[pl.BlockSpec((tt, tc), lambda j, i: (i, j)), pl.BlockSpec((2, tt, tc), lambda j, i: (0, i, j))],
        out_shape=[jax.ShapeDtypeStruct((t, f), BF16), jax.ShapeDtypeStruct((2, t, f), BF16)],
        compiler_params=_params("parallel", "parallel"),
    )(u3, u3, cw, cb)


def _ffn_act_bwd(u3, pre3, da, cw, *, name):
    _, t, f = u3.shape
    tt = _pick(t, (512, 256, 128))
    tc = _pick(f, (512, 256, 128))
    width = cw.shape[0]
    nt = t // tt
    rows = tt + HALO

    def body(u_ref, pre_ref, pren_ref, da_ref, dan_ref, w_ref, du_ref, dcw_ref, dcb_ref):
        i = pl.program_id(1)
        first, last = i == 0, i == nt - 1
        g, v = (_extend(None, pre_ref[p], pren_ref[p], None, False) for p in range(2))
        dae = _extend(None, da_ref[...], dan_ref[...], None, last)
        sg = _sigmoid(g)
        dpre = (dae * v * (sg * (1.0 + g * (1.0 - sg))), dae * (g * sg))

        @pl.when(first)
        def _():
            dcw_ref[...] = jnp.zeros_like(dcw_ref)
            dcb_ref[...] = jnp.zeros_like(dcb_ref)

        for p in range(2):
            u = u_ref[p].astype(F32)
            du = None
            for k in range(width):
                s = width - 1 - k
                d = (dpre[p] if s == 0 else pltpu.roll(dpre[p], rows - s, 0))[:tt]
                term = d * w_ref[k, p:p + 1, :]
                du = term if du is None else du + term
                dcw_ref[k, p:p + 1, :] += jnp.sum(d * u, axis=0, keepdims=True)
                if s == 0:
                    dcb_ref[p:p + 1, :] += jnp.sum(d, axis=0, keepdims=True)
            du_ref[p] = du.astype(BF16)

    cur3 = pl.BlockSpec((2, tt, tc), lambda j, i: (0, i, j))
    return pl.pallas_call(
        body, name=name, grid=(f // tc, nt),
        in_specs=[cur3, cur3,
                  pl.BlockSpec((2, HALO, tc), lambda j, i: (0, _next_idx(i, tt, t), j)),
                  pl.BlockSpec((tt, tc), lambda j, i: (i, j)),
                  pl.BlockSpec((HALO, tc), lambda j, i: (_next_idx(i, tt, t), j)),
                  pl.BlockSpec((width, 2, tc), lambda j, i: (0, 0, j))],
        out_specs=[cur3,
                   pl.BlockSpec((width, 2, tc), lambda j, i: (0, 0, j)),
                   pl.BlockSpec((2, tc), lambda j, i: (0, j))],
        out_shape=[jax.ShapeDtypeStruct((2, t, f), BF16), jax.ShapeDtypeStruct((width, 2, f), F32),
                   jax.ShapeDtypeStruct((2, f), F32)],
        compiler_params=_params("parallel", "arbitrary"),
    )(u3, pre3, pre3, da, da, cw)


def _sc_act_fwd(p3, cw, *, name):
    _, t, c = p3.shape
    tt = _pick(t, (512, 256, 128))
    tc = _pick(c, (512, 256, 128))
    width = cw.shape[0]

    def body(p_ref, pp_ref, w_ref, o_ref):
        first = pl.program_id(1) == 0
        q = _extend(pp_ref[1], p_ref[1], None, first, None) * _extend(pp_ref[2], p_ref[2], None, first, None)
        cq = _conv_causal(q, lambda k: w_ref[k:k + 1, :], width)[HALO:]
        o_ref[...] = (p_ref[0].astype(F32) * cq).astype(BF16)

    return pl.pallas_call(
        body, name=name, grid=(c // tc, t // tt),
        in_specs=[pl.BlockSpec((3, tt, tc), lambda j, i: (0, i, j)),
                  pl.BlockSpec((3, HALO, tc), lambda j, i: (0, _prev_idx(i, tt), j)),
                  pl.BlockSpec((width, tc), lambda j, i: (0, j))],
        out_specs=pl.BlockSpec((tt, tc), lambda j, i: (i, j)),
        out_shape=jax.ShapeDtypeStruct((t, c), BF16),
        compiler_params=_params("parallel", "parallel"),
    )(p3, p3, cw)


def _sc_act_bwd(p3, da, cw, *, name):
    _, t, c = p3.shape
    tt = _pick(t, (512, 256, 128))
    tc = _pick(c, (512, 256, 128))
    width = cw.shape[0]
    nt = t // tt
    ctr = slice(HALO, HALO + tt)

    def body(p_ref, pp_ref, pn_ref, da_ref, dan_ref, w_ref, dp_ref, dcw_ref):
        i = pl.program_id(1)
        first, last = i == 0, i == nt - 1
        tap = lambda k: w_ref[k:k + 1, :]
        bg, cg, hh = (_extend(pp_ref[p], p_ref[p], pn_ref[p], first, last) for p in range(3))
        q = cg * hh
        cq = _conv_causal(q, tap, width)
        dae = _extend(jnp.zeros((HALO, tc), F32), da_ref[...], dan_ref[...], False, last)
        dcq = dae * bg
        dq = _conv_anticausal(dcq, tap, width)[ctr]
        dp_ref[0] = (dae * cq)[ctr].astype(BF16)
        dp_ref[1] = (dq * hh[ctr]).astype(BF16)
        dp_ref[2] = (dq * cg[ctr]).astype(BF16)

        @pl.when(first)
        def _():
            dcw_ref[...] = jnp.zeros_like(dcw_ref)

        dc = dcq[ctr]
        for k in range(width):
            s = width - 1 - k
            qs = (q if s == 0 else pltpu.roll(q, s, 0))[ctr]
            dcw_ref[k:k + 1, :] += jnp.sum(dc * qs, axis=0, keepdims=True)

    return pl.pallas_call(
        body, name=name, grid=(c // tc, nt),
        in_specs=[pl.BlockSpec((3, tt, tc), lambda j, i: (0, i, j)),
                  pl.BlockSpec((3, HALO, tc), lambda j, i: (0, _prev_idx(i, tt), j)),
                  pl.BlockSpec((3, HALO, tc), lambda j, i: (0, _next_idx(i, tt, t), j)),
                  pl.BlockSpec((tt, tc), lambda j, i: (i, j)),
                  pl.BlockSpec((HALO, tc), lambda j, i: (_next_idx(i, tt, t), j)),
                  pl.BlockSpec((width, tc), lambda j, i: (0, j))],
        out_specs=[pl.BlockSpec((3, tt, tc), lambda j, i: (0, i, j)),
                   pl.BlockSpec((width, tc), lambda j, i: (0, j))],
        out_shape=[jax.ShapeDtypeStruct((3, t, c), BF16), jax.ShapeDtypeStruct((width, c), F32)],
        compiler_params=_params("parallel", "arbitrary"),
    )(p3, p3, p3, da, da, cw)


def _ssd_conv_fwd(zx, cw, cb, col0, *, name):
    t = zx.shape[0]
    width, c = cw.shape
    tt = _pick(t, (512, 256, 128))
    tc = _pick(math.gcd(c, col0), (512, 256, 128))
    off = col0 // tc

    def body(x_ref, xp_ref, w_ref, b_ref, o_ref, pre_ref):
        first = pl.program_id(1) == 0
        e = _extend(xp_ref[...], x_ref[...], None, first, None)
        pre = _conv_causal(e, lambda k: w_ref[k:k + 1, :], width)[HALO:] + b_ref[...]
        pre_ref[...] = pre.astype(BF16)
        o_ref[...] = (pre * _sigmoid(pre)).astype(BF16)

    out = pl.BlockSpec((tt, tc), lambda j, i: (i, j))
    return pl.pallas_call(
        body, name=name, grid=(c // tc, t // tt),
        in_specs=[pl.BlockSpec((tt, tc), lambda j, i: (i, off + j)),
                  pl.BlockSpec((HALO, tc), lambda j, i: (_prev_idx(i, tt), off + j)),
                  pl.BlockSpec((width, tc), lambda j, i: (0, j)),
                  pl.BlockSpec((1, tc), lambda j, i: (0, j))],
        out_specs=[out, out],
        out_shape=[jax.ShapeDtypeStruct((t, c), BF16)] * 2,
        compiler_params=_params("parallel", "parallel"),
    )(zx, zx, cw, cb)


def _ssd_conv_bwd(zx, pre, dxc, cw, dzx, col0, woff, *, name):
    t = zx.shape[0]
    width = cw.shape[0]
    c = dxc.shape[1]
    tt = _pick(t, (512, 256, 128))
    tc = _pick(math.gcd(math.gcd(c, col0), woff) if woff else math.gcd(c, col0), (512, 256, 128))
    nt = t // tt
    xoff, wo = (col0 + woff) // tc, woff // tc
    rows = tt + HALO

    def body(x_ref, p_ref, pn_ref, d_ref, dn_ref, w_ref, dzx_in, dzx_ref, dcw_ref, dcb_ref):
        del dzx_in
        i = pl.program_id(1)
        first, last = i == 0, i == nt - 1
        pre_e = _extend(None, p_ref[...], pn_ref[...], None, False)
        de = _extend(None, d_ref[...], dn_ref[...], None, last)
        sg = _sigmoid(pre_e)
        dpre = de * (sg * (1.0 + pre_e * (1.0 - sg)))

        @pl.when(first)
        def _():
            dcw_ref[...] = jnp.zeros_like(dcw_ref)
            dcb_ref[...] = jnp.zeros_like(dcb_ref)

        xv = x_ref[...].astype(F32)
        dx = None
        for k in range(width):
            s = width - 1 - k
            d = (dpre if s == 0 else pltpu.roll(dpre, rows - s, 0))[:tt]
            term = d * w_ref[k:k + 1, :]
            dx = term if dx is None else dx + term
            dcw_ref[k:k + 1, :] += jnp.sum(d * xv, axis=0, keepdims=True)
            if s == 0:
                dcb_ref[...] += jnp.sum(d, axis=0, keepdims=True)
        dzx_ref[...] = dx.astype(BF16)

    return pl.pallas_call(
        body, name=name, grid=(c // tc, nt),
        in_specs=[pl.BlockSpec((tt, tc), lambda j, i: (i, xoff + j)),
                  pl.BlockSpec((tt, tc), lambda j, i: (i, wo + j)),
                  pl.BlockSpec((HALO, tc), lambda j, i: (_next_idx(i, tt, t), wo + j)),
                  pl.BlockSpec((tt, tc), lambda j, i: (i, j)),
                  pl.BlockSpec((HALO, tc), lambda j, i: (_next_idx(i, tt, t), j)),
                  pl.BlockSpec((width, tc), lambda j, i: (0, wo + j)),
                  pl.BlockSpec(memory_space=pl.ANY)],
        out_specs=[pl.BlockSpec((tt, tc), lambda j, i: (i, xoff + j)),
                   pl.BlockSpec((width, tc), lambda j, i: (0, j)),
                   pl.BlockSpec((1, tc), lambda j, i: (0, j))],
        out_shape=[jax.ShapeDtypeStruct(dzx.shape, dzx.dtype), jax.ShapeDtypeStruct((width, c), F32),
                   jax.ShapeDtypeStruct((1, c), F32)],
        input_output_aliases={6: 0},
        compiler_params=_params("parallel", "arbitrary"),
    )(zx, pre, pre, dxc, dxc, cw, dzx)


def _ssd_put_ddt(ddt_g, dzx, col, *, name):
    g, t, _ = ddt_g.shape
    tt = _pick(t, (512, 256, 128))

    def body(d_ref, dzx_in, dzx_ref):
        del dzx_in
        dzx_ref[...] = jnp.sum(d_ref[...], axis=0).astype(BF16)

    return pl.pallas_call(
        body, name=name, grid=(t // tt,),
        in_specs=[pl.BlockSpec((g, tt, LANE), lambda i: (0, i, 0)), pl.BlockSpec(memory_space=pl.ANY)],
        out_specs=pl.BlockSpec((tt, LANE), lambda i: (i, col)),
        out_shape=jax.ShapeDtypeStruct(dzx.shape, dzx.dtype),
        input_output_aliases={1: 0},
        compiler_params=_params("parallel"),
    )(ddt_g, dzx)


def _dot(a, b, mode):
    return lax.dot_general(a, b, _DIMS[mode], preferred_element_type=F32)


def _dot_exact(m01, v, mode="nn"):
    hi = v.astype(BF16)
    r1 = v - hi.astype(F32)
    mid = r1.astype(BF16)
    lo = (r1 - mid.astype(F32)).astype(BF16)
    return _dot(m01, hi, mode) + _dot(m01, mid, mode) + _dot(m01, lo, mode)


def _softplus(x):
    return jnp.maximum(x, 0.0) + jnp.log(1.0 + jnp.exp(-jnp.abs(x)))


def _head_vectors(g, dt_raw, bias, alog):
    n = CHUNK
    dt = _softplus(dt_raw + bias)
    a = -jnp.exp(alog)
    tri = (lax.broadcasted_iota(jnp.int32, (n, n), 0) >= lax.broadcasted_iota(jnp.int32, (n, n), 1)).astype(BF16)
    cs = _dot_exact(tri, dt * a)
    return dt, a, cs, cs.T


def _col(v, lane_ids, h):
    return jnp.sum(jnp.where(lane_ids == h, v, 0.0), axis=1, keepdims=True)


def _row(vt, sub_ids, h):
    return jnp.sum(jnp.where(sub_ids == h, vt, 0.0), axis=0, keepdims=True)


def _ssd_specs(di, bc, nc, rev):
    cidx = (lambda c: nc - 1 - c) if rev else (lambda c: c)
    wide = lambda off: pl.BlockSpec((CHUNK, GROUP_W), lambda g, c: (cidx(c), off + g))
    lane = lambda off: pl.BlockSpec((CHUNK, LANE), lambda g, c: (cidx(c), off + g))
    fixed = lambda off: pl.BlockSpec((CHUNK, LANE), lambda g, c: (cidx(c), off))
    vec = pl.BlockSpec((1, LANE), lambda g, c: (0, 0))
    gvec = pl.BlockSpec((1, GROUP_W), lambda g, c: (0, g))
    state = pl.BlockSpec((None, None, 4, PAIR, STATE), lambda g, c: (g, cidx(c), 0, 0, 0))
    return wide, lane, fixed, vec, gvec, state


def _ssd_fwd(xbc, zx, bias, alog, dexp, nw, *, name):
    t = xbc.shape[0]
    di = nw.shape[1]
    bc = (xbc.shape[1] - di) // 2
    ng, nc = di // GROUP_W, t // CHUNK
    wide, lane, fixed, vec, gvec, state = _ssd_specs(di, bc, nc, rev=False)

    def body(xs_ref, b_ref, c_ref, dt_ref, z_ref, bias_ref, alog_ref, dexp_ref, nw_ref,
             yn_ref, y_ref, st_ref, s_scr):
        g, c = pl.program_id(0), pl.program_id(1)

        @pl.when(c == 0)
        def _():
            s_scr[...] = jnp.zeros_like(s_scr)

        n = CHUNK
        dt, a, cs, cst = _head_vectors(g, dt_ref[...].astype(F32), bias_ref[...], alog_ref[...])
        lane_ids = lax.broadcasted_iota(jnp.int32, (n, LANE), 1)
        sub_ids = lax.broadcasted_iota(jnp.int32, (LANE, n), 0)
        causal = lax.broadcasted_iota(jnp.int32, (n, n), 0) >= lax.broadcasted_iota(jnp.int32, (n, n), 1)
        half = lax.broadcasted_iota(jnp.int32, (1, PAIR), 1) < HEAD_DIM
        half_rows = lax.broadcasted_iota(jnp.int32, (PAIR, 1), 0) < HEAD_DIM
        bm, cm = b_ref[...], c_ref[...]
        gm = _dot(cm, bm, "nt")
        x = xs_ref[...].astype(F32)
        ys = []
        for q in range(4):
            h0 = g * 8 + 2 * q
            col = [_col(cs, lane_ids, h0 + e) for e in range(2)]
            row = [_row(cst, sub_ids, h0 + e) for e in range(2)]
            dtc = [_col(dt, lane_ids, h0 + e) for e in range(2)]
            last = [col[e][n - 1:n, :] for e in range(2)]
            xd = x[:, q * PAIR:(q + 1) * PAIR] * jnp.where(half, dtc[0], dtc[1])
            xd_bf = xd.astype(BF16)
            yd = []
            for e in range(2):
                lm = jnp.exp(jnp.where(causal, col[e] - row[e], -1e30))
                yd.append(_dot((gm * lm).astype(BF16), xd_bf, "nn"))
            s = s_scr[q]
            st_ref[q] = s
            ecs = jnp.where(half, jnp.exp(col[0]), jnp.exp(col[1]))
            dte = jnp.where(half, jnp.exp(last[0] - col[0]), jnp.exp(last[1] - col[1]))
            yoff = ecs * _dot(cm, s.astype(BF16), "nt")
            snew = _dot((xd * dte).astype(BF16), bm, "tn")
            s_scr[q] = s * jnp.where(half_rows, jnp.exp(last[0]), jnp.exp(last[1])) + snew
            ys.append(jnp.where(half, yd[0], yd[1]) + yoff)
        y = jnp.concatenate(ys, axis=1) + dexp_ref[...] * x
        y_ref[...] = y.astype(BF16)
        z = z_ref[...].astype(F32)
        yg = y * (z * _sigmoid(z))
        r = lax.rsqrt(jnp.mean(yg * yg, axis=1, keepdims=True) + EPS)
        yn_ref[...] = (yg * r * nw_ref[...]).astype(BF16)

    dtcol = (2 * di + 2 * bc) // LANE
    return pl.pallas_call(
        body, name=name, grid=(ng, nc),
        in_specs=[wide(0), lane(di // LANE), lane((di + bc) // LANE), fixed(dtcol), wide(0),
                  vec, vec, gvec, gvec],
        out_specs=[wide(0), wide(0), state],
        out_shape=[jax.ShapeDtypeStruct((t, di), BF16), jax.ShapeDtypeStruct((t, di), BF16),
                   jax.ShapeDtypeStruct((ng, nc, 4, PAIR, STATE), F32)],
        scratch_shapes=[pltpu.VMEM((4, PAIR, STATE), F32)],
        compiler_params=_params("parallel", "arbitrary"),
    )(xbc, xbc, xbc, zx, zx, bias, alog, dexp, nw)


def _ssd_bwd(dyn, y, xbc, zx, states, bias, alog, dexp, nw, *, name):
    t = xbc.shape[0]
    di = nw.shape[1]
    bc = (xbc.shape[1] - di) // 2
    ng, nc = di // GROUP_W, t // CHUNK
    wide, lane, fixed, vec, gvec, state = _ssd_specs(di, bc, nc, rev=True)
    acc = lambda w: pl.BlockSpec((None, 8, w), lambda g, c: (g, 0, 0))

    def body(dyn_ref, y_ref, z_ref, nw_ref, xs_ref, b_ref, c_ref, dt_ref, bias_ref, alog_ref, dexp_ref, st_ref,
             dz_ref, dxs_ref, db_ref, dc_ref, ddt_ref, small_ref, dnw_ref, ddexp_ref, ds_scr):
        g, c = pl.program_id(0), pl.program_id(1)

        @pl.when(c == 0)
        def _():
            ds_scr[...] = jnp.zeros_like(ds_scr)
            small_ref[...] = jnp.zeros_like(small_ref)
            dnw_ref[...] = jnp.zeros_like(dnw_ref)
            ddexp_ref[...] = jnp.zeros_like(ddexp_ref)

        n = CHUNK
        yv = y_ref[...].astype(F32)
        z = z_ref[...].astype(F32)
        sz = _sigmoid(z)
        silu = z * sz
        yg = yv * silu
        r = lax.rsqrt(jnp.mean(yg * yg, axis=1, keepdims=True) + EPS)
        yhat = yg * r
        dynv = dyn_ref[...].astype(F32)
        dnw_ref[0:1, :] += jnp.sum(dynv * yhat, axis=0, keepdims=True)
        dyhat = dynv * nw_ref[...]
        dyg = r * (dyhat - yhat * jnp.mean(dyhat * yhat, axis=1, keepdims=True))
        dz_ref[...] = (dyg * yv * (sz * (1.0 + z * (1.0 - sz)))).astype(BF16)
        dy = dyg * silu

        dt_in = dt_ref[...].astype(F32) + bias_ref[...]
        dt, a, cs, cst = _head_vectors(g, dt_ref[...].astype(F32), bias_ref[...], alog_ref[...])
        lane_ids = lax.broadcasted_iota(jnp.int32, (n, LANE), 1)
        sub_ids = lax.broadcasted_iota(jnp.int32, (LANE, n), 0)
        ri = lax.broadcasted_iota(jnp.int32, (n, n), 0)
        ci = lax.broadcasted_iota(jnp.int32, (n, n), 1)
        causal, causal_t = ri >= ci, ci >= ri
        is_last = lax.broadcasted_iota(jnp.int32, (n, 1), 0) == n - 1
        half = lax.broadcasted_iota(jnp.int32, (1, PAIR), 1) < HEAD_DIM
        half_rows = lax.broadcasted_iota(jnp.int32, (PAIR, 1), 0) < HEAD_DIM
        bm, cm = b_ref[...], c_ref[...]
        bf = bm.astype(F32)
        gm, gmt = _dot(cm, bm, "nt"), _dot(bm, cm, "nt")
        x = xs_ref[...].astype(F32)
        dexp = dexp_ref[...]

        dg_sum = jnp.zeros((n, n), F32)
        dgt_sum = jnp.zeros((n, n), F32)
        db_off = jnp.zeros((n, STATE), F32)
        dc_off = jnp.zeros((n, STATE), F32)
        dcs_blk = jnp.zeros((n, LANE), F32)
        ddt_blk = jnp.zeros((n, LANE), F32)
        dxs = []
        for q in range(4):
            h0 = g * 8 + 2 * q
            sl = slice(q * PAIR, (q + 1) * PAIR)
            col = [_col(cs, lane_ids, h0 + e) for e in range(2)]
            row = [_row(cst, sub_ids, h0 + e) for e in range(2)]
            dtc = [_col(dt, lane_ids, h0 + e) for e in range(2)]
            last = [col[e][n - 1:n, :] for e in range(2)]
            xp, dyp = x[:, sl], dy[:, sl]
            dtp = jnp.where(half, dtc[0], dtc[1])
            xd = xp * dtp
            xd_bf, dyp_bf = xd.astype(BF16), dyp.astype(BF16)
            ecs = jnp.where(half, jnp.exp(col[0]), jnp.exp(col[1]))
            dte = jnp.where(half, jnp.exp(last[0] - col[0]), jnp.exp(last[1] - col[1]))
            s, ds = st_ref[q], ds_scr[q]
            s_bf, ds_bf = s.astype(BF16), ds.astype(BF16)
            yoff = ecs * _dot(cm, s_bf, "nt")
            edy_bf = (ecs * dyp).astype(BF16)
            dc_off += _dot(edy_bf, s_bf, "nn")
            bds = _dot(bm, ds_bf, "nt")
            sds = s * ds
            zs = []
            for e in range(2):
                msk = half if e == 0 else jnp.logical_not(half)
                msk_rows = half_rows if e == 0 else jnp.logical_not(half_rows)
                lm = jnp.exp(jnp.where(causal, col[e] - row[e], -1e30))
                lmt = jnp.exp(jnp.where(causal_t, row[e] - col[e], -1e30))
                dym_bf = jnp.where(msk, dyp, 0.0).astype(BF16)
                xdm_bf = jnp.where(msk, xd, 0.0).astype(BF16)
                dm = _dot(dym_bf, xd_bf, "nt")
                dmt = _dot(xdm_bf, dyp_bf, "nt")
                m, mt = gm * lm, gmt * lmt
                dcs = jnp.sum(dm * m, axis=1, keepdims=True) - jnp.sum(dmt * mt, axis=1, keepdims=True)
                dg_sum += dm * lm
                dgt_sum += dmt * lmt
                zs.append(_dot(mt.astype(BF16), dyp_bf, "nn"))
                we = _dot(xdm_bf, ds_bf, "nn")
                dte_col = jnp.exp(last[e] - col[e])
                te = dte_col * jnp.sum(we * bf, axis=1, keepdims=True)
                db_off += dte_col * we
                dcs += jnp.sum(jnp.where(msk, dyp * yoff, 0.0), axis=1, keepdims=True) - te
                tail = jnp.exp(last[e]) * jnp.sum(jnp.where(msk_rows, sds, 0.0), keepdims=True) \
                    + jnp.sum(te, keepdims=True)
                dcs += jnp.where(is_last, tail, 0.0)
                dcs_blk += jnp.where(lane_ids == h0 + e, dcs, 0.0)
            dxd = jnp.where(half, zs[0], zs[1]) + dte * bds
            dxs.append(dxd * dtp + dexp[:, sl] * dyp)
            ddexp_ref[0:1, sl] += jnp.sum(dyp * xp, axis=0, keepdims=True)
            rs = dxd * xp
            for e in range(2):
                msk = half if e == 0 else jnp.logical_not(half)
                ddt_blk += jnp.where(lane_ids == h0 + e, jnp.sum(jnp.where(msk, rs, 0.0), axis=1, keepdims=True), 0.0)
            ds_scr[q] = ds * jnp.where(half_rows, jnp.exp(last[0]), jnp.exp(last[1])) + _dot(edy_bf, cm, "tn")

        dxs_ref[...] = jnp.concatenate(dxs, axis=1).astype(BF16)
        dc_ref[...] = (_dot(dg_sum.astype(BF16), bm, "nn") + dc_off).astype(BF16)
        db_ref[...] = (_dot(dgt_sum.astype(BF16), cm, "nn") + db_off).astype(BF16)
        upper = (ri <= ci).astype(BF16)
        dda = _dot_exact(upper, dcs_blk)
        ddt = dda * a + ddt_blk
        small_ref[0:1, :] += jnp.sum(dda * dt, axis=0, keepdims=True) * a
        ddt_raw = ddt * _sigmoid(dt_in)
        small_ref[1:2, :] += jnp.sum(ddt_raw, axis=0, keepdims=True)
        ddt_ref[...] = ddt_raw

    dtcol = (2 * di + 2 * bc) // LANE
    tot = 2 * di + 2 * bc + LANE
    return pl.pallas_call(
        body, name=name, grid=(ng, nc),
        in_specs=[wide(0), wide(0), wide(0), gvec, wide(0), lane(di // LANE), lane((di + bc) // LANE),
                  fixed(dtcol), vec, vec, gvec, state],
        out_specs=[wide(0), wide(0), lane(0), lane(0),
                   pl.BlockSpec((None, CHUNK, LANE), lambda g, c: (g, nc - 1 - c, 0)),
                   acc(LANE), acc(GROUP_W), acc(GROUP_W)],
        out_shape=[jax.ShapeDtypeStruct((t, tot), BF16), jax.ShapeDtypeStruct((t, di), BF16),
                   jax.ShapeDtypeStruct((t, bc), BF16), jax.ShapeDtypeStruct((t, bc), BF16),
                   jax.ShapeDtypeStruct((ng, t, LANE), F32), jax.ShapeDtypeStruct((ng, 8, LANE), F32),
                   jax.ShapeDtypeStruct((ng, 8, GROUP_W), F32), jax.ShapeDtypeStruct((ng, 8, GROUP_W), F32)],
        scratch_shapes=[pltpu.VMEM((4, PAIR, STATE), F32)],
        compiler_params=_params("parallel", "arbitrary"),
    )(dyn, y, zx, nw, xbc, xbc, xbc, zx, bias, alog, dexp, states)


HBM_ANY = pl.BlockSpec(memory_space=pl.ANY)


def _place():
    x, y, c = lax.axis_index("x"), lax.axis_index("y"), lax.axis_index("c")
    chips = [(1 - x, y), (x, 1 - y), (1 - x, 1 - y)]
    return x, y, c, chips


def _all_gather(arrs, *, name, inplace=False):
    n = len(arrs)

    def body(*refs):
        ins, outs = refs[:n], refs[n:2 * n]
        send, recv, loc = refs[2 * n:]
        x, y, c, chips = _place()
        me, sib = (x, y, c), (x, y, 1 - c)

        def blk(a, p):
            return outs[a].at[4 * p[0] + 2 * p[1] + p[2]]

        def cp(a, k, block, to, src=None):
            return pltpu.make_async_remote_copy(
                src_ref=blk(a, block) if src is None else src, dst_ref=blk(a, block),
                send_sem=send.at[a * 7 + k], recv_sem=recv.at[a * 7 + k], device_id=to, device_id_type=MESH)

        src = [None if inplace else ins[a] for a in range(n)]
        mine = [] if inplace else [pltpu.make_async_copy(ins[a], blk(a, me), loc.at[a]) for a in range(n)]
        for m in mine:
            m.start()
        started = []
        for a in range(n):
            started.append(cp(a, 0, me, sib, src=src[a]))
            started += [cp(a, 1 + j, me, (*chip, c), src=src[a]) for j, chip in enumerate(chips)]
        for s in started:
            s.start()
        for j, chip in enumerate(chips):
            for a in range(n):
                cp(a, 1 + j, (*chip, c), me).wait_recv()
                fwd = cp(a, 4 + j, (*chip, c), sib)
                fwd.start()
                started.append(fwd)
        for a in range(n):
            cp(a, 0, sib, me).wait_recv()
            for j, chip in enumerate(chips):
                cp(a, 4 + j, (*chip, 1 - c), me).wait_recv()
        for s in started:
            s.wait_send()
        for m in mine:
            m.wait()

    return pl.pallas_call(
        body, name=name,
        in_specs=[HBM_ANY] * n, out_specs=[HBM_ANY] * n,
        out_shape=[jax.ShapeDtypeStruct(a.shape if inplace else (N_DEV,) + a.shape, a.dtype) for a in arrs],
        input_output_aliases={a: a for a in range(n)} if inplace else {},
        scratch_shapes=[pltpu.SemaphoreType.DMA((7 * n,)), pltpu.SemaphoreType.DMA((7 * n,)),
                        pltpu.SemaphoreType.DMA((n,))],
    )(*arrs)


HBM_SPEC = pl.BlockSpec(memory_space=pltpu.HBM)
SEM_SPEC = pl.BlockSpec(memory_space=pltpu.SEMAPHORE)
SPLIT_EFFECT = pltpu.SideEffectType.DATAFLOW_SIDE_EFFECTING


def _split_start(arrs, plan, n_copies, after, *, name):
    m = len(arrs)

    def body(*refs):
        send, recv, token = refs[m + 1], refs[m + 2], refs[-1]
        for i, (src, dst, to) in enumerate(plan(refs[:m])):
            pltpu.make_async_remote_copy(src_ref=src, dst_ref=dst, send_sem=send.at[i], recv_sem=recv.at[i],
                                         device_id=to, device_id_type=MESH).start()
        token[...] = jnp.zeros_like(token)

    outs = pl.pallas_call(
        body, name=name,
        out_shape=(pltpu.SemaphoreType.DMA((n_copies,)), pltpu.SemaphoreType.DMA((n_copies,)),
                   *[pltpu.HBM(a.shape, a.dtype) for a in arrs], jax.ShapeDtypeStruct((8, LANE), F32)),
        in_specs=[HBM_SPEC] * m + [HBM_ANY],
        out_specs=(SEM_SPEC, SEM_SPEC, *[HBM_SPEC] * m, pl.BlockSpec(memory_space=pltpu.VMEM)),
        input_output_aliases={i: 2 + i for i in range(m)},
        compiler_params=pltpu.CompilerParams(has_side_effects=SPLIT_EFFECT),
    )(*[pltpu.with_memory_space_constraint(a, pltpu.HBM) for a in arrs], after)
    return outs[0], outs[1], list(outs[2:2 + m]), outs[-1]


def _split_wait(arrs, send, recv, after, plan, *, name):
    m = len(arrs)

    def body(*refs):
        send_ref, recv_ref = refs[m], refs[m + 1]
        for i, (src, dst, to) in enumerate(plan(refs[:m])):
            cp = pltpu.make_async_remote_copy(src_ref=src, dst_ref=dst, send_sem=send_ref.at[i],
                                              recv_sem=recv_ref.at[i], device_id=to, device_id_type=MESH)
            cp.wait_send()
            cp.wait_recv()

    outs = pl.pallas_call(
        body, name=name,
        out_shape=[pltpu.HBM(a.shape, a.dtype) for a in arrs],
        in_specs=[HBM_SPEC] * m + [SEM_SPEC, SEM_SPEC, HBM_ANY], out_specs=[HBM_SPEC] * m,
        input_output_aliases={i: i for i in range(m)},
        compiler_params=pltpu.CompilerParams(has_side_effects=SPLIT_EFFECT),
    )(*arrs, send, recv, after)
    return list(outs)


def _dev(p):
    return 4 * p[0] + 2 * p[1] + p[2]


def _plan_gather_ici(bufs):
    x, y, c, chips = _place()
    me = _dev((x, y, c))
    peers = [(x, y, 1 - c)] + [(*chip, c) for chip in chips]
    return [(b.at[me], b.at[me], p) for b in bufs for p in peers]


def _plan_gather_d2d(bufs):
    x, y, c, chips = _place()
    return [(b.at[_dev((*chip, c))], b.at[_dev((*chip, c))], (x, y, 1 - c)) for b in bufs for chip in chips]


def _plan_pair(refs):
    n = len(refs) // 2
    x, y, c, _ = _place()
    return [(refs[a].at[2 * k + 1 - c], refs[n + a].at[k], (x, y, 1 - c)) for a in range(n) for k in range(N_CHIP)]


def _plan_chip(refs):
    n = len(refs) // 2
    x, y, c, chips = _place()
    return [(refs[a].at[2 * chip[0] + chip[1]], refs[n + a].at[j], (*chip, c))
            for a in range(n) for j, chip in enumerate(chips)]


def _land(shape, dtype):
    return lax.empty(shape, dtype)


def _with_tokens(v, *tokens):
    for t in tokens:
        if t is not None:
            v = v + t[0, 0].astype(v.dtype)
    return v


def _add_pair(grad, got, core, *, name):
    k, r, c = got.shape
    tr, tc = _tile2(r, c, rows=(1024, 704, 512, 256, 128, 64, 32, 16))

    def body(core_ref, a_ref, b_ref, o_ref):
        del core_ref
        o_ref[...] = (a_ref[...].astype(F32) + b_ref[...].astype(F32)).astype(BF16)

    spec = pl.BlockSpec((None, tr, tc), lambda q, i, j, core_ref: (q, i, j))
    return pl.pallas_call(
        body, name=name,
        grid_spec=pltpu.PrefetchScalarGridSpec(
            num_scalar_prefetch=1, grid=(k, r // tr, c // tc),
            in_specs=[pl.BlockSpec((None, tr, tc), lambda q, i, j, core_ref: (2 * q + core_ref[0], i, j)), spec],
            out_specs=spec),
        out_shape=jax.ShapeDtypeStruct(got.shape, BF16),
        compiler_params=_params("parallel", "parallel", "parallel"),
    )(core, grad, got)


def _plan_all(refs):
    x, y, c, _ = _place()
    me = _dev((x, y, c))
    plan = []
    for rel in range(1, N_DEV):
        fx, fy, fc = rel >> 2 & 1, rel >> 1 & 1, rel & 1
        plan.append((refs[0], refs[1].at[me], ((1 - x) if fx else x, (1 - y) if fy else y, (1 - c) if fc else c)))
    return plan


def _sum_slots(v, land, me, *, name):
    def body(me_ref, v_ref, land_ref, o_ref):
        acc = None
        for dev in range(N_DEV):
            term = jnp.where(me_ref[0] == dev, v_ref[...], land_ref[dev])
            acc = term if acc is None else acc + term
        o_ref[...] = acc

    vm = pl.BlockSpec(memory_space=pltpu.VMEM)
    return pl.pallas_call(
        body, name=name,
        grid_spec=pltpu.PrefetchScalarGridSpec(num_scalar_prefetch=1, grid=(), in_specs=[vm, vm], out_specs=vm),
        out_shape=jax.ShapeDtypeStruct(v.shape, F32),
        compiler_params=pltpu.CompilerParams(vmem_limit_bytes=VMEM_LIMIT),
    )(me, v, land)


def _adamw_math(w, g, m, v):
    m = ADAM_B1 * m + (1.0 - ADAM_B1) * g
    v = ADAM_B2 * v + (1.0 - ADAM_B2) * (g * g)
    m_hat = m / (1.0 - ADAM_B1 ** ADAM_STEP)
    v_hat = v / (1.0 - ADAM_B2 ** ADAM_STEP)
    delta = -ADAM_LR * (m_hat / (jnp.sqrt(v_hat) + ADAM_EPS) + ADAM_WD * w)
    return delta, m, v


def _adamw_layer(w, m, v, sums, recv, chip, layer, prev, after, *, name):
    nl, r, c = w.shape
    tr, tc = _tile2(r, c)

    def body(chip_ref, w_ref, m_ref, v_ref, s_ref, p_ref, *rest):
        del chip_ref
        g_ref, d_ref, mo_ref, vo_ref, token_ref = rest[-5:]
        g = s_ref[...].astype(F32)
        for k in range(N_CHIP - 1):
            g = g + p_ref[k].astype(F32)
        delta, mn, vn = _adamw_math(w_ref[...], g, m_ref[...], v_ref[...])
        g_ref[...] = g
        d_ref[...] = delta
        mo_ref[...] = mn
        vo_ref[...] = vn
        token_ref[...] = jnp.zeros_like(token_ref)

    lay = pl.BlockSpec((None, tr, tc), lambda i, j, chip_ref: (layer, i, j))
    ins = [w, m, v, sums, recv, after] + (list(prev) if prev is not None else [])
    in_specs = [lay, lay, lay, pl.BlockSpec((None, tr, tc), lambda i, j, chip_ref: (chip_ref[0], i, j)),
                pl.BlockSpec((N_CHIP - 1, tr, tc), lambda i, j, chip_ref: (0, i, j)), HBM_ANY]
    in_specs += [HBM_ANY] * (4 if prev is not None else 0)
    token = pl.BlockSpec((8, LANE), lambda i, j, chip_ref: (0, 0))
    outs = pl.pallas_call(
        body, name=name,
        grid_spec=pltpu.PrefetchScalarGridSpec(
            num_scalar_prefetch=1, grid=(r // tr, c // tc), in_specs=in_specs, out_specs=[lay] * 4 + [token]),
        out_shape=[jax.ShapeDtypeStruct(w.shape, F32)] * 4 + [jax.ShapeDtypeStruct((8, LANE), F32)],
        input_output_aliases={7 + q: q for q in range(4)} if prev is not None else {},
        compiler_params=_params("arbitrary", "arbitrary"),
    )(chip, *ins)
    return outs[:4], outs[4]


def _adamw_small(w, g, m, v, *, name):
    def body(w_ref, g_ref, m_ref, v_ref, d_ref, mo_ref, vo_ref):
        d_ref[...], mo_ref[...], vo_ref[...] = _adamw_math(w_ref[...], g_ref[...], m_ref[...], v_ref[...])

    vm = pl.BlockSpec(memory_space=pltpu.VMEM)
    return pl.pallas_call(
        body, name=name, in_specs=[vm] * 4, out_specs=[vm] * 3,
        out_shape=[jax.ShapeDtypeStruct(w.shape, F32)] * 3,
        compiler_params=pltpu.CompilerParams(vmem_limit_bytes=VMEM_LIMIT),
    )(w, g, m, v)


def _pack(arrs):
    flat = jnp.concatenate([a.reshape(-1).astype(F32) for a in arrs])
    pad = (-flat.shape[0]) % (8 * LANE)
    return jnp.pad(flat, (0, pad)).reshape(-1, LANE)


def _unpack(packed, shapes):
    flat = packed.reshape(-1)
    out, off = [], 0
    for s in shapes:
        size = math.prod(s)
        out.append(flat[off:off + size].reshape(s))
        off += size
    return out


WEIGHTS = ['mix_norm_w', 'ffn_norm_w', 'final_norm_w', 'ssd_w_in', 'ssd_conv_w', 'ssd_conv_b', 'ssd_dt_bias',
           'ssd_a_log', 'ssd_d', 'ssd_norm_w', 'ssd_w_out', 'sc_w_in', 'sc_conv_w', 'sc_w_out', 'ffn_w_up',
           'ffn_conv_w', 'ffn_conv_b', 'ffn_w_down']
BIG = ('ssd_w_in', 'ssd_w_out', 'sc_w_in', 'sc_w_out', 'ffn_w_up', 'ffn_w_down')
SHARDED_SMALL = ('ssd_conv_w', 'sc_conv_w', 'ffn_conv_w')


def _lane_pad(v):
    return jnp.pad(v.astype(F32), (0, LANE - v.shape[0])).reshape(1, LANE)


def _gather_cols(g):
    return jnp.moveaxis(g, 0, -2).reshape(g.shape[1:-1] + (N_DEV * g.shape[-1],))


class _Gather:
    def __init__(self, bufs, tag):
        self.bufs, self.tag = bufs, tag

    def start_ici(self, after):
        self.sems = _split_start(self.bufs, _plan_gather_ici, 4 * len(self.bufs), after, name=f"ag_ici_start_{self.tag}")
        return self.sems[3]

    def hand_on(self, after):
        send, recv, bufs, _ = self.sems
        bufs = _split_wait(bufs, send, recv, after, _plan_gather_ici, name=f"ag_ici_wait_{self.tag}")
        self.sems = _split_start(bufs, _plan_gather_d2d, 3 * len(bufs), after, name=f"ag_d2d_start_{self.tag}")
        return self.sems[3]

    def finish(self, after):
        send, recv, bufs, _ = self.sems
        return _split_wait(bufs, send, recv, after, _plan_gather_d2d, name=f"ag_d2d_wait_{self.tag}")


class _Scatter:
    def __init__(self, grads, core, tag):
        self.grads, self.core, self.tag = grads, core, tag

    def start_pair(self, after):
        lands = [_land((N_CHIP,) + g.shape[1:], g.dtype) for g in self.grads]
        self.sems = _split_start(self.grads + lands, _plan_pair, N_CHIP * len(lands), after,
                                 name=f"rs_pair_start_{self.tag}")
        return self.sems[3]

    def start_chip(self, after):
        n = len(self.grads)
        send, recv, arrs, _ = self.sems
        arrs = _split_wait(arrs, send, recv, after, _plan_pair, name=f"rs_pair_wait_{self.tag}")
        self.sums = [_add_pair(g, o, self.core, name=f"rs_add_{self.tag}{a}")
                     for a, (g, o) in enumerate(zip(arrs[:n], arrs[n:]))]
        lands = [_land((N_CHIP - 1,) + s.shape[1:], s.dtype) for s in self.sums]
        self.sems = _split_start(self.sums + lands, _plan_chip, (N_CHIP - 1) * n, after,
                                 name=f"rs_chip_start_{self.tag}")
        return self.sems[3]

    def finish(self, after):
        n = len(self.grads)
        send, recv, arrs, _ = self.sems
        arrs = _split_wait(arrs, send, recv, after, _plan_chip, name=f"rs_chip_wait_{self.tag}")
        return list(zip(arrs[:n], arrs[n:]))


def kernel(x, mix_norm_w, ffn_norm_w, final_norm_w, ssd_w_in, ssd_conv_w, ssd_conv_b, ssd_dt_bias, ssd_a_log, ssd_d, ssd_norm_w, ssd_w_out, sc_w_in, sc_conv_w, sc_w_out, ffn_w_up, ffn_conv_w, ffn_conv_b, ffn_w_down, loss_target, m_mix_norm_w, m_ffn_norm_w, m_final_norm_w, m_ssd_w_in, m_ssd_conv_w, m_ssd_conv_b, m_ssd_dt_bias, m_ssd_a_log, m_ssd_d, m_ssd_norm_w, m_ssd_w_out, m_sc_w_in, m_sc_conv_w, m_sc_w_out, m_ffn_w_up, m_ffn_conv_w, m_ffn_conv_b, m_ffn_w_down, v_mix_norm_w, v_ffn_norm_w, v_final_norm_w, v_ssd_w_in, v_ssd_conv_w, v_ssd_conv_b, v_ssd_dt_bias, v_ssd_a_log, v_ssd_d, v_ssd_norm_w, v_ssd_w_out, v_sc_w_in, v_sc_conv_w, v_sc_w_out, v_ffn_w_up, v_ffn_conv_w, v_ffn_conv_b, v_ffn_w_down):
    args = locals()
    wt = {n: args[n] for n in WEIGHTS}
    mom = {n: args["m_" + n] for n in WEIGHTS}
    var = {n: args["v_" + n] for n in WEIGHTS}
    for src in (wt, mom, var):
        src['ssd_w_in'] = jnp.swapaxes(src['ssd_w_in'], 1, 2)

    t, d = x.shape[-2], x.shape[-1]
    cur = x.reshape(t, d)
    target = loss_target.reshape(t, d)
    depth = mix_norm_w.shape[0]
    n_ssd, n_sc = ssd_w_in.shape[0], sc_w_in.shape[0]
    heads = ssd_dt_bias.shape[1]
    di = ssd_norm_w.shape[1]
    conv_dim = ssd_conv_b.shape[1]
    bc = (conv_dim - di) // 2
    in_dim = N_DEV * ssd_w_in.shape[2]
    in_pad = di + conv_dim + LANE
    ff = ffn_w_down.shape[1] * N_DEV
    me = 4 * lax.axis_index("x") + 2 * lax.axis_index("y") + lax.axis_index("c")
    me_s = me.astype(jnp.int32).reshape(1)
    core_s = lax.axis_index("c").astype(jnp.int32).reshape(1)
    chip_s = (2 * lax.axis_index("x") + lax.axis_index("y")).astype(jnp.int32).reshape(1)

    names_of = {"ssd": ('ssd_w_in', 'ssd_w_out'), "sc": ('sc_w_in', 'sc_w_out'), "ffn": ('ffn_w_up', 'ffn_w_down')}
    order = []
    for i in range(depth):
        order += [("ssd" if i % 2 == 0 else "sc", i // 2), ("ffn", i)]
    def make_gather(s, after):
        kind, idx = order[s]
        bufs = []
        for n in names_of[kind]:
            after = _cast_layer(wt[n], idx, me_s, after, name=f"cast_{n}{idx}")
            bufs.append(after)
        return _Gather(bufs, f"{kind}{idx}"), after

    conv_full = [_gather_cols(g) for g in _all_gather([wt[n] for n in SHARDED_SMALL], name="ag_conv")]
    first, last_cast = make_gather(0, conv_full[0])
    gathers = [first]
    tok_a = last_cast = first.start_ici(last_cast)
    for s in range(1, len(order)):
        g, last_cast = make_gather(s, last_cast)
        gathers.append(g)
    tok_b = gathers[0].hand_on(last_cast)
    tok_c = gathers[1].start_ici(tok_b)
    weights = [None] * len(order)
    weights[0] = gathers[0].finish(tok_c)
    ssd_cw, sc_cw, ffn_cw = conv_full
    ffn_cw = ffn_cw.reshape(depth, ffn_cw.shape[1], 2, ff)
    ffn_cb = ffn_conv_b.reshape(depth, 2, ff)
    dexp = jnp.repeat(ssd_d.astype(F32), HEAD_DIM, axis=1)

    n_sub = len(order)
    full = {n: [None] * wt[n].shape[0] for n in BIG}

    def prefetch(s, after):
        return gathers[s + 2].start_ici(after) if s + 2 < n_sub else None

    def hand_on(s, after):
        return gathers[s + 1].hand_on(after) if s + 1 < n_sub else None

    def arrive(s, after):
        if s + 1 < n_sub:
            weights[s + 1] = gathers[s + 1].finish(after)
            use(s + 1)

    def use(s):
        kind, idx = order[s]
        g_in, g_out = weights[s]
        if kind == "sc":
            g_in = jnp.swapaxes(g_in, 0, 1).reshape(d, -1)
        if kind == "ssd":
            g_in = jnp.pad(g_in.reshape(in_dim, d).T, ((0, 0), (0, in_pad - in_dim)))
        n_in, n_out = names_of[kind]
        full[n_in][idx], full[n_out][idx] = g_in, g_out.reshape(-1, d)

    use(0)
    saved = []
    for i in range(depth):
        j = i // 2
        s = 2 * i
        rec = {"x_mix": cur}
        tok = prefetch(s, cur)
        h = _rmsnorm_fwd(cur, _with_tokens(mix_norm_w[i], tok, tok_c if i == 0 else None), name=f"norm_mix{i}")
        rec["h_mix"] = h
        if i % 2 == 0:
            zx = _mm_nn(h, full['ssd_w_in'][j], out_dtype=BF16, name=f"ssd_in{j}")
            cb = _with_tokens(ssd_conv_b[j].reshape(1, conv_dim), hand_on(s, zx))
            xbc, conv_pre = _ssd_conv_fwd(zx, ssd_cw[j], cb, di, name=f"ssd_conv{j}")
            ssd_vecs = (_lane_pad(ssd_dt_bias[j]), _lane_pad(ssd_a_log[j]), dexp[j].reshape(1, di),
                        ssd_norm_w[j].reshape(1, di))
            yn, y, states = _ssd_fwd(xbc, zx, *ssd_vecs, name=f"ssd_core{j}")
            arrive(s, yn)
            cur = _mm_nn(yn, full['ssd_w_out'][j], res=cur, out_dtype=F32, name=f"ssd_out{j}")
            rec.update(zx=zx, xbc=xbc, conv_pre=conv_pre, yn=yn, y=y, states=states, vecs=ssd_vecs)
        else:
            p3 = _mm_nn(h, full['sc_w_in'][j], out_dtype=BF16, out_parts=3, name=f"sc_in{j}")
            act = _sc_act_fwd(p3, _with_tokens(sc_cw[j], hand_on(s, p3)), name=f"sc_act{j}")
            arrive(s, act)
            cur = _mm_nn(act, full['sc_w_out'][j], res=cur, out_dtype=F32, name=f"sc_out{j}")
            rec.update(p3=p3, act=act)
        s += 1
        rec["x_ffn"] = cur
        h = _rmsnorm_fwd(cur, _with_tokens(ffn_norm_w[i], prefetch(s, cur)), name=f"norm_ffn{i}")
        u3 = _lin_in_fwd(h, full['ffn_w_up'][i], 2, name=f"ffn_up{i}")
        act, pre3 = _ffn_act_fwd(u3, ffn_cw[i], _with_tokens(ffn_cb[i], hand_on(s, u3)), name=f"ffn_act{i}")
        arrive(s, act)
        cur = _mm_nn(act, full['ffn_w_down'][i], res=cur, out_dtype=F32, name=f"ffn_down{i}")
        rec.update(h_ffn=h, u3=u3, pre3=pre3, ffn_act=act)
        saved.append(rec)

    dx, dxb, dw_final, loss8 = _loss_head(cur, final_norm_w, target, name="loss_head")

    small = {n: [None] * wt[n].shape[0] for n in WEIGHTS if n not in BIG and n != 'final_norm_w'}
    scatters = [None] * n_sub
    pending = None

    def chip_step(after):
        return pending.start_chip(after) if pending is not None else None

    for i in reversed(range(depth)):
        j = i // 2
        rec = saved[i]
        nb_up = ffn_w_up.shape[2]
        da = _mm_nt(dxb, full['ffn_w_down'][i], out_dtype=BF16, name=f"ffn_down_dx{i}")
        g_down = _mm_tn(rec["ffn_act"], dxb, out_dtype=BF16, name=f"ffn_down_dw{i}")
        du3, dcw, dcb = _ffn_act_bwd(rec["u3"], rec["pre3"], da, _with_tokens(ffn_cw[i], chip_step(da)),
                                     name=f"ffn_act_bwd{i}")
        g_up = _lin_in_dw(rec["h_ffn"], du3, nb_up, name=f"ffn_up_dw{i}")
        dh = _lin_in_dx(du3, full['ffn_w_up'][i], name=f"ffn_up_dx{i}")
        pending = scatters[2 * i + 1] = _Scatter([g_up, g_down.reshape(N_DEV, ff // N_DEV, d)], core_s, f"ffn{i}")
        tok = pending.start_pair(dh)
        dx, dxb, dwn = _rmsnorm_bwd(dh, rec["x_ffn"], _with_tokens(ffn_norm_w[i], tok), dx, name=f"norm_ffn_bwd{i}")
        small['ffn_conv_w'][i] = dcw.reshape(dcw.shape[0], 2 * ff)
        small['ffn_conv_b'][i] = dcb.reshape(2 * ff)
        small['ffn_norm_w'][i] = dwn.sum(axis=0)

        if i % 2 == 0:
            zx, xbc = rec["zx"], rec["xbc"]
            cw, cpre = ssd_cw[j], rec["conv_pre"]
            dyn = _mm_nt(dxb, full['ssd_w_out'][j], out_dtype=BF16, name=f"ssd_out_dx{j}")
            g_out = _mm_tn(rec["yn"], dxb, out_dtype=BF16, name=f"ssd_out_dw{j}")
            bias_t = _with_tokens(rec["vecs"][0], chip_step(dyn))
            dzx, dxs, db, dc, ddt_g, vec_acc, dnw, ddexp = _ssd_bwd(
                dyn, rec["y"], xbc, zx, rec["states"], bias_t, *rec["vecs"][1:], name=f"ssd_core_bwd{j}")
            dzx, dcw_x, dcb_x = _ssd_conv_bwd(zx, cpre, dxs, cw, dzx, di, 0, name=f"ssd_conv_bwd_x{j}")
            dzx, dcw_b, dcb_b = _ssd_conv_bwd(zx, cpre, db, cw, dzx, di, di, name=f"ssd_conv_bwd_b{j}")
            dzx, dcw_c, dcb_c = _ssd_conv_bwd(zx, cpre, dc, cw, dzx, di, di + bc, name=f"ssd_conv_bwd_c{j}")
            dzx = _ssd_put_ddt(ddt_g, dzx, (di + conv_dim) // LANE, name=f"ssd_put_ddt{j}")
            g_in = _mm_tn(rec["h_mix"], dzx, out_dtype=BF16, name=f"ssd_in_dw{j}")
            g_in = g_in[:, :in_dim].T.reshape(N_DEV, in_dim // N_DEV, d)
            dh = _mm_nt(dzx, full['ssd_w_in'][j], out_dtype=F32, name=f"ssd_in_dx{j}")
            small['ssd_conv_w'][j] = jnp.concatenate([dcw_x, dcw_b, dcw_c], axis=1)
            small['ssd_conv_b'][j] = jnp.concatenate([dcb_x, dcb_b, dcb_c], axis=1).reshape(conv_dim)
            small['ssd_a_log'][j] = vec_acc[:, 0, :heads].sum(axis=0)
            small['ssd_dt_bias'][j] = vec_acc[:, 1, :heads].sum(axis=0)
            small['ssd_d'][j] = ddexp[:, 0, :].reshape(heads, HEAD_DIM).sum(axis=1)
            small['ssd_norm_w'][j] = dnw[:, 0, :].reshape(di)
            g_out = g_out.reshape(N_DEV, di // N_DEV, d)
        else:
            nb_in = sc_w_in.shape[2]
            da = _mm_nt(dxb, full['sc_w_out'][j], out_dtype=BF16, name=f"sc_out_dx{j}")
            g_out = _mm_tn(rec["act"], dxb, out_dtype=BF16, name=f"sc_out_dw{j}")
            dp3, dcw = _sc_act_bwd(rec["p3"], da, _with_tokens(sc_cw[j], chip_step(da)), name=f"sc_act_bwd{j}")
            g_in = _mm_tn(rec["h_mix"], dp3, out_dtype=BF16, name=f"sc_in_dw{j}")
            g_in = jnp.swapaxes(g_in.reshape(d, N_DEV, nb_in), 0, 1)
            dh = _mm_nt(dp3, full['sc_w_in'][j], out_dtype=F32, name=f"sc_in_dx{j}")
            small['sc_conv_w'][j] = dcw
            g_out = g_out.reshape(N_DEV, g_out.shape[0] // N_DEV, d)
        pending = scatters[2 * i] = _Scatter([g_in, g_out], core_s, f"{order[2 * i][0]}{j}")
        tok = pending.start_pair(dh)
        dx, dxb, dwn = _rmsnorm_bwd(dh, rec["x_mix"], _with_tokens(mix_norm_w[i], tok), dx, name=f"norm_mix_bwd{i}")
        small['mix_norm_w'][i] = dwn.sum(axis=0)
    tok_last = chip_step(dx)

    small_names = [n for n in WEIGHTS if n not in BIG]
    partial = {n: jnp.stack(small[n]) for n in small}
    partial['final_norm_w'] = dw_final.sum(axis=0)
    full_shapes = [partial[n].shape for n in small_names]
    packed = _pack([loss8.sum().reshape(1)] + [partial[n] for n in small_names])
    ar_send, ar_recv, ar_bufs, tok_ar = _split_start(
        [packed, jnp.zeros((N_DEV,) + packed.shape, F32)], _plan_all, N_DEV - 1, tok_last, name="ar_small_start")

    grads, delta, new_m, new_v = {}, {}, {}, {}
    parts = {n: [None] * wt[n].shape[0] for n in BIG}
    for s in range(1, n_sub):
        kind, idx = order[s]
        parts[names_of[kind][0]][idx], parts[names_of[kind][1]][idx] = scatters[s].finish(tok_ar)
    first_in, first_out = names_of[order[0][0]]
    last_out = tok_ar
    jobs = [(n, layer) for n in reversed(BIG) for layer in reversed(range(wt[n].shape[0]))]
    jobs.sort(key=lambda job: parts[job[0]][job[1]] is None)
    chain = {n: None for n in BIG}
    for n, layer in jobs:
        if parts[n][layer] is None:
            parts[first_in][0], parts[first_out][0] = scatters[0].finish(last_out)
        chain[n], last_out = _adamw_layer(wt[n], mom[n], var[n], *parts[n][layer], chip_s, layer, chain[n], last_out,
                                          name=f"adamw_{n}{layer}")
    for n in BIG:
        grads[n], delta[n], new_m[n], new_v[n] = chain[n]
    for dst in (grads, delta, new_m, new_v):
        dst['ssd_w_in'] = jnp.swapaxes(dst['ssd_w_in'], 1, 2)

    mine, slots = _split_wait(ar_bufs, ar_send, ar_recv, last_out, _plan_all, name="ar_small_wait")
    total = _unpack(_sum_slots(mine, slots, me_s, name="ar_small_sum"), [(1,)] + full_shapes)
    loss = total[0].reshape(())
    grads.update(zip(small_names, total[1:]))
    for n in SHARDED_SMALL:
        nb = wt[n].shape[-1]
        grads[n] = lax.dynamic_slice_in_dim(grads[n], me * nb, nb, axis=grads[n].ndim - 1)
    shapes = [wt[n].shape for n in small_names]
    outs = _adamw_small(*[_pack([src[n] for n in small_names]) for src in (wt, grads, mom, var)], name="adamw_small")
    for dst, packed_out in zip((delta, new_m, new_v), outs):
        dst.update(zip(small_names, _unpack(packed_out, shapes)))

    return (loss, dx.reshape(x.shape), *[grads[n] for n in WEIGHTS], *[delta[n] for n in WEIGHTS],
            *[new_m[n] for n in WEIGHTS], *[new_v[n] for n in WEIGHTS])
```

```python
import functools
import math

import jax
import jax.numpy as jnp
from jax import lax
from jax.experimental import pallas as pl
from jax.experimental.pallas import tpu as pltpu

F32 = jnp.float32
BF16 = jnp.bfloat16
MESH = pl.DeviceIdType.MESH

N_DEV = 8
N_CHIP = 4
EPS = 1e-5
HEAD_DIM = 64
STATE = 128
CHUNK = 128
PAIR = 2 * HEAD_DIM
GROUP_W = 8 * HEAD_DIM
HALO = 16
LANE = 128
VMEM_LIMIT = 56 * 1024 * 1024

ADAM_LR = 0.001
ADAM_B1 = 0.9
ADAM_B2 = 0.999
ADAM_EPS = 1e-08
ADAM_WD = 0.01
ADAM_STEP = 10


def _pick(n, candidates):
    for c in candidates:
        if c <= n and n % c == 0:
            return c
    return n


OPERAND_VMEM = 36 * 1024 * 1024


def _pick_k(kd, other, candidates):
    for c in (kd,) + tuple(candidates):
        if c <= kd and kd % c == 0 and 2 * 2 * other * c <= OPERAND_VMEM:
            return c
    return kd


def _params(*sem):
    return pltpu.CompilerParams(dimension_semantics=sem, vmem_limit_bytes=VMEM_LIMIT)


def _sigmoid(x):
    return 0.5 * jnp.tanh(0.5 * x) + 0.5


_DIMS = {
    "nn": (((1,), (0,)), ((), ())),
    "nt": (((1,), (1,)), ((), ())),
    "tn": (((0,), (0,)), ((), ())),
}


def _matmul(mode, a, b, *, grid, a_spec, b_spec, o_spec, out_shape, acc_shape, name, res=None, res_spec=None,
            part_fn=None):
    nk = grid[2]
    dims = _DIMS[mode]
    if part_fn is None:
        part_fn = lambda a_ref, b_ref: lax.dot_general(a_ref[...], b_ref[...], dims, preferred_element_type=F32)

    def body(*refs):
        if res is None:
            a_ref, b_ref, o_ref = refs[:3]
            r_ref, scratch = None, refs[3:]
        else:
            a_ref, b_ref, r_ref, o_ref = refs[:4]
            scratch = refs[4:]
        part = part_fn(a_ref, b_ref)

        def finish(acc):
            if r_ref is not None:
                acc = acc + r_ref[...]
            o_ref[...] = acc.astype(o_ref.dtype)

        if nk == 1:
            finish(part)
        else:
            acc_ref = scratch[0]
            k = pl.program_id(2)

            @pl.when(k == 0)
            def _():
                acc_ref[...] = part

            @pl.when(k > 0)
            def _():
                acc_ref[...] += part

            @pl.when(k == nk - 1)
            def _():
                finish(acc_ref[...])

    in_specs = [a_spec, b_spec] + ([res_spec] if res is not None else [])
    args = (a, b) + ((res,) if res is not None else ())
    return pl.pallas_call(
        body, name=name, grid=grid, in_specs=in_specs, out_specs=o_spec, out_shape=out_shape,
        scratch_shapes=[pltpu.VMEM(acc_shape, F32)] if nk > 1 else [],
        compiler_params=_params("parallel", "parallel", "arbitrary"),
    )(*args)


def _mm_nn(a, b, *, out_dtype, res=None, out_parts=1, name):
    m, kd = a.shape
    n = b.shape[1]
    c = n // out_parts
    tm = _pick(m, (512, 256, 128))
    tn = _pick(c, (1152, 1024, 512, 384, 256, 128))
    tk = _pick_k(kd, tm + tn, (2816, 2048, 1024, 512, 256, 128))
    grid = (n // tn, m // tm, kd // tk)
    if out_parts == 1:
        o_spec = pl.BlockSpec((tm, tn), lambda j, i, k: (i, j))
        out_shape = jax.ShapeDtypeStruct((m, n), out_dtype)
    else:
        o_spec = _stacked_spec(tm, tn, c, lambda j, i, k: (i, j))
        out_shape = jax.ShapeDtypeStruct((out_parts, m, c), out_dtype)
    return _matmul(
        "nn", a, b, res=res, grid=grid, name=name,
        a_spec=pl.BlockSpec((tm, tk), lambda j, i, k: (i, k)),
        b_spec=pl.BlockSpec((tk, tn), lambda j, i, k: (k, j)),
        res_spec=pl.BlockSpec((tm, tn), lambda j, i, k: (i, j)),
        o_spec=o_spec, out_shape=out_shape, acc_shape=(tm, tn))


def _stacked_spec(rows, width, c, row_col):
    per = c // width

    def index(j, i, k):
        r, q = row_col(j, i, k)
        return q // per, r, q % per

    return pl.BlockSpec((None, rows, width), index)


def _mm_nt(a, b, *, out_dtype, name):
    stacked = a.ndim == 3
    m = a.shape[-2]
    n, kd = b.shape
    c = a.shape[-1]
    tm = _pick(m, (512, 256, 128))
    tn = _pick(n, (1408, 1024, 512, 256, 128))
    tk = _pick_k(c, tm + tn, (3456, 2816, 2048, 1024, 512, 384, 256, 128))
    grid = (n // tn, m // tm, kd // tk)
    a_spec = (_stacked_spec(tm, tk, c, lambda j, i, k: (i, k)) if stacked
              else pl.BlockSpec((tm, tk), lambda j, i, k: (i, k)))
    return _matmul(
        "nt", a, b, grid=grid, name=name, a_spec=a_spec,
        b_spec=pl.BlockSpec((tn, tk), lambda j, i, k: (j, k)),
        o_spec=pl.BlockSpec((tm, tn), lambda j, i, k: (i, j)),
        out_shape=jax.ShapeDtypeStruct((m, n), out_dtype), acc_shape=(tm, tn))


def _mm_tn(a, b, *, out_dtype, name):
    stacked = b.ndim == 3
    kd, m = a.shape
    c = b.shape[-1]
    n = c * (b.shape[0] if stacked else 1)
    tm = _pick(m, (512, 256, 128))
    tn = _pick(c, (1152, 1024, 512, 384, 256, 128))
    tk = _pick_k(kd, tm + tn, (2048, 1024, 512, 256, 128))
    grid = (n // tn, m // tm, kd // tk)
    b_spec = (_stacked_spec(tk, tn, c, lambda j, i, k: (k, j)) if stacked
              else pl.BlockSpec((tk, tn), lambda j, i, k: (k, j)))
    return _matmul(
        "tn", a, b, grid=grid, name=name,
        a_spec=pl.BlockSpec((tk, tm), lambda j, i, k: (k, i)), b_spec=b_spec,
        o_spec=pl.BlockSpec((tm, tn), lambda j, i, k: (i, j)),
        out_shape=jax.ShapeDtypeStruct((m, n), out_dtype), acc_shape=(tm, tn))


def _in_tile(nb, c):
    return math.gcd(nb, c)


def _lin_in_fwd(h, wg, parts, *, name):
    t, d = h.shape
    nb = wg.shape[2]
    c = N_DEV * nb // parts
    w = _in_tile(nb, c)
    nbw, cw = nb // w, c // w
    tm = _pick(t, (1024,) if w < 512 else (512, 256, 128))
    grid = (N_DEV * nbw, t // tm, 1)
    return _matmul(
        "nn", h, wg, grid=grid, name=name,
        a_spec=pl.BlockSpec((tm, d), lambda j, i, k: (i, 0)),
        b_spec=pl.BlockSpec((None, d, w), lambda j, i, k: (j // nbw, 0, j % nbw)),
        o_spec=pl.BlockSpec((None, tm, w), lambda j, i, k: (j // cw, i, j % cw)),
        out_shape=jax.ShapeDtypeStruct((parts, t, c), BF16), acc_shape=(tm, w))


def _lin_in_dx(dact, wg, *, name):
    parts, t, c = dact.shape
    d, nb = wg.shape[1], wg.shape[2]
    tm = _pick(t, (512, 256, 128))
    tn = _pick(d, (1024, 512, 256, 128))
    group = _pick_k(c, tm + tn, (2 * nb, nb)) // nb
    per = c // (group * nb)
    grid = (d // tn, t // tm, N_DEV // group)

    def blocks(a_ref, b_ref):
        acc = None
        for q in range(group):
            part = lax.dot_general(a_ref[:, q * nb:(q + 1) * nb], b_ref[q], _DIMS["nt"], preferred_element_type=F32)
            acc = part if acc is None else acc + part
        return acc

    return _matmul(
        "nt", dact, wg, grid=grid, name=name, part_fn=blocks,
        a_spec=pl.BlockSpec((None, tm, group * nb), lambda j, i, k: (k // per, i, k % per)),
        b_spec=pl.BlockSpec((group, tn, nb), lambda j, i, k: (k, j, 0)),
        o_spec=pl.BlockSpec((tm, tn), lambda j, i, k: (i, j)),
        out_shape=jax.ShapeDtypeStruct((t, d), BF16), acc_shape=(tm, tn))


def _lin_in_dw(h, dact, nb, *, name):
    t, d = h.shape
    parts, _, c = dact.shape
    w = _in_tile(nb, c)
    nbw, cw = nb // w, c // w
    tm = _pick(d, (512, 256, 128))
    tk = _pick_k(t, tm + w, (2048, 1024, 512, 256, 128))
    grid = (N_DEV * nbw, d // tm, t // tk)
    return _matmul(
        "tn", h, dact, grid=grid, name=name,
        a_spec=pl.BlockSpec((tk, tm), lambda j, i, k: (k, i)),
        b_spec=pl.BlockSpec((None, tk, w), lambda j, i, k: (j // cw, k, j % cw)),
        o_spec=pl.BlockSpec((None, tm, w), lambda j, i, k: (j // nbw, i, j % nbw)),
        out_shape=jax.ShapeDtypeStruct((N_DEV, d, nb), BF16), acc_shape=(tm, w))


def _fold8(v):
    rows, c = v.shape
    return v.reshape(rows // 8, 8, c).sum(axis=0)


def _accumulate(ref, val, first):
    @pl.when(first)
    def _():
        ref[...] = val

    @pl.when(jnp.logical_not(first))
    def _():
        ref[...] += val


def _tile2(r, c, rows=(256, 128, 64, 32, 16)):
    tr = _pick(r, rows)
    if tr < r or r <= rows[0]:
        return tr, c
    return r, _pick(c, (256, 128))


def _cast_layer(w_stack, layer, me, after, *, name):
    _, r, c = w_stack.shape
    tr, tc = _tile2(r, c)

    def body(me_ref, w_ref, after_ref, o_ref):
        del me_ref, after_ref
        o_ref[...] = w_ref[...].astype(BF16)

    return pl.pallas_call(
        body, name=name,
        grid_spec=pltpu.PrefetchScalarGridSpec(
            num_scalar_prefetch=1, grid=(r // tr, c // tc),
            in_specs=[pl.BlockSpec((None, tr, tc), lambda i, j, me_ref: (layer, i, j)), HBM_ANY],
            out_specs=pl.BlockSpec((None, tr, tc), lambda i, j, me_ref: (me_ref[0], i, j))),
        out_shape=jax.ShapeDtypeStruct((N_DEV, r, c), BF16),
        compiler_params=_params("parallel", "parallel"),
    )(me, w_stack, after)


def _rmsnorm_fwd(x, w, *, name):
    t, d = x.shape
    tt = _pick(t, (256, 128))

    def body(x_ref, w_ref, o_ref):
        xv = x_ref[...]
        r = lax.rsqrt(jnp.mean(xv * xv, axis=1, keepdims=True) + EPS)
        o_ref[...] = (xv * r * w_ref[...]).astype(BF16)

    return pl.pallas_call(
        body, name=name, grid=(t // tt,),
        in_specs=[pl.BlockSpec((tt, d), lambda i: (i, 0)), pl.BlockSpec((1, d), lambda i: (0, 0))],
        out_specs=pl.BlockSpec((tt, d), lambda i: (i, 0)),
        out_shape=jax.ShapeDtypeStruct((t, d), BF16),
        compiler_params=_params("parallel"),
    )(x, w.reshape(1, d))


def _rmsnorm_bwd(dh, x, w, dres, *, name):
    t, d = x.shape
    tt = _pick(t, (256, 128))

    def body(dh_ref, x_ref, w_ref, dres_ref, dx_ref, dxb_ref, dw_ref):
        xv = x_ref[...]
        r = lax.rsqrt(jnp.mean(xv * xv, axis=1, keepdims=True) + EPS)
        xhat = xv * r
        dhv = dh_ref[...].astype(F32)
        dxhat = dhv * w_ref[...]
        dx = dres_ref[...] + r * (dxhat - xhat * jnp.mean(dxhat * xhat, axis=1, keepdims=True))
        dx_ref[...] = dx
        dxb_ref[...] = dx.astype(BF16)
        _accumulate(dw_ref, _fold8(dhv * xhat), pl.program_id(0) == 0)

    row = pl.BlockSpec((tt, d), lambda i: (i, 0))
    return pl.pallas_call(
        body, name=name, grid=(t // tt,),
        in_specs=[row, row, pl.BlockSpec((1, d), lambda i: (0, 0)), row],
        out_specs=[row, row, pl.BlockSpec((8, d), lambda i: (0, 0))],
        out_shape=[jax.ShapeDtypeStruct((t, d), F32), jax.ShapeDtypeStruct((t, d), BF16),
                   jax.ShapeDtypeStruct((8, d), F32)],
        compiler_params=_params("arbitrary"),
    )(dh, x, w.reshape(1, d), dres)


def _loss_head(x, w, target, *, name):
    t, d = x.shape
    tt = _pick(t, (256, 128))

    def body(x_ref, w_ref, tg_ref, dx_ref, dxb_ref, dw_ref, ls_ref):
        xv = x_ref[...]
        wv = w_ref[...]
        r = lax.rsqrt(jnp.mean(xv * xv, axis=1, keepdims=True) + EPS)
        xhat = xv * r
        err = xhat * wv - tg_ref[...]
        dy = err * (1.0 / d)
        dxhat = dy * wv
        dx = r * (dxhat - xhat * jnp.mean(dxhat * xhat, axis=1, keepdims=True))
        dx_ref[...] = dx
        dxb_ref[...] = dx.astype(BF16)
        first = pl.program_id(0) == 0
        _accumulate(dw_ref, _fold8(dy * xhat), first)
        _accumulate(ls_ref, _fold8(err * err) * (0.5 / d), first)

    row = pl.BlockSpec((tt, d), lambda i: (i, 0))
    acc = pl.BlockSpec((8, d), lambda i: (0, 0))
    return pl.pallas_call(
        body, name=name, grid=(t // tt,),
        in_specs=[row, pl.BlockSpec((1, d), lambda i: (0, 0)), row],
        out_specs=[row, row, acc, acc],
        out_shape=[jax.ShapeDtypeStruct((t, d), F32), jax.ShapeDtypeStruct((t, d), BF16),
                   jax.ShapeDtypeStruct((8, d), F32), jax.ShapeDtypeStruct((8, d), F32)],
        compiler_params=_params("arbitrary"),
    )(x, w.reshape(1, d), target)


def _conv_causal(e, tap, width):
    acc = None
    for k in range(width):
        s = width - 1 - k
        term = (e if s == 0 else pltpu.roll(e, s, 0)) * tap(k)
        acc = term if acc is None else acc + term
    return acc


def _conv_anticausal(e, tap, width):
    rows = e.shape[0]
    acc = None
    for k in range(width):
        s = width - 1 - k
        term = (e if s == 0 else pltpu.roll(e, rows - s, 0)) * tap(k)
        acc = term if acc is None else acc + term
    return acc


def _extend(prev, cur, nxt, first, last):
    parts = []
    if prev is not None:
        parts.append(jnp.where(first, 0.0, prev.astype(F32)))
    parts.append(cur.astype(F32))
    if nxt is not None:
        parts.append(jnp.where(last, 0.0, nxt.astype(F32)))
    return jnp.concatenate(parts, axis=0)


def _prev_idx(i, tt):
    return jnp.maximum(i * (tt // HALO) - 1, 0)


def _next_idx(i, tt, t):
    return jnp.minimum((i + 1) * (tt // HALO), t // HALO - 1)


def _ffn_act_fwd(u3, cw, cb, *, name):
    _, t, f = u3.shape
    tt = _pick(t, (512, 256, 128))
    tc = _pick(f, (512, 256, 128))
    width = cw.shape[0]

    def body(u_ref, up_ref, w_ref, b_ref, o_ref, pre_ref):
        first = pl.program_id(1) == 0
        pre = []
        for p in range(2):
            e = _extend(up_ref[p], u_ref[p], None, first, None)
            pre.append(_conv_causal(e, lambda k: w_ref[k, p:p + 1, :], width)[HALO:] + b_ref[p:p + 1, :])
            pre_ref[p] = pre[p].astype(BF16)
        g, v = pre
        o_ref[...] = (g * _sigmoid(g) * v).astype(BF16)

    return pl.pallas_call(
        body, name=name, grid=(f // tc, t // tt),
        in_specs=[pl.BlockSpec((2, tt, tc), lambda j, i: (0, i, j)),
                  pl.BlockSpec((2, HALO, tc), lambda j, i: (0, _prev_idx(i, tt), j)),
                  pl.BlockSpec((width, 2, tc), lambda j, i: (0, 0, j)),
                  pl.BlockSpec((2, tc), lambda j, i: (0, j))],
        out_specs=[pl.BlockSpec((tt, tc), lambda j, i: (i, j)), pl.BlockSpec((2, tt, tc), lambda j, i: (0, i, j))],
        out_shape=[jax.ShapeDtypeStruct((t, f), BF16), jax.ShapeDtypeStruct((2, t, f), BF16)],
        compiler_params=_params("parallel", "parallel"),
    )(u3, u3, cw, cb)


def _ffn_act_bwd(u3, pre3, da, cw, *, name):
    _, t, f = u3.shape
    tt = _pick(t, (512, 256, 128))
    tc = _pick(f, (512, 256, 128))
    width = cw.shape[0]
    nt = t // tt
    rows = tt + HALO

    def body(u_ref, pre_ref, pren_ref, da_ref, dan_ref, w_ref, du_ref, dcw_ref, dcb_ref):
        i = pl.program_id(1)
        first, last = i == 0, i == nt - 1
        g, v = (_extend(None, pre_ref[p], pren_ref[p], None, False) for p in range(2))
        dae = _extend(None, da_ref[...], dan_ref[...], None, last)
        sg = _sigmoid(g)
        dpre = (dae * v * (sg * (1.0 + g * (1.0 - sg))), dae * (g * sg))

        @pl.when(first)
        def _():
            dcw_ref[...] = jnp.zeros_like(dcw_ref)
            dcb_ref[...] = jnp.zeros_like(dcb_ref)

        for p in range(2):
            u = u_ref[p].astype(F32)
            du = None
            for k in range(width):
                s = width - 1 - k
                d = (dpre[p] if s == 0 else pltpu.roll(dpre[p], rows - s, 0))[:tt]
                term = d * w_ref[k, p:p + 1, :]
                du = term if du is None else du + term
                dcw_ref[k, p:p + 1, :] += jnp.sum(d * u, axis=0, keepdims=True)
                if s == 0:
                    dcb_ref[p:p + 1, :] += jnp.sum(d, axis=0, keepdims=True)
            du_ref[p] = du.astype(BF16)

    cur3 = pl.BlockSpec((2, tt, tc), lambda j, i: (0, i, j))
    return pl.pallas_call(
        body, name=name, grid=(f // tc, nt),
        in_specs=[cur3, cur3,
                  pl.BlockSpec((2, HALO, tc), lambda j, i: (0, _next_idx(i, tt, t), j)),
                  pl.BlockSpec((tt, tc), lambda j, i: (i, j)),
                  pl.BlockSpec((HALO, tc), lambda j, i: (_next_idx(i, tt, t), j)),
                  pl.BlockSpec((width, 2, tc), lambda j, i: (0, 0, j))],
        out_specs=[cur3,
                   pl.BlockSpec((width, 2, tc), lambda j, i: (0, 0, j)),
                   pl.BlockSpec((2, tc), lambda j, i: (0, j))],
        out_shape=[jax.ShapeDtypeStruct((2, t, f), BF16), jax.ShapeDtypeStruct((width, 2, f), F32),
                   jax.ShapeDtypeStruct((2, f), F32)],
        compiler_params=_params("parallel", "arbitrary"),
    )(u3, pre3, pre3, da, da, cw)


def _sc_act_fwd(p3, cw, *, name):
    _, t, c = p3.shape
    tt = _pick(t, (512, 256, 128))
    tc = _pick(c, (512, 256, 128))
    width = cw.shape[0]

    def body(p_ref, pp_ref, w_ref, o_ref):
        first = pl.program_id(1) == 0
        q = _extend(pp_ref[1], p_ref[1], None, first, None) * _extend(pp_ref[2], p_ref[2], None, first, None)
        cq = _conv_causal(q, lambda k: w_ref[k:k + 1, :], width)[HALO:]
        o_ref[...] = (p_ref[0].astype(F32) * cq).astype(BF16)

    return pl.pallas_call(
        body, name=name, grid=(c // tc, t // tt),
        in_specs=[pl.BlockSpec((3, tt, tc), lambda j, i: (0, i, j)),
                  pl.BlockSpec((3, HALO, tc), lambda j, i: (0, _prev_idx(i, tt), j)),
                  pl.BlockSpec((width, tc), lambda j, i: (0, j))],
        out_specs=pl.BlockSpec((tt, tc), lambda j, i: (i, j)),
        out_shape=jax.ShapeDtypeStruct((t, c), BF16),
        compiler_params=_params("parallel", "parallel"),
    )(p3, p3, cw)


def _sc_act_bwd(p3, da, cw, *, name):
    _, t, c = p3.shape
    tt = _pick(t, (512, 256, 128))
    tc = _pick(c, (512, 256, 128))
    width = cw.shape[0]
    nt = t // tt
    ctr = slice(HALO, HALO + tt)

    def body(p_ref, pp_ref, pn_ref, da_ref, dan_ref, w_ref, dp_ref, dcw_ref):
        i = pl.program_id(1)
        first, last = i == 0, i == nt - 1
        tap = lambda k: w_ref[k:k + 1, :]
        bg, cg, hh = (_extend(pp_ref[p], p_ref[p], pn_ref[p], first, last) for p in range(3))
        q = cg * hh
        cq = _conv_causal(q, tap, width)
        dae = _extend(jnp.zeros((HALO, tc), F32), da_ref[...], dan_ref[...], False, last)
        dcq = dae * bg
        dq = _conv_anticausal(dcq, tap, width)[ctr]
        dp_ref[0] = (dae * cq)[ctr].astype(BF16)
        dp_ref[1] = (dq * hh[ctr]).astype(BF16)
        dp_ref[2] = (dq * cg[ctr]).astype(BF16)

        @pl.when(first)
        def _():
            dcw_ref[...] = jnp.zeros_like(dcw_ref)

        dc = dcq[ctr]
        for k in range(width):
            s = width - 1 - k
            qs = (q if s == 0 else pltpu.roll(q, s, 0))[ctr]
            dcw_ref[k:k + 1, :] += jnp.sum(dc * qs, axis=0, keepdims=True)

    return pl.pallas_call(
        body, name=name, grid=(c // tc, nt),
        in_specs=[pl.BlockSpec((3, tt, tc), lambda j, i: (0, i, j)),
                  pl.BlockSpec((3, HALO, tc), lambda j, i: (0, _prev_idx(i, tt), j)),
                  pl.BlockSpec((3, HALO, tc), lambda j, i: (0, _next_idx(i, tt, t), j)),
                  pl.BlockSpec((tt, tc), lambda j, i: (i, j)),
                  pl.BlockSpec((HALO, tc), lambda j, i: (_next_idx(i, tt, t), j)),
                  pl.BlockSpec((width, tc), lambda j, i: (0, j))],
        out_specs=[pl.BlockSpec((3, tt, tc), lambda j, i: (0, i, j)),
                   pl.BlockSpec((width, tc), lambda j, i: (0, j))],
        out_shape=[jax.ShapeDtypeStruct((3, t, c), BF16), jax.ShapeDtypeStruct((width, c), F32)],
        compiler_params=_params("parallel", "arbitrary"),
    )(p3, p3, p3, da, da, cw)


def _ssd_conv_fwd(zx, cw, cb, col0, *, name):
    t = zx.shape[0]
    width, c = cw.shape
    tt = _pick(t, (512, 256, 128))
    tc = _pick(math.gcd(c, col0), (512, 256, 128))
    off = col0 // tc

    def body(x_ref, xp_ref, w_ref, b_ref, o_ref, pre_ref):
        first = pl.program_id(1) == 0
        e = _extend(xp_ref[...], x_ref[...], None, first, None)
        pre = _conv_causal(e, lambda k: w_ref[k:k + 1, :], width)[HALO:] + b_ref[...]
        pre_ref[...] = pre.astype(BF16)
        o_ref[...] = (pre * _sigmoid(pre)).astype(BF16)

    out = pl.BlockSpec((tt, tc), lambda j, i: (i, j))
    return pl.pallas_call(
        body, name=name, grid=(c // tc, t // tt),
        in_specs=[pl.BlockSpec((tt, tc), lambda j, i: (i, off + j)),
                  pl.BlockSpec((HALO, tc), lambda j, i: (_prev_idx(i, tt), off + j)),
                  pl.BlockSpec((width, tc), lambda j, i: (0, j)),
                  pl.BlockSpec((1, tc), lambda j, i: (0, j))],
        out_specs=[out, out],
        out_shape=[jax.ShapeDtypeStruct((t, c), BF16)] * 2,
        compiler_params=_params("parallel", "parallel"),
    )(zx, zx, cw, cb)


def _ssd_conv_bwd(zx, pre, dxc, cw, dzx, col0, woff, *, name):
    t = zx.shape[0]
    width = cw.shape[0]
    c = dxc.shape[1]
    tt = _pick(t, (512, 256, 128))
    tc = _pick(math.gcd(math.gcd(c, col0), woff) if woff else math.gcd(c, col0), (512, 256, 128))
    nt = t // tt
    xoff, wo = (col0 + woff) // tc, woff // tc
    rows = tt + HALO

    def body(x_ref, p_ref, pn_ref, d_ref, dn_ref, w_ref, dzx_in, dzx_ref, dcw_ref, dcb_ref):
        del dzx_in
        i = pl.program_id(1)
        first, last = i == 0, i == nt - 1
        pre_e = _extend(None, p_ref[...], pn_ref[...], None, False)
        de = _extend(None, d_ref[...], dn_ref[...], None, last)
        sg = _sigmoid(pre_e)
        dpre = de * (sg * (1.0 + pre_e * (1.0 - sg)))

        @pl.when(first)
        def _():
            dcw_ref[...] = jnp.zeros_like(dcw_ref)
            dcb_ref[...] = jnp.zeros_like(dcb_ref)

        xv = x_ref[...].astype(F32)
        dx = None
        for k in range(width):
            s = width - 1 - k
            d = (dpre if s == 0 else pltpu.roll(dpre, rows - s, 0))[:tt]
            term = d * w_ref[k:k + 1, :]
            dx = term if dx is None else dx + term
            dcw_ref[k:k + 1, :] += jnp.sum(d * xv, axis=0, keepdims=True)
            if s == 0:
                dcb_ref[...] += jnp.sum(d, axis=0, keepdims=True)
        dzx_ref[...] = dx.astype(BF16)

    return pl.pallas_call(
        body, name=name, grid=(c // tc, nt),
        in_specs=[pl.BlockSpec((tt, tc), lambda j, i: (i, xoff + j)),
                  pl.BlockSpec((tt, tc), lambda j, i: (i, wo + j)),
                  pl.BlockSpec((HALO, tc), lambda j, i: (_next_idx(i, tt, t), wo + j)),
                  pl.BlockSpec((tt, tc), lambda j, i: (i, j)),
                  pl.BlockSpec((HALO, tc), lambda j, i: (_next_idx(i, tt, t), j)),
                  pl.BlockSpec((width, tc), lambda j, i: (0, wo + j)),
                  pl.BlockSpec(memory_space=pl.ANY)],
        out_specs=[pl.BlockSpec((tt, tc), lambda j, i: (i, xoff + j)),
                   pl.BlockSpec((width, tc), lambda j, i: (0, j)),
                   pl.BlockSpec((1, tc), lambda j, i: (0, j))],
        out_shape=[jax.ShapeDtypeStruct(dzx.shape, dzx.dtype), jax.ShapeDtypeStruct((width, c), F32),
                   jax.ShapeDtypeStruct((1, c), F32)],
        input_output_aliases={6: 0},
        compiler_params=_params("parallel", "arbitrary"),
    )(zx, pre, pre, dxc, dxc, cw, dzx)


def _ssd_put_ddt(ddt_g, dzx, col, *, name):
    g, t, _ = ddt_g.shape
    tt = _pick(t, (512, 256, 128))

    def body(d_ref, dzx_in, dzx_ref):
        del dzx_in
        dzx_ref[...] = jnp.sum(d_ref[...], axis=0).astype(BF16)

    return pl.pallas_call(
        body, name=name, grid=(t // tt,),
        in_specs=[pl.BlockSpec((g, tt, LANE), lambda i: (0, i, 0)), pl.BlockSpec(memory_space=pl.ANY)],
        out_specs=pl.BlockSpec((tt, LANE), lambda i: (i, col)),
        out_shape=jax.ShapeDtypeStruct(dzx.shape, dzx.dtype),
        input_output_aliases={1: 0},
        compiler_params=_params("parallel"),
    )(ddt_g, dzx)


def _dot(a, b, mode):
    return lax.dot_general(a, b, _DIMS[mode], preferred_element_type=F32)


def _dot_exact(m01, v, mode="nn"):
    hi = v.astype(BF16)
    r1 = v - hi.astype(F32)
    mid = r1.astype(BF16)
    lo = (r1 - mid.astype(F32)).astype(BF16)
    return _dot(m01, hi, mode) + _dot(m01, mid, mode) + _dot(m01, lo, mode)


def _softplus(x):
    return jnp.maximum(x, 0.0) + jnp.log(1.0 + jnp.exp(-jnp.abs(x)))


def _head_vectors(g, dt_raw, bias, alog):
    n = CHUNK
    dt = _softplus(dt_raw + bias)
    a = -jnp.exp(alog)
    tri = (lax.broadcasted_iota(jnp.int32, (n, n), 0) >= lax.broadcasted_iota(jnp.int32, (n, n), 1)).astype(BF16)
    cs = _dot_exact(tri, dt * a)
    return dt, a, cs, cs.T


def _col(v, lane_ids, h):
    return jnp.sum(jnp.where(lane_ids == h, v, 0.0), axis=1, keepdims=True)


def _row(vt, sub_ids, h):
    return jnp.sum(jnp.where(sub_ids == h, vt, 0.0), axis=0, keepdims=True)


def _ssd_specs(di, bc, nc, rev):
    cidx = (lambda c: nc - 1 - c) if rev else (lambda c: c)
    wide = lambda off: pl.BlockSpec((CHUNK, GROUP_W), lambda g, c: (cidx(c), off + g))
    lane = lambda off: pl.BlockSpec((CHUNK, LANE), lambda g, c: (cidx(c), off + g))
    fixed = lambda off: pl.BlockSpec((CHUNK, LANE), lambda g, c: (cidx(c), off))
    vec = pl.BlockSpec((1, LANE), lambda g, c: (0, 0))
    gvec = pl.BlockSpec((1, GROUP_W), lambda g, c: (0, g))
    state = pl.BlockSpec((None, None, 4, PAIR, STATE), lambda g, c: (g, cidx(c), 0, 0, 0))
    return wide, lane, fixed, vec, gvec, state


def _ssd_fwd(xbc, zx, bias, alog, dexp, nw, *, name):
    t = xbc.shape[0]
    di = nw.shape[1]
    bc = (xbc.shape[1] - di) // 2
    ng, nc = di // GROUP_W, t // CHUNK
    wide, lane, fixed, vec, gvec, state = _ssd_specs(di, bc, nc, rev=False)

    def body(xs_ref, b_ref, c_ref, dt_ref, z_ref, bias_ref, alog_ref, dexp_ref, nw_ref,
             yn_ref, y_ref, st_ref, s_scr):
        g, c = pl.program_id(0), pl.program_id(1)

        @pl.when(c == 0)
        def _():
            s_scr[...] = jnp.zeros_like(s_scr)

        n = CHUNK
        dt, a, cs, cst = _head_vectors(g, dt_ref[...].astype(F32), bias_ref[...], alog_ref[...])
        lane_ids = lax.broadcasted_iota(jnp.int32, (n, LANE), 1)
        sub_ids = lax.broadcasted_iota(jnp.int32, (LANE, n), 0)
        causal = lax.broadcasted_iota(jnp.int32, (n, n), 0) >= lax.broadcasted_iota(jnp.int32, (n, n), 1)
        half = lax.broadcasted_iota(jnp.int32, (1, PAIR), 1) < HEAD_DIM
        half_rows = lax.broadcasted_iota(jnp.int32, (PAIR, 1), 0) < HEAD_DIM
        bm, cm = b_ref[...], c_ref[...]
        gm = _dot(cm, bm, "nt")
        x = xs_ref[...].astype(F32)
        ys = []
        for q in range(4):
            h0 = g * 8 + 2 * q
            col = [_col(cs, lane_ids, h0 + e) for e in range(2)]
            row = [_row(cst, sub_ids, h0 + e) for e in range(2)]
            dtc = [_col(dt, lane_ids, h0 + e) for e in range(2)]
            last = [col[e][n - 1:n, :] for e in range(2)]
            xd = x[:, q * PAIR:(q + 1) * PAIR] * jnp.where(half, dtc[0], dtc[1])
            xd_bf = xd.astype(BF16)
            yd = []
            for e in range(2):
                lm = jnp.exp(jnp.where(causal, col[e] - row[e], -1e30))
                yd.append(_dot((gm * lm).astype(BF16), xd_bf, "nn"))
            s = s_scr[q]
            st_ref[q] = s
            ecs = jnp.where(half, jnp.exp(col[0]), jnp.exp(col[1]))
            dte = jnp.where(half, jnp.exp(last[0] - col[0]), jnp.exp(last[1] - col[1]))
            yoff = ecs * _dot(cm, s.astype(BF16), "nt")
            snew = _dot((xd * dte).astype(BF16), bm, "tn")
            s_scr[q] = s * jnp.where(half_rows, jnp.exp(last[0]), jnp.exp(last[1])) + snew
            ys.append(jnp.where(half, yd[0], yd[1]) + yoff)
        y = jnp.concatenate(ys, axis=1) + dexp_ref[...] * x
        y_ref[...] = y.astype(BF16)
        z = z_ref[...].astype(F32)
        yg = y * (z * _sigmoid(z))
        r = lax.rsqrt(jnp.mean(yg * yg, axis=1, keepdims=True) + EPS)
        yn_ref[...] = (yg * r * nw_ref[...]).astype(BF16)

    dtcol = (2 * di + 2 * bc) // LANE
    return pl.pallas_call(
        body, name=name, grid=(ng, nc),
        in_specs=[wide(0), lane(di // LANE), lane((di + bc) // LANE), fixed(dtcol), wide(0),
                  vec, vec, gvec, gvec],
        out_specs=[wide(0), wide(0), state],
        out_shape=[jax.ShapeDtypeStruct((t, di), BF16), jax.ShapeDtypeStruct((t, di), BF16),
                   jax.ShapeDtypeStruct((ng, nc, 4, PAIR, STATE), F32)],
        scratch_shapes=[pltpu.VMEM((4, PAIR, STATE), F32)],
        compiler_params=_params("parallel", "arbitrary"),
    )(xbc, xbc, xbc, zx, zx, bias, alog, dexp, nw)


def _ssd_bwd(dyn, y, xbc, zx, states, bias, alog, dexp, nw, *, name):
    t = xbc.shape[0]
    di = nw.shape[1]
    bc = (xbc.shape[1] - di) // 2
    ng, nc = di // GROUP_W, t // CHUNK
    wide, lane, fixed, vec, gvec, state = _ssd_specs(di, bc, nc, rev=True)
    acc = lambda w: pl.BlockSpec((None, 8, w), lambda g, c: (g, 0, 0))

    def body(dyn_ref, y_ref, z_ref, nw_ref, xs_ref, b_ref, c_ref, dt_ref, bias_ref, alog_ref, dexp_ref, st_ref,
             dz_ref, dxs_ref, db_ref, dc_ref, ddt_ref, small_ref, dnw_ref, ddexp_ref, ds_scr):
        g, c = pl.program_id(0), pl.program_id(1)

        @pl.when(c == 0)
        def _():
            ds_scr[...] = jnp.zeros_like(ds_scr)
            small_ref[...] = jnp.zeros_like(small_ref)
            dnw_ref[...] = jnp.zeros_like(dnw_ref)
            ddexp_ref[...] = jnp.zeros_like(ddexp_ref)

        n = CHUNK
        yv = y_ref[...].astype(F32)
        z = z_ref[...].astype(F32)
        sz = _sigmoid(z)
        silu = z * sz
        yg = yv * silu
        r = lax.rsqrt(jnp.mean(yg * yg, axis=1, keepdims=True) + EPS)
        yhat = yg * r
        dynv = dyn_ref[...].astype(F32)
        dnw_ref[0:1, :] += jnp.sum(dynv * yhat, axis=0, keepdims=True)
        dyhat = dynv * nw_ref[...]
        dyg = r * (dyhat - yhat * jnp.mean(dyhat * yhat, axis=1, keepdims=True))
        dz_ref[...] = (dyg * yv * (sz * (1.0 + z * (1.0 - sz)))).astype(BF16)
        dy = dyg * silu

        dt_in = dt_ref[...].astype(F32) + bias_ref[...]
        dt, a, cs, cst = _head_vectors(g, dt_ref[...].astype(F32), bias_ref[...], alog_ref[...])
        lane_ids = lax.broadcasted_iota(jnp.int32, (n, LANE), 1)
        sub_ids = lax.broadcasted_iota(jnp.int32, (LANE, n), 0)
        ri = lax.broadcasted_iota(jnp.int32, (n, n), 0)
        ci = lax.broadcasted_iota(jnp.int32, (n, n), 1)
        causal, causal_t = ri >= ci, ci >= ri
        is_last = lax.broadcasted_iota(jnp.int32, (n, 1), 0) == n - 1
        half = lax.broadcasted_iota(jnp.int32, (1, PAIR), 1) < HEAD_DIM
        half_rows = lax.broadcasted_iota(jnp.int32, (PAIR, 1), 0) < HEAD_DIM
        bm, cm = b_ref[...], c_ref[...]
        bf = bm.astype(F32)
        gm, gmt = _dot(cm, bm, "nt"), _dot(bm, cm, "nt")
        x = xs_ref[...].astype(F32)
        dexp = dexp_ref[...]

        dg_sum = jnp.zeros((n, n), F32)
        dgt_sum = jnp.zeros((n, n), F32)
        db_off = jnp.zeros((n, STATE), F32)
        dc_off = jnp.zeros((n, STATE), F32)
        dcs_blk = jnp.zeros((n, LANE), F32)
        ddt_blk = jnp.zeros((n, LANE), F32)
        dxs = []
        for q in range(4):
            h0 = g * 8 + 2 * q
            sl = slice(q * PAIR, (q + 1) * PAIR)
            col = [_col(cs, lane_ids, h0 + e) for e in range(2)]
            row = [_row(cst, sub_ids, h0 + e) for e in range(2)]
            dtc = [_col(dt, lane_ids, h0 + e) for e in range(2)]
            last = [col[e][n - 1:n, :] for e in range(2)]
            xp, dyp = x[:, sl], dy[:, sl]
            dtp = jnp.where(half, dtc[0], dtc[1])
            xd = xp * dtp
            xd_bf, dyp_bf = xd.astype(BF16), dyp.astype(BF16)
            ecs = jnp.where(half, jnp.exp(col[0]), jnp.exp(col[1]))
            dte = jnp.where(half, jnp.exp(last[0] - col[0]), jnp.exp(last[1] - col[1]))
            s, ds = st_ref[q], ds_scr[q]
            s_bf, ds_bf = s.astype(BF16), ds.astype(BF16)
            yoff = ecs * _dot(cm, s_bf, "nt")
            edy_bf = (ecs * dyp).astype(BF16)
            dc_off += _dot(edy_bf, s_bf, "nn")
            bds = _dot(bm, ds_bf, "nt")
            sds = s * ds
            zs = []
            for e in range(2):
                msk = half if e == 0 else jnp.logical_not(half)
                msk_rows = half_rows if e == 0 else jnp.logical_not(half_rows)
                lm = jnp.exp(jnp.where(causal, col[e] - row[e], -1e30))
                lmt = jnp.exp(jnp.where(causal_t, row[e] - col[e], -1e30))
                dym_bf = jnp.where(msk, dyp, 0.0).astype(BF16)
                xdm_bf = jnp.where(msk, xd, 0.0).astype(BF16)
                dm = _dot(dym_bf, xd_bf, "nt")
                dmt = _dot(xdm_bf, dyp_bf, "nt")
                m, mt = gm * lm, gmt * lmt
                dcs = jnp.sum(dm * m, axis=1, keepdims=True) - jnp.sum(dmt * mt, axis=1, keepdims=True)
                dg_sum += dm * lm
                dgt_sum += dmt * lmt
                zs.append(_dot(mt.astype(BF16), dyp_bf, "nn"))
                we = _dot(xdm_bf, ds_bf, "nn")
                dte_col = jnp.exp(last[e] - col[e])
                te = dte_col * jnp.sum(we * bf, axis=1, keepdims=True)
                db_off += dte_col * we
                dcs += jnp.sum(jnp.where(msk, dyp * yoff, 0.0), axis=1, keepdims=True) - te
                tail = jnp.exp(last[e]) * jnp.sum(jnp.where(msk_rows, sds, 0.0), keepdims=True) \
                    + jnp.sum(te, keepdims=True)
                dcs += jnp.where(is_last, tail, 0.0)
                dcs_blk += jnp.where(lane_ids == h0 + e, dcs, 0.0)
            dxd = jnp.where(half, zs[0], zs[1]) + dte * bds
            dxs.append(dxd * dtp + dexp[:, sl] * dyp)
            ddexp_ref[0:1, sl] += jnp.sum(dyp * xp, axis=0, keepdims=True)
            rs = dxd * xp
            for e in range(2):
                msk = half if e == 0 else jnp.logical_not(half)
                ddt_blk += jnp.where(lane_ids == h0 + e, jnp.sum(jnp.where(msk, rs, 0.0), axis=1, keepdims=True), 0.0)
            ds_scr[q] = ds * jnp.where(half_rows, jnp.exp(last[0]), jnp.exp(last[1])) + _dot(edy_bf, cm, "tn")

        dxs_ref[...] = jnp.concatenate(dxs, axis=1).astype(BF16)
        dc_ref[...] = (_dot(dg_sum.astype(BF16), bm, "nn") + dc_off).astype(BF16)
        db_ref[...] = (_dot(dgt_sum.astype(BF16), cm, "nn") + db_off).astype(BF16)
        upper = (ri <= ci).astype(BF16)
        dda = _dot_exact(upper, dcs_blk)
        ddt = dda * a + ddt_blk
        small_ref[0:1, :] += jnp.sum(dda * dt, axis=0, keepdims=True) * a
        ddt_raw = ddt * _sigmoid(dt_in)
        small_ref[1:2, :] += jnp.sum(ddt_raw, axis=0, keepdims=True)
        ddt_ref[...] = ddt_raw

    dtcol = (2 * di + 2 * bc) // LANE
    tot = 2 * di + 2 * bc + LANE
    return pl.pallas_call(
        body, name=name, grid=(ng, nc),
        in_specs=[wide(0), wide(0), wide(0), gvec, wide(0), lane(di // LANE), lane((di + bc) // LANE),
                  fixed(dtcol), vec, vec, gvec, state],
        out_specs=[wide(0), wide(0), lane(0), lane(0),
                   pl.BlockSpec((None, CHUNK, LANE), lambda g, c: (g, nc - 1 - c, 0)),
                   acc(LANE), acc(GROUP_W), acc(GROUP_W)],
        out_shape=[jax.ShapeDtypeStruct((t, tot), BF16), jax.ShapeDtypeStruct((t, di), BF16),
                   jax.ShapeDtypeStruct((t, bc), BF16), jax.ShapeDtypeStruct((t, bc), BF16),
                   jax.ShapeDtypeStruct((ng, t, LANE), F32), jax.ShapeDtypeStruct((ng, 8, LANE), F32),
                   jax.ShapeDtypeStruct((ng, 8, GROUP_W), F32), jax.ShapeDtypeStruct((ng, 8, GROUP_W), F32)],
        scratch_shapes=[pltpu.VMEM((4, PAIR, STATE), F32)],
        compiler_params=_params("parallel", "arbitrary"),
    )(dyn, y, zx, nw, xbc, xbc, xbc, zx, bias, alog, dexp, states)


HBM_ANY = pl.BlockSpec(memory_space=pl.ANY)


def _place():
    x, y, c = lax.axis_index("x"), lax.axis_index("y"), lax.axis_index("c")
    chips = [(1 - x, y), (x, 1 - y), (1 - x, 1 - y)]
    return x, y, c, chips


def _all_gather(arrs, *, name, inplace=False):
    n = len(arrs)

    def body(*refs):
        ins, outs = refs[:n], refs[n:2 * n]
        send, recv, loc = refs[2 * n:]
        x, y, c, chips = _place()
        me, sib = (x, y, c), (x, y, 1 - c)

        def blk(a, p):
            return outs[a].at[4 * p[0] + 2 * p[1] + p[2]]

        def cp(a, k, block, to, src=None):
            return pltpu.make_async_remote_copy(
                src_ref=blk(a, block) if src is None else src, dst_ref=blk(a, block),
                send_sem=send.at[a * 7 + k], recv_sem=recv.at[a * 7 + k], device_id=to, device_id_type=MESH)

        src = [None if inplace else ins[a] for a in range(n)]
        mine = [] if inplace else [pltpu.make_async_copy(ins[a], blk(a, me), loc.at[a]) for a in range(n)]
        for m in mine:
            m.start()
        started = []
        for a in range(n):
            started.append(cp(a, 0, me, sib, src=src[a]))
            started += [cp(a, 1 + j, me, (*chip, c), src=src[a]) for j, chip in enumerate(chips)]
        for s in started:
            s.start()
        for j, chip in enumerate(chips):
            for a in range(n):
                cp(a, 1 + j, (*chip, c), me).wait_recv()
                fwd = cp(a, 4 + j, (*chip, c), sib)
                fwd.start()
                started.append(fwd)
        for a in range(n):
            cp(a, 0, sib, me).wait_recv()
            for j, chip in enumerate(chips):
                cp(a, 4 + j, (*chip, 1 - c), me).wait_recv()
        for s in started:
            s.wait_send()
        for m in mine:
            m.wait()

    return pl.pallas_call(
        body, name=name,
        in_specs=[HBM_ANY] * n, out_specs=[HBM_ANY] * n,
        out_shape=[jax.ShapeDtypeStruct(a.shape if inplace else (N_DEV,) + a.shape, a.dtype) for a in arrs],
        input_output_aliases={a: a for a in range(n)} if inplace else {},
        scratch_shapes=[pltpu.SemaphoreType.DMA((7 * n,)), pltpu.SemaphoreType.DMA((7 * n,)),
                        pltpu.SemaphoreType.DMA((n,))],
    )(*arrs)


HBM_SPEC = pl.BlockSpec(memory_space=pltpu.HBM)
SEM_SPEC = pl.BlockSpec(memory_space=pltpu.SEMAPHORE)
SPLIT_EFFECT = pltpu.SideEffectType.DATAFLOW_SIDE_EFFECTING


def _split_start(arrs, plan, n_copies, after, *, name):
    m = len(arrs)

    def body(*refs):
        send, recv, token = refs[m + 1], refs[m + 2], refs[-1]
        for i, (src, dst, to) in enumerate(plan(refs[:m])):
            pltpu.make_async_remote_copy(src_ref=src, dst_ref=dst, send_sem=send.at[i], recv_sem=recv.at[i],
                                         device_id=to, device_id_type=MESH).start()
        token[...] = jnp.zeros_like(token)

    outs = pl.pallas_call(
        body, name=name,
        out_shape=(pltpu.SemaphoreType.DMA((n_copies,)), pltpu.SemaphoreType.DMA((n_copies,)),
                   *[pltpu.HBM(a.shape, a.dtype) for a in arrs], jax.ShapeDtypeStruct((8, LANE), F32)),
        in_specs=[HBM_SPEC] * m + [HBM_ANY],
        out_specs=(SEM_SPEC, SEM_SPEC, *[HBM_SPEC] * m, pl.BlockSpec(memory_space=pltpu.VMEM)),
        input_output_aliases={i: 2 + i for i in range(m)},
        compiler_params=pltpu.CompilerParams(has_side_effects=SPLIT_EFFECT),
    )(*[pltpu.with_memory_space_constraint(a, pltpu.HBM) for a in arrs], after)
    return outs[0], outs[1], list(outs[2:2 + m]), outs[-1]


def _split_wait(arrs, send, recv, after, plan, *, name):
    m = len(arrs)

    def body(*refs):
        send_ref, recv_ref = refs[m], refs[m + 1]
        for i, (src, dst, to) in enumerate(plan(refs[:m])):
            cp = pltpu.make_async_remote_copy(src_ref=src, dst_ref=dst, send_sem=send_ref.at[i],
                                              recv_sem=recv_ref.at[i], device_id=to, device_id_type=MESH)
            cp.wait_send()
            cp.wait_recv()

    outs = pl.pallas_call(
        body, name=name,
        out_shape=[pltpu.HBM(a.shape, a.dtype) for a in arrs],
        in_specs=[HBM_SPEC] * m + [SEM_SPEC, SEM_SPEC, HBM_ANY], out_specs=[HBM_SPEC] * m,
        input_output_aliases={i: i for i in range(m)},
        compiler_params=pltpu.CompilerParams(has_side_effects=SPLIT_EFFECT),
    )(*arrs, send, recv, after)
    return list(outs)


def _dev(p):
    return 4 * p[0] + 2 * p[1] + p[2]


def _plan_gather_ici(bufs):
    x, y, c, chips = _place()
    me = _dev((x, y, c))
    peers = [(x, y, 1 - c)] + [(*chip, c) for chip in chips]
    return [(b.at[me], b.at[me], p) for b in bufs for p in peers]


def _plan_gather_d2d(bufs):
    x, y, c, chips = _place()
    return [(b.at[_dev((*chip, c))], b.at[_dev((*chip, c))], (x, y, 1 - c)) for b in bufs for chip in chips]


def _plan_pair(refs):
    n = len(refs) // 2
    x, y, c, _ = _place()
    return [(refs[a].at[2 * k + 1 - c], refs[n + a].at[k], (x, y, 1 - c)) for a in range(n) for k in range(N_CHIP)]


def _plan_chip(refs):
    n = len(refs) // 2
    x, y, c, chips = _place()
    return [(refs[a].at[2 * chip[0] + chip[1]], refs[n + a].at[j], (*chip, c))
            for a in range(n) for j, chip in enumerate(chips)]


def _land(shape, dtype):
    return lax.empty(shape, dtype)


def _with_tokens(v, *tokens):
    for t in tokens:
        if t is not None:
            v = v + t[0, 0].astype(v.dtype)
    return v


def _add_pair(grad, got, core, *, name):
    k, r, c = got.shape
    tr, tc = _tile2(r, c, rows=(1024, 704, 512, 256, 128, 64, 32, 16))

    def body(core_ref, a_ref, b_ref, o_ref):
        del core_ref
        o_ref[...] = (a_ref[...].astype(F32) + b_ref[...].astype(F32)).astype(BF16)

    spec = pl.BlockSpec((None, tr, tc), lambda q, i, j, core_ref: (q, i, j))
    return pl.pallas_call(
        body, name=name,
        grid_spec=pltpu.PrefetchScalarGridSpec(
            num_scalar_prefetch=1, grid=(k, r // tr, c // tc),
            in_specs=[pl.BlockSpec((None, tr, tc), lambda q, i, j, core_ref: (2 * q + core_ref[0], i, j)), spec],
            out_specs=spec),
        out_shape=jax.ShapeDtypeStruct(got.shape, BF16),
        compiler_params=_params("parallel", "parallel", "parallel"),
    )(core, grad, got)


def _plan_all(refs):
    x, y, c, _ = _place()
    me = _dev((x, y, c))
    plan = []
    for rel in range(1, N_DEV):
        fx, fy, fc = rel >> 2 & 1, rel >> 1 & 1, rel & 1
        plan.append((refs[0], refs[1].at[me], ((1 - x) if fx else x, (1 - y) if fy else y, (1 - c) if fc else c)))
    return plan


def _sum_slots(v, land, me, *, name):
    def body(me_ref, v_ref, land_ref, o_ref):
        acc = None
        for dev in range(N_DEV):
            term = jnp.where(me_ref[0] == dev, v_ref[...], land_ref[dev])
            acc = term if acc is None else acc + term
        o_ref[...] = acc

    vm = pl.BlockSpec(memory_space=pltpu.VMEM)
    return pl.pallas_call(
        body, name=name,
        grid_spec=pltpu.PrefetchScalarGridSpec(num_scalar_prefetch=1, grid=(), in_specs=[vm, vm], out_specs=vm),
        out_shape=jax.ShapeDtypeStruct(v.shape, F32),
        compiler_params=pltpu.CompilerParams(vmem_limit_bytes=VMEM_LIMIT),
    )(me, v, land)


def _adamw_math(w, g, m, v):
    m = ADAM_B1 * m + (1.0 - ADAM_B1) * g
    v = ADAM_B2 * v + (1.0 - ADAM_B2) * (g * g)
    m_hat = m / (1.0 - ADAM_B1 ** ADAM_STEP)
    v_hat = v / (1.0 - ADAM_B2 ** ADAM_STEP)
    delta = -ADAM_LR * (m_hat / (jnp.sqrt(v_hat) + ADAM_EPS) + ADAM_WD * w)
    return delta, m, v


def _adamw_layer(w, m, v, sums, recv, chip, layer, prev, after, *, name):
    nl, r, c = w.shape
    tr, tc = _tile2(r, c)

    def body(chip_ref, w_ref, m_ref, v_ref, s_ref, p_ref, *rest):
        del chip_ref
        g_ref, d_ref, mo_ref, vo_ref, token_ref = rest[-5:]
        g = s_ref[...].astype(F32)
        for k in range(N_CHIP - 1):
            g = g + p_ref[k].astype(F32)
        delta, mn, vn = _adamw_math(w_ref[...], g, m_ref[...], v_ref[...])
        g_ref[...] = g
        d_ref[...] = delta
        mo_ref[...] = mn
        vo_ref[...] = vn
        token_ref[...] = jnp.zeros_like(token_ref)

    lay = pl.BlockSpec((None, tr, tc), lambda i, j, chip_ref: (layer, i, j))
    ins = [w, m, v, sums, recv, after] + (list(prev) if prev is not None else [])
    in_specs = [lay, lay, lay, pl.BlockSpec((None, tr, tc), lambda i, j, chip_ref: (chip_ref[0], i, j)),
                pl.BlockSpec((N_CHIP - 1, tr, tc), lambda i, j, chip_ref: (0, i, j)), HBM_ANY]
    in_specs += [HBM_ANY] * (4 if prev is not None else 0)
    token = pl.BlockSpec((8, LANE), lambda i, j, chip_ref: (0, 0))
    outs = pl.pallas_call(
        body, name=name,
        grid_spec=pltpu.PrefetchScalarGridSpec(
            num_scalar_prefetch=1, grid=(r // tr, c // tc), in_specs=in_specs, out_specs=[lay] * 4 + [token]),
        out_shape=[jax.ShapeDtypeStruct(w.shape, F32)] * 4 + [jax.ShapeDtypeStruct((8, LANE), F32)],
        input_output_aliases={7 + q: q for q in range(4)} if prev is not None else {},
        compiler_params=_params("arbitrary", "arbitrary"),
    )(chip, *ins)
    return outs[:4], outs[4]


def _adamw_small(w, g, m, v, *, name):
    def body(w_ref, g_ref, m_ref, v_ref, d_ref, mo_ref, vo_ref):
        d_ref[...], mo_ref[...], vo_ref[...] = _adamw_math(w_ref[...], g_ref[...], m_ref[...], v_ref[...])

    vm = pl.BlockSpec(memory_space=pltpu.VMEM)
    return pl.pallas_call(
        body, name=name, in_specs=[vm] * 4, out_specs=[vm] * 3,
        out_shape=[jax.ShapeDtypeStruct(w.shape, F32)] * 3,
        compiler_params=pltpu.CompilerParams(vmem_limit_bytes=VMEM_LIMIT),
    )(w, g, m, v)


def _pack(arrs):
    flat = jnp.concatenate([a.reshape(-1).astype(F32) for a in arrs])
    pad = (-flat.shape[0]) % (8 * LANE)
    return jnp.pad(flat, (0, pad)).reshape(-1, LANE)


def _unpack(packed, shapes):
    flat = packed.reshape(-1)
    out, off = [], 0
    for s in shapes:
        size = math.prod(s)
        out.append(flat[off:off + size].reshape(s))
        off += size
    return out


WEIGHTS = ['mix_norm_w', 'ffn_norm_w', 'final_norm_w', 'ssd_w_in', 'ssd_conv_w', 'ssd_conv_b', 'ssd_dt_bias',
           'ssd_a_log', 'ssd_d', 'ssd_norm_w', 'ssd_w_out', 'sc_w_in', 'sc_conv_w', 'sc_w_out', 'ffn_w_up',
           'ffn_conv_w', 'ffn_conv_b', 'ffn_w_down']
BIG = ('ssd_w_in', 'ssd_w_out', 'sc_w_in', 'sc_w_out', 'ffn_w_up', 'ffn_w_down')
SHARDED_SMALL = ('ssd_conv_w', 'sc_conv_w', 'ffn_conv_w')


def _lane_pad(v):
    return jnp.pad(v.astype(F32), (0, LANE - v.shape[0])).reshape(1, LANE)


def _gather_cols(g):
    return jnp.moveaxis(g, 0, -2).reshape(g.shape[1:-1] + (N_DEV * g.shape[-1],))


class _Gather:
    def __init__(self, bufs, tag):
        self.bufs, self.tag = bufs, tag

    def start_ici(self, after):
        self.sems = _split_start(self.bufs, _plan_gather_ici, 4 * len(self.bufs), after, name=f"ag_ici_start_{self.tag}")
        return self.sems[3]

    def hand_on(self, after):
        send, recv, bufs, _ = self.sems
        bufs = _split_wait(bufs, send, recv, after, _plan_gather_ici, name=f"ag_ici_wait_{self.tag}")
        self.sems = _split_start(bufs, _plan_gather_d2d, 3 * len(bufs), after, name=f"ag_d2d_start_{self.tag}")
        return self.sems[3]

    def finish(self, after):
        send, recv, bufs, _ = self.sems
        return _split_wait(bufs, send, recv, after, _plan_gather_d2d, name=f"ag_d2d_wait_{self.tag}")


class _Scatter:
    def __init__(self, grads, core, tag):
        self.grads, self.core, self.tag = grads, core, tag

    def start_pair(self, after):
        lands = [_land((N_CHIP,) + g.shape[1:], g.dtype) for g in self.grads]
        self.sems = _split_start(self.grads + lands, _plan_pair, N_CHIP * len(lands), after,
                                 name=f"rs_pair_start_{self.tag}")
        return self.sems[3]

    def start_chip(self, after):
        n = len(self.grads)
        send, recv, arrs, _ = self.sems
        arrs = _split_wait(arrs, send, recv, after, _plan_pair, name=f"rs_pair_wait_{self.tag}")
        self.sums = [_add_pair(g, o, self.core, name=f"rs_add_{self.tag}{a}")
                     for a, (g, o) in enumerate(zip(arrs[:n], arrs[n:]))]
        lands = [_land((N_CHIP - 1,) + s.shape[1:], s.dtype) for s in self.sums]
        self.sems = _split_start(self.sums + lands, _plan_chip, (N_CHIP - 1) * n, after,
                                 name=f"rs_chip_start_{self.tag}")
        return self.sems[3]

    def finish(self, after):
        n = len(self.grads)
        send, recv, arrs, _ = self.sems
        arrs = _split_wait(arrs, send, recv, after, _plan_chip, name=f"rs_chip_wait_{self.tag}")
        return list(zip(arrs[:n], arrs[n:]))


def kernel(x, mix_norm_w, ffn_norm_w, final_norm_w, ssd_w_in, ssd_conv_w, ssd_conv_b, ssd_dt_bias, ssd_a_log, ssd_d, ssd_norm_w, ssd_w_out, sc_w_in, sc_conv_w, sc_w_out, ffn_w_up, ffn_conv_w, ffn_conv_b, ffn_w_down, loss_target, m_mix_norm_w, m_ffn_norm_w, m_final_norm_w, m_ssd_w_in, m_ssd_conv_w, m_ssd_conv_b, m_ssd_dt_bias, m_ssd_a_log, m_ssd_d, m_ssd_norm_w, m_ssd_w_out, m_sc_w_in, m_sc_conv_w, m_sc_w_out, m_ffn_w_up, m_ffn_conv_w, m_ffn_conv_b, m_ffn_w_down, v_mix_norm_w, v_ffn_norm_w, v_final_norm_w, v_ssd_w_in, v_ssd_conv_w, v_ssd_conv_b, v_ssd_dt_bias, v_ssd_a_log, v_ssd_d, v_ssd_norm_w, v_ssd_w_out, v_sc_w_in, v_sc_conv_w, v_sc_w_out, v_ffn_w_up, v_ffn_conv_w, v_ffn_conv_b, v_ffn_w_down):
    args = locals()
    wt = {n: args[n] for n in WEIGHTS}
    mom = {n: args["m_" + n] for n in WEIGHTS}
    var = {n: args["v_" + n] for n in WEIGHTS}
    for src in (wt, mom, var):
        src['ssd_w_in'] = jnp.swapaxes(src['ssd_w_in'], 1, 2)

    t, d = x.shape[-2], x.shape[-1]
    cur = x.reshape(t, d)
    target = loss_target.reshape(t, d)
    depth = mix_norm_w.shape[0]
    n_ssd, n_sc = ssd_w_in.shape[0], sc_w_in.shape[0]
    heads = ssd_dt_bias.shape[1]
    di = ssd_norm_w.shape[1]
    conv_dim = ssd_conv_b.shape[1]
    bc = (conv_dim - di) // 2
    in_dim = N_DEV * ssd_w_in.shape[2]
    in_pad = di + conv_dim + LANE
    ff = ffn_w_down.shape[1] * N_DEV
    me = 4 * lax.axis_index("x") + 2 * lax.axis_index("y") + lax.axis_index("c")
    me_s = me.astype(jnp.int32).reshape(1)
    core_s = lax.axis_index("c").astype(jnp.int32).reshape(1)
    chip_s = (2 * lax.axis_index("x") + lax.axis_index("y")).astype(jnp.int32).reshape(1)

    names_of = {"ssd": ('ssd_w_in', 'ssd_w_out'), "sc": ('sc_w_in', 'sc_w_out'), "ffn": ('ffn_w_up', 'ffn_w_down')}
    order = []
    for i in range(depth):
        order += [("ssd" if i % 2 == 0 else "sc", i // 2), ("ffn", i)]
    def make_gather(s, after):
        kind, idx = order[s]
        bufs = []
        for n in names_of[kind]:
            after = _cast_layer(wt[n], idx, me_s, after, name=f"cast_{n}{idx}")
            bufs.append(after)
        return _Gather(bufs, f"{kind}{idx}"), after

    conv_full = [_gather_cols(g) for g in _all_gather([wt[n] for n in SHARDED_SMALL], name="ag_conv")]
    first, last_cast = make_gather(0, conv_full[0])
    gathers = [first]
    tok_a = last_cast = first.start_ici(last_cast)
    for s in range(1, len(order)):
        g, last_cast = make_gather(s, last_cast)
        gathers.append(g)
    tok_b = gathers[0].hand_on(last_cast)
    tok_c = gathers[1].start_ici(tok_b)
    weights = [None] * len(order)
    weights[0] = gathers[0].finish(tok_c)
    ssd_cw, sc_cw, ffn_cw = conv_full
    ffn_cw = ffn_cw.reshape(depth, ffn_cw.shape[1], 2, ff)
    ffn_cb = ffn_conv_b.reshape(depth, 2, ff)
    dexp = jnp.repeat(ssd_d.astype(F32), HEAD_DIM, axis=1)

    n_sub = len(order)
    full = {n: [None] * wt[n].shape[0] for n in BIG}

    def prefetch(s, after):
        return gathers[s + 2].start_ici(after) if s + 2 < n_sub else None

    def hand_on(s, after):
        return gathers[s + 1].hand_on(after) if s + 1 < n_sub else None

    def arrive(s, after):
        if s + 1 < n_sub:
            weights[s + 1] = gathers[s + 1].finish(after)
            use(s + 1)

    def use(s):
        kind, idx = order[s]
        g_in, g_out = weights[s]
        if kind == "sc":
            g_in = jnp.swapaxes(g_in, 0, 1).reshape(d, -1)
        if kind == "ssd":
            g_in = jnp.pad(g_in.reshape(in_dim, d).T, ((0, 0), (0, in_pad - in_dim)))
        n_in, n_out = names_of[kind]
        full[n_in][idx], full[n_out][idx] = g_in, g_out.reshape(-1, d)

    use(0)
    saved = []
    for i in range(depth):
        j = i // 2
        s = 2 * i
        rec = {"x_mix": cur}
        tok = prefetch(s, cur)
        h = _rmsnorm_fwd(cur, _with_tokens(mix_norm_w[i], tok, tok_c if i == 0 else None), name=f"norm_mix{i}")
        rec["h_mix"] = h
        if i % 2 == 0:
            zx = _mm_nn(h, full['ssd_w_in'][j], out_dtype=BF16, name=f"ssd_in{j}")
            cb = _with_tokens(ssd_conv_b[j].reshape(1, conv_dim), hand_on(s, zx))
            xbc, conv_pre = _ssd_conv_fwd(zx, ssd_cw[j], cb, di, name=f"ssd_conv{j}")
            ssd_vecs = (_lane_pad(ssd_dt_bias[j]), _lane_pad(ssd_a_log[j]), dexp[j].reshape(1, di),
                        ssd_norm_w[j].reshape(1, di))
            yn, y, states = _ssd_fwd(xbc, zx, *ssd_vecs, name=f"ssd_core{j}")
            arrive(s, yn)
            cur = _mm_nn(yn, full['ssd_w_out'][j], res=cur, out_dtype=F32, name=f"ssd_out{j}")
            rec.update(zx=zx, xbc=xbc, conv_pre=conv_pre, yn=yn, y=y, states=states, vecs=ssd_vecs)
        else:
            p3 = _mm_nn(h, full['sc_w_in'][j], out_dtype=BF16, out_parts=3, name=f"sc_in{j}")
            act = _sc_act_fwd(p3, _with_tokens(sc_cw[j], hand_on(s, p3)), name=f"sc_act{j}")
            arrive(s, act)
            cur = _mm_nn(act, full['sc_w_out'][j], res=cur, out_dtype=F32, name=f"sc_out{j}")
            rec.update(p3=p3, act=act)
        s += 1
        rec["x_ffn"] = cur
        h = _rmsnorm_fwd(cur, _with_tokens(ffn_norm_w[i], prefetch(s, cur)), name=f"norm_ffn{i}")
        u3 = _lin_in_fwd(h, full['ffn_w_up'][i], 2, name=f"ffn_up{i}")
        act, pre3 = _ffn_act_fwd(u3, ffn_cw[i], _with_tokens(ffn_cb[i], hand_on(s, u3)), name=f"ffn_act{i}")
        arrive(s, act)
        cur = _mm_nn(act, full['ffn_w_down'][i], res=cur, out_dtype=F32, name=f"ffn_down{i}")
        rec.update(h_ffn=h, u3=u3, pre3=pre3, ffn_act=act)
        saved.append(rec)

    dx, dxb, dw_final, loss8 = _loss_head(cur, final_norm_w, target, name="loss_head")

    small = {n: [None] * wt[n].shape[0] for n in WEIGHTS if n not in BIG and n != 'final_norm_w'}
    scatters = [None] * n_sub
    pending = None

    def chip_step(after):
        return pending.start_chip(after) if pending is not None else None

    for i in reversed(range(depth)):
        j = i // 2
        rec = saved[i]
        nb_up = ffn_w_up.shape[2]
        da = _mm_nt(dxb, full['ffn_w_down'][i], out_dtype=BF16, name=f"ffn_down_dx{i}")
        g_down = _mm_tn(rec["ffn_act"], dxb, out_dtype=BF16, name=f"ffn_down_dw{i}")
        du3, dcw, dcb = _ffn_act_bwd(rec["u3"], rec["pre3"], da, _with_tokens(ffn_cw[i], chip_step(da)),
                                     name=f"ffn_act_bwd{i}")
        g_up = _lin_in_dw(rec["h_ffn"], du3, nb_up, name=f"ffn_up_dw{i}")
        dh = _lin_in_dx(du3, full['ffn_w_up'][i], name=f"ffn_up_dx{i}")
        pending = scatters[2 * i + 1] = _Scatter([g_up, g_down.reshape(N_DEV, ff // N_DEV, d)], core_s, f"ffn{i}")
        tok = pending.start_pair(dh)
        dx, dxb, dwn = _rmsnorm_bwd(dh, rec["x_ffn"], _with_tokens(ffn_norm_w[i], tok), dx, name=f"norm_ffn_bwd{i}")
        small['ffn_conv_w'][i] = dcw.reshape(dcw.shape[0], 2 * ff)
        small['ffn_conv_b'][i] = dcb.reshape(2 * ff)
        small['ffn_norm_w'][i] = dwn.sum(axis=0)

        if i % 2 == 0:
            zx, xbc = rec["zx"], rec["xbc"]
            cw, cpre = ssd_cw[j], rec["conv_pre"]
            dyn = _mm_nt(dxb, full['ssd_w_out'][j], out_dtype=BF16, name=f"ssd_out_dx{j}")
            g_out = _mm_tn(rec["yn"], dxb, out_dtype=BF16, name=f"ssd_out_dw{j}")
            bias_t = _with_tokens(rec["vecs"][0], chip_step(dyn))
            dzx, dxs, db, dc, ddt_g, vec_acc, dnw, ddexp = _ssd_bwd(
                dyn, rec["y"], xbc, zx, rec["states"], bias_t, *rec["vecs"][1:], name=f"ssd_core_bwd{j}")
            dzx, dcw_x, dcb_x = _ssd_conv_bwd(zx, cpre, dxs, cw, dzx, di, 0, name=f"ssd_conv_bwd_x{j}")
            dzx, dcw_b, dcb_b = _ssd_conv_bwd(zx, cpre, db, cw, dzx, di, di, name=f"ssd_conv_bwd_b{j}")
            dzx, dcw_c, dcb_c = _ssd_conv_bwd(zx, cpre, dc, cw, dzx, di, di + bc, name=f"ssd_conv_bwd_c{j}")
            dzx = _ssd_put_ddt(ddt_g, dzx, (di + conv_dim) // LANE, name=f"ssd_put_ddt{j}")
            g_in = _mm_tn(rec["h_mix"], dzx, out_dtype=BF16, name=f"ssd_in_dw{j}")
            g_in = g_in[:, :in_dim].T.reshape(N_DEV, in_dim // N_DEV, d)
            dh = _mm_nt(dzx, full['ssd_w_in'][j], out_dtype=BF16, name=f"ssd_in_dx{j}")
            small['ssd_conv_w'][j] = jnp.concatenate([dcw_x, dcw_b, dcw_c], axis=1)
            small['ssd_conv_b'][j] = jnp.concatenate([dcb_x, dcb_b, dcb_c], axis=1).reshape(conv_dim)
            small['ssd_a_log'][j] = vec_acc[:, 0, :heads].sum(axis=0)
            small['ssd_dt_bias'][j] = vec_acc[:, 1, :heads].sum(axis=0)
            small['ssd_d'][j] = ddexp[:, 0, :].reshape(heads, HEAD_DIM).sum(axis=1)
            small['ssd_norm_w'][j] = dnw[:, 0, :].reshape(di)
            g_out = g_out.reshape(N_DEV, di // N_DEV, d)
        else:
            nb_in = sc_w_in.shape[2]
            da = _mm_nt(dxb, full['sc_w_out'][j], out_dtype=BF16, name=f"sc_out_dx{j}")
            g_out = _mm_tn(rec["act"], dxb, out_dtype=BF16, name=f"sc_out_dw{j}")
            dp3, dcw = _sc_act_bwd(rec["p3"], da, _with_tokens(sc_cw[j], chip_step(da)), name=f"sc_act_bwd{j}")
            g_in = _mm_tn(rec["h_mix"], dp3, out_dtype=BF16, name=f"sc_in_dw{j}")
            g_in = jnp.swapaxes(g_in.reshape(d, N_DEV, nb_in), 0, 1)
            dh = _mm_nt(dp3, full['sc_w_in'][j], out_dtype=BF16, name=f"sc_in_dx{j}")
            small['sc_conv_w'][j] = dcw
            g_out = g_out.reshape(N_DEV, g_out.shape[0] // N_DEV, d)
        pending = scatters[2 * i] = _Scatter([g_in, g_out], core_s, f"{order[2 * i][0]}{j}")
        tok = pending.start_pair(dh)
        dx, dxb, dwn = _rmsnorm_bwd(dh, rec["x_mix"], _with_tokens(mix_norm_w[i], tok), dx, name=f"norm_mix_bwd{i}")
        small['mix_norm_w'][i] = dwn.sum(axis=0)
    tok_last = chip_step(dx)

    small_names = [n for n in WEIGHTS if n not in BIG]
    partial = {n: jnp.stack(small[n]) for n in small}
    partial['final_norm_w'] = dw_final.sum(axis=0)
    full_shapes = [partial[n].shape for n in small_names]
    packed = _pack([loss8.sum().reshape(1)] + [partial[n] for n in small_names])
    ar_send, ar_recv, ar_bufs, tok_ar = _split_start(
        [packed, jnp.zeros((N_DEV,) + packed.shape, F32)], _plan_all, N_DEV - 1, tok_last, name="ar_small_start")

    grads, delta, new_m, new_v = {}, {}, {}, {}
    parts = {n: [None] * wt[n].shape[0] for n in BIG}
    for s in range(1, n_sub):
        kind, idx = order[s]
        parts[names_of[kind][0]][idx], parts[names_of[kind][1]][idx] = scatters[s].finish(tok_ar)
    first_in, first_out = names_of[order[0][0]]
    last_out = tok_ar
    jobs = [(n, layer) for n in reversed(BIG) for layer in reversed(range(wt[n].shape[0]))]
    jobs.sort(key=lambda job: parts[job[0]][job[1]] is None)
    chain = {n: None for n in BIG}
    for n, layer in jobs:
        if parts[n][layer] is None:
            parts[first_in][0], parts[first_out][0] = scatters[0].finish(last_out)
        chain[n], last_out = _adamw_layer(wt[n], mom[n], var[n], *parts[n][layer], chip_s, layer, chain[n], last_out,
                                          name=f"adamw_{n}{layer}")
    for n in BIG:
        grads[n], delta[n], new_m[n], new_v[n] = chain[n]
    for dst in (grads, delta, new_m, new_v):
        dst['ssd_w_in'] = jnp.swapaxes(dst['ssd_w_in'], 1, 2)

    mine, slots = _split_wait(ar_bufs, ar_send, ar_recv, last_out, _plan_all, name="ar_small_wait")
    total = _unpack(_sum_slots(mine, slots, me_s, name="ar_small_sum"), [(1,)] + full_shapes)
    loss = total[0].reshape(())
    grads.update(zip(small_names, total[1:]))
    for n in SHARDED_SMALL:
        nb = wt[n].shape[-1]
        grads[n] = lax.dynamic_slice_in_dim(grads[n], me * nb, nb, axis=grads[n].ndim - 1)
    shapes = [wt[n].shape for n in small_names]
    outs = _adamw_small(*[_pack([src[n] for n in small_names]) for src in (wt, grads, mom, var)], name="adamw_small")
    for dst, packed_out in zip((delta, new_m, new_v), outs):
        dst.update(zip(small_names, _unpack(packed_out, shapes)))

    return (loss, dx.reshape(x.shape), *[grads[n] for n in WEIGHTS], *[delta[n] for n in WEIGHTS],
            *[new_m[n] for n in WEIGHTS], *[new_v[n] for n in WEIGHTS])
```

```python
import functools
import math

import jax
import jax.numpy as jnp
from jax import lax
from jax.experimental import pallas as pl
from jax.experimental.pallas import tpu as pltpu

F32 = jnp.float32
BF16 = jnp.bfloat16
MESH = pl.DeviceIdType.MESH

N_DEV = 8
N_CHIP = 4
EPS = 1e-5
HEAD_DIM = 64
STATE = 128
CHUNK = 128
PAIR = 2 * HEAD_DIM
GROUP_W = 8 * HEAD_DIM
HALO = 16
LANE = 128
VMEM_LIMIT = 56 * 1024 * 1024

ADAM_LR = 0.001
ADAM_B1 = 0.9
ADAM_B2 = 0.999
ADAM_EPS = 1e-08
ADAM_WD = 0.01
ADAM_STEP = 10


def _pick(n, candidates):
    for c in candidates:
        if c <= n and n % c == 0:
            return c
    return n


OPERAND_VMEM = 36 * 1024 * 1024


def _pick_k(kd, other, candidates):
    for c in (kd,) + tuple(candidates):
        if c <= kd and kd % c == 0 and 2 * 2 * other * c <= OPERAND_VMEM:
            return c
    return kd


def _params(*sem):
    return pltpu.CompilerParams(dimension_semantics=sem, vmem_limit_bytes=VMEM_LIMIT)


def _sigmoid(x):
    return 0.5 * jnp.tanh(0.5 * x) + 0.5


_DIMS = {
    "nn": (((1,), (0,)), ((), ())),
    "nt": (((1,), (1,)), ((), ())),
    "tn": (((0,), (0,)), ((), ())),
}


def _matmul(mode, a, b, *, grid, a_spec, b_spec, o_spec, out_shape, acc_shape, name, res=None, res_spec=None,
            part_fn=None):
    nk = grid[2]
    dims = _DIMS[mode]
    if part_fn is None:
        part_fn = lambda a_ref, b_ref: lax.dot_general(a_ref[...], b_ref[...], dims, preferred_element_type=F32)

    def body(*refs):
        if res is None:
            a_ref, b_ref, o_ref = refs[:3]
            r_ref, scratch = None, refs[3:]
        else:
            a_ref, b_ref, r_ref, o_ref = refs[:4]
            scratch = refs[4:]
        part = part_fn(a_ref, b_ref)

        def finish(acc):
            if r_ref is not None:
                acc = acc + r_ref[...]
            o_ref[...] = acc.astype(o_ref.dtype)

        if nk == 1:
            finish(part)
        else:
            acc_ref = scratch[0]
            k = pl.program_id(2)

            @pl.when(k == 0)
            def _():
                acc_ref[...] = part

            @pl.when(k > 0)
            def _():
                acc_ref[...] += part

            @pl.when(k == nk - 1)
            def _():
                finish(acc_ref[...])

    in_specs = [a_spec, b_spec] + ([res_spec] if res is not None else [])
    args = (a, b) + ((res,) if res is not None else ())
    return pl.pallas_call(
        body, name=name, grid=grid, in_specs=in_specs, out_specs=o_spec, out_shape=out_shape,
        scratch_shapes=[pltpu.VMEM(acc_shape, F32)] if nk > 1 else [],
        compiler_params=_params("parallel", "parallel", "arbitrary"),
    )(*args)


def _mm_nn(a, b, *, out_dtype, res=None, out_parts=1, name):
    m, kd = a.shape
    n = b.shape[1]
    c = n // out_parts
    tm = _pick(m, (512, 256, 128))
    tn = _pick(c, (1152, 1024, 512, 384, 256, 128))
    tk = _pick_k(kd, tm + tn, (2816, 2048, 1024, 512, 256, 128))
    grid = (n // tn, m // tm, kd // tk)
    if out_parts == 1:
        o_spec = pl.BlockSpec((tm, tn), lambda j, i, k: (i, j))
        out_shape = jax.ShapeDtypeStruct((m, n), out_dtype)
    else:
        o_spec = _stacked_spec(tm, tn, c, lambda j, i, k: (i, j))
        out_shape = jax.ShapeDtypeStruct((out_parts, m, c), out_dtype)
    return _matmul(
        "nn", a, b, res=res, grid=grid, name=name,
        a_spec=pl.BlockSpec((tm, tk), lambda j, i, k: (i, k)),
        b_spec=pl.BlockSpec((tk, tn), lambda j, i, k: (k, j)),
        res_spec=pl.BlockSpec((tm, tn), lambda j, i, k: (i, j)),
        o_spec=o_spec, out_shape=out_shape, acc_shape=(tm, tn))


def _stacked_spec(rows, width, c, row_col):
    per = c // width

    def index(j, i, k):
        r, q = row_col(j, i, k)
        return q // per, r, q % per

    return pl.BlockSpec((None, rows, width), index)


def _mm_nt(a, b, *, out_dtype, name):
    stacked = a.ndim == 3
    m = a.shape[-2]
    n, kd = b.shape
    c = a.shape[-1]
    tm = _pick(m, (512, 256, 128))
    tn = _pick(n, (1408, 1024, 512, 256, 128))
    tk = _pick_k(c, tm + tn, (3456, 2816, 2048, 1024, 512, 384, 256, 128))
    grid = (n // tn, m // tm, kd // tk)
    a_spec = (_stacked_spec(tm, tk, c, lambda j, i, k: (i, k)) if stacked
              else pl.BlockSpec((tm, tk), lambda j, i, k: (i, k)))
    return _matmul(
        "nt", a, b, grid=grid, name=name, a_spec=a_spec,
        b_spec=pl.BlockSpec((tn, tk), lambda j, i, k: (j, k)),
        o_spec=pl.BlockSpec((tm, tn), lambda j, i, k: (i, j)),
        out_shape=jax.ShapeDtypeStruct((m, n), out_dtype), acc_shape=(tm, tn))


def _mm_tn(a, b, *, out_dtype, name):
    stacked = b.ndim == 3
    kd, m = a.shape
    c = b.shape[-1]
    n = c * (b.shape[0] if stacked else 1)
    tm = _pick(m, (512, 256, 128))
    tn = _pick(c, (1152, 1024, 512, 384, 256, 128))
    tk = _pick_k(kd, tm + tn, (2048, 1024, 512, 256, 128))
    grid = (n // tn, m // tm, kd // tk)
    b_spec = (_stacked_spec(tk, tn, c, lambda j, i, k: (k, j)) if stacked
              else pl.BlockSpec((tk, tn), lambda j, i, k: (k, j)))
    return _matmul(
        "tn", a, b, grid=grid, name=name,
        a_spec=pl.BlockSpec((tk, tm), lambda j, i, k: (k, i)), b_spec=b_spec,
        o_spec=pl.BlockSpec((tm, tn), lambda j, i, k: (i, j)),
        out_shape=jax.ShapeDtypeStruct((m, n), out_dtype), acc_shape=(tm, tn))


def _in_tile(nb, c):
    return math.gcd(nb, c)


def _lin_in_fwd(h, wg, parts, *, name):
    t, d = h.shape
    nb = wg.shape[2]
    c = N_DEV * nb // parts
    w = _in_tile(nb, c)
    nbw, cw = nb // w, c // w
    tm = _pick(t, (1024,) if w < 512 else (512, 256, 128))
    grid = (N_DEV * nbw, t // tm, 1)
    return _matmul(
        "nn", h, wg, grid=grid, name=name,
        a_spec=pl.BlockSpec((tm, d), lambda j, i, k: (i, 0)),
        b_spec=pl.BlockSpec((None, d, w), lambda j, i, k: (j // nbw, 0, j % nbw)),
        o_spec=pl.BlockSpec((None, tm, w), lambda j, i, k: (j // cw, i, j % cw)),
        out_shape=jax.ShapeDtypeStruct((parts, t, c), BF16), acc_shape=(tm, w))


def _lin_in_dx(dact, wg, *, name):
    parts, t, c = dact.shape
    d, nb = wg.shape[1], wg.shape[2]
    tm = _pick(t, (512, 256, 128))
    tn = _pick(d, (1024, 512, 256, 128))
    group = _pick_k(c, tm + tn, (2 * nb, nb)) // nb
    per = c // (group * nb)
    grid = (d // tn, t // tm, N_DEV // group)

    def blocks(a_ref, b_ref):
        acc = None
        for q in range(group):
            part = lax.dot_general(a_ref[:, q * nb:(q + 1) * nb], b_ref[q], _DIMS["nt"], preferred_element_type=F32)
            acc = part if acc is None else acc + part
        return acc

    return _matmul(
        "nt", dact, wg, grid=grid, name=name, part_fn=blocks,
        a_spec=pl.BlockSpec((None, tm, group * nb), lambda j, i, k: (k // per, i, k % per)),
        b_spec=pl.BlockSpec((group, tn, nb), lambda j, i, k: (k, j, 0)),
        o_spec=pl.BlockSpec((tm, tn), lambda j, i, k: (i, j)),
        out_shape=jax.ShapeDtypeStruct((t, d), BF16), acc_shape=(tm, tn))


def _lin_in_dw(h, dact, nb, *, name):
    t, d = h.shape
    parts, _, c = dact.shape
    w = _in_tile(nb, c)
    nbw, cw = nb // w, c // w
    tm = _pick(d, (512, 256, 128))
    tk = _pick_k(t, tm + w, (2048, 1024, 512, 256, 128))
    grid = (N_DEV * nbw, d // tm, t // tk)
    return _matmul(
        "tn", h, dact, grid=grid, name=name,
        a_spec=pl.BlockSpec((tk, tm), lambda j, i, k: (k, i)),
        b_spec=pl.BlockSpec((None, tk, w), lambda j, i, k: (j // cw, k, j % cw)),
        o_spec=pl.BlockSpec((None, tm, w), lambda j, i, k: (j // nbw, i, j % nbw)),
        out_shape=jax.ShapeDtypeStruct((N_DEV, d, nb), BF16), acc_shape=(tm, w))


def _fold8(v):
    rows, c = v.shape
    return v.reshape(rows // 8, 8, c).sum(axis=0)


def _accumulate(ref, val, first):
    @pl.when(first)
    def _():
        ref[...] = val

    @pl.when(jnp.logical_not(first))
    def _():
        ref[...] += val


def _tile2(r, c, rows=(256, 128, 64, 32, 16)):
    tr = _pick(r, rows)
    if tr < r or r <= rows[0]:
        return tr, c
    return r, _pick(c, (256, 128))


def _cast_layer(w_stack, layer, me, after, *, name):
    _, r, c = w_stack.shape
    tr, tc = _tile2(r, c)

    def body(me_ref, w_ref, after_ref, o_ref):
        del me_ref, after_ref
        o_ref[...] = w_ref[...].astype(BF16)

    return pl.pallas_call(
        body, name=name,
        grid_spec=pltpu.PrefetchScalarGridSpec(
            num_scalar_prefetch=1, grid=(r // tr, c // tc),
            in_specs=[pl.BlockSpec((None, tr, tc), lambda i, j, me_ref: (layer, i, j)), HBM_ANY],
            out_specs=pl.BlockSpec((None, tr, tc), lambda i, j, me_ref: (me_ref[0], i, j))),
        out_shape=jax.ShapeDtypeStruct((N_DEV, r, c), BF16),
        compiler_params=_params("parallel", "parallel"),
    )(me, w_stack, after)


def _rmsnorm_fwd(x, w, *, name):
    t, d = x.shape
    tt = _pick(t, (256, 128))

    def body(x_ref, w_ref, o_ref):
        xv = x_ref[...]
        r = lax.rsqrt(jnp.mean(xv * xv, axis=1, keepdims=True) + EPS)
        o_ref[...] = (xv * r * w_ref[...]).astype(BF16)

    return pl.pallas_call(
        body, name=name, grid=(t // tt,),
        in_specs=[pl.BlockSpec((tt, d), lambda i: (i, 0)), pl.BlockSpec((1, d), lambda i: (0, 0))],
        out_specs=pl.BlockSpec((tt, d), lambda i: (i, 0)),
        out_shape=jax.ShapeDtypeStruct((t, d), BF16),
        compiler_params=_params("parallel"),
    )(x, w.reshape(1, d))


def _rmsnorm_bwd(dh, x, w, dres, *, name):
    t, d = x.shape
    tt = _pick(t, (256, 128))

    def body(dh_ref, x_ref, w_ref, dres_ref, dx_ref, dxb_ref, dw_ref):
        xv = x_ref[...]
        r = lax.rsqrt(jnp.mean(xv * xv, axis=1, keepdims=True) + EPS)
        xhat = xv * r
        dhv = dh_ref[...].astype(F32)
        dxhat = dhv * w_ref[...]
        dx = dres_ref[...] + r * (dxhat - xhat * jnp.mean(dxhat * xhat, axis=1, keepdims=True))
        dx_ref[...] = dx
        dxb_ref[...] = dx.astype(BF16)
        _accumulate(dw_ref, _fold8(dhv * xhat), pl.program_id(0) == 0)

    row = pl.BlockSpec((tt, d), lambda i: (i, 0))
    return pl.pallas_call(
        body, name=name, grid=(t // tt,),
        in_specs=[row, row, pl.BlockSpec((1, d), lambda i: (0, 0)), row],
        out_specs=[row, row, pl.BlockSpec((8, d), lambda i: (0, 0))],
        out_shape=[jax.ShapeDtypeStruct((t, d), F32), jax.ShapeDtypeStruct((t, d), BF16),
                   jax.ShapeDtypeStruct((8, d), F32)],
        compiler_params=_params("arbitrary"),
    )(dh, x, w.reshape(1, d), dres)


def _loss_head(x, w, target, *, name):
    t, d = x.shape
    tt = _pick(t, (256, 128))

    def body(x_ref, w_ref, tg_ref, dx_ref, dxb_ref, dw_ref, ls_ref):
        xv = x_ref[...]
        wv = w_ref[...]
        r = lax.rsqrt(jnp.mean(xv * xv, axis=1, keepdims=True) + EPS)
        xhat = xv * r
        err = xhat * wv - tg_ref[...]
        dy = err * (1.0 / d)
        dxhat = dy * wv
        dx = r * (dxhat - xhat * jnp.mean(dxhat * xhat, axis=1, keepdims=True))
        dx_ref[...] = dx
        dxb_ref[...] = dx.astype(BF16)
        first = pl.program_id(0) == 0
        _accumulate(dw_ref, _fold8(dy * xhat), first)
        _accumulate(ls_ref, _fold8(err * err) * (0.5 / d), first)

    row = pl.BlockSpec((tt, d), lambda i: (i, 0))
    acc = pl.BlockSpec((8, d), lambda i: (0, 0))
    return pl.pallas_call(
        body, name=name, grid=(t // tt,),
        in_specs=[row, pl.BlockSpec((1, d), lambda i: (0, 0)), row],
        out_specs=[row, row, acc, acc],
        out_shape=[jax.ShapeDtypeStruct((t, d), F32), jax.ShapeDtypeStruct((t, d), BF16),
                   jax.ShapeDtypeStruct((8, d), F32), jax.ShapeDtypeStruct((8, d), F32)],
        compiler_params=_params("arbitrary"),
    )(x, w.reshape(1, d), target)


def _conv_causal(e, tap, width):
    acc = None
    for k in range(width):
        s = width - 1 - k
        term = (e if s == 0 else pltpu.roll(e, s, 0)) * tap(k)
        acc = term if acc is None else acc + term
    return acc


def _conv_anticausal(e, tap, width):
    rows = e.shape[0]
    acc = None
    for k in range(width):
        s = width - 1 - k
        term = (e if s == 0 else pltpu.roll(e, rows - s, 0)) * tap(k)
        acc = term if acc is None else acc + term
    return acc


def _extend(prev, cur, nxt, first, last):
    parts = []
    if prev is not None:
        parts.append(jnp.where(first, 0.0, prev.astype(F32)))
    parts.append(cur.astype(F32))
    if nxt is not None:
        parts.append(jnp.where(last, 0.0, nxt.astype(F32)))
    return jnp.concatenate(parts, axis=0)


def _prev_idx(i, tt):
    return jnp.maximum(i * (tt // HALO) - 1, 0)


def _next_idx(i, tt, t):
    return jnp.minimum((i + 1) * (tt // HALO), t // HALO - 1)


def _ffn_act_fwd(u3, cw, cb, *, name):
    _, t, f = u3.shape
    tt = _pick(t, (512, 256, 128))
    tc = _pick(f, (256, 128))
    width = cw.shape[0]

    def body(u_ref, up_ref, w_ref, b_ref, o_ref, pre_ref):
        first = pl.program_id(1) == 0
        pre = []
        for p in range(2):
            e = _extend(up_ref[p], u_ref[p], None, first, None)
            pre.append(_conv_causal(e, lambda k: w_ref[k, p:p + 1, :], width)[HALO:] + b_ref[p:p + 1, :])
            pre_ref[p] = pre[p].astype(BF16)
        g, v = pre
        o_ref[...] = (g * _sigmoid(g) * v).astype(BF16)

    return pl.pallas_call(
        body, name=name, grid=(f // tc, t // tt),
        in_specs=[pl.BlockSpec((2, tt, tc), lambda j, i: (0, i, j)),
                  pl.BlockSpec((2, HALO, tc), lambda j, i: (0, _prev_idx(i, tt), j)),
                  pl.BlockSpec((width, 2, tc), lambda j, i: (0, 0, j)),
                  pl.BlockSpec((2, tc), lambda j, i: (0, j))],
        out_specs=[pl.BlockSpec((tt, tc), lambda j, i: (i, j)), pl.BlockSpec((2, tt, tc), lambda j, i: (0, i, j))],
        out_shape=[jax.ShapeDtypeStruct((t, f), BF16), jax.ShapeDtypeStruct((2, t, f), BF16)],
        compiler_params=_params("parallel", "parallel"),
    )(u3, u3, cw, cb)


def _ffn_act_bwd(u3, pre3, da, cw, *, name):
    _, t, f = u3.shape
    tt = _pick(t, (512, 256, 128))
    tc = _pick(f, (256, 128))
    width = cw.shape[0]
    nt = t // tt
    rows = tt + HALO

    def body(u_ref, pre_ref, pren_ref, da_ref, dan_ref, w_ref, du_ref, dcw_ref, dcb_ref):
        i = pl.program_id(1)
        first, last = i == 0, i == nt - 1
        g, v = (_extend(None, pre_ref[p], pren_ref[p], None, False) for p in range(2))
        dae = _extend(None, da_ref[...], dan_ref[...], None, last)
        sg = _sigmoid(g)
        dpre = (dae * v * (sg * (1.0 + g * (1.0 - sg))), dae * (g * sg))

        @pl.when(first)
        def _():
            dcw_ref[...] = jnp.zeros_like(dcw_ref)
            dcb_ref[...] = jnp.zeros_like(dcb_ref)

        for p in range(2):
            u = u_ref[p].astype(F32)
            du = None
            for k in range(width):
                s = width - 1 - k
                d = (dpre[p] if s == 0 else pltpu.roll(dpre[p], rows - s, 0))[:tt]
                term = d * w_ref[k, p:p + 1, :]
                du = term if du is None else du + term
                dcw_ref[k, p:p + 1, :] += jnp.sum(d * u, axis=0, keepdims=True)
                if s == 0:
                    dcb_ref[p:p + 1, :] += jnp.sum(d, axis=0, keepdims=True)
            du_ref[p] = du.astype(BF16)

    cur3 = pl.BlockSpec((2, tt, tc), lambda j, i: (0, i, j))
    return pl.pallas_call(
        body, name=name, grid=(f // tc, nt),
        in_specs=[cur3, cur3,
                  pl.BlockSpec((2, HALO, tc), lambda j, i: (0, _next_idx(i, tt, t), j)),
                  pl.BlockSpec((tt, tc), lambda j, i: (i, j)),
                  pl.BlockSpec((HALO, tc), lambda j, i: (_next_idx(i, tt, t), j)),
                  pl.BlockSpec((width, 2, tc), lambda j, i: (0, 0, j))],
        out_specs=[cur3,
                   pl.BlockSpec((width, 2, tc), lambda j, i: (0, 0, j)),
                   pl.BlockSpec((2, tc), lambda j, i: (0, j))],
        out_shape=[jax.ShapeDtypeStruct((2, t, f), BF16), jax.ShapeDtypeStruct((width, 2, f), F32),
                   jax.ShapeDtypeStruct((2, f), F32)],
        compiler_params=_params("parallel", "arbitrary"),
    )(u3, pre3, pre3, da, da, cw)


def _sc_act_fwd(p3, cw, *, name):
    _, t, c = p3.shape
    tt = _pick(t, (512, 256, 128))
    tc = _pick(c, (512, 256, 128))
    width = cw.shape[0]

    def body(p_ref, pp_ref, w_ref, o_ref):
        first = pl.program_id(1) == 0
        q = _extend(pp_ref[1], p_ref[1], None, first, None) * _extend(pp_ref[2], p_ref[2], None, first, None)
        cq = _conv_causal(q, lambda k: w_ref[k:k + 1, :], width)[HALO:]
        o_ref[...] = (p_ref[0].astype(F32) * cq).astype(BF16)

    return pl.pallas_call(
        body, name=name, grid=(c // tc, t // tt),
        in_specs=[pl.BlockSpec((3, tt, tc), lambda j, i: (0, i, j)),
                  pl.BlockSpec((3, HALO, tc), lambda j, i: (0, _prev_idx(i, tt), j)),
                  pl.BlockSpec((width, tc), lambda j, i: (0, j))],
        out_specs=pl.BlockSpec((tt, tc), lambda j, i: (i, j)),
        out_shape=jax.ShapeDtypeStruct((t, c), BF16),
        compiler_params=_params("parallel", "parallel"),
    )(p3, p3, cw)


def _sc_act_bwd(p3, da, cw, *, name):
    _, t, c = p3.shape
    tt = _pick(t, (512, 256, 128))
    tc = _pick(c, (512, 256, 128))
    width = cw.shape[0]
    nt = t // tt
    ctr = slice(HALO, HALO + tt)

    def body(p_ref, pp_ref, pn_ref, da_ref, dan_ref, w_ref, dp_ref, dcw_ref):
        i = pl.program_id(1)
        first, last = i == 0, i == nt - 1
        tap = lambda k: w_ref[k:k + 1, :]
        bg, cg, hh = (_extend(pp_ref[p], p_ref[p], pn_ref[p], first, last) for p in range(3))
        q = cg * hh
        cq = _conv_causal(q, tap, width)
        dae = _extend(jnp.zeros((HALO, tc), F32), da_ref[...], dan_ref[...], False, last)
        dcq = dae * bg
        dq = _conv_anticausal(dcq, tap, width)[ctr]
        dp_ref[0] = (dae * cq)[ctr].astype(BF16)
        dp_ref[1] = (dq * hh[ctr]).astype(BF16)
        dp_ref[2] = (dq * cg[ctr]).astype(BF16)

        @pl.when(first)
        def _():
            dcw_ref[...] = jnp.zeros_like(dcw_ref)

        dc = dcq[ctr]
        for k in range(width):
            s = width - 1 - k
            qs = (q if s == 0 else pltpu.roll(q, s, 0))[ctr]
            dcw_ref[k:k + 1, :] += jnp.sum(dc * qs, axis=0, keepdims=True)

    return pl.pallas_call(
        body, name=name, grid=(c // tc, nt),
        in_specs=[pl.BlockSpec((3, tt, tc), lambda j, i: (0, i, j)),
                  pl.BlockSpec((3, HALO, tc), lambda j, i: (0, _prev_idx(i, tt), j)),
                  pl.BlockSpec((3, HALO, tc), lambda j, i: (0, _next_idx(i, tt, t), j)),
                  pl.BlockSpec((tt, tc), lambda j, i: (i, j)),
                  pl.BlockSpec((HALO, tc), lambda j, i: (_next_idx(i, tt, t), j)),
                  pl.BlockSpec((width, tc), lambda j, i: (0, j))],
        out_specs=[pl.BlockSpec((3, tt, tc), lambda j, i: (0, i, j)),
                   pl.BlockSpec((width, tc), lambda j, i: (0, j))],
        out_shape=[jax.ShapeDtypeStruct((3, t, c), BF16), jax.ShapeDtypeStruct((width, c), F32)],
        compiler_params=_params("parallel", "arbitrary"),
    )(p3, p3, p3, da, da, cw)


def _ssd_conv_fwd(zx, cw, cb, col0, *, name):
    t = zx.shape[0]
    width, c = cw.shape
    tt = _pick(t, (512, 256, 128))
    tc = _pick(math.gcd(c, col0), (512, 256, 128))
    off = col0 // tc

    def body(x_ref, xp_ref, w_ref, b_ref, o_ref, pre_ref):
        first = pl.program_id(1) == 0
        e = _extend(xp_ref[...], x_ref[...], None, first, None)
        pre = _conv_causal(e, lambda k: w_ref[k:k + 1, :], width)[HALO:] + b_ref[...]
        pre_ref[...] = pre.astype(BF16)
        o_ref[...] = (pre * _sigmoid(pre)).astype(BF16)

    out = pl.BlockSpec((tt, tc), lambda j, i: (i, j))
    return pl.pallas_call(
        body, name=name, grid=(c // tc, t // tt),
        in_specs=[pl.BlockSpec((tt, tc), lambda j, i: (i, off + j)),
                  pl.BlockSpec((HALO, tc), lambda j, i: (_prev_idx(i, tt), off + j)),
                  pl.BlockSpec((width, tc), lambda j, i: (0, j)),
                  pl.BlockSpec((1, tc), lambda j, i: (0, j))],
        out_specs=[out, out],
        out_shape=[jax.ShapeDtypeStruct((t, c), BF16)] * 2,
        compiler_params=_params("parallel", "parallel"),
    )(zx, zx, cw, cb)


def _ssd_conv_bwd(zx, pre, dxc, cw, dzx, col0, woff, *, name):
    t = zx.shape[0]
    width = cw.shape[0]
    c = dxc.shape[1]
    tt = _pick(t, (512, 256, 128))
    tc = _pick(math.gcd(math.gcd(c, col0), woff) if woff else math.gcd(c, col0), (512, 256, 128))
    nt = t // tt
    xoff, wo = (col0 + woff) // tc, woff // tc
    rows = tt + HALO

    def body(x_ref, p_ref, pn_ref, d_ref, dn_ref, w_ref, dzx_in, dzx_ref, dcw_ref, dcb_ref):
        del dzx_in
        i = pl.program_id(1)
        first, last = i == 0, i == nt - 1
        pre_e = _extend(None, p_ref[...], pn_ref[...], None, False)
        de = _extend(None, d_ref[...], dn_ref[...], None, last)
        sg = _sigmoid(pre_e)
        dpre = de * (sg * (1.0 + pre_e * (1.0 - sg)))

        @pl.when(first)
        def _():
            dcw_ref[...] = jnp.zeros_like(dcw_ref)
            dcb_ref[...] = jnp.zeros_like(dcb_ref)

        xv = x_ref[...].astype(F32)
        dx = None
        for k in range(width):
            s = width - 1 - k
            d = (dpre if s == 0 else pltpu.roll(dpre, rows - s, 0))[:tt]
            term = d * w_ref[k:k + 1, :]
            dx = term if dx is None else dx + term
            dcw_ref[k:k + 1, :] += jnp.sum(d * xv, axis=0, keepdims=True)
            if s == 0:
                dcb_ref[...] += jnp.sum(d, axis=0, keepdims=True)
        dzx_ref[...] = dx.astype(BF16)

    return pl.pallas_call(
        body, name=name, grid=(c // tc, nt),
        in_specs=[pl.BlockSpec((tt, tc), lambda j, i: (i, xoff + j)),
                  pl.BlockSpec((tt, tc), lambda j, i: (i, wo + j)),
                  pl.BlockSpec((HALO, tc), lambda j, i: (_next_idx(i, tt, t), wo + j)),
                  pl.BlockSpec((tt, tc), lambda j, i: (i, j)),
                  pl.BlockSpec((HALO, tc), lambda j, i: (_next_idx(i, tt, t), j)),
                  pl.BlockSpec((width, tc), lambda j, i: (0, wo + j)),
                  pl.BlockSpec(memory_space=pl.ANY)],
        out_specs=[pl.BlockSpec((tt, tc), lambda j, i: (i, xoff + j)),
                   pl.BlockSpec((width, tc), lambda j, i: (0, j)),
                   pl.BlockSpec((1, tc), lambda j, i: (0, j))],
        out_shape=[jax.ShapeDtypeStruct(dzx.shape, dzx.dtype), jax.ShapeDtypeStruct((width, c), F32),
                   jax.ShapeDtypeStruct((1, c), F32)],
        input_output_aliases={6: 0},
        compiler_params=_params("parallel", "arbitrary"),
    )(zx, pre, pre, dxc, dxc, cw, dzx)


def _ssd_put_ddt(ddt_g, dzx, col, *, name):
    g, t, _ = ddt_g.shape
    tt = _pick(t, (512, 256, 128))

    def body(d_ref, dzx_in, dzx_ref):
        del dzx_in
        dzx_ref[...] = jnp.sum(d_ref[...], axis=0).astype(BF16)

    return pl.pallas_call(
        body, name=name, grid=(t // tt,),
        in_specs=[pl.BlockSpec((g, tt, LANE), lambda i: (0, i, 0)), pl.BlockSpec(memory_space=pl.ANY)],
        out_specs=pl.BlockSpec((tt, LANE), lambda i: (i, col)),
        out_shape=jax.ShapeDtypeStruct(dzx.shape, dzx.dtype),
        input_output_aliases={1: 0},
        compiler_params=_params("parallel"),
    )(ddt_g, dzx)


def _dot(a, b, mode):
    return lax.dot_general(a, b, _DIMS[mode], preferred_element_type=F32)


def _dot_exact(m01, v, mode="nn"):
    hi = v.astype(BF16)
    r1 = v - hi.astype(F32)
    mid = r1.astype(BF16)
    lo = (r1 - mid.astype(F32)).astype(BF16)
    return _dot(m01, hi, mode) + _dot(m01, mid, mode) + _dot(m01, lo, mode)


def _softplus(x):
    return jnp.maximum(x, 0.0) + jnp.log(1.0 + jnp.exp(-jnp.abs(x)))


def _head_vectors(g, dt_raw, bias, alog):
    n = CHUNK
    dt = _softplus(dt_raw + bias)
    a = -jnp.exp(alog)
    tri = (lax.broadcasted_iota(jnp.int32, (n, n), 0) >= lax.broadcasted_iota(jnp.int32, (n, n), 1)).astype(BF16)
    cs = _dot_exact(tri, dt * a)
    return dt, a, cs, cs.T


def _col(v, lane_ids, h):
    return jnp.sum(jnp.where(lane_ids == h, v, 0.0), axis=1, keepdims=True)


def _row(vt, sub_ids, h):
    return jnp.sum(jnp.where(sub_ids == h, vt, 0.0), axis=0, keepdims=True)


def _ssd_specs(di, bc, nc, rev):
    cidx = (lambda c: nc - 1 - c) if rev else (lambda c: c)
    wide = lambda off: pl.BlockSpec((CHUNK, GROUP_W), lambda g, c: (cidx(c), off + g))
    lane = lambda off: pl.BlockSpec((CHUNK, LANE), lambda g, c: (cidx(c), off + g))
    fixed = lambda off: pl.BlockSpec((CHUNK, LANE), lambda g, c: (cidx(c), off))
    vec = pl.BlockSpec((1, LANE), lambda g, c: (0, 0))
    gvec = pl.BlockSpec((1, GROUP_W), lambda g, c: (0, g))
    state = pl.BlockSpec((None, None, 4, PAIR, STATE), lambda g, c: (g, cidx(c), 0, 0, 0))
    return wide, lane, fixed, vec, gvec, state


def _ssd_fwd(xbc, zx, bias, alog, dexp, nw, *, name):
    t = xbc.shape[0]
    di = nw.shape[1]
    bc = (xbc.shape[1] - di) // 2
    ng, nc = di // GROUP_W, t // CHUNK
    wide, lane, fixed, vec, gvec, state = _ssd_specs(di, bc, nc, rev=False)

    def body(xs_ref, b_ref, c_ref, dt_ref, z_ref, bias_ref, alog_ref, dexp_ref, nw_ref,
             yn_ref, y_ref, st_ref, s_scr):
        g, c = pl.program_id(0), pl.program_id(1)

        @pl.when(c == 0)
        def _():
            s_scr[...] = jnp.zeros_like(s_scr)

        n = CHUNK
        dt, a, cs, cst = _head_vectors(g, dt_ref[...].astype(F32), bias_ref[...], alog_ref[...])
        lane_ids = lax.broadcasted_iota(jnp.int32, (n, LANE), 1)
        sub_ids = lax.broadcasted_iota(jnp.int32, (LANE, n), 0)
        causal = lax.broadcasted_iota(jnp.int32, (n, n), 0) >= lax.broadcasted_iota(jnp.int32, (n, n), 1)
        half = lax.broadcasted_iota(jnp.int32, (1, PAIR), 1) < HEAD_DIM
        half_rows = lax.broadcasted_iota(jnp.int32, (PAIR, 1), 0) < HEAD_DIM
        bm, cm = b_ref[...], c_ref[...]
        gm = _dot(cm, bm, "nt")
        x = xs_ref[...].astype(F32)
        ys = []
        for q in range(4):
            h0 = g * 8 + 2 * q
            col = [_col(cs, lane_ids, h0 + e) for e in range(2)]
            row = [_row(cst, sub_ids, h0 + e) for e in range(2)]
            dtc = [_col(dt, lane_ids, h0 + e) for e in range(2)]
            last = [col[e][n - 1:n, :] for e in range(2)]
            xd = x[:, q * PAIR:(q + 1) * PAIR] * jnp.where(half, dtc[0], dtc[1])
            xd_bf = xd.astype(BF16)
            yd = []
            for e in range(2):
                lm = jnp.exp(jnp.where(causal, col[e] - row[e], -1e30))
                yd.append(_dot((gm * lm).astype(BF16), xd_bf, "nn"))
            s = s_scr[q]
            st_ref[q] = s
            ecs = jnp.where(half, jnp.exp(col[0]), jnp.exp(col[1]))
            dte = jnp.where(half, jnp.exp(last[0] - col[0]), jnp.exp(last[1] - col[1]))
            yoff = ecs * _dot(cm, s.astype(BF16), "nt")
            snew = _dot((xd * dte).astype(BF16), bm, "tn")
            s_scr[q] = s * jnp.where(half_rows, jnp.exp(last[0]), jnp.exp(last[1])) + snew
            ys.append(jnp.where(half, yd[0], yd[1]) + yoff)
        y = jnp.concatenate(ys, axis=1) + dexp_ref[...] * x
        y_ref[...] = y.astype(BF16)
        z = z_ref[...].astype(F32)
        yg = y * (z * _sigmoid(z))
        r = lax.rsqrt(jnp.mean(yg * yg, axis=1, keepdims=True) + EPS)
        yn_ref[...] = (yg * r * nw_ref[...]).astype(BF16)

    dtcol = (2 * di + 2 * bc) // LANE
    return pl.pallas_call(
        body, name=name, grid=(ng, nc),
        in_specs=[wide(0), lane(di // LANE), lane((di + bc) // LANE), fixed(dtcol), wide(0),
                  vec, vec, gvec, gvec],
        out_specs=[wide(0), wide(0), state],
        out_shape=[jax.ShapeDtypeStruct((t, di), BF16), jax.ShapeDtypeStruct((t, di), BF16),
                   jax.ShapeDtypeStruct((ng, nc, 4, PAIR, STATE), F32)],
        scratch_shapes=[pltpu.VMEM((4, PAIR, STATE), F32)],
        compiler_params=_params("parallel", "arbitrary"),
    )(xbc, xbc, xbc, zx, zx, bias, alog, dexp, nw)


def _ssd_bwd(dyn, y, xbc, zx, states, bias, alog, dexp, nw, *, name):
    t = xbc.shape[0]
    di = nw.shape[1]
    bc = (xbc.shape[1] - di) // 2
    ng, nc = di // GROUP_W, t // CHUNK
    wide, lane, fixed, vec, gvec, state = _ssd_specs(di, bc, nc, rev=True)
    acc = lambda w: pl.BlockSpec((None, 8, w), lambda g, c: (g, 0, 0))

    def body(dyn_ref, y_ref, z_ref, nw_ref, xs_ref, b_ref, c_ref, dt_ref, bias_ref, alog_ref, dexp_ref, st_ref,
             dz_ref, dxs_ref, db_ref, dc_ref, ddt_ref, small_ref, dnw_ref, ddexp_ref, ds_scr):
        g, c = pl.program_id(0), pl.program_id(1)

        @pl.when(c == 0)
        def _():
            ds_scr[...] = jnp.zeros_like(ds_scr)
            small_ref[...] = jnp.zeros_like(small_ref)
            dnw_ref[...] = jnp.zeros_like(dnw_ref)
            ddexp_ref[...] = jnp.zeros_like(ddexp_ref)

        n = CHUNK
        yv = y_ref[...].astype(F32)
        z = z_ref[...].astype(F32)
        sz = _sigmoid(z)
        silu = z * sz
        yg = yv * silu
        r = lax.rsqrt(jnp.mean(yg * yg, axis=1, keepdims=True) + EPS)
        yhat = yg * r
        dynv = dyn_ref[...].astype(F32)
        dnw_ref[0:1, :] += jnp.sum(dynv * yhat, axis=0, keepdims=True)
        dyhat = dynv * nw_ref[...]
        dyg = r * (dyhat - yhat * jnp.mean(dyhat * yhat, axis=1, keepdims=True))
        dz_ref[...] = (dyg * yv * (sz * (1.0 + z * (1.0 - sz)))).astype(BF16)
        dy = dyg * silu

        dt_in = dt_ref[...].astype(F32) + bias_ref[...]
        dt, a, cs, cst = _head_vectors(g, dt_ref[...].astype(F32), bias_ref[...], alog_ref[...])
        lane_ids = lax.broadcasted_iota(jnp.int32, (n, LANE), 1)
        sub_ids = lax.broadcasted_iota(jnp.int32, (LANE, n), 0)
        ri = lax.broadcasted_iota(jnp.int32, (n, n), 0)
        ci = lax.broadcasted_iota(jnp.int32, (n, n), 1)
        causal, causal_t = ri >= ci, ci >= ri
        is_last = lax.broadcasted_iota(jnp.int32, (n, 1), 0) == n - 1
        half = lax.broadcasted_iota(jnp.int32, (1, PAIR), 1) < HEAD_DIM
        half_rows = lax.broadcasted_iota(jnp.int32, (PAIR, 1), 0) < HEAD_DIM
        bm, cm = b_ref[...], c_ref[...]
        bf = bm.astype(F32)
        gm, gmt = _dot(cm, bm, "nt"), _dot(bm, cm, "nt")
        x = xs_ref[...].astype(F32)
        dexp = dexp_ref[...]

        dg_sum = jnp.zeros((n, n), F32)
        dgt_sum = jnp.zeros((n, n), F32)
        db_off = jnp.zeros((n, STATE), F32)
        dc_off = jnp.zeros((n, STATE), F32)
        dcs_blk = jnp.zeros((n, LANE), F32)
        ddt_blk = jnp.zeros((n, LANE), F32)
        dxs = []
        for q in range(4):
            h0 = g * 8 + 2 * q
            sl = slice(q * PAIR, (q + 1) * PAIR)
            col = [_col(cs, lane_ids, h0 + e) for e in range(2)]
            row = [_row(cst, sub_ids, h0 + e) for e in range(2)]
            dtc = [_col(dt, lane_ids, h0 + e) for e in range(2)]
            last = [col[e][n - 1:n, :] for e in range(2)]
            xp, dyp = x[:, sl], dy[:, sl]
            dtp = jnp.where(half, dtc[0], dtc[1])
            xd = xp * dtp
            xd_bf, dyp_bf = xd.astype(BF16), dyp.astype(BF16)
            ecs = jnp.where(half, jnp.exp(col[0]), jnp.exp(col[1]))
            dte = jnp.where(half, jnp.exp(last[0] - col[0]), jnp.exp(last[1] - col[1]))
            s, ds = st_ref[q], ds_scr[q]
            s_bf, ds_bf = s.astype(BF16), ds.astype(BF16)
            yoff = ecs * _dot(cm, s_bf, "nt")
            edy_bf = (ecs * dyp).astype(BF16)
            dc_off += _dot(edy_bf, s_bf, "nn")
            bds = _dot(bm, ds_bf, "nt")
            sds = s * ds
            zs = []
            for e in range(2):
                msk = half if e == 0 else jnp.logical_not(half)
                msk_rows = half_rows if e == 0 else jnp.logical_not(half_rows)
                lm = jnp.exp(jnp.where(causal, col[e] - row[e], -1e30))
                lmt = jnp.exp(jnp.where(causal_t, row[e] - col[e], -1e30))
                dym_bf = jnp.where(msk, dyp, 0.0).astype(BF16)
                xdm_bf = jnp.where(msk, xd, 0.0).astype(BF16)
                dm = _dot(dym_bf, xd_bf, "nt")
                dmt = _dot(xdm_bf, dyp_bf, "nt")
                m, mt = gm * lm, gmt * lmt
                dcs = jnp.sum(dm * m, axis=1, keepdims=True) - jnp.sum(dmt * mt, axis=1, keepdims=True)
                dg_sum += dm * lm
                dgt_sum += dmt * lmt
                zs.append(_dot(mt.astype(BF16), dyp_bf, "nn"))
                we = _dot(xdm_bf, ds_bf, "nn")
                dte_col = jnp.exp(last[e] - col[e])
                te = dte_col * jnp.sum(we * bf, axis=1, keepdims=True)
                db_off += dte_col * we
                dcs += jnp.sum(jnp.where(msk, dyp * yoff, 0.0), axis=1, keepdims=True) - te
                tail = jnp.exp(last[e]) * jnp.sum(jnp.where(msk_rows, sds, 0.0), keepdims=True) \
                    + jnp.sum(te, keepdims=True)
                dcs += jnp.where(is_last, tail, 0.0)
                dcs_blk += jnp.where(lane_ids == h0 + e, dcs, 0.0)
            dxd = jnp.where(half, zs[0], zs[1]) + dte * bds
            dxs.append(dxd * dtp + dexp[:, sl] * dyp)
            ddexp_ref[0:1, sl] += jnp.sum(dyp * xp, axis=0, keepdims=True)
            rs = dxd * xp
            for e in range(2):
                msk = half if e == 0 else jnp.logical_not(half)
                ddt_blk += jnp.where(lane_ids == h0 + e, jnp.sum(jnp.where(msk, rs, 0.0), axis=1, keepdims=True), 0.0)
            ds_scr[q] = ds * jnp.where(half_rows, jnp.exp(last[0]), jnp.exp(last[1])) + _dot(edy_bf, cm, "tn")

        dxs_ref[...] = jnp.concatenate(dxs, axis=1).astype(BF16)
        dc_ref[...] = (_dot(dg_sum.astype(BF16), bm, "nn") + dc_off).astype(BF16)
        db_ref[...] = (_dot(dgt_sum.astype(BF16), cm, "nn") + db_off).astype(BF16)
        upper = (ri <= ci).astype(BF16)
        dda = _dot_exact(upper, dcs_blk)
        ddt = dda * a + ddt_blk
        small_ref[0:1, :] += jnp.sum(dda * dt, axis=0, keepdims=True) * a
        ddt_raw = ddt * _sigmoid(dt_in)
        small_ref[1:2, :] += jnp.sum(ddt_raw, axis=0, keepdims=True)
        ddt_ref[...] = ddt_raw

    dtcol = (2 * di + 2 * bc) // LANE
    tot = 2 * di + 2 * bc + LANE
    return pl.pallas_call(
        body, name=name, grid=(ng, nc),
        in_specs=[wide(0), wide(0), wide(0), gvec, wide(0), lane(di // LANE), lane((di + bc) // LANE),
                  fixed(dtcol), vec, vec, gvec, state],
        out_specs=[wide(0), wide(0), lane(0), lane(0),
                   pl.BlockSpec((None, CHUNK, LANE), lambda g, c: (g, nc - 1 - c, 0)),
                   acc(LANE), acc(GROUP_W), acc(GROUP_W)],
        out_shape=[jax.ShapeDtypeStruct((t, tot), BF16), jax.ShapeDtypeStruct((t, di), BF16),
                   jax.ShapeDtypeStruct((t, bc), BF16), jax.ShapeDtypeStruct((t, bc), BF16),
                   jax.ShapeDtypeStruct((ng, t, LANE), F32), jax.ShapeDtypeStruct((ng, 8, LANE), F32),
                   jax.ShapeDtypeStruct((ng, 8, GROUP_W), F32), jax.ShapeDtypeStruct((ng, 8, GROUP_W), F32)],
        scratch_shapes=[pltpu.VMEM((4, PAIR, STATE), F32)],
        compiler_params=_params("parallel", "arbitrary"),
    )(dyn, y, zx, nw, xbc, xbc, xbc, zx, bias, alog, dexp, states)


HBM_ANY = pl.BlockSpec(memory_space=pl.ANY)


def _place():
    x, y, c = lax.axis_index("x"), lax.axis_index("y"), lax.axis_index("c")
    chips = [(1 - x, y), (x, 1 - y), (1 - x, 1 - y)]
    return x, y, c, chips


def _all_gather(arrs, *, name, inplace=False):
    n = len(arrs)

    def body(*refs):
        ins, outs = refs[:n], refs[n:2 * n]
        send, recv, loc = refs[2 * n:]
        x, y, c, chips = _place()
        me, sib = (x, y, c), (x, y, 1 - c)

        def blk(a, p):
            return outs[a].at[4 * p[0] + 2 * p[1] + p[2]]

        def cp(a, k, block, to, src=None):
            return pltpu.make_async_remote_copy(
                src_ref=blk(a, block) if src is None else src, dst_ref=blk(a, block),
                send_sem=send.at[a * 7 + k], recv_sem=recv.at[a * 7 + k], device_id=to, device_id_type=MESH)

        src = [None if inplace else ins[a] for a in range(n)]
        mine = [] if inplace else [pltpu.make_async_copy(ins[a], blk(a, me), loc.at[a]) for a in range(n)]
        for m in mine:
            m.start()
        started = []
        for a in range(n):
            started.append(cp(a, 0, me, sib, src=src[a]))
            started += [cp(a, 1 + j, me, (*chip, c), src=src[a]) for j, chip in enumerate(chips)]
        for s in started:
            s.start()
        for j, chip in enumerate(chips):
            for a in range(n):
                cp(a, 1 + j, (*chip, c), me).wait_recv()
                fwd = cp(a, 4 + j, (*chip, c), sib)
                fwd.start()
                started.append(fwd)
        for a in range(n):
            cp(a, 0, sib, me).wait_recv()
            for j, chip in enumerate(chips):
                cp(a, 4 + j, (*chip, 1 - c), me).wait_recv()
        for s in started:
            s.wait_send()
        for m in mine:
            m.wait()

    return pl.pallas_call(
        body, name=name,
        in_specs=[HBM_ANY] * n, out_specs=[HBM_ANY] * n,
        out_shape=[jax.ShapeDtypeStruct(a.shape if inplace else (N_DEV,) + a.shape, a.dtype) for a in arrs],
        input_output_aliases={a: a for a in range(n)} if inplace else {},
        scratch_shapes=[pltpu.SemaphoreType.DMA((7 * n,)), pltpu.SemaphoreType.DMA((7 * n,)),
                        pltpu.SemaphoreType.DMA((n,))],
    )(*arrs)


HBM_SPEC = pl.BlockSpec(memory_space=pltpu.HBM)
SEM_SPEC = pl.BlockSpec(memory_space=pltpu.SEMAPHORE)
SPLIT_EFFECT = pltpu.SideEffectType.DATAFLOW_SIDE_EFFECTING


def _split_start(arrs, plan, n_copies, after, *, name):
    m = len(arrs)

    def body(*refs):
        send, recv, token = refs[m + 1], refs[m + 2], refs[-1]
        for i, (src, dst, to) in enumerate(plan(refs[:m])):
            pltpu.make_async_remote_copy(src_ref=src, dst_ref=dst, send_sem=send.at[i], recv_sem=recv.at[i],
                                         device_id=to, device_id_type=MESH).start()
        token[...] = jnp.zeros_like(token)

    outs = pl.pallas_call(
        body, name=name,
        out_shape=(pltpu.SemaphoreType.DMA((n_copies,)), pltpu.SemaphoreType.DMA((n_copies,)),
                   *[pltpu.HBM(a.shape, a.dtype) for a in arrs], jax.ShapeDtypeStruct((8, LANE), F32)),
        in_specs=[HBM_SPEC] * m + [HBM_ANY],
        out_specs=(SEM_SPEC, SEM_SPEC, *[HBM_SPEC] * m, pl.BlockSpec(memory_space=pltpu.VMEM)),
        input_output_aliases={i: 2 + i for i in range(m)},
        compiler_params=pltpu.CompilerParams(has_side_effects=SPLIT_EFFECT),
    )(*[pltpu.with_memory_space_constraint(a, pltpu.HBM) for a in arrs], after)
    return outs[0], outs[1], list(outs[2:2 + m]), outs[-1]


def _split_wait(arrs, send, recv, after, plan, *, name):
    m = len(arrs)

    def body(*refs):
        send_ref, recv_ref = refs[m], refs[m + 1]
        for i, (src, dst, to) in enumerate(plan(refs[:m])):
            cp = pltpu.make_async_remote_copy(src_ref=src, dst_ref=dst, send_sem=send_ref.at[i],
                                              recv_sem=recv_ref.at[i], device_id=to, device_id_type=MESH)
            cp.wait_send()
            cp.wait_recv()

    outs = pl.pallas_call(
        body, name=name,
        out_shape=[pltpu.HBM(a.shape, a.dtype) for a in arrs],
        in_specs=[HBM_SPEC] * m + [SEM_SPEC, SEM_SPEC, HBM_ANY], out_specs=[HBM_SPEC] * m,
        input_output_aliases={i: i for i in range(m)},
        compiler_params=pltpu.CompilerParams(has_side_effects=SPLIT_EFFECT),
    )(*arrs, send, recv, after)
    return list(outs)


def _dev(p):
    return 4 * p[0] + 2 * p[1] + p[2]


def _plan_gather_ici(bufs):
    x, y, c, chips = _place()
    me = _dev((x, y, c))
    peers = [(x, y, 1 - c)] + [(*chip, c) for chip in chips]
    return [(b.at[me], b.at[me], p) for b in bufs for p in peers]


def _plan_gather_d2d(bufs):
    x, y, c, chips = _place()
    return [(b.at[_dev((*chip, c))], b.at[_dev((*chip, c))], (x, y, 1 - c)) for b in bufs for chip in chips]


def _plan_pair(refs):
    n = len(refs) // 2
    x, y, c, _ = _place()
    return [(refs[a].at[2 * k + 1 - c], refs[n + a].at[k], (x, y, 1 - c)) for a in range(n) for k in range(N_CHIP)]


def _plan_chip(refs):
    n = len(refs) // 2
    x, y, c, chips = _place()
    return [(refs[a].at[2 * chip[0] + chip[1]], refs[n + a].at[j], (*chip, c))
            for a in range(n) for j, chip in enumerate(chips)]


def _land(shape, dtype):
    return lax.empty(shape, dtype)


def _with_tokens(v, *tokens):
    for t in tokens:
        if t is not None:
            v = v + t[0, 0].astype(v.dtype)
    return v


def _add_pair(grad, got, core, *, name):
    k, r, c = got.shape
    tr, tc = _tile2(r, c, rows=(1024, 704, 512, 256, 128, 64, 32, 16))

    def body(core_ref, a_ref, b_ref, o_ref):
        del core_ref
        o_ref[...] = (a_ref[...].astype(F32) + b_ref[...].astype(F32)).astype(BF16)

    spec = pl.BlockSpec((None, tr, tc), lambda q, i, j, core_ref: (q, i, j))
    return pl.pallas_call(
        body, name=name,
        grid_spec=pltpu.PrefetchScalarGridSpec(
            num_scalar_prefetch=1, grid=(k, r // tr, c // tc),
            in_specs=[pl.BlockSpec((None, tr, tc), lambda q, i, j, core_ref: (2 * q + core_ref[0], i, j)), spec],
            out_specs=spec),
        out_shape=jax.ShapeDtypeStruct(got.shape, BF16),
        compiler_params=_params("parallel", "parallel", "parallel"),
    )(core, grad, got)


def _plan_all(refs):
    x, y, c, _ = _place()
    me = _dev((x, y, c))
    plan = []
    for rel in range(1, N_DEV):
        fx, fy, fc = rel >> 2 & 1, rel >> 1 & 1, rel & 1
        plan.append((refs[0], refs[1].at[me], ((1 - x) if fx else x, (1 - y) if fy else y, (1 - c) if fc else c)))
    return plan


def _sum_slots(v, land, me, *, name):
    def body(me_ref, v_ref, land_ref, o_ref):
        acc = None
        for dev in range(N_DEV):
            term = jnp.where(me_ref[0] == dev, v_ref[...], land_ref[dev])
            acc = term if acc is None else acc + term
        o_ref[...] = acc

    vm = pl.BlockSpec(memory_space=pltpu.VMEM)
    return pl.pallas_call(
        body, name=name,
        grid_spec=pltpu.PrefetchScalarGridSpec(num_scalar_prefetch=1, grid=(), in_specs=[vm, vm], out_specs=vm),
        out_shape=jax.ShapeDtypeStruct(v.shape, F32),
        compiler_params=pltpu.CompilerParams(vmem_limit_bytes=VMEM_LIMIT),
    )(me, v, land)


def _adamw_math(w, g, m, v):
    m = ADAM_B1 * m + (1.0 - ADAM_B1) * g
    v = ADAM_B2 * v + (1.0 - ADAM_B2) * (g * g)
    m_hat = m / (1.0 - ADAM_B1 ** ADAM_STEP)
    v_hat = v / (1.0 - ADAM_B2 ** ADAM_STEP)
    delta = -ADAM_LR * (m_hat / (jnp.sqrt(v_hat) + ADAM_EPS) + ADAM_WD * w)
    return delta, m, v


def _adamw_layer(w, m, v, sums, recv, chip, layer, prev, after, *, name):
    nl, r, c = w.shape
    tr, tc = _tile2(r, c)

    def body(chip_ref, w_ref, m_ref, v_ref, s_ref, p_ref, *rest):
        del chip_ref
        g_ref, d_ref, mo_ref, vo_ref, token_ref = rest[-5:]
        g = s_ref[...].astype(F32)
        for k in range(N_CHIP - 1):
            g = g + p_ref[k].astype(F32)
        delta, mn, vn = _adamw_math(w_ref[...], g, m_ref[...], v_ref[...])
        g_ref[...] = g
        d_ref[...] = delta
        mo_ref[...] = mn
        vo_ref[...] = vn
        token_ref[...] = jnp.zeros_like(token_ref)

    lay = pl.BlockSpec((None, tr, tc), lambda i, j, chip_ref: (layer, i, j))
    ins = [w, m, v, sums, recv, after] + (list(prev) if prev is not None else [])
    in_specs = [lay, lay, lay, pl.BlockSpec((None, tr, tc), lambda i, j, chip_ref: (chip_ref[0], i, j)),
                pl.BlockSpec((N_CHIP - 1, tr, tc), lambda i, j, chip_ref: (0, i, j)), HBM_ANY]
    in_specs += [HBM_ANY] * (4 if prev is not None else 0)
    token = pl.BlockSpec((8, LANE), lambda i, j, chip_ref: (0, 0))
    outs = pl.pallas_call(
        body, name=name,
        grid_spec=pltpu.PrefetchScalarGridSpec(
            num_scalar_prefetch=1, grid=(r // tr, c // tc), in_specs=in_specs, out_specs=[lay] * 4 + [token]),
        out_shape=[jax.ShapeDtypeStruct(w.shape, F32)] * 4 + [jax.ShapeDtypeStruct((8, LANE), F32)],
        input_output_aliases={7 + q: q for q in range(4)} if prev is not None else {},
        compiler_params=_params("arbitrary", "arbitrary"),
    )(chip, *ins)
    return outs[:4], outs[4]


def _adamw_small(w, g, m, v, *, name):
    def body(w_ref, g_ref, m_ref, v_ref, d_ref, mo_ref, vo_ref):
        d_ref[...], mo_ref[...], vo_ref[...] = _adamw_math(w_ref[...], g_ref[...], m_ref[...], v_ref[...])

    vm = pl.BlockSpec(memory_space=pltpu.VMEM)
    return pl.pallas_call(
        body, name=name, in_specs=[vm] * 4, out_specs=[vm] * 3,
        out_shape=[jax.ShapeDtypeStruct(w.shape, F32)] * 3,
        compiler_params=pltpu.CompilerParams(vmem_limit_bytes=VMEM_LIMIT),
    )(w, g, m, v)


def _pack(arrs):
    flat = jnp.concatenate([a.reshape(-1).astype(F32) for a in arrs])
    pad = (-flat.shape[0]) % (8 * LANE)
    return jnp.pad(flat, (0, pad)).reshape(-1, LANE)


def _unpack(packed, shapes):
    flat = packed.reshape(-1)
    out, off = [], 0
    for s in shapes:
        size = math.prod(s)
        out.append(flat[off:off + size].reshape(s))
        off += size
    return out


WEIGHTS = ['mix_norm_w', 'ffn_norm_w', 'final_norm_w', 'ssd_w_in', 'ssd_conv_w', 'ssd_conv_b', 'ssd_dt_bias',
           'ssd_a_log', 'ssd_d', 'ssd_norm_w', 'ssd_w_out', 'sc_w_in', 'sc_conv_w', 'sc_w_out', 'ffn_w_up',
           'ffn_conv_w', 'ffn_conv_b', 'ffn_w_down']
BIG = ('ssd_w_in', 'ssd_w_out', 'sc_w_in', 'sc_w_out', 'ffn_w_up', 'ffn_w_down')
SHARDED_SMALL = ('ssd_conv_w', 'sc_conv_w', 'ffn_conv_w')


def _lane_pad(v):
    return jnp.pad(v.astype(F32), (0, LANE - v.shape[0])).reshape(1, LANE)


def _gather_cols(g):
    return jnp.moveaxis(g, 0, -2).reshape(g.shape[1:-1] + (N_DEV * g.shape[-1],))


class _Gather:
    def __init__(self, bufs, tag):
        self.bufs, self.tag = bufs, tag

    def start_ici(self, after):
        self.sems = _split_start(self.bufs, _plan_gather_ici, 4 * len(self.bufs), after, name=f"ag_ici_start_{self.tag}")
        return self.sems[3]

    def hand_on(self, after):
        send, recv, bufs, _ = self.sems
        bufs = _split_wait(bufs, send, recv, after, _plan_gather_ici, name=f"ag_ici_wait_{self.tag}")
        self.sems = _split_start(bufs, _plan_gather_d2d, 3 * len(bufs), after, name=f"ag_d2d_start_{self.tag}")
        return self.sems[3]

    def finish(self, after):
        send, recv, bufs, _ = self.sems
        return _split_wait(bufs, send, recv, after, _plan_gather_d2d, name=f"ag_d2d_wait_{self.tag}")


class _Scatter:
    def __init__(self, grads, core, tag):
        self.grads, self.core, self.tag = grads, core, tag

    def start_pair(self, after):
        lands = [_land((N_CHIP,) + g.shape[1:], g.dtype) for g in self.grads]
        self.sems = _split_start(self.grads + lands, _plan_pair, N_CHIP * len(lands), after,
                                 name=f"rs_pair_start_{self.tag}")
        return self.sems[3]

    def start_chip(self, after):
        n = len(self.grads)
        send, recv, arrs, _ = self.sems
        arrs = _split_wait(arrs, send, recv, after, _plan_pair, name=f"rs_pair_wait_{self.tag}")
        self.sums = [_add_pair(g, o, self.core, name=f"rs_add_{self.tag}{a}")
                     for a, (g, o) in enumerate(zip(arrs[:n], arrs[n:]))]
        lands = [_land((N_CHIP - 1,) + s.shape[1:], s.dtype) for s in self.sums]
        self.sems = _split_start(self.sums + lands, _plan_chip, (N_CHIP - 1) * n, after,
                                 name=f"rs_chip_start_{self.tag}")
        return self.sems[3]

    def finish(self, after):
        n = len(self.grads)
        send, recv, arrs, _ = self.sems
        arrs = _split_wait(arrs, send, recv, after, _plan_chip, name=f"rs_chip_wait_{self.tag}")
        return list(zip(arrs[:n], arrs[n:]))


def kernel(x, mix_norm_w, ffn_norm_w, final_norm_w, ssd_w_in, ssd_conv_w, ssd_conv_b, ssd_dt_bias, ssd_a_log, ssd_d, ssd_norm_w, ssd_w_out, sc_w_in, sc_conv_w, sc_w_out, ffn_w_up, ffn_conv_w, ffn_conv_b, ffn_w_down, loss_target, m_mix_norm_w, m_ffn_norm_w, m_final_norm_w, m_ssd_w_in, m_ssd_conv_w, m_ssd_conv_b, m_ssd_dt_bias, m_ssd_a_log, m_ssd_d, m_ssd_norm_w, m_ssd_w_out, m_sc_w_in, m_sc_conv_w, m_sc_w_out, m_ffn_w_up, m_ffn_conv_w, m_ffn_conv_b, m_ffn_w_down, v_mix_norm_w, v_ffn_norm_w, v_final_norm_w, v_ssd_w_in, v_ssd_conv_w, v_ssd_conv_b, v_ssd_dt_bias, v_ssd_a_log, v_ssd_d, v_ssd_norm_w, v_ssd_w_out, v_sc_w_in, v_sc_conv_w, v_sc_w_out, v_ffn_w_up, v_ffn_conv_w, v_ffn_conv_b, v_ffn_w_down):
    args = locals()
    wt = {n: args[n] for n in WEIGHTS}
    mom = {n: args["m_" + n] for n in WEIGHTS}
    var = {n: args["v_" + n] for n in WEIGHTS}
    for src in (wt, mom, var):
        src['ssd_w_in'] = jnp.swapaxes(src['ssd_w_in'], 1, 2)

    t, d = x.shape[-2], x.shape[-1]
    cur = x.reshape(t, d)
    target = loss_target.reshape(t, d)
    depth = mix_norm_w.shape[0]
    n_ssd, n_sc = ssd_w_in.shape[0], sc_w_in.shape[0]
    heads = ssd_dt_bias.shape[1]
    di = ssd_norm_w.shape[1]
    conv_dim = ssd_conv_b.shape[1]
    bc = (conv_dim - di) // 2
    in_dim = N_DEV * ssd_w_in.shape[2]
    in_pad = di + conv_dim + LANE
    ff = ffn_w_down.shape[1] * N_DEV
    me = 4 * lax.axis_index("x") + 2 * lax.axis_index("y") + lax.axis_index("c")
    me_s = me.astype(jnp.int32).reshape(1)
    core_s = lax.axis_index("c").astype(jnp.int32).reshape(1)
    chip_s = (2 * lax.axis_index("x") + lax.axis_index("y")).astype(jnp.int32).reshape(1)

    names_of = {"ssd": ('ssd_w_in', 'ssd_w_out'), "sc": ('sc_w_in', 'sc_w_out'), "ffn": ('ffn_w_up', 'ffn_w_down')}
    order = []
    for i in range(depth):
        order += [("ssd" if i % 2 == 0 else "sc", i // 2), ("ffn", i)]
    def make_gather(s, after):
        kind, idx = order[s]
        bufs = []
        for n in names_of[kind]:
            after = _cast_layer(wt[n], idx, me_s, after, name=f"cast_{n}{idx}")
            bufs.append(after)
        return _Gather(bufs, f"{kind}{idx}"), after

    conv_full = [_gather_cols(g) for g in _all_gather([wt[n] for n in SHARDED_SMALL], name="ag_conv")]
    first, last_cast = make_gather(0, conv_full[0])
    gathers = [first]
    tok_a = last_cast = first.start_ici(last_cast)
    for s in range(1, len(order)):
        g, last_cast = make_gather(s, last_cast)
        gathers.append(g)
    tok_b = gathers[0].hand_on(last_cast)
    tok_c = gathers[1].start_ici(tok_b)
    weights = [None] * len(order)
    weights[0] = gathers[0].finish(tok_c)
    ssd_cw, sc_cw, ffn_cw = conv_full
    ffn_cw = ffn_cw.reshape(depth, ffn_cw.shape[1], 2, ff)
    ffn_cb = ffn_conv_b.reshape(depth, 2, ff)
    dexp = jnp.repeat(ssd_d.astype(F32), HEAD_DIM, axis=1)

    n_sub = len(order)
    full = {n: [None] * wt[n].shape[0] for n in BIG}

    def prefetch(s, after):
        return gathers[s + 2].start_ici(after) if s + 2 < n_sub else None

    def hand_on(s, after):
        return gathers[s + 1].hand_on(after) if s + 1 < n_sub else None

    def arrive(s, after):
        if s + 1 < n_sub:
            weights[s + 1] = gathers[s + 1].finish(after)
            use(s + 1)

    def use(s):
        kind, idx = order[s]
        g_in, g_out = weights[s]
        if kind == "sc":
            g_in = jnp.swapaxes(g_in, 0, 1).reshape(d, -1)
        if kind == "ssd":
            g_in = jnp.pad(g_in.reshape(in_dim, d).T, ((0, 0), (0, in_pad - in_dim)))
        n_in, n_out = names_of[kind]
        full[n_in][idx], full[n_out][idx] = g_in, g_out.reshape(-1, d)

    use(0)
    saved = []
    for i in range(depth):
        j = i // 2
        s = 2 * i
        rec = {"x_mix": cur}
        tok = prefetch(s, cur)
        h = _rmsnorm_fwd(cur, _with_tokens(mix_norm_w[i], tok, tok_c if i == 0 else None), name=f"norm_mix{i}")
        rec["h_mix"] = h
        if i % 2 == 0:
            zx = _mm_nn(h, full['ssd_w_in'][j], out_dtype=BF16, name=f"ssd_in{j}")
            cb = _with_tokens(ssd_conv_b[j].reshape(1, conv_dim), hand_on(s, zx))
            xbc, conv_pre = _ssd_conv_fwd(zx, ssd_cw[j], cb, di, name=f"ssd_conv{j}")
            ssd_vecs = (_lane_pad(ssd_dt_bias[j]), _lane_pad(ssd_a_log[j]), dexp[j].reshape(1, di),
                        ssd_norm_w[j].reshape(1, di))
            yn, y, states = _ssd_fwd(xbc, zx, *ssd_vecs, name=f"ssd_core{j}")
            arrive(s, yn)
            cur = _mm_nn(yn, full['ssd_w_out'][j], res=cur, out_dtype=F32, name=f"ssd_out{j}")
            rec.update(zx=zx, xbc=xbc, conv_pre=conv_pre, yn=yn, y=y, states=states, vecs=ssd_vecs)
        else:
            p3 = _mm_nn(h, full['sc_w_in'][j], out_dtype=BF16, out_parts=3, name=f"sc_in{j}")
            act = _sc_act_fwd(p3, _with_tokens(sc_cw[j], hand_on(s, p3)), name=f"sc_act{j}")
            arrive(s, act)
            cur = _mm_nn(act, full['sc_w_out'][j], res=cur, out_dtype=F32, name=f"sc_out{j}")
            rec.update(p3=p3, act=act)
        s += 1
        rec["x_ffn"] = cur
        h = _rmsnorm_fwd(cur, _with_tokens(ffn_norm_w[i], prefetch(s, cur)), name=f"norm_ffn{i}")
        u3 = _lin_in_fwd(h, full['ffn_w_up'][i], 2, name=f"ffn_up{i}")
        act, pre3 = _ffn_act_fwd(u3, ffn_cw[i], _with_tokens(ffn_cb[i], hand_on(s, u3)), name=f"ffn_act{i}")
        arrive(s, act)
        cur = _mm_nn(act, full['ffn_w_down'][i], res=cur, out_dtype=F32, name=f"ffn_down{i}")
        rec.update(h_ffn=h, u3=u3, pre3=pre3, ffn_act=act)
        saved.append(rec)

    dx, dxb, dw_final, loss8 = _loss_head(cur, final_norm_w, target, name="loss_head")

    small = {n: [None] * wt[n].shape[0] for n in WEIGHTS if n not in BIG and n != 'final_norm_w'}
    scatters = [None] * n_sub
    pending = None

    def chip_step(after):
        return pending.start_chip(after) if pending is not None else None

    for i in reversed(range(depth)):
        j = i // 2
        rec = saved[i]
        nb_up = ffn_w_up.shape[2]
        da = _mm_nt(dxb, full['ffn_w_down'][i], out_dtype=BF16, name=f"ffn_down_dx{i}")
        g_down = _mm_tn(rec["ffn_act"], dxb, out_dtype=BF16, name=f"ffn_down_dw{i}")
        du3, dcw, dcb = _ffn_act_bwd(rec["u3"], rec["pre3"], da, _with_tokens(ffn_cw[i], chip_step(da)),
                                     name=f"ffn_act_bwd{i}")
        g_up = _lin_in_dw(rec["h_ffn"], du3, nb_up, name=f"ffn_up_dw{i}")
        dh = _lin_in_dx(du3, full['ffn_w_up'][i], name=f"ffn_up_dx{i}")
        pending = scatters[2 * i + 1] = _Scatter([g_up, g_down.reshape(N_DEV, ff // N_DEV, d)], core_s, f"ffn{i}")
        tok = pending.start_pair(dh)
        dx, dxb, dwn = _rmsnorm_bwd(dh, rec["x_ffn"], _with_tokens(ffn_norm_w[i], tok), dx, name=f"norm_ffn_bwd{i}")
        small['ffn_conv_w'][i] = dcw.reshape(dcw.shape[0], 2 * ff)
        small['ffn_conv_b'][i] = dcb.reshape(2 * ff)
        small['ffn_norm_w'][i] = dwn.sum(axis=0)

        if i % 2 == 0:
            zx, xbc = rec["zx"], rec["xbc"]
            cw, cpre = ssd_cw[j], rec["conv_pre"]
            dyn = _mm_nt(dxb, full['ssd_w_out'][j], out_dtype=BF16, name=f"ssd_out_dx{j}")
            g_out = _mm_tn(rec["yn"], dxb, out_dtype=BF16, name=f"ssd_out_dw{j}")
            bias_t = _with_tokens(rec["vecs"][0], chip_step(dyn))
            dzx, dxs, db, dc, ddt_g, vec_acc, dnw, ddexp = _ssd_bwd(
                dyn, rec["y"], xbc, zx, rec["states"], bias_t, *rec["vecs"][1:], name=f"ssd_core_bwd{j}")
            dzx, dcw_x, dcb_x = _ssd_conv_bwd(zx, cpre, dxs, cw, dzx, di, 0, name=f"ssd_conv_bwd_x{j}")
            dzx, dcw_b, dcb_b = _ssd_conv_bwd(zx, cpre, db, cw, dzx, di, di, name=f"ssd_conv_bwd_b{j}")
            dzx, dcw_c, dcb_c = _ssd_conv_bwd(zx, cpre, dc, cw, dzx, di, di + bc, name=f"ssd_conv_bwd_c{j}")
            dzx = _ssd_put_ddt(ddt_g, dzx, (di + conv_dim) // LANE, name=f"ssd_put_ddt{j}")
            g_in = _mm_tn(rec["h_mix"], dzx, out_dtype=BF16, name=f"ssd_in_dw{j}")
            g_in = g_in[:, :in_dim].T.reshape(N_DEV, in_dim // N_DEV, d)
            dh = _mm_nt(dzx, full['ssd_w_in'][j], out_dtype=BF16, name=f"ssd_in_dx{j}")
            small['ssd_conv_w'][j] = jnp.concatenate([dcw_x, dcw_b, dcw_c], axis=1)
            small['ssd_conv_b'][j] = jnp.concatenate([dcb_x, dcb_b, dcb_c], axis=1).reshape(conv_dim)
            small['ssd_a_log'][j] = vec_acc[:, 0, :heads].sum(axis=0)
            small['ssd_dt_bias'][j] = vec_acc[:, 1, :heads].sum(axis=0)
            small['ssd_d'][j] = ddexp[:, 0, :].reshape(heads, HEAD_DIM).sum(axis=1)
            small['ssd_norm_w'][j] = dnw[:, 0, :].reshape(di)
            g_out = g_out.reshape(N_DEV, di // N_DEV, d)
        else:
            nb_in = sc_w_in.shape[2]
            da = _mm_nt(dxb, full['sc_w_out'][j], out_dtype=BF16, name=f"sc_out_dx{j}")
            g_out = _mm_tn(rec["act"], dxb, out_dtype=BF16, name=f"sc_out_dw{j}")
            dp3, dcw = _sc_act_bwd(rec["p3"], da, _with_tokens(sc_cw[j], chip_step(da)), name=f"sc_act_bwd{j}")
            g_in = _mm_tn(rec["h_mix"], dp3, out_dtype=BF16, name=f"sc_in_dw{j}")
            g_in = jnp.swapaxes(g_in.reshape(d, N_DEV, nb_in), 0, 1)
            dh = _mm_nt(dp3, full['sc_w_in'][j], out_dtype=BF16, name=f"sc_in_dx{j}")
            small['sc_conv_w'][j] = dcw
            g_out = g_out.reshape(N_DEV, g_out.shape[0] // N_DEV, d)
        pending = scatters[2 * i] = _Scatter([g_in, g_out], core_s, f"{order[2 * i][0]}{j}")
        tok = pending.start_pair(dh)
        dx, dxb, dwn = _rmsnorm_bwd(dh, rec["x_mix"], _with_tokens(mix_norm_w[i], tok), dx, name=f"norm_mix_bwd{i}")
        small['mix_norm_w'][i] = dwn.sum(axis=0)
    tok_last = chip_step(dx)

    small_names = [n for n in WEIGHTS if n not in BIG]
    partial = {n: jnp.stack(small[n]) for n in small}
    partial['final_norm_w'] = dw_final.sum(axis=0)
    full_shapes = [partial[n].shape for n in small_names]
    packed = _pack([loss8.sum().reshape(1)] + [partial[n] for n in small_names])
    ar_send, ar_recv, ar_bufs, tok_ar = _split_start(
        [packed, jnp.zeros((N_DEV,) + packed.shape, F32)], _plan_all, N_DEV - 1, tok_last, name="ar_small_start")

    grads, delta, new_m, new_v = {}, {}, {}, {}
    parts = {n: [None] * wt[n].shape[0] for n in BIG}
    for s in range(1, n_sub):
        kind, idx = order[s]
        parts[names_of[kind][0]][idx], parts[names_of[kind][1]][idx] = scatters[s].finish(tok_ar)
    first_in, first_out = names_of[order[0][0]]
    last_out = tok_ar
    jobs = [(n, layer) for n in reversed(BIG) for layer in reversed(range(wt[n].shape[0]))]
    jobs.sort(key=lambda job: parts[job[0]][job[1]] is None)
    chain = {n: None for n in BIG}
    for n, layer in jobs:
        if parts[n][layer] is None:
            parts[first_in][0], parts[first_out][0] = scatters[0].finish(last_out)
        chain[n], last_out = _adamw_layer(wt[n], mom[n], var[n], *parts[n][layer], chip_s, layer, chain[n], last_out,
                                          name=f"adamw_{n}{layer}")
    for n in BIG:
        grads[n], delta[n], new_m[n], new_v[n] = chain[n]
    for dst in (grads, delta, new_m, new_v):
        dst['ssd_w_in'] = jnp.swapaxes(dst['ssd_w_in'], 1, 2)

    mine, slots = _split_wait(ar_bufs, ar_send, ar_recv, last_out, _plan_all, name="ar_small_wait")
    total = _unpack(_sum_slots(mine, slots, me_s, name="ar_small_sum"), [(1,)] + full_shapes)
    loss = total[0].reshape(())
    grads.update(zip(small_names, total[1:]))
    for n in SHARDED_SMALL:
        nb = wt[n].shape[-1]
        grads[n] = lax.dynamic_slice_in_dim(grads[n], me * nb, nb, axis=grads[n].ndim - 1)
    shapes = [wt[n].shape for n in small_names]
    outs = _adamw_small(*[_pack([src[n] for n in small_names]) for src in (wt, grads, mom, var)], name="adamw_small")
    for dst, packed_out in zip((delta, new_m, new_v), outs):
        dst.update(zip(small_names, _unpack(packed_out, shapes)))

    return (loss, dx.reshape(x.shape), *[grads[n] for n in WEIGHTS], *[delta[n] for n in WEIGHTS],
            *[new_m[n] for n in WEIGHTS], *[new_v[n] for n in WEIGHTS])
```
